```python
import math
import jax, jax.numpy as jnp
from jax import lax
import numpy as np

D_MODEL = 2048
BATCH = 8
SEQ = 8192
DEPTH = 4

S5_WIDTH = D_MODEL // 2
S5_GROUP = 16
S5_GROUPS = S5_WIDTH // S5_GROUP
S5_STATE = 64
S5_DT_MIN = 0.001
S5_DT_MAX = 0.1
DN_HEADS = 8
DN_DK = 128
DN_DV = 128
DN_QK_WIDTH = DN_HEADS * DN_DK
DN_V_WIDTH = DN_HEADS * DN_DV
DN_CONV = 4
DN_CHUNK = 64
DN_DT_MIN = 0.001
DN_DT_MAX = 0.1
FFN_DIM = 5632
FFN_CONV = 3
NORM_EPS = 1e-6

OFF_U = S5_WIDTH
OFF_QKV = OFF_U + 2 * DN_QK_WIDTH + DN_V_WIDTH
OFF_Z = OFF_QKV + DN_V_WIDTH
OFF_BETA = OFF_Z + DN_HEADS
OFF_ALPHA = OFF_BETA + DN_HEADS
OFF_GS = OFF_ALPHA + D_MODEL
N_IN = OFF_GS + D_MODEL
SPLITS = [OFF_U, OFF_QKV, OFF_Z, OFF_BETA, OFF_ALPHA, OFF_GS]

kernel_name = "hybrid_s5_gdn_convffn"


def rmsnorm(x, w):
    xf = x.astype(jnp.float32)
    y = xf * lax.rsqrt(jnp.mean(xf * xf, axis=-1, keepdims=True) + NORM_EPS) * w.astype(jnp.float32)
    return y.astype(x.dtype)


def l2norm(x):
    return x * lax.rsqrt(jnp.sum(x * x, axis=-1, keepdims=True) + NORM_EPS)


def causal_dwconv(x, w):
    K, C = w.shape
    return lax.conv_general_dilated(
        x, w[:, None, :].astype(x.dtype), window_strides=(1,), padding=[(K - 1, 0)],
        dimension_numbers=("NWC", "WIO", "NWC"), feature_group_count=C)


def s5_branch(u, log_dt, a_re, a_im, b_re, b_im, c_re, c_im, d):
    Bn, L, _ = u.shape
    f32 = jnp.float32
    uf = u.astype(f32)
    ug = uf.reshape(Bn, L, S5_GROUPS, S5_GROUP)
    lr, li = a_re.astype(f32), a_im.astype(f32)
    dt = jnp.exp(log_dt.astype(f32))[:, None]
    mag = jnp.exp(lr * dt)
    abar_re, abar_im = mag * jnp.cos(li * dt), mag * jnp.sin(li * dt)
    den = lr * lr + li * li
    nr, ni = abar_re - 1.0, abar_im
    coef_re = (nr * lr + ni * li) / den
    coef_im = (ni * lr - nr * li) / den
    br, bi = b_re.astype(f32), b_im.astype(f32)
    bbar_re = coef_re[..., None] * br - coef_im[..., None] * bi
    bbar_im = coef_re[..., None] * bi + coef_im[..., None] * br
    bu_re = jnp.einsum("gph,blgh->blgp", bbar_re, ug)
    bu_im = jnp.einsum("gph,blgh->blgp", bbar_im, ug)
    ar_t = jnp.broadcast_to(abar_re, bu_re.shape)
    ai_t = jnp.broadcast_to(abar_im, bu_re.shape)

    def combine(e1, e2):
        a1r, a1i, b1r, b1i = e1
        a2r, a2i, b2r, b2i = e2
        return (a2r * a1r - a2i * a1i,
                a2r * a1i + a2i * a1r,
                a2r * b1r - a2i * b1i + b2r,
                a2r * b1i + a2i * b1r + b2i)

    _, _, xr, xi = lax.associative_scan(combine, (ar_t, ai_t, bu_re, bu_im), axis=1)
    y = (jnp.einsum("ghp,blgp->blgh", c_re.astype(f32), xr)
         - jnp.einsum("ghp,blgp->blgh", c_im.astype(f32), xi))
    y = y.reshape(Bn, L, S5_WIDTH) + d.astype(f32) * uf
    return jax.nn.gelu(y, approximate=False).astype(u.dtype)


def chunk_gated_delta_rule(q, k, v, g, beta):
    Bn, L, H, DK = q.shape
    DV = v.shape[-1]
    N, C = L // DN_CHUNK, DN_CHUNK

    def chunks(t):
        return t.reshape(Bn, N, C, H, -1).transpose(1, 0, 3, 2, 4)

    qc, kc, vc = chunks(q), chunks(k), chunks(v)
    bc = beta.reshape(Bn, N, C, H).transpose(1, 0, 3, 2)
    gc = jnp.cumsum(g.reshape(Bn, N, C, H).transpose(1, 0, 3, 2), axis=-1)
    tril = jnp.tril(jnp.ones((C, C), dtype=bool))
    strict = jnp.tril(jnp.ones((C, C), dtype=bool), -1)
    decay = jnp.exp(jnp.where(tril, gc[..., :, None] - gc[..., None, :], -jnp.inf))
    k_beta = kc * bc[..., None]
    v_beta = vc * bc[..., None]
    lmat = jnp.where(strict, jnp.einsum("nbhcd,nbhsd->nbhcs", k_beta, kc) * decay, 0.0)
    eye = jnp.eye(C, dtype=q.dtype)
    rhs = jnp.concatenate([v_beta, k_beta * jnp.exp(gc)[..., None]], axis=-1)
    sol = lax.linalg.triangular_solve(eye + lmat, rhs, left_side=True, lower=True, unit_diagonal=True)
    u_c, w_c = sol[..., :DV], sol[..., DV:]
    attn = jnp.where(tril, jnp.einsum("nbhcd,nbhsd->nbhcs", qc, kc) * decay, 0.0)

    def step(S, inp):
        q_i, k_i, u_i, w_i, g_i, a_i = inp
        v_new = u_i - jnp.einsum("bhcd,bhdv->bhcv", w_i, S)
        o_i = (jnp.einsum("bhcd,bhdv->bhcv", q_i * jnp.exp(g_i)[..., None], S)
               + jnp.einsum("bhcs,bhsv->bhcv", a_i, v_new))
        g_last = g_i[..., -1]
        S = (S * jnp.exp(g_last)[..., None, None]
             + jnp.einsum("bhcd,bhcv->bhdv", k_i * jnp.exp(g_last[..., None] - g_i)[..., None], v_new))
        return S, o_i

    S0 = jnp.zeros((Bn, H, DK, DV), q.dtype)
    _, o = lax.scan(step, S0, (qc, kc, u_c, w_c, gc, attn))
    return o.transpose(1, 0, 3, 2, 4).reshape(Bn, L, H, DV)


def deltanet_branch(qkv, z, b_raw, a_raw, conv_w, a_log, dt_bias, norm_w):
    Bn, L, _ = qkv.shape
    f32 = jnp.float32
    qkv = jax.nn.silu(causal_dwconv(qkv, conv_w))
    q, k, v = jnp.split(qkv, [DN_QK_WIDTH, 2 * DN_QK_WIDTH], axis=-1)
    q = l2norm(q.reshape(Bn, L, DN_HEADS, DN_DK).astype(f32)) * (DN_DK ** -0.5)
    k = l2norm(k.reshape(Bn, L, DN_HEADS, DN_DK).astype(f32))
    v = v.reshape(Bn, L, DN_HEADS, DN_DV).astype(f32)
    beta = jax.nn.sigmoid(b_raw.astype(f32))
    g = -jnp.exp(a_log.astype(f32)) * jax.nn.softplus(a_raw.astype(f32) + dt_bias.astype(f32))
    o = chunk_gated_delta_rule(q, k, v, g, beta)
    o = rmsnorm(o, norm_w) * jax.nn.silu(z.reshape(Bn, L, DN_HEADS, DN_DV).astype(f32))
    return o.reshape(Bn, L, DN_V_WIDTH).astype(qkv.dtype)


def _fwd_setup_inputs(seed: int = 0) -> dict:
    key = jax.random.key(seed)
    ks = jax.random.split(key, 26)
    f32 = jnp.float32

    def nrm(k, shape, scale):
        return jax.random.normal(k, shape, f32) * scale

    G, P, HG = S5_GROUPS, S5_STATE, S5_GROUP
    x = nrm(ks[0], (BATCH, SEQ, D_MODEL), 1.0)
    mix_norm_w = 1.0 + nrm(ks[1], (DEPTH, D_MODEL), 0.02)
    w_in = nrm(ks[2], (DEPTH, D_MODEL, N_IN), D_MODEL ** -0.5)
    s5_log_dt = jax.random.uniform(ks[3], (DEPTH, G), f32, math.log(S5_DT_MIN), math.log(S5_DT_MAX))
    s5_a_re = -0.5 + nrm(ks[4], (DEPTH, G, P), 0.01)
    s5_a_im = math.pi * jnp.arange(P, dtype=f32) + nrm(ks[5], (DEPTH, G, P), 0.01)
    s5_b_re = nrm(ks[6], (DEPTH, G, P, HG), (2 * HG) ** -0.5)
    s5_b_im = nrm(ks[7], (DEPTH, G, P, HG), (2 * HG) ** -0.5)
    s5_c_re = nrm(ks[8], (DEPTH, G, HG, P), (2 * P) ** -0.5)
    s5_c_im = nrm(ks[9], (DEPTH, G, HG, P), (2 * P) ** -0.5)
    s5_d = nrm(ks[10], (DEPTH, S5_WIDTH), 1.0)
    s5_glu_w = nrm(ks[11], (DEPTH, S5_WIDTH, 2 * D_MODEL), S5_WIDTH ** -0.5)
    dn_conv_w = nrm(ks[12], (DEPTH, DN_CONV, 2 * DN_QK_WIDTH + DN_V_WIDTH), DN_CONV ** -0.5)
    dn_a_log = jnp.log(jax.random.uniform(ks[13], (DEPTH, DN_HEADS), f32, 1.0, 16.0))
    dn_dt = jnp.exp(jax.random.uniform(ks[14], (DEPTH, DN_HEADS), f32, math.log(DN_DT_MIN), math.log(DN_DT_MAX)))
    dn_dt_bias = dn_dt + jnp.log(-jnp.expm1(-dn_dt))
    dn_norm_w = 1.0 + nrm(ks[15], (DEPTH, DN_DV), 0.02)
    dn_proj_w = nrm(ks[16], (DEPTH, DN_V_WIDTH, D_MODEL), DN_V_WIDTH ** -0.5)
    w_out = nrm(ks[17], (DEPTH, D_MODEL, D_MODEL), D_MODEL ** -0.5)
    ffn_norm_w = 1.0 + nrm(ks[18], (DEPTH, D_MODEL), 0.02)
    ffn_up = nrm(ks[19], (DEPTH, D_MODEL, 2 * FFN_DIM), D_MODEL ** -0.5)
    ffn_conv_w = nrm(ks[20], (DEPTH, FFN_CONV, 2 * FFN_DIM), FFN_CONV ** -0.5)
    ffn_down = nrm(ks[21], (DEPTH, FFN_DIM, D_MODEL), FFN_DIM ** -0.5)
    final_norm_w = 1.0 + nrm(ks[22], (D_MODEL,), 0.02)
    return {"x": x, "mix_norm_w": mix_norm_w, "w_in": w_in, "s5_log_dt": s5_log_dt,
            "s5_a_re": s5_a_re, "s5_a_im": s5_a_im, "s5_b_re": s5_b_re, "s5_b_im": s5_b_im,
            "s5_c_re": s5_c_re, "s5_c_im": s5_c_im, "s5_d": s5_d, "s5_glu_w": s5_glu_w,
            "dn_conv_w": dn_conv_w, "dn_a_log": dn_a_log, "dn_dt_bias": dn_dt_bias,
            "dn_norm_w": dn_norm_w, "dn_proj_w": dn_proj_w, "w_out": w_out,
            "ffn_norm_w": ffn_norm_w, "ffn_up": ffn_up, "ffn_conv_w": ffn_conv_w,
            "ffn_down": ffn_down, "final_norm_w": final_norm_w}


def _fwd_reference(x, mix_norm_w, w_in, s5_log_dt, s5_a_re, s5_a_im, s5_b_re, s5_b_im,
              s5_c_re, s5_c_im, s5_d, s5_glu_w, dn_conv_w, dn_a_log, dn_dt_bias,
              dn_norm_w, dn_proj_w, w_out, ffn_norm_w, ffn_up, ffn_conv_w, ffn_down,
              final_norm_w):
    for l in range(DEPTH):
        h = rmsnorm(x, mix_norm_w[l])
        proj = jnp.einsum("bld,de->ble", h, w_in[l])
        u, qkv, z, b_raw, a_raw, g_s5, g_dn = jnp.split(proj, SPLITS, axis=-1)
        y_s5 = s5_branch(u, s5_log_dt[l], s5_a_re[l], s5_a_im[l], s5_b_re[l], s5_b_im[l],
                         s5_c_re[l], s5_c_im[l], s5_d[l])
        glu_a, glu_b = jnp.split(jnp.einsum("blc,ce->ble", y_s5, s5_glu_w[l]), 2, axis=-1)
        br_s5 = glu_a * jax.nn.sigmoid(glu_b)
        y_dn = deltanet_branch(qkv, z, b_raw, a_raw, dn_conv_w[l], dn_a_log[l], dn_dt_bias[l], dn_norm_w[l])
        br_dn = jnp.einsum("blc,cd->bld", y_dn, dn_proj_w[l])
        merged = jax.nn.sigmoid(g_s5) * br_s5 + jax.nn.sigmoid(g_dn) * br_dn
        x = x + jnp.einsum("bld,de->ble", merged, w_out[l])
        h = rmsnorm(x, ffn_norm_w[l])
        up = causal_dwconv(jnp.einsum("bld,df->blf", h, ffn_up[l]), ffn_conv_w[l])
        act, val = jnp.split(up, 2, axis=-1)
        x = x + jnp.einsum("blf,fd->bld", jax.nn.silu(act) * val, ffn_down[l])
    return rmsnorm(x, final_norm_w)


import jax as _jax
import jax.numpy as _jnp

TWIN_FORMAT = 'train_step'
FWD_PARAMS = ['x', 'mix_norm_w', 'w_in', 's5_log_dt', 's5_a_re', 's5_a_im', 's5_b_re', 's5_b_im', 's5_c_re', 's5_c_im', 's5_d', 's5_glu_w', 'dn_conv_w', 'dn_a_log', 'dn_dt_bias', 'dn_norm_w', 'dn_proj_w', 'w_out', 'ffn_norm_w', 'ffn_up', 'ffn_conv_w', 'ffn_down', 'final_norm_w']
TWIN_WEIGHTS = ['mix_norm_w', 'w_in', 's5_log_dt', 's5_a_re', 's5_a_im', 's5_b_re', 's5_b_im', 's5_c_re', 's5_c_im', 's5_d', 's5_glu_w', 'dn_conv_w', 'dn_a_log', 'dn_dt_bias', 'dn_norm_w', 'dn_proj_w', 'w_out', 'ffn_norm_w', 'ffn_up', 'ffn_conv_w', 'ffn_down', 'final_norm_w']
TWIN_DIFF_INPUT = 'x'
TWIN_INPUTS = ['x', 'mix_norm_w', 'w_in', 's5_log_dt', 's5_a_re', 's5_a_im', 's5_b_re', 's5_b_im', 's5_c_re', 's5_c_im', 's5_d', 's5_glu_w', 'dn_conv_w', 'dn_a_log', 'dn_dt_bias', 'dn_norm_w', 'dn_proj_w', 'w_out', 'ffn_norm_w', 'ffn_up', 'ffn_conv_w', 'ffn_down', 'final_norm_w', 'loss_target', 'm_mix_norm_w', 'm_w_in', 'm_s5_log_dt', 'm_s5_a_re', 'm_s5_a_im', 'm_s5_b_re', 'm_s5_b_im', 'm_s5_c_re', 'm_s5_c_im', 'm_s5_d', 'm_s5_glu_w', 'm_dn_conv_w', 'm_dn_a_log', 'm_dn_dt_bias', 'm_dn_norm_w', 'm_dn_proj_w', 'm_w_out', 'm_ffn_norm_w', 'm_ffn_up', 'm_ffn_conv_w', 'm_ffn_down', 'm_final_norm_w', 'v_mix_norm_w', 'v_w_in', 'v_s5_log_dt', 'v_s5_a_re', 'v_s5_a_im', 'v_s5_b_re', 'v_s5_b_im', 'v_s5_c_re', 'v_s5_c_im', 'v_s5_d', 'v_s5_glu_w', 'v_dn_conv_w', 'v_dn_a_log', 'v_dn_dt_bias', 'v_dn_norm_w', 'v_dn_proj_w', 'v_w_out', 'v_ffn_norm_w', 'v_ffn_up', 'v_ffn_conv_w', 'v_ffn_down', 'v_final_norm_w']
TWIN_OUTPUTS = ['loss', 'grad_x', 'grad_mix_norm_w', 'grad_w_in', 'grad_s5_log_dt', 'grad_s5_a_re', 'grad_s5_a_im', 'grad_s5_b_re', 'grad_s5_b_im', 'grad_s5_c_re', 'grad_s5_c_im', 'grad_s5_d', 'grad_s5_glu_w', 'grad_dn_conv_w', 'grad_dn_a_log', 'grad_dn_dt_bias', 'grad_dn_norm_w', 'grad_dn_proj_w', 'grad_w_out', 'grad_ffn_norm_w', 'grad_ffn_up', 'grad_ffn_conv_w', 'grad_ffn_down', 'grad_final_norm_w', 'delta_mix_norm_w', 'delta_w_in', 'delta_s5_log_dt', 'delta_s5_a_re', 'delta_s5_a_im', 'delta_s5_b_re', 'delta_s5_b_im', 'delta_s5_c_re', 'delta_s5_c_im', 'delta_s5_d', 'delta_s5_glu_w', 'delta_dn_conv_w', 'delta_dn_a_log', 'delta_dn_dt_bias', 'delta_dn_norm_w', 'delta_dn_proj_w', 'delta_w_out', 'delta_ffn_norm_w', 'delta_ffn_up', 'delta_ffn_conv_w', 'delta_ffn_down', 'delta_final_norm_w', 'new_m_mix_norm_w', 'new_m_w_in', 'new_m_s5_log_dt', 'new_m_s5_a_re', 'new_m_s5_a_im', 'new_m_s5_b_re', 'new_m_s5_b_im', 'new_m_s5_c_re', 'new_m_s5_c_im', 'new_m_s5_d', 'new_m_s5_glu_w', 'new_m_dn_conv_w', 'new_m_dn_a_log', 'new_m_dn_dt_bias', 'new_m_dn_norm_w', 'new_m_dn_proj_w', 'new_m_w_out', 'new_m_ffn_norm_w', 'new_m_ffn_up', 'new_m_ffn_conv_w', 'new_m_ffn_down', 'new_m_final_norm_w', 'new_v_mix_norm_w', 'new_v_w_in', 'new_v_s5_log_dt', 'new_v_s5_a_re', 'new_v_s5_a_im', 'new_v_s5_b_re', 'new_v_s5_b_im', 'new_v_s5_c_re', 'new_v_s5_c_im', 'new_v_s5_d', 'new_v_s5_glu_w', 'new_v_dn_conv_w', 'new_v_dn_a_log', 'new_v_dn_dt_bias', 'new_v_dn_norm_w', 'new_v_dn_proj_w', 'new_v_w_out', 'new_v_ffn_norm_w', 'new_v_ffn_up', 'new_v_ffn_conv_w', 'new_v_ffn_down', 'new_v_final_norm_w']
TWIN_LEAF_KINDS = {'loss': 'loss', 'grad_x': 'grad_x', 'grad_mix_norm_w': 'grad_w', 'grad_w_in': 'grad_w', 'grad_s5_log_dt': 'grad_w', 'grad_s5_a_re': 'grad_w', 'grad_s5_a_im': 'grad_w', 'grad_s5_b_re': 'grad_w', 'grad_s5_b_im': 'grad_w', 'grad_s5_c_re': 'grad_w', 'grad_s5_c_im': 'grad_w', 'grad_s5_d': 'grad_w', 'grad_s5_glu_w': 'grad_w', 'grad_dn_conv_w': 'grad_w', 'grad_dn_a_log': 'grad_w', 'grad_dn_dt_bias': 'grad_w', 'grad_dn_norm_w': 'grad_w', 'grad_dn_proj_w': 'grad_w', 'grad_w_out': 'grad_w', 'grad_ffn_norm_w': 'grad_w', 'grad_ffn_up': 'grad_w', 'grad_ffn_conv_w': 'grad_w', 'grad_ffn_down': 'grad_w', 'grad_final_norm_w': 'grad_w', 'delta_mix_norm_w': 'delta_w', 'delta_w_in': 'delta_w', 'delta_s5_log_dt': 'delta_w', 'delta_s5_a_re': 'delta_w', 'delta_s5_a_im': 'delta_w', 'delta_s5_b_re': 'delta_w', 'delta_s5_b_im': 'delta_w', 'delta_s5_c_re': 'delta_w', 'delta_s5_c_im': 'delta_w', 'delta_s5_d': 'delta_w', 'delta_s5_glu_w': 'delta_w', 'delta_dn_conv_w': 'delta_w', 'delta_dn_a_log': 'delta_w', 'delta_dn_dt_bias': 'delta_w', 'delta_dn_norm_w': 'delta_w', 'delta_dn_proj_w': 'delta_w', 'delta_w_out': 'delta_w', 'delta_ffn_norm_w': 'delta_w', 'delta_ffn_up': 'delta_w', 'delta_ffn_conv_w': 'delta_w', 'delta_ffn_down': 'delta_w', 'delta_final_norm_w': 'delta_w', 'new_m_mix_norm_w': 'new_m', 'new_m_w_in': 'new_m', 'new_m_s5_log_dt': 'new_m', 'new_m_s5_a_re': 'new_m', 'new_m_s5_a_im': 'new_m', 'new_m_s5_b_re': 'new_m', 'new_m_s5_b_im': 'new_m', 'new_m_s5_c_re': 'new_m', 'new_m_s5_c_im': 'new_m', 'new_m_s5_d': 'new_m', 'new_m_s5_glu_w': 'new_m', 'new_m_dn_conv_w': 'new_m', 'new_m_dn_a_log': 'new_m', 'new_m_dn_dt_bias': 'new_m', 'new_m_dn_norm_w': 'new_m', 'new_m_dn_proj_w': 'new_m', 'new_m_w_out': 'new_m', 'new_m_ffn_norm_w': 'new_m', 'new_m_ffn_up': 'new_m', 'new_m_ffn_conv_w': 'new_m', 'new_m_ffn_down': 'new_m', 'new_m_final_norm_w': 'new_m', 'new_v_mix_norm_w': 'new_v', 'new_v_w_in': 'new_v', 'new_v_s5_log_dt': 'new_v', 'new_v_s5_a_re': 'new_v', 'new_v_s5_a_im': 'new_v', 'new_v_s5_b_re': 'new_v', 'new_v_s5_b_im': 'new_v', 'new_v_s5_c_re': 'new_v', 'new_v_s5_c_im': 'new_v', 'new_v_s5_d': 'new_v', 'new_v_s5_glu_w': 'new_v', 'new_v_dn_conv_w': 'new_v', 'new_v_dn_a_log': 'new_v', 'new_v_dn_dt_bias': 'new_v', 'new_v_dn_norm_w': 'new_v', 'new_v_dn_proj_w': 'new_v', 'new_v_w_out': 'new_v', 'new_v_ffn_norm_w': 'new_v', 'new_v_ffn_up': 'new_v', 'new_v_ffn_conv_w': 'new_v', 'new_v_ffn_down': 'new_v', 'new_v_final_norm_w': 'new_v'}


def _forward(args):
    return _fwd_reference(*[args[k] for k in FWD_PARAMS])


def _output_shape():
    def fwd():
        inp = _fwd_setup_inputs(0)
        return _fwd_reference(*[inp[k] for k in FWD_PARAMS])
    out = _jax.eval_shape(fwd)
    return out.shape, out.dtype

N_MICROBATCH = 1
ADAM_LR = 0.001
ADAM_B1 = 0.9
ADAM_B2 = 0.999
ADAM_EPS = 1e-08
ADAM_WD = 0.01
ADAM_STEP = 10
PER_EXAMPLE_BATCH_AXIS = {'x': 0, 'loss_target': 0}
SHARED_INPUTS = []
_WEIGHT_DTYPES = {'mix_norm_w': _jnp.float32, 'w_in': _jnp.float32, 's5_log_dt': _jnp.float32, 's5_a_re': _jnp.float32, 's5_a_im': _jnp.float32, 's5_b_re': _jnp.float32, 's5_b_im': _jnp.float32, 's5_c_re': _jnp.float32, 's5_c_im': _jnp.float32, 's5_d': _jnp.float32, 's5_glu_w': _jnp.float32, 'dn_conv_w': _jnp.float32, 'dn_a_log': _jnp.float32, 'dn_dt_bias': _jnp.float32, 'dn_norm_w': _jnp.float32, 'dn_proj_w': _jnp.float32, 'w_out': _jnp.float32, 'ffn_norm_w': _jnp.float32, 'ffn_up': _jnp.float32, 'ffn_conv_w': _jnp.float32, 'ffn_down': _jnp.float32, 'final_norm_w': _jnp.float32}
MOMENT_SCALE = {'mix_norm_w': 7.893013e-02, 'w_in': 3.716553e-02, 's5_log_dt': 1.483477e+00, 's5_a_re': 2.132157e-03, 's5_a_im': 2.018081e-03, 's5_b_re': 1.285771e-03, 's5_b_im': 1.296636e-03, 's5_c_re': 2.595752e-03, 's5_c_im': 2.570941e-03, 's5_d': 3.899003e-02, 's5_glu_w': 1.921382e-02, 'dn_conv_w': 4.624299e-02, 'dn_a_log': 2.027847e-01, 'dn_dt_bias': 1.979419e-01, 'dn_norm_w': 1.594119e-01, 'dn_proj_w': 4.236288e-02, 'w_out': 4.990011e-02, 'ffn_norm_w': 9.531367e-02, 'ffn_up': 4.070993e-02, 'ffn_conv_w': 4.032625e-02, 'ffn_down': 6.641349e-02, 'final_norm_w': 3.198452e+01}


def _to_microbatches(a, axis):
    t = _jnp.moveaxis(a, axis, 0)
    t = t.reshape((N_MICROBATCH, t.shape[0] // N_MICROBATCH) + t.shape[1:])
    return _jnp.moveaxis(t, 1, axis + 1)


def setup_inputs(seed: int = 0) -> dict:
    inp = _fwd_setup_inputs(seed)
    key = _jax.random.fold_in(_jax.random.key(seed), 7919)
    shape, _ = _output_shape()
    out = dict(inp)
    out["loss_target"] = _jax.random.normal(_jax.random.fold_in(key, 0), shape, _jnp.float32)
    for i, name in enumerate(TWIN_WEIGHTS):
        w = inp[name].astype(_jnp.float32)
        if MOMENT_SCALE is None:
            s = _jnp.sqrt(_jnp.mean(_jnp.square(w)) + 1e-30)
        else:
            s = MOMENT_SCALE[name]
        km, kv = _jax.random.split(_jax.random.fold_in(key, i + 1))
        out[name] = w
        out["m_" + name] = s * _jax.random.normal(km, w.shape, _jnp.float32)
        out["v_" + name] = (s * s) * _jax.random.uniform(kv, w.shape, _jnp.float32, 0.5, 1.5)
    if N_MICROBATCH > 1:
        for name, axis in PER_EXAMPLE_BATCH_AXIS.items():
            out[name] = _to_microbatches(out[name], axis)
    return {'x': out['x'], 'mix_norm_w': out['mix_norm_w'], 'w_in': out['w_in'], 's5_log_dt': out['s5_log_dt'], 's5_a_re': out['s5_a_re'], 's5_a_im': out['s5_a_im'], 's5_b_re': out['s5_b_re'], 's5_b_im': out['s5_b_im'], 's5_c_re': out['s5_c_re'], 's5_c_im': out['s5_c_im'], 's5_d': out['s5_d'], 's5_glu_w': out['s5_glu_w'], 'dn_conv_w': out['dn_conv_w'], 'dn_a_log': out['dn_a_log'], 'dn_dt_bias': out['dn_dt_bias'], 'dn_norm_w': out['dn_norm_w'], 'dn_proj_w': out['dn_proj_w'], 'w_out': out['w_out'], 'ffn_norm_w': out['ffn_norm_w'], 'ffn_up': out['ffn_up'], 'ffn_conv_w': out['ffn_conv_w'], 'ffn_down': out['ffn_down'], 'final_norm_w': out['final_norm_w'], 'loss_target': out['loss_target'], 'm_mix_norm_w': out['m_mix_norm_w'], 'm_w_in': out['m_w_in'], 'm_s5_log_dt': out['m_s5_log_dt'], 'm_s5_a_re': out['m_s5_a_re'], 'm_s5_a_im': out['m_s5_a_im'], 'm_s5_b_re': out['m_s5_b_re'], 'm_s5_b_im': out['m_s5_b_im'], 'm_s5_c_re': out['m_s5_c_re'], 'm_s5_c_im': out['m_s5_c_im'], 'm_s5_d': out['m_s5_d'], 'm_s5_glu_w': out['m_s5_glu_w'], 'm_dn_conv_w': out['m_dn_conv_w'], 'm_dn_a_log': out['m_dn_a_log'], 'm_dn_dt_bias': out['m_dn_dt_bias'], 'm_dn_norm_w': out['m_dn_norm_w'], 'm_dn_proj_w': out['m_dn_proj_w'], 'm_w_out': out['m_w_out'], 'm_ffn_norm_w': out['m_ffn_norm_w'], 'm_ffn_up': out['m_ffn_up'], 'm_ffn_conv_w': out['m_ffn_conv_w'], 'm_ffn_down': out['m_ffn_down'], 'm_final_norm_w': out['m_final_norm_w'], 'v_mix_norm_w': out['v_mix_norm_w'], 'v_w_in': out['v_w_in'], 'v_s5_log_dt': out['v_s5_log_dt'], 'v_s5_a_re': out['v_s5_a_re'], 'v_s5_a_im': out['v_s5_a_im'], 'v_s5_b_re': out['v_s5_b_re'], 'v_s5_b_im': out['v_s5_b_im'], 'v_s5_c_re': out['v_s5_c_re'], 'v_s5_c_im': out['v_s5_c_im'], 'v_s5_d': out['v_s5_d'], 'v_s5_glu_w': out['v_s5_glu_w'], 'v_dn_conv_w': out['v_dn_conv_w'], 'v_dn_a_log': out['v_dn_a_log'], 'v_dn_dt_bias': out['v_dn_dt_bias'], 'v_dn_norm_w': out['v_dn_norm_w'], 'v_dn_proj_w': out['v_dn_proj_w'], 'v_w_out': out['v_w_out'], 'v_ffn_norm_w': out['v_ffn_norm_w'], 'v_ffn_up': out['v_ffn_up'], 'v_ffn_conv_w': out['v_ffn_conv_w'], 'v_ffn_down': out['v_ffn_down'], 'v_final_norm_w': out['v_final_norm_w']}


def _loss(weights, diff, rest, loss_target):
    with _jax.named_scope("forward"):
        args = {**rest, TWIN_DIFF_INPUT: diff, **{k: w.astype(_WEIGHT_DTYPES[k]) for k, w in weights.items()}}
        y = _forward(args)
    with _jax.named_scope("loss_head"):
        err = _jnp.square(y.astype(_jnp.float32) - loss_target)
        return 0.5 * _jnp.sum(_jnp.mean(err, axis=-1)) if err.ndim else 0.5 * err


def _adamw(w, g, m, v):
    m = ADAM_B1 * m + (1.0 - ADAM_B1) * g
    v = ADAM_B2 * v + (1.0 - ADAM_B2) * _jnp.square(g)
    m_hat = m / (1.0 - ADAM_B1 ** ADAM_STEP)
    v_hat = v / (1.0 - ADAM_B2 ** ADAM_STEP)
    delta = -ADAM_LR * (m_hat / (_jnp.sqrt(v_hat) + ADAM_EPS) + ADAM_WD * w)
    return delta, m, v


def reference(x, mix_norm_w, w_in, s5_log_dt, s5_a_re, s5_a_im, s5_b_re, s5_b_im, s5_c_re, s5_c_im, s5_d, s5_glu_w, dn_conv_w, dn_a_log, dn_dt_bias, dn_norm_w, dn_proj_w, w_out, ffn_norm_w, ffn_up, ffn_conv_w, ffn_down, final_norm_w, loss_target, m_mix_norm_w, m_w_in, m_s5_log_dt, m_s5_a_re, m_s5_a_im, m_s5_b_re, m_s5_b_im, m_s5_c_re, m_s5_c_im, m_s5_d, m_s5_glu_w, m_dn_conv_w, m_dn_a_log, m_dn_dt_bias, m_dn_norm_w, m_dn_proj_w, m_w_out, m_ffn_norm_w, m_ffn_up, m_ffn_conv_w, m_ffn_down, m_final_norm_w, v_mix_norm_w, v_w_in, v_s5_log_dt, v_s5_a_re, v_s5_a_im, v_s5_b_re, v_s5_b_im, v_s5_c_re, v_s5_c_im, v_s5_d, v_s5_glu_w, v_dn_conv_w, v_dn_a_log, v_dn_dt_bias, v_dn_norm_w, v_dn_proj_w, v_w_out, v_ffn_norm_w, v_ffn_up, v_ffn_conv_w, v_ffn_down, v_final_norm_w):
    given = dict(x=x, mix_norm_w=mix_norm_w, w_in=w_in, s5_log_dt=s5_log_dt, s5_a_re=s5_a_re, s5_a_im=s5_a_im, s5_b_re=s5_b_re, s5_b_im=s5_b_im, s5_c_re=s5_c_re, s5_c_im=s5_c_im, s5_d=s5_d, s5_glu_w=s5_glu_w, dn_conv_w=dn_conv_w, dn_a_log=dn_a_log, dn_dt_bias=dn_dt_bias, dn_norm_w=dn_norm_w, dn_proj_w=dn_proj_w, w_out=w_out, ffn_norm_w=ffn_norm_w, ffn_up=ffn_up, ffn_conv_w=ffn_conv_w, ffn_down=ffn_down, final_norm_w=final_norm_w, loss_target=loss_target, m_mix_norm_w=m_mix_norm_w, m_w_in=m_w_in, m_s5_log_dt=m_s5_log_dt, m_s5_a_re=m_s5_a_re, m_s5_a_im=m_s5_a_im, m_s5_b_re=m_s5_b_re, m_s5_b_im=m_s5_b_im, m_s5_c_re=m_s5_c_re, m_s5_c_im=m_s5_c_im, m_s5_d=m_s5_d, m_s5_glu_w=m_s5_glu_w, m_dn_conv_w=m_dn_conv_w, m_dn_a_log=m_dn_a_log, m_dn_dt_bias=m_dn_dt_bias, m_dn_norm_w=m_dn_norm_w, m_dn_proj_w=m_dn_proj_w, m_w_out=m_w_out, m_ffn_norm_w=m_ffn_norm_w, m_ffn_up=m_ffn_up, m_ffn_conv_w=m_ffn_conv_w, m_ffn_down=m_ffn_down, m_final_norm_w=m_final_norm_w, v_mix_norm_w=v_mix_norm_w, v_w_in=v_w_in, v_s5_log_dt=v_s5_log_dt, v_s5_a_re=v_s5_a_re, v_s5_a_im=v_s5_a_im, v_s5_b_re=v_s5_b_re, v_s5_b_im=v_s5_b_im, v_s5_c_re=v_s5_c_re, v_s5_c_im=v_s5_c_im, v_s5_d=v_s5_d, v_s5_glu_w=v_s5_glu_w, v_dn_conv_w=v_dn_conv_w, v_dn_a_log=v_dn_a_log, v_dn_dt_bias=v_dn_dt_bias, v_dn_norm_w=v_dn_norm_w, v_dn_proj_w=v_dn_proj_w, v_w_out=v_w_out, v_ffn_norm_w=v_ffn_norm_w, v_ffn_up=v_ffn_up, v_ffn_conv_w=v_ffn_conv_w, v_ffn_down=v_ffn_down, v_final_norm_w=v_final_norm_w)
    weights = {n: given[n] for n in TWIN_WEIGHTS}
    shared = {n: given[n] for n in SHARED_INPUTS}
    per_example = {n: given[n] for n in ['x']}
    grad_fn = _jax.value_and_grad(_loss, argnums=(0, 1))

    def one_microbatch(ex, loss_target):
        ex = dict(ex)
        diff = ex.pop(TWIN_DIFF_INPUT)
        return grad_fn(weights, diff, {**shared, **ex}, loss_target)

    if N_MICROBATCH == 1:
        loss, (grad_w, grad_x) = one_microbatch(per_example, given["loss_target"])
    else:
        def body(carry, xs):
            loss_sum, grad_sum = carry
            l_k, (gw_k, gx_k) = one_microbatch(xs[0], xs[1])
            with _jax.named_scope("update"):
                return (loss_sum + l_k, _jax.tree.map(_jnp.add, grad_sum, gw_k)), gx_k

        init = (_jnp.zeros((), _jnp.float32), _jax.tree.map(_jnp.zeros_like, weights))
        (loss, grad_w), grad_x = _jax.lax.scan(body, init, (per_example, given["loss_target"]))
    with _jax.named_scope("update"):
        delta_w, new_m, new_v = {}, {}, {}
        for n in TWIN_WEIGHTS:
            delta_w[n], new_m[n], new_v[n] = _adamw(weights[n], grad_w[n], given["m_" + n], given["v_" + n])
    return (loss, grad_x, *[grad_w[n] for n in TWIN_WEIGHTS], *[delta_w[n] for n in TWIN_WEIGHTS],
            *[new_m[n] for n in TWIN_WEIGHTS], *[new_v[n] for n in TWIN_WEIGHTS])
```

```python
import functools
import math

import jax
import jax.numpy as jnp
from jax import lax
from jax.experimental import pallas as pl
from jax.experimental.pallas import tpu as pltpu

f32 = jnp.float32
bf16 = jnp.bfloat16
S = jax.ShapeDtypeStruct

NORM_EPS = 1e-6
DN_CHUNK = 64
S5_GROUP = 16
ADAM_LR, ADAM_B1, ADAM_B2, ADAM_EPS, ADAM_WD, ADAM_STEP = 0.001, 0.9, 0.999, 1e-08, 0.01, 10
VMEM_LIMIT_BYTES = 56 * 1024 * 1024
HALO = 8
MESH = pl.DeviceIdType.MESH
N_CHIPS = 4


def _pick(n, cands):
    for c in cands:
        if n % c == 0:
            return c
    return n


def _cparams(sem):
    return pltpu.CompilerParams(dimension_semantics=sem, vmem_limit_bytes=VMEM_LIMIT_BYTES)


_DIMS = {"nn": ((1,), (0,)), "nt": ((1,), (1,)), "tn": ((0,), (0,))}


def mm(a, b, mode, name, extras=(), epi=None, out_dtypes=(f32,), tm=None, tn=None, tk=None):
    if mode == "tn":
        K, M = a.shape
    else:
        M, K = a.shape
    if mode == "nt":
        N, K2 = b.shape
    else:
        K2, N = b.shape
    assert K == K2, (a.shape, b.shape, mode)
    tm = tm or _pick(M, (512, 256, 128, 64, 32, 16, 8))
    tn = tn or _pick(N, (1024, 512, 256, 128))
    tk = tk or (K if K <= 2048 else _pick(K, (2048, 1024, 512, 256, 128)))
    nk = K // tk
    assert M % tm == 0 and N % tn == 0 and K % tk == 0
    if mode == "tn":
        a_spec = pl.BlockSpec((tk, tm), lambda j, i, k: (k, i))
    else:
        a_spec = pl.BlockSpec((tm, tk), lambda j, i, k: (i, k))
    if mode == "nt":
        b_spec = pl.BlockSpec((tn, tk), lambda j, i, k: (j, k))
    else:
        b_spec = pl.BlockSpec((tk, tn), lambda j, i, k: (k, j))
    ex_specs = []
    for e in extras:
        if e.shape == (M, N):
            ex_specs.append(pl.BlockSpec((tm, tn), lambda j, i, k: (i, j)))
        elif e.shape == (1, N):
            ex_specs.append(pl.BlockSpec((1, tn), lambda j, i, k: (0, j)))
        elif e.shape == (M, 1):
            ex_specs.append(pl.BlockSpec((tm, 1), lambda j, i, k: (i, 0)))
        else:
            raise ValueError((e.shape, M, N))
    ne, no = len(extras), len(out_dtypes)
    dims = (_DIMS[mode], ((), ()))

    def body(a_ref, b_ref, *rest):
        ex, outs = rest[:ne], rest[ne:ne + no]
        p = lax.dot_general(a_ref[...].astype(bf16), b_ref[...].astype(bf16), dims, preferred_element_type=f32)

        def finish(acc):
            res = epi(acc, *[e[...] for e in ex]) if epi is not None else (acc,)
            for o, r in zip(outs, res):
                o[...] = r.astype(o.dtype)

        if nk == 1:
            finish(p)
        else:
            acc_ref = rest[-1]
            k = pl.program_id(2)

            @pl.when(k == 0)
            def _():
                acc_ref[...] = p

            @pl.when(k > 0)
            def _():
                acc_ref[...] += p

            @pl.when(k == nk - 1)
            def _():
                finish(acc_ref[...])

    outs = pl.pallas_call(
        body,
        name=name,
        grid=(N // tn, M // tm, nk),
        in_specs=[a_spec, b_spec] + ex_specs,
        out_specs=[pl.BlockSpec((tm, tn), lambda j, i, k: (i, j)) for _ in out_dtypes],
        out_shape=[S((M, N), dt) for dt in out_dtypes],
        scratch_shapes=[pltpu.VMEM((tm, tn), f32)] if nk > 1 else [],
        compiler_params=_cparams(("parallel", "parallel", "arbitrary")),
    )(a, b, *extras)
    return outs[0] if no == 1 else tuple(outs)


def _add(acc, prev):
    return (acc + prev,)


def rowk(name, fn, L, tl, ncol, ins, outs, accs=()):
    nrow = L // tl
    assert L % tl == 0 and tl % HALO == 0
    hb = tl // HALO

    def cw_of(c_total):
        assert c_total % ncol == 0, (name, c_total, ncol)
        return c_total // ncol

    in_specs = []
    for arr, kind in ins:
        if kind == "rows":
            in_specs.append(pl.BlockSpec((tl, cw_of(arr.shape[1])), lambda j, i: (i, j)))
        elif kind == "prev":
            in_specs.append(pl.BlockSpec((HALO, cw_of(arr.shape[1])), lambda j, i: (jnp.maximum(i * hb - 1, 0), j)))
        elif kind == "next":
            in_specs.append(pl.BlockSpec((HALO, cw_of(arr.shape[1])), lambda j, i: (jnp.minimum((i + 1) * hb, nrow * hb - 1), j)))
        elif kind == "cols":
            in_specs.append(pl.BlockSpec((arr.shape[0], cw_of(arr.shape[1])), lambda j, i: (0, j)))
        elif kind == "const":
            in_specs.append(pl.BlockSpec(arr.shape, lambda j, i: (0,) * arr.ndim))
        else:
            raise ValueError(kind)
    out_specs = [pl.BlockSpec((tl, cw_of(c)), lambda j, i: (i, j)) for c, _ in outs]
    out_shape = [S((L, c), dt) for c, dt in outs]
    out_specs += [pl.BlockSpec((r, cw_of(c)), lambda j, i: (0, j)) for r, c, _ in accs]
    out_shape += [S((r, c), dt) for r, c, dt in accs]
    ni, no, na = len(ins), len(outs), len(accs)

    def body(*refs):
        i = pl.program_id(1)
        res = fn(i, nrow, *[r[...] for r in refs[:ni]])
        for o, r in zip(refs[ni:ni + no], res[:no]):
            o[...] = r.astype(o.dtype)
        for o, r in zip(refs[ni + no:ni + no + na], res[no:]):
            @pl.when(i == 0)
            def _(o=o, r=r):
                o[...] = r.astype(o.dtype)

            @pl.when(i > 0)
            def _(o=o, r=r):
                o[...] += r.astype(o.dtype)

    res = pl.pallas_call(
        body,
        name=name,
        grid=(ncol, nrow),
        in_specs=in_specs,
        out_specs=out_specs,
        out_shape=out_shape,
        compiler_params=_cparams(("parallel", "arbitrary")),
    )(*[a for a, _ in ins])
    return tuple(res)


def _sigmoid(x):
    return 1.0 / (1.0 + jnp.exp(-x))


def _silu(x):
    return x * _sigmoid(x)


def _dsilu(x):
    s = _sigmoid(x)
    return s * (1.0 + x * (1.0 - s))


def _erf(x):
    a = jnp.abs(x)
    t = 1.0 / (1.0 + 0.3275911 * a)
    poly = t * (0.254829592 + t * (-0.284496736 + t * (1.421413741 + t * (-1.453152027 + t * 1.061405429))))
    y = 1.0 - poly * jnp.exp(-a * a)
    return jnp.where(x < 0, -y, y)


def _gelu(x):
    return 0.5 * x * (1.0 + _erf(x * (2.0 ** -0.5)))


def _dgelu(x):
    cdf = 0.5 * (1.0 + _erf(x * (2.0 ** -0.5)))
    pdf = jnp.exp(-0.5 * x * x) * (1.0 / math.sqrt(2.0 * math.pi))
    return cdf + x * pdf


def _rms(x, w):
    return x * lax.rsqrt(jnp.mean(x * x, axis=-1, keepdims=True) + NORM_EPS) * w


def _rms_bwd(x, w, dy):
    d = x.shape[-1]
    r = lax.rsqrt(jnp.mean(x * x, axis=-1, keepdims=True) + NORM_EPS)
    wdy = w * dy
    dx = r * wdy - x * (r * r * r) * (jnp.sum(x * wdy, axis=-1, keepdims=True) / d)
    dw = jnp.sum(x * r * dy, axis=0, keepdims=True)
    return dx, dw


def _from_cols(cols, width):
    tl = cols[0].shape[0]
    lane = lax.broadcasted_iota(jnp.int32, (tl, width), 1)
    out = jnp.zeros((tl, width), f32)
    for n, col in enumerate(cols):
        out = jnp.where(lane == n, col, out)
    return out


def _from_rows(rows):
    c = rows[0].shape[1]
    sub = lax.broadcasted_iota(jnp.int32, (len(rows), c), 0)
    out = jnp.zeros((len(rows), c), f32)
    for n, row in enumerate(rows):
        out = jnp.where(sub == n, row, out)
    return out


def _shift_down(x, halo, s, first):
    if s == 0:
        return x
    tl = x.shape[0]
    halo = jnp.where(first, 0.0, halo)
    xx = jnp.concatenate([halo, x], axis=0)
    return pltpu.roll(xx, s, 0)[HALO:HALO + tl]


def _shift_up(x, halo, s, last):
    if s == 0:
        return x
    tl = x.shape[0]
    halo = jnp.where(last, 0.0, halo)
    xx = jnp.concatenate([x, halo], axis=0)
    return pltpu.roll(xx, tl + HALO - s, 0)[0:tl]


def _causal_conv(x, halo, w, first):
    kw = w.shape[0]
    shifted = [_shift_down(x, halo, kw - 1 - j, first) for j in range(kw)]
    out = shifted[0] * w[0:1]
    for j in range(1, kw):
        out = out + shifted[j] * w[j:j + 1]
    return out, shifted


def rms_fwd(x, w, name):
    L, D = x.shape

    def fn(i, n, xb, wb):
        return (_rms(xb, wb),)

    return rowk(name, fn, L, _pick(L, (256, 128, 64, 32, 16, 8)), 1, [(x, "rows"), (w, "const")], [(D, bf16)])[0]


def rms_bwd(x, w, dh, dres, name):
    L, D = x.shape

    def fn(i, n, xb, wb, dhb, drb):
        dx, dw = _rms_bwd(xb, wb, dhb)
        dx = dx + drb
        return dx, dx, dw

    return rowk(name, fn, L, _pick(L, (256, 128, 64, 32, 16, 8)), 1,
                [(x, "rows"), (w, "const"), (dh, "rows"), (dres, "rows")], [(D, f32), (D, bf16)], [(1, D, f32)])


def loss_head(x, w, target, name):
    L, D = x.shape

    def fn(i, n, xb, wb, tb):
        err = _rms(xb, wb) - tb
        loss = 0.5 * jnp.sum(err * err) / D
        dx, dw = _rms_bwd(xb, wb, err / D)
        return dx, dx, jnp.full((8, 128), loss, f32), dw

    return rowk(name, fn, L, _pick(L, (256, 128, 64, 32, 16, 8)), 1,
                [(x, "rows"), (w, "const"), (target, "rows")], [(D, f32), (D, bf16)], [(8, 128, f32), (1, D, f32)])


def _s5_disc_math(lr, li, logdt, br, bi):
    dt = jnp.exp(logdt)
    mag = jnp.exp(lr * dt)
    ar, ai = mag * jnp.cos(li * dt), mag * jnp.sin(li * dt)
    den = lr * lr + li * li
    nr, ni = ar - 1.0, ai
    cr = (nr * lr + ni * li) / den
    ci = (ni * lr - nr * li) / den
    return ar, ai, cr * br - ci * bi, cr * bi + ci * br


def _disc_call(body, name, ins, out_widths):
    GP = ins[0].shape[0]
    tl = _pick(GP, (512, 256, 128, 64, 32, 16, 8))
    spec = lambda w: pl.BlockSpec((tl, w), lambda i: (i, 0))
    return pl.pallas_call(body, name=name, grid=(GP // tl,),
                          in_specs=[spec(a.shape[1]) for a in ins], out_specs=[spec(w) for w in out_widths],
                          out_shape=[S((GP, w), f32) for w in out_widths], compiler_params=_cparams(("parallel",)))(*ins)


def s5_disc(lr, li, logdt, br, bi, name):
    HG = br.shape[1]

    def body(lr_ref, li_ref, dt_ref, br_ref, bi_ref, ar_ref, ai_ref, bbr_ref, bbi_ref):
        ar, ai, bbr, bbi = _s5_disc_math(lr_ref[...], li_ref[...], dt_ref[...], br_ref[...], bi_ref[...])
        ar_ref[...], ai_ref[...], bbr_ref[...], bbi_ref[...] = ar, ai, bbr, bbi

    return _disc_call(body, name, [lr, li, logdt, br, bi], [1, 1, HG, HG])


def s5_disc_bwd(lr, li, logdt, br, bi, dar, dai, dbbr, dbbi, name):
    HG = br.shape[1]

    def body(lr_ref, li_ref, dt_ref, br_ref, bi_ref, dar_ref, dai_ref, dbbr_ref, dbbi_ref, *outs):
        _, vjp = jax.vjp(_s5_disc_math, lr_ref[...], li_ref[...], dt_ref[...], br_ref[...], bi_ref[...])
        for o, g in zip(outs, vjp((dar_ref[...], dai_ref[...], dbbr_ref[...], dbbi_ref[...]))):
            o[...] = g

    return _disc_call(body, name, [lr, li, logdt, br, bi, dar, dai, dbbr, dbbi], [1, 1, 1, HG, HG])


SCAN_TB = 256


def s5_scan_fwd(bur, bui, ar, ai, name):
    L, GP = bur.shape
    cw = _pick(GP, (1024, 512, 256, 128))
    tb = _pick(L, (SCAN_TB, 128, 64, 32, 16, 8))

    def body(bur_ref, bui_ref, ar_ref, ai_ref, xr_ref, xi_ref, cr_ref, ci_ref):
        @pl.when(pl.program_id(1) == 0)
        def _():
            cr_ref[...] = jnp.zeros_like(cr_ref)
            ci_ref[...] = jnp.zeros_like(ci_ref)

        a_r, a_i = ar_ref[...], ai_ref[...]

        def step(t, carry):
            xr, xi = carry
            row = pl.ds(t, 1)
            nr = a_r * xr - a_i * xi + bur_ref[row, :]
            ni = a_r * xi + a_i * xr + bui_ref[row, :]
            xr_ref[row, :] = nr
            xi_ref[row, :] = ni
            return nr, ni

        xr, xi = lax.fori_loop(0, tb, step, (cr_ref[...], ci_ref[...]), unroll=8)
        cr_ref[...] = xr
        ci_ref[...] = xi

    blk = pl.BlockSpec((tb, cw), lambda j, i: (i, j))
    vec = pl.BlockSpec((1, cw), lambda j, i: (0, j))
    return pl.pallas_call(
        body, name=name, grid=(GP // cw, L // tb),
        in_specs=[blk, blk, vec, vec], out_specs=[blk, blk],
        out_shape=[S((L, GP), f32), S((L, GP), f32)],
        scratch_shapes=[pltpu.VMEM((1, cw), f32), pltpu.VMEM((1, cw), f32)],
        compiler_params=_cparams(("parallel", "arbitrary")),
    )(bur, bui, ar, ai)


def s5_scan_bwd(gr, gi, xr, xi, ar, ai, name):
    L, GP = gr.shape
    cw = _pick(GP, (1024, 512, 256, 128))
    tb = _pick(L, (SCAN_TB, 128, 64, 32, 16, 8))
    nt = L // tb

    def body(gr_ref, gi_ref, xr_ref, xi_ref, ar_ref, ai_ref, gxr_ref, gxi_ref, dar_ref, dai_ref, cr_ref, ci_ref):
        @pl.when(pl.program_id(1) == 0)
        def _():
            cr_ref[...] = jnp.zeros_like(cr_ref)
            ci_ref[...] = jnp.zeros_like(ci_ref)
            dar_ref[...] = jnp.zeros_like(dar_ref)
            dai_ref[...] = jnp.zeros_like(dai_ref)

        a_r, a_i = ar_ref[...], ai_ref[...]

        def step(s, carry):
            cr, ci, dr, di = carry
            row = pl.ds(tb - 1 - s, 1)
            x_r, x_i = xr_ref[row, :], xi_ref[row, :]
            dr = dr + cr * x_r + ci * x_i
            di = di + ci * x_r - cr * x_i
            nr = gr_ref[row, :] + a_r * cr + a_i * ci
            ni = gi_ref[row, :] + a_r * ci - a_i * cr
            gxr_ref[row, :] = nr
            gxi_ref[row, :] = ni
            return nr, ni, dr, di

        cr, ci, dr, di = lax.fori_loop(0, tb, step, (cr_ref[...], ci_ref[...], dar_ref[...], dai_ref[...]), unroll=8)
        cr_ref[...] = cr
        ci_ref[...] = ci
        dar_ref[...] = dr
        dai_ref[...] = di

    blk = pl.BlockSpec((tb, cw), lambda j, i: (nt - 1 - i, j))
    vec = pl.BlockSpec((1, cw), lambda j, i: (0, j))
    return pl.pallas_call(
        body, name=name, grid=(GP // cw, nt),
        in_specs=[blk, blk, blk, blk, vec, vec], out_specs=[blk, blk, vec, vec],
        out_shape=[S((L, GP), f32), S((L, GP), f32), S((1, GP), f32), S((1, GP), f32)],
        scratch_shapes=[pltpu.VMEM((1, cw), f32), pltpu.VMEM((1, cw), f32)],
        compiler_params=_cparams(("parallel", "arbitrary")),
    )(gr, gi, xr, xi, ar, ai)


def _dn_heads_math(cq, ck, cv, braw, araw, alog, dtb, dk):
    q, k, v = _silu(cq), _silu(ck), _silu(cv)
    q = q * lax.rsqrt(jnp.sum(q * q, axis=-1, keepdims=True) + NORM_EPS) * (dk ** -0.5)
    k = k * lax.rsqrt(jnp.sum(k * k, axis=-1, keepdims=True) + NORM_EPS)
    beta = _sigmoid(braw)
    g = -jnp.exp(alog) * jax.nn.softplus(araw + dtb)
    return q, k, v, beta, g


def dn_prep(qkv, convw, ba, alog, dtb, H, name):
    L, W = qkv.shape
    hk = W // 3
    dk = hk // H

    def fn(i, n, xb, hb, wb, bab, alb, dtbb):
        c, _ = _causal_conv(xb, hb, wb, i == 0)
        qs, ks, vs, bs, gs = [], [], [], [], []
        for h in range(H):
            sl = lambda o: c[:, o + h * dk:o + (h + 1) * dk]
            q, k, v, beta, g = _dn_heads_math(sl(0), sl(hk), sl(2 * hk), bab[:, h:h + 1], bab[:, H + h:H + h + 1],
                                              alb[:, h:h + 1], dtbb[:, h:h + 1], dk)
            qs.append(q), ks.append(k), vs.append(v), bs.append(beta), gs.append(g)
        cat = lambda xs: jnp.concatenate(xs, axis=1)
        return cat(qs), cat(ks), cat(vs), _from_cols(bs + gs, 2 * H)

    return rowk(name, fn, L, _pick(L, (128, 64, 32, 16, 8)), 1,
                [(qkv, "rows"), (qkv, "prev"), (convw, "const"), (ba, "rows"), (alog, "const"), (dtb, "const")],
                [(hk, f32), (hk, f32), (hk, f32), (2 * H, f32)])


def dn_prep_bwd(qkv, convw, ba, alog, dtb, dq, dk_, dv, dbg, H, name):
    L, W = qkv.shape
    hk = W // 3
    dk = hk // H
    kw = convw.shape[0]
    nba = ba.shape[1]

    def fn(i, n, xb, hb, wb, bab, alb, dtbb, dqb, dkb, dvb, dbgb):
        c, shifted = _causal_conv(xb, hb, wb, i == 0)
        dcs = [None] * (3 * H)
        dbr, dar, dal, ddt = [], [], [], []
        for h in range(H):
            sl = lambda a, o: a[:, o + h * dk:o + (h + 1) * dk]
            args = (sl(c, 0), sl(c, hk), sl(c, 2 * hk), bab[:, h:h + 1], bab[:, H + h:H + h + 1],
                    alb[:, h:h + 1], dtbb[:, h:h + 1])
            _, vjp = jax.vjp(lambda *a: _dn_heads_math(*a, dk), *args)
            g = vjp((sl(dqb, 0), sl(dkb, 0), sl(dvb, 0), dbgb[:, h:h + 1], dbgb[:, H + h:H + h + 1]))
            dcs[h], dcs[H + h], dcs[2 * H + h] = g[0], g[1], g[2]
            dbr.append(g[3]), dar.append(g[4]), dal.append(g[5]), ddt.append(g[6])
        dc = jnp.concatenate(dcs, axis=1)
        dba = _from_cols(dbr + dar, nba)
        dw = _from_rows([jnp.sum(dc * shifted[j], axis=0, keepdims=True) for j in range(kw)])
        return dc, dba, dw, _from_cols(dal, H), _from_cols(ddt, H)

    return rowk(name, fn, L, _pick(L, (128, 64, 32, 16, 8)), 1,
                [(qkv, "rows"), (qkv, "prev"), (convw, "const"), (ba, "rows"), (alog, "const"), (dtb, "const"),
                 (dq, "rows"), (dk_, "rows"), (dv, "rows"), (dbg, "rows")],
                [(W, f32), (nba, bf16)], [(kw, W, f32), (1, H, f32), (1, H, f32)])


def conv_t(dc, w, name):
    L, C = dc.shape
    kw = w.shape[0]
    ncol = C // _pick(C, (1024, 768, 512, 256, 128))

    def fn(i, n, db, hb, wb):
        out = db * wb[kw - 1:kw]
        for j in range(kw - 1):
            out = out + _shift_up(db, hb, kw - 1 - j, i == n - 1) * wb[j:j + 1]
        return (out,)

    return rowk(name, fn, L, _pick(L, (256, 128, 64, 32, 16, 8)), ncol,
                [(dc, "rows"), (dc, "next"), (w, "cols")], [(C, bf16)])[0]


def _bdot(a, b, mode):
    return lax.dot_general(a.astype(bf16), b.astype(bf16), (_DIMS[mode], ((), ())), preferred_element_type=f32)


def _hdot(a, b, mode):
    return lax.dot_general(a, b, (_DIMS[mode], ((), ())), preferred_element_type=f32, precision=lax.Precision.HIGHEST)


def _make_dot(raw):
    @functools.partial(jax.custom_vjp, nondiff_argnums=(2,))
    def dot(a, b, mode):
        return raw(a, b, mode)

    def fwd(a, b, mode):
        return raw(a, b, mode), (a, b)

    def bwd(mode, res, ct):
        a, b = res
        if mode == "nn":
            return raw(ct, b, "nt"), raw(a, ct, "tn")
        if mode == "nt":
            return raw(ct, b, "nn"), raw(ct, a, "tn")
        return raw(b, ct, "nt"), raw(a, ct, "nn")

    dot.defvjp(fwd, bwd)
    return dot


_dot16 = _make_dot(_bdot)
_dot32 = _make_dot(_hdot)


@jax.custom_vjp
def _unit_lower_inv(lmat):
    c = lmat.shape[0]
    eye = (lax.broadcasted_iota(jnp.int32, (c, c), 0) == lax.broadcasted_iota(jnp.int32, (c, c), 1)).astype(f32)
    p = -lmat
    t = eye + p
    for _ in range(int(math.log2(c)) - 1):
        p = _hdot(p, p, "nn")
        t = t + _hdot(t, p, "nn")
    return t


def _uli_fwd(lmat):
    t = _unit_lower_inv(lmat)
    return t, t


def _uli_bwd(t, dt):
    return (-_hdot(_hdot(t, dt, "tn"), t, "nt"),)


_unit_lower_inv.defvjp(_uli_fwd, _uli_bwd)


def _dn_chunk_math(s_in, q, k, v, gcol, grow, bcol):
    c = q.shape[0]
    ri = lax.broadcasted_iota(jnp.int32, (c, c), 0)
    ci = lax.broadcasted_iota(jnp.int32, (c, c), 1)
    tril = (ri >= ci).astype(f32)
    strict = (ri > ci).astype(f32)
    gc_col = jnp.sum(tril * grow, axis=1, keepdims=True)
    gc_row = jnp.sum((1.0 - strict) * gcol, axis=0, keepdims=True)
    g_last = jnp.sum(gcol, axis=0, keepdims=True)
    decay = jnp.exp((gc_col - gc_row) * tril) * tril
    kb = k * bcol
    vb = v * bcol
    lmat = _dot16(kb, k, "nt") * decay * strict
    t = _unit_lower_inv(lmat)
    u = _dot32(t, vb, "nn")
    w = _dot32(t, kb * jnp.exp(gc_col), "nn")
    attn = _dot16(q, k, "nt") * decay
    v_new = u - _dot16(w, s_in, "nn")
    o = _dot16(q * jnp.exp(gc_col), s_in, "nn") + _dot16(attn, v_new, "nn")
    s_out = s_in * jnp.exp(g_last) + _dot16(k * jnp.exp(g_last - gc_col), v_new, "tn")
    return o, s_out


def dn_chunk_fwd(qn, kn, vv, bg, bgt, H, name):
    L, hk = qn.shape
    dk = hk // H
    c = DN_CHUNK
    nc = L // c

    def body(q_ref, k_ref, v_ref, bg_ref, bgt_ref, o_ref, ss_ref, s_ref):
        @pl.when(pl.program_id(0) == 0)
        def _():
            s_ref[...] = jnp.zeros_like(s_ref)

        bgb, bgtb = bg_ref[...], bgt_ref[0]
        for h in range(H):
            sl = slice(h * dk, (h + 1) * dk)
            s_in = s_ref[h]
            ss_ref[0, h] = s_in
            o, s_out = _dn_chunk_math(s_in, q_ref[:, sl], k_ref[:, sl], v_ref[:, sl],
                                      bgb[:, H + h:H + h + 1], bgtb[H + h:H + h + 1, :], bgb[:, h:h + 1])
            o_ref[:, sl] = o
            s_ref[h] = s_out

    row = lambda w: pl.BlockSpec((c, w), lambda n: (n, 0))
    return pl.pallas_call(
        body, name=name, grid=(nc,),
        in_specs=[row(hk), row(hk), row(hk), row(2 * H), pl.BlockSpec((1, 2 * H, c), lambda n: (n, 0, 0))],
        out_specs=[row(hk), pl.BlockSpec((1, H, dk, dk), lambda n: (n, 0, 0, 0))],
        out_shape=[S((L, hk), f32), S((nc, H, dk, dk), f32)],
        scratch_shapes=[pltpu.VMEM((H, dk, dk), f32)],
        compiler_params=_cparams(("arbitrary",)),
    )(qn, kn, vv, bg, bgt)


def dn_chunk_bwd(qn, kn, vv, bg, bgt, ss, do, H, name):
    L, hk = qn.shape
    dk = hk // H
    c = DN_CHUNK
    nc = L // c

    def body(q_ref, k_ref, v_ref, bg_ref, bgt_ref, ss_ref, do_ref, dq_ref, dk_ref, dv_ref, dbg_ref, dgt_ref, ds_ref):
        @pl.when(pl.program_id(0) == 0)
        def _():
            ds_ref[...] = jnp.zeros_like(ds_ref)

        bgb, bgtb = bg_ref[...], bgt_ref[0]
        dbs, dgs, dgr = [], [], []
        for h in range(H):
            sl = slice(h * dk, (h + 1) * dk)
            args = (ss_ref[0, h], q_ref[:, sl], k_ref[:, sl], v_ref[:, sl],
                    bgb[:, H + h:H + h + 1], bgtb[H + h:H + h + 1, :], bgb[:, h:h + 1])
            _, vjp = jax.vjp(_dn_chunk_math, *args)
            g = vjp((do_ref[:, sl], ds_ref[h]))
            ds_ref[h] = g[0]
            dq_ref[:, sl], dk_ref[:, sl], dv_ref[:, sl] = g[1], g[2], g[3]
            dgs.append(g[4]), dgr.append(g[5]), dbs.append(g[6])
        dbg_ref[...] = _from_cols(dbs + dgs, 2 * H)
        dgt_ref[0] = _from_rows(dgr)

    row = lambda w: pl.BlockSpec((c, w), lambda n: (nc - 1 - n, 0))
    return pl.pallas_call(
        body, name=name, grid=(nc,),
        in_specs=[row(hk), row(hk), row(hk), row(2 * H), pl.BlockSpec((1, 2 * H, c), lambda n: (nc - 1 - n, 0, 0)),
                  pl.BlockSpec((1, H, dk, dk), lambda n: (nc - 1 - n, 0, 0, 0)), row(hk)],
        out_specs=[row(hk), row(hk), row(hk), row(2 * H), pl.BlockSpec((1, H, c), lambda n: (nc - 1 - n, 0, 0))],
        out_shape=[S((L, hk), f32), S((L, hk), f32), S((L, hk), f32), S((L, 2 * H), f32), S((nc, H, c), f32)],
        scratch_shapes=[pltpu.VMEM((H, dk, dk), f32)],
        compiler_params=_cparams(("arbitrary",)),
    )(qn, kn, vv, bg, bgt, ss, do)


def _dn_gate_math(o, z, w):
    return _rms(o, w) * _silu(z)


def dn_gate(o, z, w, H, name):
    L, hv = o.shape
    dv = hv // H

    def fn(i, n, ob, zb, wb):
        return (jnp.concatenate([_dn_gate_math(ob[:, h * dv:(h + 1) * dv], zb[:, h * dv:(h + 1) * dv], wb)
                                 for h in range(H)], axis=1),)

    return rowk(name, fn, L, _pick(L, (256, 128, 64, 32, 16, 8)), 1, [(o, "rows"), (z, "rows"), (w, "const")], [(hv, bf16)])[0]


def dn_gate_bwd(o, z, w, dy, H, name):
    L, hv = o.shape
    dv = hv // H

    def fn(i, n, ob, zb, wb, dyb):
        dos, dzs, dw = [], [], 0.0
        for h in range(H):
            sl = slice(h * dv, (h + 1) * dv)
            _, vjp = jax.vjp(_dn_gate_math, ob[:, sl], zb[:, sl], wb)
            a, b, c = vjp(dyb[:, sl])
            dos.append(a), dzs.append(b)
            dw = dw + c
        return jnp.concatenate(dos, axis=1), jnp.concatenate(dzs, axis=1), dw

    return rowk(name, fn, L, _pick(L, (256, 128, 64, 32, 16, 8)), 1,
                [(o, "rows"), (z, "rows"), (w, "const"), (dy, "rows")], [(hv, f32), (hv, bf16)], [(1, dv, f32)])


def ffn_mid(ua, uv, wa, wv, name):
    L, F = ua.shape
    ncol = F // _pick(F, (512, 256, 128))

    def fn(i, n, ab, ah, vb, vh, wab, wvb):
        ca, _ = _causal_conv(ab, ah, wab, i == 0)
        cv, _ = _causal_conv(vb, vh, wvb, i == 0)
        return (_silu(ca) * cv,)

    return rowk(name, fn, L, _pick(L, (256, 128, 64, 32, 16, 8)), ncol,
                [(ua, "rows"), (ua, "prev"), (uv, "rows"), (uv, "prev"), (wa, "cols"), (wv, "cols")], [(F, bf16)])[0]


def ffn_mid_bwd(ua, uv, wa, wv, dh, name):
    L, F = ua.shape
    kw = wa.shape[0]
    ncol = F // _pick(F, (512, 256, 128))

    def fn(i, n, ab, ah, vb, vh, wab, wvb, dhb):
        ca, sa = _causal_conv(ab, ah, wab, i == 0)
        cv, sv = _causal_conv(vb, vh, wvb, i == 0)
        dca = dhb * cv * _dsilu(ca)
        dcv = dhb * _silu(ca)
        dwa = _from_rows([jnp.sum(dca * sa[j], axis=0, keepdims=True) for j in range(kw)])
        dwv = _from_rows([jnp.sum(dcv * sv[j], axis=0, keepdims=True) for j in range(kw)])
        return dca, dcv, dwa, dwv

    return rowk(name, fn, L, _pick(L, (256, 128, 64, 32, 16, 8)), ncol,
                [(ua, "rows"), (ua, "prev"), (uv, "rows"), (uv, "prev"), (wa, "cols"), (wv, "cols"), (dh, "rows")],
                [(F, f32), (F, f32)], [(kw, F, f32), (kw, F, f32)])


def _merge_epi(acc, gs, gd, ga, gb):
    return acc, _sigmoid(gs) * ga * _sigmoid(gb) + _sigmoid(gd) * acc


def merge_bwd(dm, gs, gd, ga, gb, brdn, name):
    L, D = dm.shape
    ncol = D // _pick(D, (512, 256, 128))

    def fn(i, n, dmb, gsb, gdb, gab, gbb, brb):
        ss, sd, sb = _sigmoid(gsb), _sigmoid(gdb), _sigmoid(gbb)
        br_s5 = gab * sb
        dbr_s5 = dmb * ss
        return (dmb * br_s5 * ss * (1.0 - ss), dmb * brb * sd * (1.0 - sd), dbr_s5 * sb,
                dbr_s5 * gab * sb * (1.0 - sb), dmb * sd)

    return rowk(name, fn, L, _pick(L, (256, 128, 64, 32, 16, 8)), ncol,
                [(a, "rows") for a in (dm, gs, gd, ga, gb, brdn)], [(D, bf16)] * 5)


def s5_out_bwd(dy, ypre, u, d, name):
    L, W = dy.shape

    def fn(i, n, dyb, yb, ub, db):
        dyp = dyb * _dgelu(yb)
        return dyp, db * dyp, jnp.sum(dyp * ub, axis=0, keepdims=True)

    return rowk(name, fn, L, _pick(L, (256, 128, 64, 32, 16, 8)), 1,
                [(dy, "rows"), (ypre, "rows"), (u, "rows"), (d, "const")], [(W, bf16), (W, f32)], [(1, W, f32)])


def _s5_y_epi(acc, y1, u, d):
    ypre = acc + y1 + d * u
    return ypre, _gelu(ypre)


def adamw(w, g, m, v, name):
    R, C = w.shape
    tl = _pick(R, (256, 128, 64, 32, 16, 8))
    if R * C * 4 <= 2 * 1024 * 1024:
        tl = R

    def body(w_ref, g_ref, m_ref, v_ref, d_ref, nm_ref, nv_ref):
        gg = g_ref[...]
        nm = ADAM_B1 * m_ref[...] + (1.0 - ADAM_B1) * gg
        nv = ADAM_B2 * v_ref[...] + (1.0 - ADAM_B2) * (gg * gg)
        m_hat = nm / (1.0 - ADAM_B1 ** ADAM_STEP)
        v_hat = nv / (1.0 - ADAM_B2 ** ADAM_STEP)
        d_ref[...] = -ADAM_LR * (m_hat / (jnp.sqrt(v_hat) + ADAM_EPS) + ADAM_WD * w_ref[...])
        nm_ref[...] = nm
        nv_ref[...] = nv

    blk = pl.BlockSpec((tl, C), lambda i: (i, 0))
    return pl.pallas_call(body, name=name, grid=(R // tl,), in_specs=[blk] * 4, out_specs=[blk] * 3,
                          out_shape=[S((R, C), f32)] * 3, compiler_params=_cparams(("parallel",)))(w, g, m, v)


def sum_slots(x, name, out_dtype=f32):
    n, R, C = x.shape
    tl = _pick(R, (512, 256, 128, 64, 32, 16, 8))

    def body(x_ref, o_ref):
        acc = x_ref[0].astype(f32)
        for s in range(1, n):
            acc = acc + x_ref[s].astype(f32)
        o_ref[...] = acc.astype(o_ref.dtype)

    return pl.pallas_call(body, name=name, grid=(R // tl,),
                          in_specs=[pl.BlockSpec((n, tl, C), lambda i: (0, i, 0))],
                          out_specs=pl.BlockSpec((tl, C), lambda i: (i, 0)),
                          out_shape=S((R, C), out_dtype), compiler_params=_cparams(("parallel",)))(x)


_ANY = pl.BlockSpec(memory_space=pl.ANY)


def _coords():
    return lax.axis_index("x"), lax.axis_index("y"), lax.axis_index("c")


def chip_exchange(src, name, same=False):
    out_shape = (N_CHIPS,) + src.shape if same else src.shape
    assert out_shape[0] == N_CHIPS

    def body(src_ref, out_ref, send_sems, recv_sems, local_sem):
        x, y, c = _coords()
        me = 2 * x + y
        slot = (lambda j: src_ref) if same else (lambda j: src_ref.at[j])
        mine = pltpu.make_async_copy(slot(me), out_ref.at[me], local_sem)
        mine.start()
        peers = [(1 - x, y), (x, 1 - y), (1 - x, 1 - y)]
        copies = []
        for k, (px, py) in enumerate(peers):
            cp = pltpu.make_async_remote_copy(
                src_ref=slot(2 * px + py), dst_ref=out_ref.at[me],
                send_sem=send_sems.at[k], recv_sem=recv_sems.at[k],
                device_id=(px, py, c), device_id_type=MESH)
            cp.start()
            copies.append(cp)
        for k, (px, py) in enumerate(peers):
            pltpu.make_async_remote_copy(
                src_ref=slot(me), dst_ref=out_ref.at[2 * px + py],
                send_sem=send_sems.at[k], recv_sem=recv_sems.at[k],
                device_id=(px, py, c), device_id_type=MESH).wait_recv()
        for cp in copies:
            cp.wait_send()
        mine.wait()

    return pl.pallas_call(
        body, name=name, in_specs=[_ANY], out_specs=_ANY, out_shape=S(out_shape, src.dtype),
        scratch_shapes=[pltpu.SemaphoreType.DMA((3,)), pltpu.SemaphoreType.DMA((3,)), pltpu.SemaphoreType.DMA],
    )(src)


def sibling_exchange(src, name):
    def body(src_ref, out_ref, send_sem, recv_sem):
        x, y, c = _coords()
        cp = pltpu.make_async_remote_copy(src_ref=src_ref, dst_ref=out_ref, send_sem=send_sem, recv_sem=recv_sem,
                                          device_id=(x, y, 1 - c), device_id_type=MESH)
        cp.start()
        cp.wait()

    return pl.pallas_call(
        body, name=name, in_specs=[_ANY], out_specs=_ANY, out_shape=S(src.shape, src.dtype),
        scratch_shapes=[pltpu.SemaphoreType.DMA, pltpu.SemaphoreType.DMA],
    )(src)


def _block_diag(blocks):
    G, a, b = blocks.shape
    eye = jnp.eye(G, dtype=blocks.dtype)
    return (blocks[:, :, None, :] * eye[:, None, :, None]).reshape(G * a, G * b)


def _block_diag_extract(mat, G):
    a, b = mat.shape[0] // G, mat.shape[1] // G
    m4 = mat.reshape(G, a, G, b)
    return jnp.sum(m4 * jnp.eye(G, dtype=mat.dtype)[:, None, :, None], axis=2)


def _flat_pack(arrs, dtype, lanes=1024, row_mult=16):
    flat = jnp.concatenate([a.reshape(-1).astype(dtype) for a in arrs])
    n = flat.shape[0]
    per = lanes * row_mult
    pad = (-n) % per
    if pad:
        flat = jnp.concatenate([flat, jnp.zeros((pad,), dtype)])
    return flat.reshape(-1, lanes)


def _flat_unpack(buf, shapes):
    flat = buf.reshape(-1)
    out, off = [], 0
    for s in shapes:
        n = math.prod(s)
        out.append(flat[off:off + n].reshape(s))
        off += n
    return out


def _layer_weights(p, H):
    w_in = p["w_in"]
    D = w_in.shape[0]
    s5w = p["s5_d"].shape[-1]
    hk = p["dn_proj_w"].shape[0]
    off_u, off_qkv = s5w, s5w + 3 * hk
    off_z = off_qkv + hk
    off_b, off_a = off_z + H, off_z + 2 * H
    off_gs = off_a + D
    q = {}
    q["Wu"] = w_in[:, :off_u]
    q["Wqkv"] = w_in[:, off_u:off_qkv]
    q["Wz"] = w_in[:, off_qkv:off_z]
    q["Wba"] = jnp.pad(w_in[:, off_z:off_a], ((0, 0), (0, 128 - 2 * H)))
    q["Wgs"] = w_in[:, off_a:off_gs]
    q["Wgd"] = w_in[:, off_gs:]
    glu = p["s5_glu_w"]
    q["Wga"], q["Wgb"] = glu[:, :D], glu[:, D:]
    F = p["ffn_down"].shape[0]
    q["Wupa"], q["Wupv"] = p["ffn_up"][:, :F], p["ffn_up"][:, F:]
    q["cwa"], q["cwv"] = p["ffn_conv_w"][:, :F], p["ffn_conv_w"][:, F:]
    for k in ("dn_proj_w", "w_out", "ffn_down", "dn_conv_w", "mix_norm_w", "ffn_norm_w", "dn_norm_w",
              "dn_a_log", "dn_dt_bias", "s5_d"):
        q[k] = p[k]
    return q


def _s5_params(p, tag):
    G, P = p["s5_a_re"].shape
    HG = p["s5_b_re"].shape[-1]
    col = lambda a: a.reshape(G * P, 1)
    lr, li = col(p["s5_a_re"]), col(p["s5_a_im"])
    logdt = col(jnp.broadcast_to(p["s5_log_dt"][:, None], (G, P)))
    br, bi = p["s5_b_re"].reshape(G * P, HG), p["s5_b_im"].reshape(G * P, HG)
    ar, ai, bbr, bbi = s5_disc(lr, li, logdt, br, bi, "s5_disc")
    bd = lambda m: _block_diag(m.reshape(G, P, HG).transpose(0, 2, 1)).astype(bf16)
    cd = lambda m: _block_diag(m.transpose(0, 2, 1)).astype(bf16)
    return dict(lr=lr, li=li, logdt=logdt, br=br, bi=bi, ar=ar.reshape(1, G * P), ai=ai.reshape(1, G * P),
                Bre=bd(bbr), Bim=bd(bbi), CreT=cd(p["s5_c_re"]), mCimT=cd(-p["s5_c_im"]), G=G, P=P, HG=HG)


def layer_fwd(x, q, s5, H):
    r = {"x": x}
    h1 = rms_fwd(x, q["mix_norm_w"], "rms_mix")
    r["h1"] = h1
    u32, u16 = mm(h1, q["Wu"], "nn", "proj_u", out_dtypes=(f32, bf16), epi=lambda a: (a, a))
    qkv = mm(h1, q["Wqkv"], "nn", "proj_qkv")
    z = mm(h1, q["Wz"], "nn", "proj_z")
    ba = mm(h1, q["Wba"], "nn", "proj_ba")
    gs = mm(h1, q["Wgs"], "nn", "proj_gs")
    gd = mm(h1, q["Wgd"], "nn", "proj_gd")
    r.update(u32=u32, u16=u16, qkv=qkv, z=z, ba=ba, gs=gs, gd=gd)
    bur = mm(u16, s5["Bre"], "nn", "s5_bu_re")
    bui = mm(u16, s5["Bim"], "nn", "s5_bu_im")
    xr, xi = s5_scan_fwd(bur, bui, s5["ar"], s5["ai"], "s5_scan_fwd")
    y1 = mm(xr, s5["CreT"], "nn", "s5_y_re")
    ypre, ys5 = mm(xi, s5["mCimT"], "nn", "s5_y_im", extras=(y1, u32, q["s5_d"]), epi=_s5_y_epi, out_dtypes=(f32, bf16))
    ga = mm(ys5, q["Wga"], "nn", "glu_a")
    gb = mm(ys5, q["Wgb"], "nn", "glu_b")
    r.update(xr=xr, xi=xi, ypre=ypre, ys5=ys5, ga=ga, gb=gb)
    qn, kn, vv, bg = dn_prep(qkv, q["dn_conv_w"], ba, q["dn_a_log"], q["dn_dt_bias"], H, "dn_prep")
    L = x.shape[0]
    bgt = bg.reshape(L // DN_CHUNK, DN_CHUNK, 2 * H).transpose(0, 2, 1)
    o, ss = dn_chunk_fwd(qn, kn, vv, bg, bgt, H, "dn_chunk_fwd")
    ydn = dn_gate(o, z, q["dn_norm_w"], H, "dn_gate")
    brdn, merged = mm(ydn, q["dn_proj_w"], "nn", "dn_proj", extras=(gs, gd, ga, gb), epi=_merge_epi, out_dtypes=(f32, bf16))
    r.update(qn=qn, kn=kn, vv=vv, bg=bg, bgt=bgt, ss=ss, o=o, ydn=ydn, brdn=brdn, merged=merged)
    x1 = mm(merged, q["w_out"], "nn", "out_proj", extras=(x,), epi=_add)
    h2 = rms_fwd(x1, q["ffn_norm_w"], "rms_ffn")
    ua = mm(h2, q["Wupa"], "nn", "ffn_up_a")
    uv = mm(h2, q["Wupv"], "nn", "ffn_up_v")
    hmid = ffn_mid(ua, uv, q["cwa"], q["cwv"], "ffn_mid")
    x2 = mm(hmid, q["ffn_down"], "nn", "ffn_down", extras=(x1,), epi=_add)
    r.update(x1=x1, h2=h2, ua=ua, uv=uv, hmid=hmid)
    return x2, r


def layer_bwd(dx2, dx2b, r, q, s5, H):
    g = {}
    dhmid = mm(dx2b, q["ffn_down"], "nt", "d_hmid")
    g["ffn_down"] = mm(r["hmid"], dx2b, "tn", "dw_ffn_down")
    dca, dcv, dwa, dwv = ffn_mid_bwd(r["ua"], r["uv"], q["cwa"], q["cwv"], dhmid, "ffn_mid_bwd")
    g["ffn_conv_w"] = jnp.concatenate([dwa, dwv], axis=1)
    dua = conv_t(dca, q["cwa"], "ffn_conv_t_a")
    duv = conv_t(dcv, q["cwv"], "ffn_conv_t_v")
    dh2 = mm(dua, q["Wupa"], "nt", "d_h2_a")
    dh2 = mm(duv, q["Wupv"], "nt", "d_h2_v", extras=(dh2,), epi=_add)
    g["ffn_up"] = jnp.concatenate([mm(r["h2"], dua, "tn", "dw_up_a"), mm(r["h2"], duv, "tn", "dw_up_v")], axis=1)
    dx1, dx1b, dffn_w = rms_bwd(r["x1"], q["ffn_norm_w"], dh2, dx2, "rms_ffn_bwd")
    g["ffn_norm_w"] = dffn_w[0]
    dm = mm(dx1b, q["w_out"], "nt", "d_merged")
    g["w_out"] = mm(r["merged"], dx1b, "tn", "dw_out")
    dgs, dgd, dga, dgb, dbrdn = merge_bwd(dm, r["gs"], r["gd"], r["ga"], r["gb"], r["brdn"], "merge_bwd")
    dydn = mm(dbrdn, q["dn_proj_w"], "nt", "d_ydn")
    g["dn_proj_w"] = mm(r["ydn"], dbrdn, "tn", "dw_dn_proj")
    do, dz, dnw = dn_gate_bwd(r["o"], r["z"], q["dn_norm_w"], dydn, H, "dn_gate_bwd")
    g["dn_norm_w"] = dnw[0]
    dq, dk, dv, dbg, dgt = dn_chunk_bwd(r["qn"], r["kn"], r["vv"], r["bg"], r["bgt"], r["ss"], do, H, "dn_chunk_bwd")
    L = dq.shape[0]
    dbg = dbg + jnp.concatenate([jnp.zeros((L, H), f32), dgt.transpose(0, 2, 1).reshape(L, H)], axis=1)
    dc, dba, dcw, dal, ddt = dn_prep_bwd(r["qkv"], q["dn_conv_w"], r["ba"], q["dn_a_log"], q["dn_dt_bias"],
                                         dq, dk, dv, dbg, H, "dn_prep_bwd")
    g["dn_conv_w"], g["dn_a_log"], g["dn_dt_bias"] = dcw, dal[0], ddt[0]
    dqkv = conv_t(dc, q["dn_conv_w"], "dn_conv_t")
    dys5 = mm(dga, q["Wga"], "nt", "d_ys5_a")
    dys5 = mm(dgb, q["Wgb"], "nt", "d_ys5_b", extras=(dys5,), epi=_add)
    g["s5_glu_w"] = jnp.concatenate([mm(r["ys5"], dga, "tn", "dw_glu_a"), mm(r["ys5"], dgb, "tn", "dw_glu_b")], axis=1)
    dyp, du_direct, dd = s5_out_bwd(dys5, r["ypre"], r["u32"], q["s5_d"], "s5_out_bwd")
    g["s5_d"] = dd[0]
    gdr = mm(dyp, s5["CreT"], "nt", "s5_gx_re")
    gdi = mm(dyp, s5["mCimT"], "nt", "s5_gx_im")
    G, P, HG = s5["G"], s5["P"], s5["HG"]
    dcre = _block_diag_extract(mm(r["xr"], dyp, "tn", "dw_s5_c_re"), G)
    dcim = _block_diag_extract(mm(r["xi"], dyp, "tn", "dw_s5_c_im"), G)
    g["s5_c_re"], g["s5_c_im"] = dcre.transpose(0, 2, 1), -dcim.transpose(0, 2, 1)
    gxr, gxi, dar, dai = s5_scan_bwd(gdr, gdi, r["xr"], r["xi"], s5["ar"], s5["ai"], "s5_scan_bwd")
    dbre = _block_diag_extract(mm(r["u16"], gxr, "tn", "dw_s5_b_re"), G)
    dbim = _block_diag_extract(mm(r["u16"], gxi, "tn", "dw_s5_b_im"), G)
    tocol = lambda m: m.transpose(0, 2, 1).reshape(G * P, HG)
    dlr, dli, dlogdt, dbr, dbi = s5_disc_bwd(s5["lr"], s5["li"], s5["logdt"], s5["br"], s5["bi"],
                                             dar.reshape(G * P, 1), dai.reshape(G * P, 1), tocol(dbre), tocol(dbim), "s5_disc_bwd")
    g["s5_a_re"], g["s5_a_im"] = dlr.reshape(G, P), dli.reshape(G, P)
    g["s5_log_dt"] = jnp.sum(dlogdt.reshape(G, P), axis=1)
    g["s5_b_re"], g["s5_b_im"] = dbr.reshape(G, P, HG), dbi.reshape(G, P, HG)
    du = mm(gxr, s5["Bre"], "nt", "d_u_re")
    du = mm(gxi, s5["Bim"], "nt", "d_u_im", extras=(du, du_direct), epi=lambda a, b, c: (a + b + c,), out_dtypes=(bf16,))
    h1 = r["h1"]
    segs = [("Wu", du), ("Wqkv", dqkv), ("Wz", dz), ("Wba", dba), ("Wgs", dgs), ("Wgd", dgd)]
    dh1 = None
    dws = []
    for name, dseg in segs:
        if dh1 is None:
            dh1 = mm(dseg, q[name], "nt", "d_h1_" + name)
        else:
            dh1 = mm(dseg, q[name], "nt", "d_h1_" + name, extras=(dh1,), epi=_add)
        dw = mm(h1, dseg, "tn", "dw_in_" + name)
        dws.append(dw[:, :2 * H] if name == "Wba" else dw)
    g["w_in"] = jnp.concatenate(dws, axis=1)
    dx, dxb, dmix = rms_bwd(r["x"], q["mix_norm_w"], dh1, dx1, "rms_mix_bwd")
    g["mix_norm_w"] = dmix[0]
    return dx, dxb, g


BIG = ("w_in", "s5_glu_w", "dn_proj_w", "w_out", "ffn_up", "ffn_down")
SHARDED_SMALL = ("dn_conv_w", "ffn_conv_w")
COL_SHARDED = ("w_in", "s5_glu_w", "dn_proj_w", "ffn_up", "dn_conv_w", "ffn_conv_w")
REPL = ("mix_norm_w", "s5_log_dt", "s5_a_re", "s5_a_im", "s5_b_re", "s5_b_im", "s5_c_re", "s5_c_im", "s5_d",
        "dn_a_log", "dn_dt_bias", "dn_norm_w", "ffn_norm_w")
WEIGHTS = ['mix_norm_w', 'w_in', 's5_log_dt', 's5_a_re', 's5_a_im', 's5_b_re', 's5_b_im', 's5_c_re', 's5_c_im', 's5_d',
           's5_glu_w', 'dn_conv_w', 'dn_a_log', 'dn_dt_bias', 'dn_norm_w', 'dn_proj_w', 'w_out', 'ffn_norm_w', 'ffn_up',
           'ffn_conv_w', 'ffn_down', 'final_norm_w']


def _join_shards(name, shards):
    return jnp.concatenate(shards, axis=-1 if name in COL_SHARDED else -2)


def _split_shards(name, full):
    return jnp.split(full, N_CHIPS, axis=-1 if name in COL_SHARDED else -2)


def gather_layer_weights(shards):
    names = BIG
    shapes = [shards[n].shape for n in names]
    pack = _flat_pack([shards[n] for n in names], bf16, row_mult=32)
    R = pack.shape[0]
    c = lax.axis_index("c")
    half = lax.dynamic_slice_in_dim(pack, c * (R // 2), R // 2, axis=0)
    got = chip_exchange(half, "ag_chips", same=True)
    other = sibling_exchange(got, "ag_sibling")
    lo = jnp.where(c == 0, got, other)
    hi = jnp.where(c == 0, other, got)
    full = jnp.concatenate([lo, hi], axis=1)
    per_chip = [_flat_unpack(full[j], shapes) for j in range(N_CHIPS)]
    return {n: _join_shards(n, [per_chip[j][k] for j in range(N_CHIPS)]) for k, n in enumerate(names)}


def gather_small_sharded(shards):
    names = SHARDED_SMALL
    shapes = [shards[n].shape for n in names]
    got = chip_exchange(_flat_pack([shards[n] for n in names], f32, lanes=128, row_mult=8), "ag_small", same=True)
    per_chip = [_flat_unpack(got[j], shapes) for j in range(N_CHIPS)]
    return {n: _join_shards(n, [per_chip[j][k] for j in range(N_CHIPS)]) for k, n in enumerate(names)}


def reduce_scatter_layer_grads(g):
    names = BIG + SHARDED_SMALL
    split = {n: _split_shards(n, g[n]) for n in names}
    shapes = [split[n][0].shape for n in names]
    pack = jnp.stack([_flat_pack([split[n][j] for n in names], f32, row_mult=32) for j in range(N_CHIPS)])
    R = pack.shape[1]
    c = lax.axis_index("c")
    mine = lax.dynamic_slice_in_dim(pack, c * (R // 2), R // 2, axis=1)
    theirs = lax.dynamic_slice_in_dim(pack, (1 - c) * (R // 2), R // 2, axis=1)
    got = sibling_exchange(theirs.reshape(N_CHIPS * (R // 2), -1), "rs_sibling").reshape(mine.shape)
    pair = jnp.stack([jnp.where(c == 0, mine, got), jnp.where(c == 0, got, mine)])
    chip_sum = sum_slots(pair.reshape(2, N_CHIPS * (R // 2), -1), "rs_pair_sum", bf16).reshape(mine.shape)
    parts = chip_exchange(chip_sum, "rs_chips")
    half = sum_slots(parts, "rs_chip_sum")
    other = sibling_exchange(half, "rs_share")
    full = jnp.concatenate([jnp.where(c == 0, half, other), jnp.where(c == 0, other, half)], axis=0)
    return dict(zip(names, _flat_unpack(full, shapes)))


def all_reduce_small(arrs):
    shapes = [a.shape for a in arrs]
    pack = _flat_pack(arrs, f32, lanes=128, row_mult=8)
    from_chips = chip_exchange(pack, "ar_chips", same=True)
    from_sib = sibling_exchange(from_chips, "ar_sibling")
    c = lax.axis_index("c")
    both = jnp.concatenate([jnp.where(c == 0, from_chips, from_sib), jnp.where(c == 0, from_sib, from_chips)], axis=0)
    return _flat_unpack(sum_slots(both, "ar_sum"), shapes)


def kernel(x, mix_norm_w, w_in, s5_log_dt, s5_a_re, s5_a_im, s5_b_re, s5_b_im, s5_c_re, s5_c_im, s5_d, s5_glu_w, dn_conv_w, dn_a_log, dn_dt_bias, dn_norm_w, dn_proj_w, w_out, ffn_norm_w, ffn_up, ffn_conv_w, ffn_down, final_norm_w, loss_target, m_mix_norm_w, m_w_in, m_s5_log_dt, m_s5_a_re, m_s5_a_im, m_s5_b_re, m_s5_b_im, m_s5_c_re, m_s5_c_im, m_s5_d, m_s5_glu_w, m_dn_conv_w, m_dn_a_log, m_dn_dt_bias, m_dn_norm_w, m_dn_proj_w, m_w_out, m_ffn_norm_w, m_ffn_up, m_ffn_conv_w, m_ffn_down, m_final_norm_w, v_mix_norm_w, v_w_in, v_s5_log_dt, v_s5_a_re, v_s5_a_im, v_s5_b_re, v_s5_b_im, v_s5_c_re, v_s5_c_im, v_s5_d, v_s5_glu_w, v_dn_conv_w, v_dn_a_log, v_dn_dt_bias, v_dn_norm_w, v_dn_proj_w, v_w_out, v_ffn_norm_w, v_ffn_up, v_ffn_conv_w, v_ffn_down, v_final_norm_w):
    args = locals()
    W = {n: args[n] for n in WEIGHTS}
    M = {n: args["m_" + n] for n in WEIGHTS}
    V = {n: args["v_" + n] for n in WEIGHTS}
    depth = mix_norm_w.shape[0]
    H = dn_a_log.shape[1]
    xs = x[0]
    target = loss_target[0]

    layers = []
    conv_full = gather_small_sharded({n: W[n] for n in SHARDED_SMALL})
    for l in range(depth):
        p = gather_layer_weights({n: W[n][l] for n in BIG})
        for n in REPL:
            p[n] = W[n][l]
        for n in SHARDED_SMALL:
            p[n] = conv_full[n][l]
        for n in ("mix_norm_w", "ffn_norm_w", "dn_norm_w", "dn_a_log", "dn_dt_bias", "s5_d"):
            p[n] = p[n].reshape(1, -1)
        layers.append((_layer_weights(p, H), _s5_params(p, l)))
    res = []
    for l in range(depth):
        xs, r = layer_fwd(xs, layers[l][0], layers[l][1], H)
        res.append(r)
    dx, dxb, loss_part, dfinal = loss_head(xs, final_norm_w.reshape(1, -1), target, "loss_head")
    loss = lax.psum(loss_part[0, 0], ("x", "y", "c"))

    grads = [None] * depth
    for l in reversed(range(depth)):
        dx, dxb, grads[l] = layer_bwd(dx, dxb, res[l], layers[l][0], layers[l][1], H)
    grad_x = dx[None]

    G = {}
    sharded = [reduce_scatter_layer_grads(grads[l]) for l in range(depth)]
    for n in BIG + SHARDED_SMALL:
        G[n] = jnp.stack([sharded[l][n] for l in range(depth)])
    small = [jnp.stack([grads[l][n] for l in range(depth)]) for n in REPL] + [dfinal[0]]
    for n, a in zip(REPL + ("final_norm_w",), all_reduce_small(small)):
        G[n] = a

    delta, new_m, new_v = {}, {}, {}
    for n in WEIGHTS:
        shape = W[n].shape
        size = math.prod(shape)
        if n in BIG + SHARDED_SMALL:
            two_d = (size // shape[-1], shape[-1])
        else:
            two_d = (size // 128, 128) if size % 128 == 0 else (1, size)
        d, nm, nv = adamw(W[n].reshape(two_d), G[n].reshape(two_d), M[n].reshape(two_d), V[n].reshape(two_d), "adamw_" + n)
        delta[n], new_m[n], new_v[n] = d.reshape(shape), nm.reshape(shape), nv.reshape(shape)
        G[n] = G[n].reshape(shape)
    return (loss, grad_x, *[G[n] for n in WEIGHTS], *[delta[n] for n in WEIGHTS],
            *[new_m[n] for n in WEIGHTS], *[new_v[n] for n in WEIGHTS])
```

```python
import functools
import math

import jax
import jax.numpy as jnp
from jax import lax
from jax.experimental import pallas as pl
from jax.experimental.pallas import tpu as pltpu

f32 = jnp.float32
bf16 = jnp.bfloat16
S = jax.ShapeDtypeStruct

NORM_EPS = 1e-6
DN_CHUNK = 64
S5_GROUP = 16
ADAM_LR, ADAM_B1, ADAM_B2, ADAM_EPS, ADAM_WD, ADAM_STEP = 0.001, 0.9, 0.999, 1e-08, 0.01, 10
VMEM_LIMIT_BYTES = 56 * 1024 * 1024
HALO = 8
MESH = pl.DeviceIdType.MESH
N_CHIPS = 4


def _pick(n, cands):
    for c in cands:
        if n % c == 0:
            return c
    return n


MM_VMEM_BUDGET = 40 * 1024 * 1024
MM_MAX_TK = 2816


def _pick_k(K):
    if K <= MM_MAX_TK or K % 128:
        return K
    return max(d for d in range(128, MM_MAX_TK + 1, 128) if K % d == 0)


def _cparams(sem):
    return pltpu.CompilerParams(dimension_semantics=sem, vmem_limit_bytes=VMEM_LIMIT_BYTES)


_DIMS = {"nn": ((1,), (0,)), "nt": ((1,), (1,)), "tn": ((0,), (0,))}


def mm(a, b, mode, name, extras=(), epi=None, out_dtypes=(f32,), tm=None, tn=None, tk=None):
    if mode == "tn":
        K, M = a.shape
    else:
        M, K = a.shape
    if mode == "nt":
        N, K2 = b.shape
    else:
        K2, N = b.shape
    assert K == K2, (a.shape, b.shape, mode)
    tm = tm or _pick(M, (1024, 512, 256, 128, 64, 32, 16, 8))
    tn = tn or _pick(N, (1024, 512, 256, 128))
    tk = tk or _pick_k(K)

    def vmem_estimate(tm_):
        tiles = tm_ * tk * a.dtype.itemsize + tk * tn * b.dtype.itemsize
        tiles += sum(tm_ * tn * e.dtype.itemsize for e in extras if e.shape == (M, N))
        tiles += sum(tm_ * tn * jnp.dtype(dt).itemsize for dt in out_dtypes)
        return 2 * tiles + 3 * tm_ * tn * 4

    while vmem_estimate(tm) > MM_VMEM_BUDGET and tm % 16 == 0 and tm > 128:
        tm //= 2
    nk = K // tk
    assert M % tm == 0 and N % tn == 0 and K % tk == 0
    if mode == "tn":
        a_spec = pl.BlockSpec((tk, tm), lambda j, i, k: (k, i))
    else:
        a_spec = pl.BlockSpec((tm, tk), lambda j, i, k: (i, k))
    if mode == "nt":
        b_spec = pl.BlockSpec((tn, tk), lambda j, i, k: (j, k))
    else:
        b_spec = pl.BlockSpec((tk, tn), lambda j, i, k: (k, j))
    ex_specs = []
    for e in extras:
        if e.shape == (M, N):
            ex_specs.append(pl.BlockSpec((tm, tn), lambda j, i, k: (i, j)))
        elif e.shape == (1, N):
            ex_specs.append(pl.BlockSpec((1, tn), lambda j, i, k: (0, j)))
        elif e.shape == (M, 1):
            ex_specs.append(pl.BlockSpec((tm, 1), lambda j, i, k: (i, 0)))
        else:
            raise ValueError((e.shape, M, N))
    ne, no = len(extras), len(out_dtypes)
    dims = (_DIMS[mode], ((), ()))

    def body(a_ref, b_ref, *rest):
        ex, outs = rest[:ne], rest[ne:ne + no]
        p = lax.dot_general(a_ref[...].astype(bf16), b_ref[...].astype(bf16), dims, preferred_element_type=f32)

        def finish(acc):
            res = epi(acc, *[e[...] for e in ex]) if epi is not None else (acc,)
            for o, r in zip(outs, res):
                o[...] = r.astype(o.dtype)

        if nk == 1:
            finish(p)
        else:
            acc_ref = rest[-1]
            k = pl.program_id(2)

            @pl.when(k == 0)
            def _():
                acc_ref[...] = p

            @pl.when(k > 0)
            def _():
                acc_ref[...] += p

            @pl.when(k == nk - 1)
            def _():
                finish(acc_ref[...])

    outs = pl.pallas_call(
        body,
        name=name,
        grid=(N // tn, M // tm, nk),
        in_specs=[a_spec, b_spec] + ex_specs,
        out_specs=[pl.BlockSpec((tm, tn), lambda j, i, k: (i, j)) for _ in out_dtypes],
        out_shape=[S((M, N), dt) for dt in out_dtypes],
        scratch_shapes=[pltpu.VMEM((tm, tn), f32)] if nk > 1 else [],
        compiler_params=_cparams(("parallel", "parallel", "arbitrary")),
    )(a, b, *extras)
    return outs[0] if no == 1 else tuple(outs)


def _add(acc, prev):
    return (acc + prev,)


def rowk(name, fn, L, tl, ncol, ins, outs, accs=()):
    nrow = L // tl
    assert L % tl == 0 and tl % HALO == 0
    hb = tl // HALO

    def cw_of(c_total):
        assert c_total % ncol == 0, (name, c_total, ncol)
        return c_total // ncol

    in_specs = []
    for arr, kind in ins:
        if kind == "rows":
            in_specs.append(pl.BlockSpec((tl, cw_of(arr.shape[1])), lambda j, i: (i, j)))
        elif kind == "prev":
            in_specs.append(pl.BlockSpec((HALO, cw_of(arr.shape[1])), lambda j, i: (jnp.maximum(i * hb - 1, 0), j)))
        elif kind == "next":
            in_specs.append(pl.BlockSpec((HALO, cw_of(arr.shape[1])), lambda j, i: (jnp.minimum((i + 1) * hb, nrow * hb - 1), j)))
        elif kind == "cols":
            in_specs.append(pl.BlockSpec((arr.shape[0], cw_of(arr.shape[1])), lambda j, i: (0, j)))
        elif kind == "const":
            in_specs.append(pl.BlockSpec(arr.shape, lambda j, i: (0,) * arr.ndim))
        else:
            raise ValueError(kind)
    out_specs = [pl.BlockSpec((tl, cw_of(c)), lambda j, i: (i, j)) for c, _ in outs]
    out_shape = [S((L, c), dt) for c, dt in outs]
    out_specs += [pl.BlockSpec((r, cw_of(c)), lambda j, i: (0, j)) for r, c, _ in accs]
    out_shape += [S((r, c), dt) for r, c, dt in accs]
    ni, no, na = len(ins), len(outs), len(accs)

    def body(*refs):
        i = pl.program_id(1)
        res = fn(i, nrow, *[r[...] for r in refs[:ni]])
        for o, r in zip(refs[ni:ni + no], res[:no]):
            o[...] = r.astype(o.dtype)
        for o, r in zip(refs[ni + no:ni + no + na], res[no:]):
            @pl.when(i == 0)
            def _(o=o, r=r):
                o[...] = r.astype(o.dtype)

            @pl.when(i > 0)
            def _(o=o, r=r):
                o[...] += r.astype(o.dtype)

    res = pl.pallas_call(
        body,
        name=name,
        grid=(ncol, nrow),
        in_specs=in_specs,
        out_specs=out_specs,
        out_shape=out_shape,
        compiler_params=_cparams(("parallel", "arbitrary")),
    )(*[a for a, _ in ins])
    return tuple(res)


def _sigmoid(x):
    return 1.0 / (1.0 + jnp.exp(-x))


def _silu(x):
    return x * _sigmoid(x)


def _dsilu(x):
    s = _sigmoid(x)
    return s * (1.0 + x * (1.0 - s))


def _erf(x):
    a = jnp.abs(x)
    t = 1.0 / (1.0 + 0.3275911 * a)
    poly = t * (0.254829592 + t * (-0.284496736 + t * (1.421413741 + t * (-1.453152027 + t * 1.061405429))))
    y = 1.0 - poly * jnp.exp(-a * a)
    return jnp.where(x < 0, -y, y)


def _gelu(x):
    return 0.5 * x * (1.0 + _erf(x * (2.0 ** -0.5)))


def _dgelu(x):
    cdf = 0.5 * (1.0 + _erf(x * (2.0 ** -0.5)))
    pdf = jnp.exp(-0.5 * x * x) * (1.0 / math.sqrt(2.0 * math.pi))
    return cdf + x * pdf


def _rms(x, w):
    return x * lax.rsqrt(jnp.mean(x * x, axis=-1, keepdims=True) + NORM_EPS) * w


def _rms_bwd(x, w, dy):
    d = x.shape[-1]
    r = lax.rsqrt(jnp.mean(x * x, axis=-1, keepdims=True) + NORM_EPS)
    wdy = w * dy
    dx = r * wdy - x * (r * r * r) * (jnp.sum(x * wdy, axis=-1, keepdims=True) / d)
    dw = jnp.sum(x * r * dy, axis=0, keepdims=True)
    return dx, dw


def _from_cols(cols, width):
    tl = cols[0].shape[0]
    lane = lax.broadcasted_iota(jnp.int32, (tl, width), 1)
    out = jnp.zeros((tl, width), f32)
    for n, col in enumerate(cols):
        out = jnp.where(lane == n, col, out)
    return out


def _from_rows(rows):
    c = rows[0].shape[1]
    sub = lax.broadcasted_iota(jnp.int32, (len(rows), c), 0)
    out = jnp.zeros((len(rows), c), f32)
    for n, row in enumerate(rows):
        out = jnp.where(sub == n, row, out)
    return out


def _shift_down(x, halo, s, first):
    if s == 0:
        return x
    tl = x.shape[0]
    halo = jnp.where(first, 0.0, halo)
    xx = jnp.concatenate([halo, x], axis=0)
    return pltpu.roll(xx, s, 0)[HALO:HALO + tl]


def _shift_up(x, halo, s, last):
    if s == 0:
        return x
    tl = x.shape[0]
    halo = jnp.where(last, 0.0, halo)
    xx = jnp.concatenate([x, halo], axis=0)
    return pltpu.roll(xx, tl + HALO - s, 0)[0:tl]


def _causal_conv(x, halo, w, first):
    kw = w.shape[0]
    shifted = [_shift_down(x, halo, kw - 1 - j, first) for j in range(kw)]
    out = shifted[0] * w[0:1]
    for j in range(1, kw):
        out = out + shifted[j] * w[j:j + 1]
    return out, shifted


def rms_fwd(x, w, name):
    L, D = x.shape

    def fn(i, n, xb, wb):
        return (_rms(xb, wb),)

    return rowk(name, fn, L, _pick(L, (256, 128, 64, 32, 16, 8)), 1, [(x, "rows"), (w, "const")], [(D, bf16)])[0]


def rms_bwd(x, w, dh, dres, name):
    L, D = x.shape

    def fn(i, n, xb, wb, dhb, drb):
        dx, dw = _rms_bwd(xb, wb, dhb)
        dx = dx + drb
        return dx, dx, dw

    return rowk(name, fn, L, _pick(L, (256, 128, 64, 32, 16, 8)), 1,
                [(x, "rows"), (w, "const"), (dh, "rows"), (dres, "rows")], [(D, f32), (D, bf16)], [(1, D, f32)])


def loss_head(x, w, target, name):
    L, D = x.shape

    def fn(i, n, xb, wb, tb):
        err = _rms(xb, wb) - tb
        loss = 0.5 * jnp.sum(err * err) / D
        dx, dw = _rms_bwd(xb, wb, err / D)
        return dx, dx, jnp.full((8, 128), loss, f32), dw

    return rowk(name, fn, L, _pick(L, (256, 128, 64, 32, 16, 8)), 1,
                [(x, "rows"), (w, "const"), (target, "rows")], [(D, f32), (D, bf16)], [(8, 128, f32), (1, D, f32)])


def _s5_disc_math(lr, li, logdt, br, bi):
    dt = jnp.exp(logdt)
    mag = jnp.exp(lr * dt)
    ar, ai = mag * jnp.cos(li * dt), mag * jnp.sin(li * dt)
    den = lr * lr + li * li
    nr, ni = ar - 1.0, ai
    cr = (nr * lr + ni * li) / den
    ci = (ni * lr - nr * li) / den
    return ar, ai, cr * br - ci * bi, cr * bi + ci * br


def _disc_call(body, name, ins, out_widths):
    GP = ins[0].shape[0]
    tl = _pick(GP, (512, 256, 128, 64, 32, 16, 8))
    spec = lambda w: pl.BlockSpec((tl, w), lambda i: (i, 0))
    return pl.pallas_call(body, name=name, grid=(GP // tl,),
                          in_specs=[spec(a.shape[1]) for a in ins], out_specs=[spec(w) for w in out_widths],
                          out_shape=[S((GP, w), f32) for w in out_widths], compiler_params=_cparams(("parallel",)))(*ins)


def s5_disc(lr, li, logdt, br, bi, name):
    HG = br.shape[1]

    def body(lr_ref, li_ref, dt_ref, br_ref, bi_ref, ar_ref, ai_ref, bbr_ref, bbi_ref):
        ar, ai, bbr, bbi = _s5_disc_math(lr_ref[...], li_ref[...], dt_ref[...], br_ref[...], bi_ref[...])
        ar_ref[...], ai_ref[...], bbr_ref[...], bbi_ref[...] = ar, ai, bbr, bbi

    return _disc_call(body, name, [lr, li, logdt, br, bi], [1, 1, HG, HG])


def s5_disc_bwd(lr, li, logdt, br, bi, dar, dai, dbbr, dbbi, name):
    HG = br.shape[1]

    def body(lr_ref, li_ref, dt_ref, br_ref, bi_ref, dar_ref, dai_ref, dbbr_ref, dbbi_ref, *outs):
        _, vjp = jax.vjp(_s5_disc_math, lr_ref[...], li_ref[...], dt_ref[...], br_ref[...], bi_ref[...])
        for o, g in zip(outs, vjp((dar_ref[...], dai_ref[...], dbbr_ref[...], dbbi_ref[...]))):
            o[...] = g

    return _disc_call(body, name, [lr, li, logdt, br, bi, dar, dai, dbbr, dbbi], [1, 1, 1, HG, HG])


SCAN_TB = 256


def s5_scan_fwd(bur, bui, ar, ai, name):
    L, GP = bur.shape
    cw = _pick(GP, (1024, 512, 256, 128))
    tb = _pick(L, (SCAN_TB, 128, 64, 32, 16, 8))

    def body(bur_ref, bui_ref, ar_ref, ai_ref, xr_ref, xi_ref, cr_ref, ci_ref):
        @pl.when(pl.program_id(1) == 0)
        def _():
            cr_ref[...] = jnp.zeros_like(cr_ref)
            ci_ref[...] = jnp.zeros_like(ci_ref)

        a_r, a_i = ar_ref[...], ai_ref[...]

        def step(t, carry):
            xr, xi = carry
            row = pl.ds(t, 1)
            nr = a_r * xr - a_i * xi + bur_ref[row, :]
            ni = a_r * xi + a_i * xr + bui_ref[row, :]
            xr_ref[row, :] = nr
            xi_ref[row, :] = ni
            return nr, ni

        xr, xi = lax.fori_loop(0, tb, step, (cr_ref[...], ci_ref[...]), unroll=8)
        cr_ref[...] = xr
        ci_ref[...] = xi

    blk = pl.BlockSpec((tb, cw), lambda j, i: (i, j))
    vec = pl.BlockSpec((1, cw), lambda j, i: (0, j))
    return pl.pallas_call(
        body, name=name, grid=(GP // cw, L // tb),
        in_specs=[blk, blk, vec, vec], out_specs=[blk, blk],
        out_shape=[S((L, GP), f32), S((L, GP), f32)],
        scratch_shapes=[pltpu.VMEM((1, cw), f32), pltpu.VMEM((1, cw), f32)],
        compiler_params=_cparams(("parallel", "arbitrary")),
    )(bur, bui, ar, ai)


def s5_scan_bwd(gr, gi, xr, xi, ar, ai, name):
    L, GP = gr.shape
    cw = _pick(GP, (1024, 512, 256, 128))
    tb = _pick(L, (SCAN_TB, 128, 64, 32, 16, 8))
    nt = L // tb

    def body(gr_ref, gi_ref, xr_ref, xi_ref, ar_ref, ai_ref, gxr_ref, gxi_ref, dar_ref, dai_ref, cr_ref, ci_ref):
        @pl.when(pl.program_id(1) == 0)
        def _():
            cr_ref[...] = jnp.zeros_like(cr_ref)
            ci_ref[...] = jnp.zeros_like(ci_ref)
            dar_ref[...] = jnp.zeros_like(dar_ref)
            dai_ref[...] = jnp.zeros_like(dai_ref)

        a_r, a_i = ar_ref[...], ai_ref[...]

        def step(s, carry):
            cr, ci, dr, di = carry
            row = pl.ds(tb - 1 - s, 1)
            x_r, x_i = xr_ref[row, :], xi_ref[row, :]
            dr = dr + cr * x_r + ci * x_i
            di = di + ci * x_r - cr * x_i
            nr = gr_ref[row, :] + a_r * cr + a_i * ci
            ni = gi_ref[row, :] + a_r * ci - a_i * cr
            gxr_ref[row, :] = nr
            gxi_ref[row, :] = ni
            return nr, ni, dr, di

        cr, ci, dr, di = lax.fori_loop(0, tb, step, (cr_ref[...], ci_ref[...], dar_ref[...], dai_ref[...]), unroll=8)
        cr_ref[...] = cr
        ci_ref[...] = ci
        dar_ref[...] = dr
        dai_ref[...] = di

    blk = pl.BlockSpec((tb, cw), lambda j, i: (nt - 1 - i, j))
    vec = pl.BlockSpec((1, cw), lambda j, i: (0, j))
    return pl.pallas_call(
        body, name=name, grid=(GP // cw, nt),
        in_specs=[blk, blk, blk, blk, vec, vec], out_specs=[blk, blk, vec, vec],
        out_shape=[S((L, GP), f32), S((L, GP), f32), S((1, GP), f32), S((1, GP), f32)],
        scratch_shapes=[pltpu.VMEM((1, cw), f32), pltpu.VMEM((1, cw), f32)],
        compiler_params=_cparams(("parallel", "arbitrary")),
    )(gr, gi, xr, xi, ar, ai)


def _dn_heads_math(cq, ck, cv, braw, araw, alog, dtb, dk):
    q, k, v = _silu(cq), _silu(ck), _silu(cv)
    q = q * lax.rsqrt(jnp.sum(q * q, axis=-1, keepdims=True) + NORM_EPS) * (dk ** -0.5)
    k = k * lax.rsqrt(jnp.sum(k * k, axis=-1, keepdims=True) + NORM_EPS)
    beta = _sigmoid(braw)
    g = -jnp.exp(alog) * jax.nn.softplus(araw + dtb)
    return q, k, v, beta, g


def dn_prep(qkv, convw, ba, alog, dtb, H, name):
    L, W = qkv.shape
    hk = W // 3
    dk = hk // H

    def fn(i, n, xb, hb, wb, bab, alb, dtbb):
        c, _ = _causal_conv(xb, hb, wb, i == 0)
        qs, ks, vs, bs, gs = [], [], [], [], []
        for h in range(H):
            sl = lambda o: c[:, o + h * dk:o + (h + 1) * dk]
            q, k, v, beta, g = _dn_heads_math(sl(0), sl(hk), sl(2 * hk), bab[:, h:h + 1], bab[:, H + h:H + h + 1],
                                              alb[:, h:h + 1], dtbb[:, h:h + 1], dk)
            qs.append(q), ks.append(k), vs.append(v), bs.append(beta), gs.append(g)
        cat = lambda xs: jnp.concatenate(xs, axis=1)
        return cat(qs), cat(ks), cat(vs), _from_cols(bs + gs, 2 * H)

    return rowk(name, fn, L, _pick(L, (128, 64, 32, 16, 8)), 1,
                [(qkv, "rows"), (qkv, "prev"), (convw, "const"), (ba, "rows"), (alog, "const"), (dtb, "const")],
                [(hk, f32), (hk, f32), (hk, f32), (2 * H, f32)])


def dn_prep_bwd(qkv, convw, ba, alog, dtb, dq, dk_, dv, dbg, H, name):
    L, W = qkv.shape
    hk = W // 3
    dk = hk // H
    kw = convw.shape[0]
    nba = ba.shape[1]

    def fn(i, n, xb, hb, wb, bab, alb, dtbb, dqb, dkb, dvb, dbgb):
        c, shifted = _causal_conv(xb, hb, wb, i == 0)
        dcs = [None] * (3 * H)
        dbr, dar, dal, ddt = [], [], [], []
        for h in range(H):
            sl = lambda a, o: a[:, o + h * dk:o + (h + 1) * dk]
            args = (sl(c, 0), sl(c, hk), sl(c, 2 * hk), bab[:, h:h + 1], bab[:, H + h:H + h + 1],
                    alb[:, h:h + 1], dtbb[:, h:h + 1])
            _, vjp = jax.vjp(lambda *a: _dn_heads_math(*a, dk), *args)
            g = vjp((sl(dqb, 0), sl(dkb, 0), sl(dvb, 0), dbgb[:, h:h + 1], dbgb[:, H + h:H + h + 1]))
            dcs[h], dcs[H + h], dcs[2 * H + h] = g[0], g[1], g[2]
            dbr.append(g[3]), dar.append(g[4]), dal.append(g[5]), ddt.append(g[6])
        dc = jnp.concatenate(dcs, axis=1)
        dba = _from_cols(dbr + dar, nba)
        dw = _from_rows([jnp.sum(dc * shifted[j], axis=0, keepdims=True) for j in range(kw)])
        return dc, dba, dw, _from_cols(dal, H), _from_cols(ddt, H)

    return rowk(name, fn, L, _pick(L, (128, 64, 32, 16, 8)), 1,
                [(qkv, "rows"), (qkv, "prev"), (convw, "const"), (ba, "rows"), (alog, "const"), (dtb, "const"),
                 (dq, "rows"), (dk_, "rows"), (dv, "rows"), (dbg, "rows")],
                [(W, f32), (nba, bf16)], [(kw, W, f32), (1, H, f32), (1, H, f32)])


def conv_t(dc, w, name):
    L, C = dc.shape
    kw = w.shape[0]
    ncol = C // _pick(C, (1536, 1408, 1024, 768, 512, 256, 128))

    def fn(i, n, db, hb, wb):
        out = db * wb[kw - 1:kw]
        for j in range(kw - 1):
            out = out + _shift_up(db, hb, kw - 1 - j, i == n - 1) * wb[j:j + 1]
        return (out,)

    return rowk(name, fn, L, _pick(L, (256, 128, 64, 32, 16, 8)), ncol,
                [(dc, "rows"), (dc, "next"), (w, "cols")], [(C, bf16)])[0]


_BDIMS = {"nn": (((2,), (1,)), ((0,), (0,))), "nt": (((2,), (2,)), ((0,), (0,))), "tn": (((1,), (1,)), ((0,), (0,)))}


def _bdot(a, b, mode):
    return lax.dot_general(a.astype(bf16), b.astype(bf16), _BDIMS[mode], preferred_element_type=f32)


def _split16(a):
    hi = a.astype(bf16)
    return hi, (a - hi.astype(f32)).astype(bf16)


def _hdot(a, b, mode):
    ah, al = _split16(a)
    bh, bl = _split16(b)
    d = lambda x, y: lax.dot_general(x, y, _BDIMS[mode], preferred_element_type=f32)
    return d(ah, bh) + (d(ah, bl) + d(al, bh))


def _make_dot(raw):
    @functools.partial(jax.custom_vjp, nondiff_argnums=(2,))
    def dot(a, b, mode):
        return raw(a, b, mode)

    def fwd(a, b, mode):
        return raw(a, b, mode), (a, b)

    def bwd(mode, res, ct):
        a, b = res
        if mode == "nn":
            return raw(ct, b, "nt"), raw(a, ct, "tn")
        if mode == "nt":
            return raw(ct, b, "nn"), raw(ct, a, "tn")
        return raw(b, ct, "nt"), raw(a, ct, "nn")

    dot.defvjp(fwd, bwd)
    return dot


_dot16 = _make_dot(_bdot)
_dot32 = _make_dot(_hdot)


@jax.custom_vjp
def _unit_lower_inv(lmat):
    c = lmat.shape[-1]
    eye = (lax.broadcasted_iota(jnp.int32, (c, c), 0) == lax.broadcasted_iota(jnp.int32, (c, c), 1)).astype(f32)
    p = -lmat
    t = eye + p
    for _ in range(int(math.log2(c)) - 1):
        p = _hdot(p, p, "nn")
        t = t + _hdot(t, p, "nn")
    return t


def _uli_fwd(lmat):
    t = _unit_lower_inv(lmat)
    return t, t


def _uli_bwd(t, dt):
    return (-_hdot(_hdot(t, dt, "tn"), t, "nt"),)


_unit_lower_inv.defvjp(_uli_fwd, _uli_bwd)


def _dn_chunk_math(s_in, q, k, v, gcol, grow, bcol):
    c = q.shape[1]
    ri = lax.broadcasted_iota(jnp.int32, (c, c), 0)
    ci = lax.broadcasted_iota(jnp.int32, (c, c), 1)
    tril = (ri >= ci).astype(f32)
    strict = (ri > ci).astype(f32)
    gc_col = jnp.sum(tril * grow, axis=2, keepdims=True)
    gc_row = jnp.sum((1.0 - strict) * gcol, axis=1, keepdims=True)
    g_last = jnp.sum(gcol, axis=1, keepdims=True)
    decay = jnp.exp((gc_col - gc_row) * tril) * tril
    kb = k * bcol
    vb = v * bcol
    lmat = _dot16(kb, k, "nt") * decay * strict
    t = _unit_lower_inv(lmat)
    u = _dot32(t, vb, "nn")
    w = _dot32(t, kb * jnp.exp(gc_col), "nn")
    attn = _dot16(q, k, "nt") * decay
    v_new = u - _dot16(w, s_in, "nn")
    o = _dot16(q * jnp.exp(gc_col), s_in, "nn") + _dot16(attn, v_new, "nn")
    s_out = s_in * jnp.exp(g_last) + _dot16(k * jnp.exp(g_last - gc_col), v_new, "tn")
    return o, s_out


def _dn_load(q_ref, k_ref, v_ref, bg_ref, bgt_ref, H, dk):
    heads = lambda ref: jnp.stack([ref[:, h * dk:(h + 1) * dk] for h in range(H)])
    bgb, bgtb = bg_ref[...], bgt_ref[0]
    gcol = jnp.stack([bgb[:, H + h:H + h + 1] for h in range(H)])
    bcol = jnp.stack([bgb[:, h:h + 1] for h in range(H)])
    grow = jnp.stack([bgtb[H + h:H + h + 1, :] for h in range(H)])
    return heads(q_ref), heads(k_ref), heads(v_ref), gcol, grow, bcol


def dn_chunk_fwd(qn, kn, vv, bg, bgt, H, name):
    L, hk = qn.shape
    dk = hk // H
    c = DN_CHUNK
    nc = L // c

    def body(q_ref, k_ref, v_ref, bg_ref, bgt_ref, o_ref, ss_ref, s_ref):
        @pl.when(pl.program_id(0) == 0)
        def _():
            s_ref[...] = jnp.zeros_like(s_ref)

        s_in = s_ref[...]
        ss_ref[0] = s_in
        o, s_out = _dn_chunk_math(s_in, *_dn_load(q_ref, k_ref, v_ref, bg_ref, bgt_ref, H, dk))
        for h in range(H):
            o_ref[:, h * dk:(h + 1) * dk] = o[h]
        s_ref[...] = s_out

    row = lambda w: pl.BlockSpec((c, w), lambda n: (n, 0))
    return pl.pallas_call(
        body, name=name, grid=(nc,),
        in_specs=[row(hk), row(hk), row(hk), row(2 * H), pl.BlockSpec((1, 2 * H, c), lambda n: (n, 0, 0))],
        out_specs=[row(hk), pl.BlockSpec((1, H, dk, dk), lambda n: (n, 0, 0, 0))],
        out_shape=[S((L, hk), f32), S((nc, H, dk, dk), f32)],
        scratch_shapes=[pltpu.VMEM((H, dk, dk), f32)],
        compiler_params=_cparams(("arbitrary",)),
    )(qn, kn, vv, bg, bgt)


def dn_chunk_bwd(qn, kn, vv, bg, bgt, ss, do, H, name):
    L, hk = qn.shape
    dk = hk // H
    c = DN_CHUNK
    nc = L // c

    def body(q_ref, k_ref, v_ref, bg_ref, bgt_ref, ss_ref, do_ref, dq_ref, dk_ref, dv_ref, dbg_ref, dgt_ref, ds_ref):
        @pl.when(pl.program_id(0) == 0)
        def _():
            ds_ref[...] = jnp.zeros_like(ds_ref)

        args = (ss_ref[0],) + _dn_load(q_ref, k_ref, v_ref, bg_ref, bgt_ref, H, dk)
        _, vjp = jax.vjp(_dn_chunk_math, *args)
        do = jnp.stack([do_ref[:, h * dk:(h + 1) * dk] for h in range(H)])
        ds, dq, dkk, dv, dgcol, dgrow, dbcol = vjp((do, ds_ref[...]))
        ds_ref[...] = ds
        for h in range(H):
            sl = slice(h * dk, (h + 1) * dk)
            dq_ref[:, sl], dk_ref[:, sl], dv_ref[:, sl] = dq[h], dkk[h], dv[h]
        dbg_ref[...] = _from_cols([dbcol[h] for h in range(H)] + [dgcol[h] for h in range(H)], 2 * H)
        dgt_ref[0] = _from_rows([dgrow[h] for h in range(H)])

    row = lambda w: pl.BlockSpec((c, w), lambda n: (nc - 1 - n, 0))
    return pl.pallas_call(
        body, name=name, grid=(nc,),
        in_specs=[row(hk), row(hk), row(hk), row(2 * H), pl.BlockSpec((1, 2 * H, c), lambda n: (nc - 1 - n, 0, 0)),
                  pl.BlockSpec((1, H, dk, dk), lambda n: (nc - 1 - n, 0, 0, 0)), row(hk)],
        out_specs=[row(hk), row(hk), row(hk), row(2 * H), pl.BlockSpec((1, H, c), lambda n: (nc - 1 - n, 0, 0))],
        out_shape=[S((L, hk), f32), S((L, hk), f32), S((L, hk), f32), S((L, 2 * H), f32), S((nc, H, c), f32)],
        scratch_shapes=[pltpu.VMEM((H, dk, dk), f32)],
        compiler_params=_cparams(("arbitrary",)),
    )(qn, kn, vv, bg, bgt, ss, do)


def _dn_gate_math(o, z, w):
    return _rms(o, w) * _silu(z)


def dn_gate(o, z, w, H, name):
    L, hv = o.shape
    dv = hv // H

    def fn(i, n, ob, zb, wb):
        return (jnp.concatenate([_dn_gate_math(ob[:, h * dv:(h + 1) * dv], zb[:, h * dv:(h + 1) * dv], wb)
                                 for h in range(H)], axis=1),)

    return rowk(name, fn, L, _pick(L, (256, 128, 64, 32, 16, 8)), 1, [(o, "rows"), (z, "rows"), (w, "const")], [(hv, bf16)])[0]


def dn_gate_bwd(o, z, w, dy, H, name):
    L, hv = o.shape
    dv = hv // H

    def fn(i, n, ob, zb, wb, dyb):
        dos, dzs, dw = [], [], 0.0
        for h in range(H):
            sl = slice(h * dv, (h + 1) * dv)
            _, vjp = jax.vjp(_dn_gate_math, ob[:, sl], zb[:, sl], wb)
            a, b, c = vjp(dyb[:, sl])
            dos.append(a), dzs.append(b)
            dw = dw + c
        return jnp.concatenate(dos, axis=1), jnp.concatenate(dzs, axis=1), dw

    return rowk(name, fn, L, _pick(L, (256, 128, 64, 32, 16, 8)), 1,
                [(o, "rows"), (z, "rows"), (w, "const"), (dy, "rows")], [(hv, f32), (hv, bf16)], [(1, dv, f32)])


def ffn_mid(ua, uv, wa, wv, name):
    L, F = ua.shape
    ncol = F // _pick(F, (1408, 1024, 512, 256, 128))

    def fn(i, n, ab, ah, vb, vh, wab, wvb):
        ca, _ = _causal_conv(ab, ah, wab, i == 0)
        cv, _ = _causal_conv(vb, vh, wvb, i == 0)
        return (_silu(ca) * cv,)

    return rowk(name, fn, L, _pick(L, (256, 128, 64, 32, 16, 8)), ncol,
                [(ua, "rows"), (ua, "prev"), (uv, "rows"), (uv, "prev"), (wa, "cols"), (wv, "cols")], [(F, bf16)])[0]


def ffn_mid_bwd(ua, uv, wa, wv, dh, name):
    L, F = ua.shape
    kw = wa.shape[0]
    ncol = F // _pick(F, (1408, 1024, 512, 256, 128))

    def fn(i, n, ab, ah, vb, vh, wab, wvb, dhb):
        ca, sa = _causal_conv(ab, ah, wab, i == 0)
        cv, sv = _causal_conv(vb, vh, wvb, i == 0)
        dca = dhb * cv * _dsilu(ca)
        dcv = dhb * _silu(ca)
        dwa = _from_rows([jnp.sum(dca * sa[j], axis=0, keepdims=True) for j in range(kw)])
        dwv = _from_rows([jnp.sum(dcv * sv[j], axis=0, keepdims=True) for j in range(kw)])
        return dca, dcv, dwa, dwv

    return rowk(name, fn, L, _pick(L, (256, 128, 64, 32, 16, 8)), ncol,
                [(ua, "rows"), (ua, "prev"), (uv, "rows"), (uv, "prev"), (wa, "cols"), (wv, "cols"), (dh, "rows")],
                [(F, f32), (F, f32)], [(kw, F, f32), (kw, F, f32)])


def _merge_epi(acc, gs, gd, ga, gb):
    return acc, _sigmoid(gs) * ga * _sigmoid(gb) + _sigmoid(gd) * acc


def merge_bwd(dm, gs, gd, ga, gb, brdn, name):
    L, D = dm.shape
    ncol = D // _pick(D, (1024, 512, 256, 128))

    def fn(i, n, dmb, gsb, gdb, gab, gbb, brb):
        ss, sd, sb = _sigmoid(gsb), _sigmoid(gdb), _sigmoid(gbb)
        br_s5 = gab * sb
        dbr_s5 = dmb * ss
        return (dmb * br_s5 * ss * (1.0 - ss), dmb * brb * sd * (1.0 - sd), dbr_s5 * sb,
                dbr_s5 * gab * sb * (1.0 - sb), dmb * sd)

    return rowk(name, fn, L, _pick(L, (256, 128, 64, 32, 16, 8)), ncol,
                [(a, "rows") for a in (dm, gs, gd, ga, gb, brdn)], [(D, bf16)] * 5)


def s5_out_bwd(dy, ypre, u, d, name):
    L, W = dy.shape

    def fn(i, n, dyb, yb, ub, db):
        dyp = dyb * _dgelu(yb)
        return dyp, db * dyp, jnp.sum(dyp * ub, axis=0, keepdims=True)

    return rowk(name, fn, L, _pick(L, (256, 128, 64, 32, 16, 8)), 1,
                [(dy, "rows"), (ypre, "rows"), (u, "rows"), (d, "const")], [(W, bf16), (W, f32)], [(1, W, f32)])


def _s5_y_epi(acc, y1, u, d):
    ypre = acc + y1 + d * u
    return ypre, _gelu(ypre)


def adamw(w, g, m, v, name):
    R, C = w.shape
    tl = _pick(R, (256, 128, 64, 32, 16, 8))
    if R * C * 4 <= 2 * 1024 * 1024:
        tl = R

    def body(w_ref, g_ref, m_ref, v_ref, d_ref, nm_ref, nv_ref):
        gg = g_ref[...]
        nm = ADAM_B1 * m_ref[...] + (1.0 - ADAM_B1) * gg
        nv = ADAM_B2 * v_ref[...] + (1.0 - ADAM_B2) * (gg * gg)
        m_hat = nm / (1.0 - ADAM_B1 ** ADAM_STEP)
        v_hat = nv / (1.0 - ADAM_B2 ** ADAM_STEP)
        d_ref[...] = -ADAM_LR * (m_hat / (jnp.sqrt(v_hat) + ADAM_EPS) + ADAM_WD * w_ref[...])
        nm_ref[...] = nm
        nv_ref[...] = nv

    blk = pl.BlockSpec((tl, C), lambda i: (i, 0))
    return pl.pallas_call(body, name=name, grid=(R // tl,), in_specs=[blk] * 4, out_specs=[blk] * 3,
                          out_shape=[S((R, C), f32)] * 3, compiler_params=_cparams(("parallel",)))(w, g, m, v)


def sum_slots(x, name, out_dtype=f32):
    n, R, C = x.shape
    tl = _pick(R, (512, 256, 128, 64, 32, 16, 8))

    def body(x_ref, o_ref):
        acc = x_ref[0].astype(f32)
        for s in range(1, n):
            acc = acc + x_ref[s].astype(f32)
        o_ref[...] = acc.astype(o_ref.dtype)

    return pl.pallas_call(body, name=name, grid=(R // tl,),
                          in_specs=[pl.BlockSpec((n, tl, C), lambda i: (0, i, 0))],
                          out_specs=pl.BlockSpec((tl, C), lambda i: (i, 0)),
                          out_shape=S((R, C), out_dtype), compiler_params=_cparams(("parallel",)))(x)


_ANY = pl.BlockSpec(memory_space=pl.ANY)


def _coords():
    return lax.axis_index("x"), lax.axis_index("y"), lax.axis_index("c")


def chip_exchange(src, name, same=False):
    out_shape = (N_CHIPS,) + src.shape if same else src.shape
    assert out_shape[0] == N_CHIPS

    def body(src_ref, out_ref, send_sems, recv_sems, local_sem):
        x, y, c = _coords()
        me = 2 * x + y
        slot = (lambda j: src_ref) if same else (lambda j: src_ref.at[j])
        mine = pltpu.make_async_copy(slot(me), out_ref.at[me], local_sem)
        mine.start()
        peers = [(1 - x, y), (x, 1 - y), (1 - x, 1 - y)]
        copies = []
        for k, (px, py) in enumerate(peers):
            cp = pltpu.make_async_remote_copy(
                src_ref=slot(2 * px + py), dst_ref=out_ref.at[me],
                send_sem=send_sems.at[k], recv_sem=recv_sems.at[k],
                device_id=(px, py, c), device_id_type=MESH)
            cp.start()
            copies.append(cp)
        for k, (px, py) in enumerate(peers):
            pltpu.make_async_remote_copy(
                src_ref=slot(me), dst_ref=out_ref.at[2 * px + py],
                send_sem=send_sems.at[k], recv_sem=recv_sems.at[k],
                device_id=(px, py, c), device_id_type=MESH).wait_recv()
        for cp in copies:
            cp.wait_send()
        mine.wait()

    return pl.pallas_call(
        body, name=name, in_specs=[_ANY], out_specs=_ANY, out_shape=S(out_shape, src.dtype),
        scratch_shapes=[pltpu.SemaphoreType.DMA((3,)), pltpu.SemaphoreType.DMA((3,)), pltpu.SemaphoreType.DMA],
    )(src)


def sibling_exchange(src, name):
    def body(src_ref, out_ref, send_sem, recv_sem):
        x, y, c = _coords()
        cp = pltpu.make_async_remote_copy(src_ref=src_ref, dst_ref=out_ref, send_sem=send_sem, recv_sem=recv_sem,
                                          device_id=(x, y, 1 - c), device_id_type=MESH)
        cp.start()
        cp.wait()

    return pl.pallas_call(
        body, name=name, in_specs=[_ANY], out_specs=_ANY, out_shape=S(src.shape, src.dtype),
        scratch_shapes=[pltpu.SemaphoreType.DMA, pltpu.SemaphoreType.DMA],
    )(src)


def _block_diag(blocks):
    G, a, b = blocks.shape
    eye = jnp.eye(G, dtype=blocks.dtype)
    return (blocks[:, :, None, :] * eye[:, None, :, None]).reshape(G * a, G * b)


def _block_diag_extract(mat, G):
    a, b = mat.shape[0] // G, mat.shape[1] // G
    m4 = mat.reshape(G, a, G, b)
    return jnp.sum(m4 * jnp.eye(G, dtype=mat.dtype)[:, None, :, None], axis=2)


def _flat_pack(arrs, dtype, lanes=1024, row_mult=16):
    flat = jnp.concatenate([a.reshape(-1).astype(dtype) for a in arrs])
    n = flat.shape[0]
    per = lanes * row_mult
    pad = (-n) % per
    if pad:
        flat = jnp.concatenate([flat, jnp.zeros((pad,), dtype)])
    return flat.reshape(-1, lanes)


def _flat_unpack(buf, shapes):
    flat = buf.reshape(-1)
    out, off = [], 0
    for s in shapes:
        n = math.prod(s)
        out.append(flat[off:off + n].reshape(s))
        off += n
    return out


def _layer_weights(p, H):
    w_in = p["w_in"]
    D = w_in.shape[0]
    s5w = p["s5_d"].shape[-1]
    hk = p["dn_proj_w"].shape[0]
    off_u, off_qkv = s5w, s5w + 3 * hk
    off_z = off_qkv + hk
    off_b, off_a = off_z + H, off_z + 2 * H
    off_gs = off_a + D
    q = {}
    q["Wu"] = w_in[:, :off_u]
    q["Wqkv"] = w_in[:, off_u:off_qkv]
    q["Wz"] = w_in[:, off_qkv:off_z]
    q["Wba"] = jnp.pad(w_in[:, off_z:off_a], ((0, 0), (0, 128 - 2 * H)))
    q["Wgs"] = w_in[:, off_a:off_gs]
    q["Wgd"] = w_in[:, off_gs:]
    glu = p["s5_glu_w"]
    q["Wga"], q["Wgb"] = glu[:, :D], glu[:, D:]
    F = p["ffn_down"].shape[0]
    q["Wupa"], q["Wupv"] = p["ffn_up"][:, :F], p["ffn_up"][:, F:]
    q["cwa"], q["cwv"] = p["ffn_conv_w"][:, :F], p["ffn_conv_w"][:, F:]
    for k in ("dn_proj_w", "w_out", "ffn_down", "dn_conv_w", "mix_norm_w", "ffn_norm_w", "dn_norm_w",
              "dn_a_log", "dn_dt_bias", "s5_d"):
        q[k] = p[k]
    return q


def _s5_params(p, tag):
    G, P = p["s5_a_re"].shape
    HG = p["s5_b_re"].shape[-1]
    col = lambda a: a.reshape(G * P, 1)
    lr, li = col(p["s5_a_re"]), col(p["s5_a_im"])
    logdt = col(jnp.broadcast_to(p["s5_log_dt"][:, None], (G, P)))
    br, bi = p["s5_b_re"].reshape(G * P, HG), p["s5_b_im"].reshape(G * P, HG)
    ar, ai, bbr, bbi = s5_disc(lr, li, logdt, br, bi, "s5_disc")
    bd = lambda m: _block_diag(m.reshape(G, P, HG).transpose(0, 2, 1)).astype(bf16)
    cd = lambda m: _block_diag(m.transpose(0, 2, 1)).astype(bf16)
    return dict(lr=lr, li=li, logdt=logdt, br=br, bi=bi, ar=ar.reshape(1, G * P), ai=ai.reshape(1, G * P),
                Bre=bd(bbr), Bim=bd(bbi), CreT=cd(p["s5_c_re"]), mCimT=cd(-p["s5_c_im"]), G=G, P=P, HG=HG)


def layer_fwd(x, q, s5, H):
    r = {"x": x}
    h1 = rms_fwd(x, q["mix_norm_w"], "rms_mix")
    r["h1"] = h1
    u32, u16 = mm(h1, q["Wu"], "nn", "proj_u", out_dtypes=(f32, bf16), epi=lambda a: (a, a))
    qkv = mm(h1, q["Wqkv"], "nn", "proj_qkv")
    z = mm(h1, q["Wz"], "nn", "proj_z")
    ba = mm(h1, q["Wba"], "nn", "proj_ba")
    gs = mm(h1, q["Wgs"], "nn", "proj_gs")
    gd = mm(h1, q["Wgd"], "nn", "proj_gd")
    r.update(u32=u32, u16=u16, qkv=qkv, z=z, ba=ba, gs=gs, gd=gd)
    bur = mm(u16, s5["Bre"], "nn", "s5_bu_re")
    bui = mm(u16, s5["Bim"], "nn", "s5_bu_im")
    xr, xi = s5_scan_fwd(bur, bui, s5["ar"], s5["ai"], "s5_scan_fwd")
    y1 = mm(xr, s5["CreT"], "nn", "s5_y_re")
    ypre, ys5 = mm(xi, s5["mCimT"], "nn", "s5_y_im", extras=(y1, u32, q["s5_d"]), epi=_s5_y_epi, out_dtypes=(f32, bf16))
    ga = mm(ys5, q["Wga"], "nn", "glu_a")
    gb = mm(ys5, q["Wgb"], "nn", "glu_b")
    r.update(xr=xr, xi=xi, ypre=ypre, ys5=ys5, ga=ga, gb=gb)
    qn, kn, vv, bg = dn_prep(qkv, q["dn_conv_w"], ba, q["dn_a_log"], q["dn_dt_bias"], H, "dn_prep")
    L = x.shape[0]
    bgt = bg.reshape(L // DN_CHUNK, DN_CHUNK, 2 * H).transpose(0, 2, 1)
    o, ss = dn_chunk_fwd(qn, kn, vv, bg, bgt, H, "dn_chunk_fwd")
    ydn = dn_gate(o, z, q["dn_norm_w"], H, "dn_gate")
    brdn, merged = mm(ydn, q["dn_proj_w"], "nn", "dn_proj", extras=(gs, gd, ga, gb), epi=_merge_epi, out_dtypes=(f32, bf16))
    r.update(qn=qn, kn=kn, vv=vv, bg=bg, bgt=bgt, ss=ss, o=o, ydn=ydn, brdn=brdn, merged=merged)
    x1 = mm(merged, q["w_out"], "nn", "out_proj", extras=(x,), epi=_add)
    h2 = rms_fwd(x1, q["ffn_norm_w"], "rms_ffn")
    ua = mm(h2, q["Wupa"], "nn", "ffn_up_a")
    uv = mm(h2, q["Wupv"], "nn", "ffn_up_v")
    hmid = ffn_mid(ua, uv, q["cwa"], q["cwv"], "ffn_mid")
    x2 = mm(hmid, q["ffn_down"], "nn", "ffn_down", extras=(x1,), epi=_add)
    r.update(x1=x1, h2=h2, ua=ua, uv=uv, hmid=hmid)
    return x2, r


def layer_bwd(dx2, dx2b, r, q, s5, H):
    g = {}
    dhmid = mm(dx2b, q["ffn_down"], "nt", "d_hmid")
    g["ffn_down"] = mm(r["hmid"], dx2b, "tn", "dw_ffn_down")
    dca, dcv, dwa, dwv = ffn_mid_bwd(r["ua"], r["uv"], q["cwa"], q["cwv"], dhmid, "ffn_mid_bwd")
    g["ffn_conv_w"] = jnp.concatenate([dwa, dwv], axis=1)
    dua = conv_t(dca, q["cwa"], "ffn_conv_t_a")
    duv = conv_t(dcv, q["cwv"], "ffn_conv_t_v")
    dh2 = mm(dua, q["Wupa"], "nt", "d_h2_a")
    dh2 = mm(duv, q["Wupv"], "nt", "d_h2_v", extras=(dh2,), epi=_add)
    g["ffn_up"] = jnp.concatenate([mm(r["h2"], dua, "tn", "dw_up_a"), mm(r["h2"], duv, "tn", "dw_up_v")], axis=1)
    dx1, dx1b, dffn_w = rms_bwd(r["x1"], q["ffn_norm_w"], dh2, dx2, "rms_ffn_bwd")
    g["ffn_norm_w"] = dffn_w[0]
    dm = mm(dx1b, q["w_out"], "nt", "d_merged")
    g["w_out"] = mm(r["merged"], dx1b, "tn", "dw_out")
    dgs, dgd, dga, dgb, dbrdn = merge_bwd(dm, r["gs"], r["gd"], r["ga"], r["gb"], r["brdn"], "merge_bwd")
    dydn = mm(dbrdn, q["dn_proj_w"], "nt", "d_ydn")
    g["dn_proj_w"] = mm(r["ydn"], dbrdn, "tn", "dw_dn_proj")
    do, dz, dnw = dn_gate_bwd(r["o"], r["z"], q["dn_norm_w"], dydn, H, "dn_gate_bwd")
    g["dn_norm_w"] = dnw[0]
    dq, dk, dv, dbg, dgt = dn_chunk_bwd(r["qn"], r["kn"], r["vv"], r["bg"], r["bgt"], r["ss"], do, H, "dn_chunk_bwd")
    L = dq.shape[0]
    dbg = dbg + jnp.concatenate([jnp.zeros((L, H), f32), dgt.transpose(0, 2, 1).reshape(L, H)], axis=1)
    dc, dba, dcw, dal, ddt = dn_prep_bwd(r["qkv"], q["dn_conv_w"], r["ba"], q["dn_a_log"], q["dn_dt_bias"],
                                         dq, dk, dv, dbg, H, "dn_prep_bwd")
    g["dn_conv_w"], g["dn_a_log"], g["dn_dt_bias"] = dcw, dal[0], ddt[0]
    dqkv = conv_t(dc, q["dn_conv_w"], "dn_conv_t")
    dys5 = mm(dga, q["Wga"], "nt", "d_ys5_a")
    dys5 = mm(dgb, q["Wgb"], "nt", "d_ys5_b", extras=(dys5,), epi=_add)
    g["s5_glu_w"] = jnp.concatenate([mm(r["ys5"], dga, "tn", "dw_glu_a"), mm(r["ys5"], dgb, "tn", "dw_glu_b")], axis=1)
    dyp, du_direct, dd = s5_out_bwd(dys5, r["ypre"], r["u32"], q["s5_d"], "s5_out_bwd")
    g["s5_d"] = dd[0]
    gdr = mm(dyp, s5["CreT"], "nt", "s5_gx_re")
    gdi = mm(dyp, s5["mCimT"], "nt", "s5_gx_im")
    G, P, HG = s5["G"], s5["P"], s5["HG"]
    dcre = _block_diag_extract(mm(r["xr"], dyp, "tn", "dw_s5_c_re"), G)
    dcim = _block_diag_extract(mm(r["xi"], dyp, "tn", "dw_s5_c_im"), G)
    g["s5_c_re"], g["s5_c_im"] = dcre.transpose(0, 2, 1), -dcim.transpose(0, 2, 1)
    gxr, gxi, dar, dai = s5_scan_bwd(gdr, gdi, r["xr"], r["xi"], s5["ar"], s5["ai"], "s5_scan_bwd")
    dbre = _block_diag_extract(mm(r["u16"], gxr, "tn", "dw_s5_b_re"), G)
    dbim = _block_diag_extract(mm(r["u16"], gxi, "tn", "dw_s5_b_im"), G)
    tocol = lambda m: m.transpose(0, 2, 1).reshape(G * P, HG)
    dlr, dli, dlogdt, dbr, dbi = s5_disc_bwd(s5["lr"], s5["li"], s5["logdt"], s5["br"], s5["bi"],
                                             dar.reshape(G * P, 1), dai.reshape(G * P, 1), tocol(dbre), tocol(dbim), "s5_disc_bwd")
    g["s5_a_re"], g["s5_a_im"] = dlr.reshape(G, P), dli.reshape(G, P)
    g["s5_log_dt"] = jnp.sum(dlogdt.reshape(G, P), axis=1)
    g["s5_b_re"], g["s5_b_im"] = dbr.reshape(G, P, HG), dbi.reshape(G, P, HG)
    du = mm(gxr, s5["Bre"], "nt", "d_u_re")
    du = mm(gxi, s5["Bim"], "nt", "d_u_im", extras=(du, du_direct), epi=lambda a, b, c: (a + b + c,), out_dtypes=(bf16,))
    h1 = r["h1"]
    segs = [("Wu", du), ("Wqkv", dqkv), ("Wz", dz), ("Wba", dba), ("Wgs", dgs), ("Wgd", dgd)]
    dh1 = None
    dws = []
    for name, dseg in segs:
        if dh1 is None:
            dh1 = mm(dseg, q[name], "nt", "d_h1_" + name)
        else:
            dh1 = mm(dseg, q[name], "nt", "d_h1_" + name, extras=(dh1,), epi=_add)
        dw = mm(h1, dseg, "tn", "dw_in_" + name)
        dws.append(dw[:, :2 * H] if name == "Wba" else dw)
    g["w_in"] = jnp.concatenate(dws, axis=1)
    dx, dxb, dmix = rms_bwd(r["x"], q["mix_norm_w"], dh1, dx1, "rms_mix_bwd")
    g["mix_norm_w"] = dmix[0]
    return dx, dxb, g


BIG = ("w_in", "s5_glu_w", "dn_proj_w", "w_out", "ffn_up", "ffn_down")
SHARDED_SMALL = ("dn_conv_w", "ffn_conv_w")
COL_SHARDED = ("w_in", "s5_glu_w", "dn_proj_w", "ffn_up", "dn_conv_w", "ffn_conv_w")
REPL = ("mix_norm_w", "s5_log_dt", "s5_a_re", "s5_a_im", "s5_b_re", "s5_b_im", "s5_c_re", "s5_c_im", "s5_d",
        "dn_a_log", "dn_dt_bias", "dn_norm_w", "ffn_norm_w")
WEIGHTS = ['mix_norm_w', 'w_in', 's5_log_dt', 's5_a_re', 's5_a_im', 's5_b_re', 's5_b_im', 's5_c_re', 's5_c_im', 's5_d',
           's5_glu_w', 'dn_conv_w', 'dn_a_log', 'dn_dt_bias', 'dn_norm_w', 'dn_proj_w', 'w_out', 'ffn_norm_w', 'ffn_up',
           'ffn_conv_w', 'ffn_down', 'final_norm_w']


def _join_shards(name, shards):
    return jnp.concatenate(shards, axis=-1 if name in COL_SHARDED else -2)


def _split_shards(name, full):
    return jnp.split(full, N_CHIPS, axis=-1 if name in COL_SHARDED else -2)


def gather_layer_weights(shards):
    names = BIG
    shapes = [shards[n].shape for n in names]
    pack = _flat_pack([shards[n] for n in names], bf16, row_mult=32)
    R = pack.shape[0]
    c = lax.axis_index("c")
    half = lax.dynamic_slice_in_dim(pack, c * (R // 2), R // 2, axis=0)
    got = chip_exchange(half, "ag_chips", same=True)
    other = sibling_exchange(got, "ag_sibling")
    lo = jnp.where(c == 0, got, other)
    hi = jnp.where(c == 0, other, got)
    full = jnp.concatenate([lo, hi], axis=1)
    per_chip = [_flat_unpack(full[j], shapes) for j in range(N_CHIPS)]
    return {n: _join_shards(n, [per_chip[j][k] for j in range(N_CHIPS)]) for k, n in enumerate(names)}


def gather_small_sharded(shards):
    names = SHARDED_SMALL
    shapes = [shards[n].shape for n in names]
    got = chip_exchange(_flat_pack([shards[n] for n in names], f32, lanes=128, row_mult=8), "ag_small", same=True)
    per_chip = [_flat_unpack(got[j], shapes) for j in range(N_CHIPS)]
    return {n: _join_shards(n, [per_chip[j][k] for j in range(N_CHIPS)]) for k, n in enumerate(names)}


def reduce_scatter_layer_grads(g):
    names = BIG + SHARDED_SMALL
    split = {n: _split_shards(n, g[n]) for n in names}
    shapes = [split[n][0].shape for n in names]
    pack = jnp.stack([_flat_pack([split[n][j] for n in names], f32, row_mult=32) for j in range(N_CHIPS)])
    R = pack.shape[1]
    c = lax.axis_index("c")
    mine = lax.dynamic_slice_in_dim(pack, c * (R // 2), R // 2, axis=1)
    theirs = lax.dynamic_slice_in_dim(pack, (1 - c) * (R // 2), R // 2, axis=1)
    got = sibling_exchange(theirs.reshape(N_CHIPS * (R // 2), -1), "rs_sibling").reshape(mine.shape)
    pair = jnp.stack([jnp.where(c == 0, mine, got), jnp.where(c == 0, got, mine)])
    chip_sum = sum_slots(pair.reshape(2, N_CHIPS * (R // 2), -1), "rs_pair_sum", bf16).reshape(mine.shape)
    parts = chip_exchange(chip_sum, "rs_chips")
    half = sum_slots(parts, "rs_chip_sum")
    other = sibling_exchange(half, "rs_share")
    full = jnp.concatenate([jnp.where(c == 0, half, other), jnp.where(c == 0, other, half)], axis=0)
    return dict(zip(names, _flat_unpack(full, shapes)))


def all_reduce_small(arrs):
    shapes = [a.shape for a in arrs]
    pack = _flat_pack(arrs, f32, lanes=128, row_mult=8)
    from_chips = chip_exchange(pack, "ar_chips", same=True)
    from_sib = sibling_exchange(from_chips, "ar_sibling")
    c = lax.axis_index("c")
    both = jnp.concatenate([jnp.where(c == 0, from_chips, from_sib), jnp.where(c == 0, from_sib, from_chips)], axis=0)
    return _flat_unpack(sum_slots(both, "ar_sum"), shapes)


def kernel(x, mix_norm_w, w_in, s5_log_dt, s5_a_re, s5_a_im, s5_b_re, s5_b_im, s5_c_re, s5_c_im, s5_d, s5_glu_w, dn_conv_w, dn_a_log, dn_dt_bias, dn_norm_w, dn_proj_w, w_out, ffn_norm_w, ffn_up, ffn_conv_w, ffn_down, final_norm_w, loss_target, m_mix_norm_w, m_w_in, m_s5_log_dt, m_s5_a_re, m_s5_a_im, m_s5_b_re, m_s5_b_im, m_s5_c_re, m_s5_c_im, m_s5_d, m_s5_glu_w, m_dn_conv_w, m_dn_a_log, m_dn_dt_bias, m_dn_norm_w, m_dn_proj_w, m_w_out, m_ffn_norm_w, m_ffn_up, m_ffn_conv_w, m_ffn_down, m_final_norm_w, v_mix_norm_w, v_w_in, v_s5_log_dt, v_s5_a_re, v_s5_a_im, v_s5_b_re, v_s5_b_im, v_s5_c_re, v_s5_c_im, v_s5_d, v_s5_glu_w, v_dn_conv_w, v_dn_a_log, v_dn_dt_bias, v_dn_norm_w, v_dn_proj_w, v_w_out, v_ffn_norm_w, v_ffn_up, v_ffn_conv_w, v_ffn_down, v_final_norm_w):
    args = locals()
    W = {n: args[n] for n in WEIGHTS}
    M = {n: args["m_" + n] for n in WEIGHTS}
    V = {n: args["v_" + n] for n in WEIGHTS}
    depth = mix_norm_w.shape[0]
    H = dn_a_log.shape[1]
    xs = x[0]
    target = loss_target[0]

    layers = []
    conv_full = gather_small_sharded({n: W[n] for n in SHARDED_SMALL})
    for l in range(depth):
        p = gather_layer_weights({n: W[n][l] for n in BIG})
        for n in REPL:
            p[n] = W[n][l]
        for n in SHARDED_SMALL:
            p[n] = conv_full[n][l]
        for n in ("mix_norm_w", "ffn_norm_w", "dn_norm_w", "dn_a_log", "dn_dt_bias", "s5_d"):
            p[n] = p[n].reshape(1, -1)
        layers.append((_layer_weights(p, H), _s5_params(p, l)))
    res = []
    for l in range(depth):
        xs, r = layer_fwd(xs, layers[l][0], layers[l][1], H)
        res.append(r)
    dx, dxb, loss_part, dfinal = loss_head(xs, final_norm_w.reshape(1, -1), target, "loss_head")
    loss = lax.psum(loss_part[0, 0], ("x", "y", "c"))

    grads = [None] * depth
    for l in reversed(range(depth)):
        dx, dxb, grads[l] = layer_bwd(dx, dxb, res[l], layers[l][0], layers[l][1], H)
    grad_x = dx[None]

    G = {}
    sharded = [reduce_scatter_layer_grads(grads[l]) for l in range(depth)]
    for n in BIG + SHARDED_SMALL:
        G[n] = jnp.stack([sharded[l][n] for l in range(depth)])
    small = [jnp.stack([grads[l][n] for l in range(depth)]) for n in REPL] + [dfinal[0]]
    for n, a in zip(REPL + ("final_norm_w",), all_reduce_small(small)):
        G[n] = a

    delta, new_m, new_v = {}, {}, {}
    for n in WEIGHTS:
        shape = W[n].shape
        size = math.prod(shape)
        if n in BIG + SHARDED_SMALL:
            two_d = (size // shape[-1], shape[-1])
        else:
            two_d = (size // 128, 128) if size % 128 == 0 else (1, size)
        d, nm, nv = adamw(W[n].reshape(two_d), G[n].reshape(two_d), M[n].reshape(two_d), V[n].reshape(two_d), "adamw_" + n)
        delta[n], new_m[n], new_v[n] = d.reshape(shape), nm.reshape(shape), nv.reshape(shape)
        G[n] = G[n].reshape(shape)
    return (loss, grad_x, *[G[n] for n in WEIGHTS], *[delta[n] for n in WEIGHTS],
            *[new_m[n] for n in WEIGHTS], *[new_v[n] for n in WEIGHTS])
```

```python
import functools
import math

import jax
import jax.numpy as jnp
from jax import lax
from jax.experimental import pallas as pl
from jax.experimental.pallas import tpu as pltpu

f32 = jnp.float32
bf16 = jnp.bfloat16
S = jax.ShapeDtypeStruct

NORM_EPS = 1e-6
DN_CHUNK = 64
S5_GROUP = 16
ADAM_LR, ADAM_B1, ADAM_B2, ADAM_EPS, ADAM_WD, ADAM_STEP = 0.001, 0.9, 0.999, 1e-08, 0.01, 10
VMEM_LIMIT_BYTES = 56 * 1024 * 1024
HALO = 8
MESH = pl.DeviceIdType.MESH
N_CHIPS = 4


def _pick(n, cands):
    for c in cands:
        if n % c == 0:
            return c
    return n


MM_VMEM_BUDGET = 40 * 1024 * 1024
MM_MAX_TK = 2816


def _pick_k(K):
    if K <= MM_MAX_TK or K % 128:
        return K
    return max(d for d in range(128, MM_MAX_TK + 1, 128) if K % d == 0)


def _cparams(sem):
    return pltpu.CompilerParams(dimension_semantics=sem, vmem_limit_bytes=VMEM_LIMIT_BYTES)


_DIMS = {"nn": ((1,), (0,)), "nt": ((1,), (1,)), "tn": ((0,), (0,))}


class Win:
    def __init__(self, arr, c0, nc):
        self.arr, self.c0, self.nc = arr, c0, nc


def mm(a, b, mode, name, extras=(), epi=None, out_dtypes=(f32,), out_into=None):
    barr, c0 = (b.arr, b.c0) if isinstance(b, Win) else (b, 0)
    if mode == "tn":
        K, M = a.shape
    else:
        M, K = a.shape
    if mode == "nt":
        N, K2 = barr.shape
        K2 = b.nc if isinstance(b, Win) else K2
        n_off, k_off = 0, c0
    else:
        K2, N = barr.shape
        N = b.nc if isinstance(b, Win) else N
        n_off, k_off = c0, 0
    assert K == K2, (a.shape, barr.shape, mode)
    o_tot, o_off, o_alias = out_into if out_into is not None else (N, 0, None)
    tm = _pick(M, (1024, 512, 256, 128, 64, 32, 16, 8))
    tn = _pick(math.gcd(math.gcd(N, n_off), o_off), (1024, 512, 256, 128))
    tk = _pick_k(math.gcd(K, k_off))

    def vmem_estimate(tm_):
        tiles = tm_ * tk * a.dtype.itemsize + tk * tn * barr.dtype.itemsize
        tiles += sum(tm_ * tn * e.dtype.itemsize for e in extras if e.shape == (M, N))
        tiles += sum(tm_ * tn * jnp.dtype(dt).itemsize for dt in out_dtypes)
        return 2 * tiles + 3 * tm_ * tn * 4

    while vmem_estimate(tm) > MM_VMEM_BUDGET and tm % 16 == 0 and tm > 128:
        tm //= 2
    nk = K // tk
    assert M % tm == 0 and N % tn == 0 and K % tk == 0 and n_off % tn == 0 and k_off % tk == 0 and o_off % tn == 0
    nb, kb, ob = n_off // tn, k_off // tk, o_off // tn
    if mode == "tn":
        a_spec = pl.BlockSpec((tk, tm), lambda j, i, k: (k, i))
    else:
        a_spec = pl.BlockSpec((tm, tk), lambda j, i, k: (i, k))
    if mode == "nt":
        b_spec = pl.BlockSpec((tn, tk), lambda j, i, k: (j, k + kb))
    else:
        b_spec = pl.BlockSpec((tk, tn), lambda j, i, k: (k, j + nb))
    ex_specs = []
    for e in extras:
        if e.shape == (M, N):
            ex_specs.append(pl.BlockSpec((tm, tn), lambda j, i, k: (i, j)))
        elif e.shape == (1, N):
            ex_specs.append(pl.BlockSpec((1, tn), lambda j, i, k: (0, j)))
        elif e.shape == (M, 1):
            ex_specs.append(pl.BlockSpec((tm, 1), lambda j, i, k: (i, 0)))
        else:
            raise ValueError((e.shape, M, N))
    ne, no = len(extras), len(out_dtypes)
    na = 1 if o_alias is not None else 0
    assert out_into is None or no == 1
    dims = (_DIMS[mode], ((), ()))

    def body(a_ref, b_ref, *rest):
        ex, outs = rest[:ne], rest[ne + na:ne + na + no]
        p = lax.dot_general(a_ref[...].astype(bf16), b_ref[...].astype(bf16), dims, preferred_element_type=f32)

        def finish(acc):
            res = epi(acc, *[e[...] for e in ex]) if epi is not None else (acc,)
            for o, r in zip(outs, res):
                o[...] = r.astype(o.dtype)

        if nk == 1:
            finish(p)
        else:
            acc_ref = rest[-1]
            k = pl.program_id(2)

            @pl.when(k == 0)
            def _():
                acc_ref[...] = p

            @pl.when(k > 0)
            def _():
                acc_ref[...] += p

            @pl.when(k == nk - 1)
            def _():
                finish(acc_ref[...])

    outs = pl.pallas_call(
        body,
        name=name,
        grid=(N // tn, M // tm, nk),
        in_specs=[a_spec, b_spec] + ex_specs + [pl.BlockSpec(memory_space=pl.ANY)] * na,
        out_specs=[pl.BlockSpec((tm, tn), lambda j, i, k: (i, j + ob)) for _ in out_dtypes],
        out_shape=[S((M, o_tot), dt) for dt in out_dtypes],
        scratch_shapes=[pltpu.VMEM((tm, tn), f32)] if nk > 1 else [],
        input_output_aliases={2 + ne: 0} if na else {},
        compiler_params=_cparams(("parallel", "parallel", "arbitrary")),
    )(a, barr, *extras, *([o_alias] if na else []))
    return outs[0] if no == 1 else tuple(outs)


def mm_bd(a, b, mode, name, ga, gb, extras=(), epi=None, out_dtypes=(f32,)):
    T = max(1, min(256 // min(ga, gb), 1024 // max(ga, gb)))
    if mode == "tn":
        K, M = a.shape
        N = b.shape[1]
        G = M // ga
        T = min(T, G)
        tm, tn, tk = T * ga, T * gb, _pick_k(K)
        nk = K // tk
        grid = (G // T, 1, nk)
        a_spec = pl.BlockSpec((tk, tm), lambda j, i, k: (k, j))
        b_spec = pl.BlockSpec((tk, tn), lambda j, i, k: (k, j))
        o_spec = pl.BlockSpec((tm, tn), lambda j, i, k: (j, 0))
        out_shape = (M, tn)
    else:
        M = a.shape[0]
        if mode == "nn":
            G = b.shape[0] // ga
            T = min(T, G)
            kw, tn, N = T * ga, T * gb, G * gb
            b_spec = pl.BlockSpec((kw, tn), lambda j, i, k: (j, j))
        else:
            G = b.shape[0] // ga
            T = min(T, G)
            kw, tn, N = T * gb, T * ga, G * ga
            b_spec = pl.BlockSpec((tn, kw), lambda j, i, k: (j, j))
        tm = _pick(M, (1024, 512, 256, 128, 64, 32, 16, 8))
        nk = 1
        grid = (G // T, M // tm, 1)
        a_spec = pl.BlockSpec((tm, kw), lambda j, i, k: (i, j))
        o_spec = pl.BlockSpec((tm, tn), lambda j, i, k: (i, j))
        out_shape = (M, N)
    ex_specs = []
    for e in extras:
        if e.shape == out_shape:
            ex_specs.append(o_spec)
        elif e.shape == (1, out_shape[1]):
            ex_specs.append(pl.BlockSpec((1, tn), lambda j, i, k: (0, j)))
        else:
            raise ValueError((e.shape, out_shape))
    ne, no = len(extras), len(out_dtypes)
    dims = (_DIMS[mode], ((), ()))

    def body(a_ref, b_ref, *rest):
        ex, outs = rest[:ne], rest[ne:ne + no]
        p = lax.dot_general(a_ref[...].astype(bf16), b_ref[...].astype(bf16), dims, preferred_element_type=f32)

        def finish(acc):
            res = epi(acc, *[e[...] for e in ex]) if epi is not None else (acc,)
            for o, r in zip(outs, res):
                o[...] = r.astype(o.dtype)

        if nk == 1:
            finish(p)
        else:
            acc_ref = rest[-1]
            k = pl.program_id(2)

            @pl.when(k == 0)
            def _():
                acc_ref[...] = p

            @pl.when(k > 0)
            def _():
                acc_ref[...] += p

            @pl.when(k == nk - 1)
            def _():
                finish(acc_ref[...])

    outs = pl.pallas_call(
        body, name=name, grid=grid, in_specs=[a_spec, b_spec] + ex_specs, out_specs=[o_spec] * no,
        out_shape=[S(out_shape, dt) for dt in out_dtypes],
        scratch_shapes=[pltpu.VMEM((tm, tn), f32)] if nk > 1 else [],
        compiler_params=_cparams(("parallel", "parallel", "arbitrary")),
    )(a, b, *extras)
    return outs[0] if no == 1 else tuple(outs)


def _diag_blocks(tiles, G, ga, gb):
    T = tiles.shape[1] // gb
    t5 = tiles.reshape(G // T, T, ga, T, gb)
    return jnp.sum(t5 * jnp.eye(T, dtype=tiles.dtype)[None, :, None, :, None], axis=3).reshape(G, ga, gb)


def _add(acc, prev):
    return (acc + prev,)


def rowk(name, fn, L, tl, ncol, ins, outs, accs=()):
    nrow = L // tl
    assert L % tl == 0 and tl % HALO == 0
    hb = tl // HALO

    def cw_of(c_total):
        assert c_total % ncol == 0, (name, c_total, ncol)
        return c_total // ncol

    in_specs = []
    for arr, kind in ins:
        if kind == "rows":
            in_specs.append(pl.BlockSpec((tl, cw_of(arr.shape[1])), lambda j, i: (i, j)))
        elif kind == "prev":
            in_specs.append(pl.BlockSpec((HALO, cw_of(arr.shape[1])), lambda j, i: (jnp.maximum(i * hb - 1, 0), j)))
        elif kind == "next":
            in_specs.append(pl.BlockSpec((HALO, cw_of(arr.shape[1])), lambda j, i: (jnp.minimum((i + 1) * hb, nrow * hb - 1), j)))
        elif kind == "cols":
            in_specs.append(pl.BlockSpec((arr.shape[0], cw_of(arr.shape[1])), lambda j, i: (0, j)))
        elif kind == "const":
            in_specs.append(pl.BlockSpec(arr.shape, lambda j, i: (0,) * arr.ndim))
        else:
            raise ValueError(kind)
    out_specs = [pl.BlockSpec((tl, cw_of(c)), lambda j, i: (i, j)) for c, _ in outs]
    out_shape = [S((L, c), dt) for c, dt in outs]
    out_specs += [pl.BlockSpec((r, cw_of(c)), lambda j, i: (0, j)) for r, c, _ in accs]
    out_shape += [S((r, c), dt) for r, c, dt in accs]
    ni, no, na = len(ins), len(outs), len(accs)

    def body(*refs):
        i = pl.program_id(1)
        res = fn(i, nrow, *[r[...] for r in refs[:ni]])
        for o, r in zip(refs[ni:ni + no], res[:no]):
            o[...] = r.astype(o.dtype)
        for o, r in zip(refs[ni + no:ni + no + na], res[no:]):
            @pl.when(i == 0)
            def _(o=o, r=r):
                o[...] = r.astype(o.dtype)

            @pl.when(i > 0)
            def _(o=o, r=r):
                o[...] += r.astype(o.dtype)

    res = pl.pallas_call(
        body,
        name=name,
        grid=(ncol, nrow),
        in_specs=in_specs,
        out_specs=out_specs,
        out_shape=out_shape,
        compiler_params=_cparams(("parallel", "arbitrary")),
    )(*[a for a, _ in ins])
    return tuple(res)


def _sigmoid(x):
    return 1.0 / (1.0 + jnp.exp(-x))


def _silu(x):
    return x * _sigmoid(x)


def _dsilu(x):
    s = _sigmoid(x)
    return s * (1.0 + x * (1.0 - s))


def _erf(x):
    a = jnp.abs(x)
    t = 1.0 / (1.0 + 0.3275911 * a)
    poly = t * (0.254829592 + t * (-0.284496736 + t * (1.421413741 + t * (-1.453152027 + t * 1.061405429))))
    y = 1.0 - poly * jnp.exp(-a * a)
    return jnp.where(x < 0, -y, y)


def _gelu(x):
    return 0.5 * x * (1.0 + _erf(x * (2.0 ** -0.5)))


def _dgelu(x):
    cdf = 0.5 * (1.0 + _erf(x * (2.0 ** -0.5)))
    pdf = jnp.exp(-0.5 * x * x) * (1.0 / math.sqrt(2.0 * math.pi))
    return cdf + x * pdf


def _rms(x, w):
    return x * lax.rsqrt(jnp.mean(x * x, axis=-1, keepdims=True) + NORM_EPS) * w


def _rms_bwd(x, w, dy):
    d = x.shape[-1]
    r = lax.rsqrt(jnp.mean(x * x, axis=-1, keepdims=True) + NORM_EPS)
    wdy = w * dy
    dx = r * wdy - x * (r * r * r) * (jnp.sum(x * wdy, axis=-1, keepdims=True) / d)
    dw = jnp.sum(x * r * dy, axis=0, keepdims=True)
    return dx, dw


def _from_cols(cols, width):
    tl = cols[0].shape[0]
    lane = lax.broadcasted_iota(jnp.int32, (tl, width), 1)
    out = jnp.zeros((tl, width), f32)
    for n, col in enumerate(cols):
        out = jnp.where(lane == n, col, out)
    return out


def _from_rows(rows):
    c = rows[0].shape[1]
    sub = lax.broadcasted_iota(jnp.int32, (len(rows), c), 0)
    out = jnp.zeros((len(rows), c), f32)
    for n, row in enumerate(rows):
        out = jnp.where(sub == n, row, out)
    return out


def _shift_down(x, halo, s, first):
    if s == 0:
        return x
    tl = x.shape[0]
    halo = jnp.where(first, 0.0, halo)
    xx = jnp.concatenate([halo, x], axis=0)
    return pltpu.roll(xx, s, 0)[HALO:HALO + tl]


def _shift_up(x, halo, s, last):
    if s == 0:
        return x
    tl = x.shape[0]
    halo = jnp.where(last, 0.0, halo)
    xx = jnp.concatenate([x, halo], axis=0)
    return pltpu.roll(xx, tl + HALO - s, 0)[0:tl]


def _causal_conv(x, halo, w, first):
    kw = w.shape[0]
    shifted = [_shift_down(x, halo, kw - 1 - j, first) for j in range(kw)]
    out = shifted[0] * w[0:1]
    for j in range(1, kw):
        out = out + shifted[j] * w[j:j + 1]
    return out, shifted


def rms_fwd(x, w, name):
    L, D = x.shape

    def fn(i, n, xb, wb):
        return (_rms(xb, wb),)

    return rowk(name, fn, L, _pick(L, (256, 128, 64, 32, 16, 8)), 1, [(x, "rows"), (w, "const")], [(D, bf16)])[0]


def rms_bwd(x, w, dh, dres, name):
    L, D = x.shape

    def fn(i, n, xb, wb, dhb, drb):
        dx, dw = _rms_bwd(xb, wb, dhb)
        dx = dx + drb
        return dx, dx, dw

    return rowk(name, fn, L, _pick(L, (256, 128, 64, 32, 16, 8)), 1,
                [(x, "rows"), (w, "const"), (dh, "rows"), (dres, "rows")], [(D, f32), (D, bf16)], [(1, D, f32)])


def loss_head(x, w, target, name):
    L, D = x.shape

    def fn(i, n, xb, wb, tb):
        err = _rms(xb, wb) - tb
        loss = 0.5 * jnp.sum(err * err) / D
        dx, dw = _rms_bwd(xb, wb, err / D)
        return dx, dx, jnp.full((8, 128), loss, f32), dw

    return rowk(name, fn, L, _pick(L, (256, 128, 64, 32, 16, 8)), 1,
                [(x, "rows"), (w, "const"), (target, "rows")], [(D, f32), (D, bf16)], [(8, 128, f32), (1, D, f32)])


def _s5_disc_math(lr, li, logdt, br, bi):
    dt = jnp.exp(logdt)
    mag = jnp.exp(lr * dt)
    ar, ai = mag * jnp.cos(li * dt), mag * jnp.sin(li * dt)
    den = lr * lr + li * li
    nr, ni = ar - 1.0, ai
    cr = (nr * lr + ni * li) / den
    ci = (ni * lr - nr * li) / den
    return ar, ai, cr * br - ci * bi, cr * bi + ci * br


def _disc_call(body, name, ins, out_widths):
    GP = ins[0].shape[0]
    tl = _pick(GP, (512, 256, 128, 64, 32, 16, 8))
    spec = lambda w: pl.BlockSpec((tl, w), lambda i: (i, 0))
    return pl.pallas_call(body, name=name, grid=(GP // tl,),
                          in_specs=[spec(a.shape[1]) for a in ins], out_specs=[spec(w) for w in out_widths],
                          out_shape=[S((GP, w), f32) for w in out_widths], compiler_params=_cparams(("parallel",)))(*ins)


def s5_disc(lr, li, logdt, br, bi, name):
    HG = br.shape[1]

    def body(lr_ref, li_ref, dt_ref, br_ref, bi_ref, ar_ref, ai_ref, bbr_ref, bbi_ref):
        ar, ai, bbr, bbi = _s5_disc_math(lr_ref[...], li_ref[...], dt_ref[...], br_ref[...], bi_ref[...])
        ar_ref[...], ai_ref[...], bbr_ref[...], bbi_ref[...] = ar, ai, bbr, bbi

    return _disc_call(body, name, [lr, li, logdt, br, bi], [1, 1, HG, HG])


def s5_disc_bwd(lr, li, logdt, br, bi, dar, dai, dbbr, dbbi, name):
    HG = br.shape[1]

    def body(lr_ref, li_ref, dt_ref, br_ref, bi_ref, dar_ref, dai_ref, dbbr_ref, dbbi_ref, *outs):
        _, vjp = jax.vjp(_s5_disc_math, lr_ref[...], li_ref[...], dt_ref[...], br_ref[...], bi_ref[...])
        for o, g in zip(outs, vjp((dar_ref[...], dai_ref[...], dbbr_ref[...], dbbi_ref[...]))):
            o[...] = g

    return _disc_call(body, name, [lr, li, logdt, br, bi, dar, dai, dbbr, dbbi], [1, 1, 1, HG, HG])


SCAN_TB = 256


def s5_scan_fwd(bur, bui, ar, ai, name):
    L, GP = bur.shape
    cw = _pick(GP, (1024, 512, 256, 128))
    tb = _pick(L, (SCAN_TB, 128, 64, 32, 16, 8))

    def body(bur_ref, bui_ref, ar_ref, ai_ref, xr_ref, xi_ref, cr_ref, ci_ref):
        @pl.when(pl.program_id(1) == 0)
        def _():
            cr_ref[...] = jnp.zeros_like(cr_ref)
            ci_ref[...] = jnp.zeros_like(ci_ref)

        a_r, a_i = ar_ref[...], ai_ref[...]

        def step(t, carry):
            xr, xi = carry
            row = pl.ds(t, 1)
            nr = a_r * xr - a_i * xi + bur_ref[row, :]
            ni = a_r * xi + a_i * xr + bui_ref[row, :]
            xr_ref[row, :] = nr
            xi_ref[row, :] = ni
            return nr, ni

        xr, xi = lax.fori_loop(0, tb, step, (cr_ref[...], ci_ref[...]), unroll=8)
        cr_ref[...] = xr
        ci_ref[...] = xi

    blk = pl.BlockSpec((tb, cw), lambda j, i: (i, j))
    vec = pl.BlockSpec((1, cw), lambda j, i: (0, j))
    return pl.pallas_call(
        body, name=name, grid=(GP // cw, L // tb),
        in_specs=[blk, blk, vec, vec], out_specs=[blk, blk],
        out_shape=[S((L, GP), f32), S((L, GP), f32)],
        scratch_shapes=[pltpu.VMEM((1, cw), f32), pltpu.VMEM((1, cw), f32)],
        compiler_params=_cparams(("parallel", "arbitrary")),
    )(bur, bui, ar, ai)


def s5_scan_bwd(gr, gi, xr, xi, ar, ai, name):
    L, GP = gr.shape
    cw = _pick(GP, (1024, 512, 256, 128))
    tb = _pick(L, (SCAN_TB, 128, 64, 32, 16, 8))
    nt = L // tb

    def body(gr_ref, gi_ref, xr_ref, xi_ref, ar_ref, ai_ref, gxr_ref, gxi_ref, dar_ref, dai_ref, cr_ref, ci_ref):
        @pl.when(pl.program_id(1) == 0)
        def _():
            cr_ref[...] = jnp.zeros_like(cr_ref)
            ci_ref[...] = jnp.zeros_like(ci_ref)
            dar_ref[...] = jnp.zeros_like(dar_ref)
            dai_ref[...] = jnp.zeros_like(dai_ref)

        a_r, a_i = ar_ref[...], ai_ref[...]

        def step(s, carry):
            cr, ci, dr, di = carry
            row = pl.ds(tb - 1 - s, 1)
            x_r, x_i = xr_ref[row, :], xi_ref[row, :]
            dr = dr + cr * x_r + ci * x_i
            di = di + ci * x_r - cr * x_i
            nr = gr_ref[row, :] + a_r * cr + a_i * ci
            ni = gi_ref[row, :] + a_r * ci - a_i * cr
            gxr_ref[row, :] = nr
            gxi_ref[row, :] = ni
            return nr, ni, dr, di

        cr, ci, dr, di = lax.fori_loop(0, tb, step, (cr_ref[...], ci_ref[...], dar_ref[...], dai_ref[...]), unroll=8)
        cr_ref[...] = cr
        ci_ref[...] = ci
        dar_ref[...] = dr
        dai_ref[...] = di

    blk = pl.BlockSpec((tb, cw), lambda j, i: (nt - 1 - i, j))
    vec = pl.BlockSpec((1, cw), lambda j, i: (0, j))
    return pl.pallas_call(
        body, name=name, grid=(GP // cw, nt),
        in_specs=[blk, blk, blk, blk, vec, vec], out_specs=[blk, blk, vec, vec],
        out_shape=[S((L, GP), f32), S((L, GP), f32), S((1, GP), f32), S((1, GP), f32)],
        scratch_shapes=[pltpu.VMEM((1, cw), f32), pltpu.VMEM((1, cw), f32)],
        compiler_params=_cparams(("parallel", "arbitrary")),
    )(gr, gi, xr, xi, ar, ai)


def _dn_heads_math(cq, ck, cv, braw, araw, alog, dtb, dk):
    q, k, v = _silu(cq), _silu(ck), _silu(cv)
    q = q * lax.rsqrt(jnp.sum(q * q, axis=-1, keepdims=True) + NORM_EPS) * (dk ** -0.5)
    k = k * lax.rsqrt(jnp.sum(k * k, axis=-1, keepdims=True) + NORM_EPS)
    beta = _sigmoid(braw)
    g = -jnp.exp(alog) * jax.nn.softplus(araw + dtb)
    return q, k, v, beta, g


def dn_prep(qkv, convw, ba, alog, dtb, H, name):
    L, W = qkv.shape
    hk = W // 3
    dk = hk // H

    def fn(i, n, xb, hb, wb, bab, alb, dtbb):
        c, _ = _causal_conv(xb, hb, wb, i == 0)
        qs, ks, vs, bs, gs = [], [], [], [], []
        for h in range(H):
            sl = lambda o: c[:, o + h * dk:o + (h + 1) * dk]
            q, k, v, beta, g = _dn_heads_math(sl(0), sl(hk), sl(2 * hk), bab[:, h:h + 1], bab[:, H + h:H + h + 1],
                                              alb[:, h:h + 1], dtbb[:, h:h + 1], dk)
            qs.append(q), ks.append(k), vs.append(v), bs.append(beta), gs.append(g)
        cat = lambda xs: jnp.concatenate(xs, axis=1)
        return cat(qs), cat(ks), cat(vs), _from_cols(bs + gs, 2 * H)

    return rowk(name, fn, L, _pick(L, (128, 64, 32, 16, 8)), 1,
                [(qkv, "rows"), (qkv, "prev"), (convw, "const"), (ba, "rows"), (alog, "const"), (dtb, "const")],
                [(hk, f32), (hk, f32), (hk, f32), (2 * H, f32)])


def dn_prep_bwd(qkv, convw, ba, alog, dtb, dq, dk_, dv, dbg, H, name):
    L, W = qkv.shape
    hk = W // 3
    dk = hk // H
    kw = convw.shape[0]
    nba = ba.shape[1]

    def fn(i, n, xb, hb, wb, bab, alb, dtbb, dqb, dkb, dvb, dbgb):
        c, shifted = _causal_conv(xb, hb, wb, i == 0)
        dcs = [None] * (3 * H)
        dbr, dar, dal, ddt = [], [], [], []
        for h in range(H):
            sl = lambda a, o: a[:, o + h * dk:o + (h + 1) * dk]
            args = (sl(c, 0), sl(c, hk), sl(c, 2 * hk), bab[:, h:h + 1], bab[:, H + h:H + h + 1],
                    alb[:, h:h + 1], dtbb[:, h:h + 1])
            _, vjp = jax.vjp(lambda *a: _dn_heads_math(*a, dk), *args)
            g = vjp((sl(dqb, 0), sl(dkb, 0), sl(dvb, 0), dbgb[:, h:h + 1], dbgb[:, H + h:H + h + 1]))
            dcs[h], dcs[H + h], dcs[2 * H + h] = g[0], g[1], g[2]
            dbr.append(g[3]), dar.append(g[4]), dal.append(g[5]), ddt.append(g[6])
        dc = jnp.concatenate(dcs, axis=1)
        dba = _from_cols(dbr + dar, nba)
        dw = _from_rows([jnp.sum(dc * shifted[j], axis=0, keepdims=True) for j in range(kw)])
        return dc, dba, dw, _from_cols(dal, H), _from_cols(ddt, H)

    return rowk(name, fn, L, _pick(L, (128, 64, 32, 16, 8)), 1,
                [(qkv, "rows"), (qkv, "prev"), (convw, "const"), (ba, "rows"), (alog, "const"), (dtb, "const"),
                 (dq, "rows"), (dk_, "rows"), (dv, "rows"), (dbg, "rows")],
                [(W, f32), (nba, bf16)], [(kw, W, f32), (1, H, f32), (1, H, f32)])


def conv_t(dc, w, name):
    L, C = dc.shape
    kw = w.shape[0]
    ncol = C // _pick(C, (1536, 1408, 1024, 768, 512, 256, 128))

    def fn(i, n, db, hb, wb):
        out = db * wb[kw - 1:kw]
        for j in range(kw - 1):
            out = out + _shift_up(db, hb, kw - 1 - j, i == n - 1) * wb[j:j + 1]
        return (out,)

    return rowk(name, fn, L, _pick(L, (256, 128, 64, 32, 16, 8)), ncol,
                [(dc, "rows"), (dc, "next"), (w, "cols")], [(C, bf16)])[0]


_BDIMS = {"nn": (((2,), (1,)), ((0,), (0,))), "nt": (((2,), (2,)), ((0,), (0,))), "tn": (((1,), (1,)), ((0,), (0,)))}


def _bdot(a, b, mode):
    return lax.dot_general(a.astype(bf16), b.astype(bf16), _BDIMS[mode], preferred_element_type=f32)


def _split16(a):
    hi = a.astype(bf16)
    return hi, (a - hi.astype(f32)).astype(bf16)


def _hdot(a, b, mode):
    ah, al = _split16(a)
    bh, bl = _split16(b)
    d = lambda x, y: lax.dot_general(x, y, _BDIMS[mode], preferred_element_type=f32)
    return d(ah, bh) + (d(ah, bl) + d(al, bh))


def _make_dot(raw):
    @functools.partial(jax.custom_vjp, nondiff_argnums=(2,))
    def dot(a, b, mode):
        return raw(a, b, mode)

    def fwd(a, b, mode):
        return raw(a, b, mode), (a, b)

    def bwd(mode, res, ct):
        a, b = res
        if mode == "nn":
            return raw(ct, b, "nt"), raw(a, ct, "tn")
        if mode == "nt":
            return raw(ct, b, "nn"), raw(ct, a, "tn")
        return raw(b, ct, "nt"), raw(a, ct, "nn")

    dot.defvjp(fwd, bwd)
    return dot


_dot16 = _make_dot(_bdot)
_dot32 = _make_dot(_hdot)


@jax.custom_vjp
def _unit_lower_inv(lmat):
    c = lmat.shape[-1]
    eye = (lax.broadcasted_iota(jnp.int32, (c, c), 0) == lax.broadcasted_iota(jnp.int32, (c, c), 1)).astype(f32)
    p = -lmat
    t = eye + p
    for _ in range(int(math.log2(c)) - 1):
        p = _hdot(p, p, "nn")
        t = t + _hdot(t, p, "nn")
    return t


def _uli_fwd(lmat):
    t = _unit_lower_inv(lmat)
    return t, t


def _uli_bwd(t, dt):
    return (-_hdot(_hdot(t, dt, "tn"), t, "nt"),)


_unit_lower_inv.defvjp(_uli_fwd, _uli_bwd)


def _dn_chunk_math(s_in, q, k, v, gcol, grow, bcol):
    c = q.shape[1]
    ri = lax.broadcasted_iota(jnp.int32, (c, c), 0)
    ci = lax.broadcasted_iota(jnp.int32, (c, c), 1)
    tril = (ri >= ci).astype(f32)
    strict = (ri > ci).astype(f32)
    gc_col = jnp.sum(tril * grow, axis=2, keepdims=True)
    gc_row = jnp.sum((1.0 - strict) * gcol, axis=1, keepdims=True)
    g_last = jnp.sum(gcol, axis=1, keepdims=True)
    decay = jnp.exp((gc_col - gc_row) * tril) * tril
    kb = k * bcol
    vb = v * bcol
    lmat = _dot16(kb, k, "nt") * decay * strict
    t = _unit_lower_inv(lmat)
    u = _dot32(t, vb, "nn")
    w = _dot32(t, kb * jnp.exp(gc_col), "nn")
    attn = _dot16(q, k, "nt") * decay
    v_new = u - _dot16(w, s_in, "nn")
    o = _dot16(q * jnp.exp(gc_col), s_in, "nn") + _dot16(attn, v_new, "nn")
    s_out = s_in * jnp.exp(g_last) + _dot16(k * jnp.exp(g_last - gc_col), v_new, "tn")
    return o, s_out


def _dn_load(q_ref, k_ref, v_ref, bg_ref, bgt_ref, H, dk):
    heads = lambda ref: jnp.stack([ref[:, h * dk:(h + 1) * dk] for h in range(H)])
    bgb, bgtb = bg_ref[...], bgt_ref[0]
    gcol = jnp.stack([bgb[:, H + h:H + h + 1] for h in range(H)])
    bcol = jnp.stack([bgb[:, h:h + 1] for h in range(H)])
    grow = jnp.stack([bgtb[H + h:H + h + 1, :] for h in range(H)])
    return heads(q_ref), heads(k_ref), heads(v_ref), gcol, grow, bcol


def dn_chunk_fwd(qn, kn, vv, bg, bgt, H, name):
    L, hk = qn.shape
    dk = hk // H
    c = DN_CHUNK
    nc = L // c

    def body(q_ref, k_ref, v_ref, bg_ref, bgt_ref, o_ref, ss_ref, s_ref):
        @pl.when(pl.program_id(0) == 0)
        def _():
            s_ref[...] = jnp.zeros_like(s_ref)

        s_in = s_ref[...]
        ss_ref[0] = s_in
        o, s_out = _dn_chunk_math(s_in, *_dn_load(q_ref, k_ref, v_ref, bg_ref, bgt_ref, H, dk))
        for h in range(H):
            o_ref[:, h * dk:(h + 1) * dk] = o[h]
        s_ref[...] = s_out

    row = lambda w: pl.BlockSpec((c, w), lambda n: (n, 0))
    return pl.pallas_call(
        body, name=name, grid=(nc,),
        in_specs=[row(hk), row(hk), row(hk), row(2 * H), pl.BlockSpec((1, 2 * H, c), lambda n: (n, 0, 0))],
        out_specs=[row(hk), pl.BlockSpec((1, H, dk, dk), lambda n: (n, 0, 0, 0))],
        out_shape=[S((L, hk), f32), S((nc, H, dk, dk), f32)],
        scratch_shapes=[pltpu.VMEM((H, dk, dk), f32)],
        compiler_params=_cparams(("arbitrary",)),
    )(qn, kn, vv, bg, bgt)


def dn_chunk_bwd(qn, kn, vv, bg, bgt, ss, do, H, name):
    L, hk = qn.shape
    dk = hk // H
    c = DN_CHUNK
    nc = L // c

    def body(q_ref, k_ref, v_ref, bg_ref, bgt_ref, ss_ref, do_ref, dq_ref, dk_ref, dv_ref, dbg_ref, dgt_ref, ds_ref):
        @pl.when(pl.program_id(0) == 0)
        def _():
            ds_ref[...] = jnp.zeros_like(ds_ref)

        args = (ss_ref[0],) + _dn_load(q_ref, k_ref, v_ref, bg_ref, bgt_ref, H, dk)
        _, vjp = jax.vjp(_dn_chunk_math, *args)
        do = jnp.stack([do_ref[:, h * dk:(h + 1) * dk] for h in range(H)])
        ds, dq, dkk, dv, dgcol, dgrow, dbcol = vjp((do, ds_ref[...]))
        ds_ref[...] = ds
        for h in range(H):
            sl = slice(h * dk, (h + 1) * dk)
            dq_ref[:, sl], dk_ref[:, sl], dv_ref[:, sl] = dq[h], dkk[h], dv[h]
        dbg_ref[...] = _from_cols([dbcol[h] for h in range(H)] + [dgcol[h] for h in range(H)], 2 * H)
        dgt_ref[0] = _from_rows([dgrow[h] for h in range(H)])

    row = lambda w: pl.BlockSpec((c, w), lambda n: (nc - 1 - n, 0))
    return pl.pallas_call(
        body, name=name, grid=(nc,),
        in_specs=[row(hk), row(hk), row(hk), row(2 * H), pl.BlockSpec((1, 2 * H, c), lambda n: (nc - 1 - n, 0, 0)),
                  pl.BlockSpec((1, H, dk, dk), lambda n: (nc - 1 - n, 0, 0, 0)), row(hk)],
        out_specs=[row(hk), row(hk), row(hk), row(2 * H), pl.BlockSpec((1, H, c), lambda n: (nc - 1 - n, 0, 0))],
        out_shape=[S((L, hk), f32), S((L, hk), f32), S((L, hk), f32), S((L, 2 * H), f32), S((nc, H, c), f32)],
        scratch_shapes=[pltpu.VMEM((H, dk, dk), f32)],
        compiler_params=_cparams(("arbitrary",)),
    )(qn, kn, vv, bg, bgt, ss, do)


def _dn_gate_math(o, z, w):
    return _rms(o, w) * _silu(z)


def dn_gate(o, z, w, H, name):
    L, hv = o.shape
    dv = hv // H

    def fn(i, n, ob, zb, wb):
        return (jnp.concatenate([_dn_gate_math(ob[:, h * dv:(h + 1) * dv], zb[:, h * dv:(h + 1) * dv], wb)
                                 for h in range(H)], axis=1),)

    return rowk(name, fn, L, _pick(L, (256, 128, 64, 32, 16, 8)), 1, [(o, "rows"), (z, "rows"), (w, "const")], [(hv, bf16)])[0]


def dn_gate_bwd(o, z, w, dy, H, name):
    L, hv = o.shape
    dv = hv // H

    def fn(i, n, ob, zb, wb, dyb):
        dos, dzs, dw = [], [], 0.0
        for h in range(H):
            sl = slice(h * dv, (h + 1) * dv)
            _, vjp = jax.vjp(_dn_gate_math, ob[:, sl], zb[:, sl], wb)
            a, b, c = vjp(dyb[:, sl])
            dos.append(a), dzs.append(b)
            dw = dw + c
        return jnp.concatenate(dos, axis=1), jnp.concatenate(dzs, axis=1), dw

    return rowk(name, fn, L, _pick(L, (256, 128, 64, 32, 16, 8)), 1,
                [(o, "rows"), (z, "rows"), (w, "const"), (dy, "rows")], [(hv, f32), (hv, bf16)], [(1, dv, f32)])


def ffn_mid(ua, uv, wa, wv, name):
    L, F = ua.shape
    ncol = F // _pick(F, (1408, 1024, 512, 256, 128))

    def fn(i, n, ab, ah, vb, vh, wab, wvb):
        ca, _ = _causal_conv(ab, ah, wab, i == 0)
        cv, _ = _causal_conv(vb, vh, wvb, i == 0)
        return (_silu(ca) * cv,)

    return rowk(name, fn, L, _pick(L, (256, 128, 64, 32, 16, 8)), ncol,
                [(ua, "rows"), (ua, "prev"), (uv, "rows"), (uv, "prev"), (wa, "cols"), (wv, "cols")], [(F, bf16)])[0]


def ffn_mid_bwd(ua, uv, wa, wv, dh, name):
    L, F = ua.shape
    kw = wa.shape[0]
    ncol = F // _pick(F, (1408, 1024, 512, 256, 128))

    def fn(i, n, ab, ah, vb, vh, wab, wvb, dhb):
        ca, sa = _causal_conv(ab, ah, wab, i == 0)
        cv, sv = _causal_conv(vb, vh, wvb, i == 0)
        dca = dhb * cv * _dsilu(ca)
        dcv = dhb * _silu(ca)
        dwa = _from_rows([jnp.sum(dca * sa[j], axis=0, keepdims=True) for j in range(kw)])
        dwv = _from_rows([jnp.sum(dcv * sv[j], axis=0, keepdims=True) for j in range(kw)])
        return dca, dcv, dwa, dwv

    return rowk(name, fn, L, _pick(L, (256, 128, 64, 32, 16, 8)), ncol,
                [(ua, "rows"), (ua, "prev"), (uv, "rows"), (uv, "prev"), (wa, "cols"), (wv, "cols"), (dh, "rows")],
                [(F, f32), (F, f32)], [(kw, F, f32), (kw, F, f32)])


def _merge_epi(acc, gs, gd, ga, gb):
    return acc, _sigmoid(gs) * ga * _sigmoid(gb) + _sigmoid(gd) * acc


def merge_bwd(dm, gs, gd, ga, gb, brdn, name):
    L, D = dm.shape
    ncol = D // _pick(D, (1024, 512, 256, 128))

    def fn(i, n, dmb, gsb, gdb, gab, gbb, brb):
        ss, sd, sb = _sigmoid(gsb), _sigmoid(gdb), _sigmoid(gbb)
        br_s5 = gab * sb
        dbr_s5 = dmb * ss
        return (dmb * br_s5 * ss * (1.0 - ss), dmb * brb * sd * (1.0 - sd), dbr_s5 * sb,
                dbr_s5 * gab * sb * (1.0 - sb), dmb * sd)

    return rowk(name, fn, L, _pick(L, (256, 128, 64, 32, 16, 8)), ncol,
                [(a, "rows") for a in (dm, gs, gd, ga, gb, brdn)], [(D, bf16)] * 5)


def s5_out_bwd(dy, ypre, u, d, name):
    L, W = dy.shape

    def fn(i, n, dyb, yb, ub, db):
        dyp = dyb * _dgelu(yb)
        return dyp, db * dyp, jnp.sum(dyp * ub, axis=0, keepdims=True)

    return rowk(name, fn, L, _pick(L, (256, 128, 64, 32, 16, 8)), 1,
                [(dy, "rows"), (ypre, "rows"), (u, "rows"), (d, "const")], [(W, bf16), (W, f32)], [(1, W, f32)])


def _s5_y_epi(acc, y1, u, d):
    ypre = acc + y1 + d * u
    return ypre, _gelu(ypre)


def adamw(w, g, m, v, name):
    R, C = w.shape
    tl = _pick(R, (256, 128, 64, 32, 16, 8))
    if R * C * 4 <= 2 * 1024 * 1024:
        tl = R

    def body(w_ref, g_ref, m_ref, v_ref, d_ref, nm_ref, nv_ref):
        gg = g_ref[...]
        nm = ADAM_B1 * m_ref[...] + (1.0 - ADAM_B1) * gg
        nv = ADAM_B2 * v_ref[...] + (1.0 - ADAM_B2) * (gg * gg)
        m_hat = nm / (1.0 - ADAM_B1 ** ADAM_STEP)
        v_hat = nv / (1.0 - ADAM_B2 ** ADAM_STEP)
        d_ref[...] = -ADAM_LR * (m_hat / (jnp.sqrt(v_hat) + ADAM_EPS) + ADAM_WD * w_ref[...])
        nm_ref[...] = nm
        nv_ref[...] = nv

    blk = pl.BlockSpec((tl, C), lambda i: (i, 0))
    return pl.pallas_call(body, name=name, grid=(R // tl,), in_specs=[blk] * 4, out_specs=[blk] * 3,
                          out_shape=[S((R, C), f32)] * 3, compiler_params=_cparams(("parallel",)))(w, g, m, v)


def sum_slots(x, name, out_dtype=f32):
    n, R, C = x.shape
    tl = _pick(R, (512, 256, 128, 64, 32, 16, 8))

    def body(x_ref, o_ref):
        acc = x_ref[0].astype(f32)
        for s in range(1, n):
            acc = acc + x_ref[s].astype(f32)
        o_ref[...] = acc.astype(o_ref.dtype)

    return pl.pallas_call(body, name=name, grid=(R // tl,),
                          in_specs=[pl.BlockSpec((n, tl, C), lambda i: (0, i, 0))],
                          out_specs=pl.BlockSpec((tl, C), lambda i: (i, 0)),
                          out_shape=S((R, C), out_dtype), compiler_params=_cparams(("parallel",)))(x)


_ANY = pl.BlockSpec(memory_space=pl.ANY)


def _coords():
    return lax.axis_index("x"), lax.axis_index("y"), lax.axis_index("c")


def chip_exchange(src, name, same=False):
    out_shape = (N_CHIPS,) + src.shape if same else src.shape
    assert out_shape[0] == N_CHIPS

    def body(src_ref, out_ref, send_sems, recv_sems, local_sem):
        x, y, c = _coords()
        me = 2 * x + y
        slot = (lambda j: src_ref) if same else (lambda j: src_ref.at[j])
        mine = pltpu.make_async_copy(slot(me), out_ref.at[me], local_sem)
        mine.start()
        peers = [(1 - x, y), (x, 1 - y), (1 - x, 1 - y)]
        copies = []
        for k, (px, py) in enumerate(peers):
            cp = pltpu.make_async_remote_copy(
                src_ref=slot(2 * px + py), dst_ref=out_ref.at[me],
                send_sem=send_sems.at[k], recv_sem=recv_sems.at[k],
                device_id=(px, py, c), device_id_type=MESH)
            cp.start()
            copies.append(cp)
        for k, (px, py) in enumerate(peers):
            pltpu.make_async_remote_copy(
                src_ref=slot(me), dst_ref=out_ref.at[2 * px + py],
                send_sem=send_sems.at[k], recv_sem=recv_sems.at[k],
                device_id=(px, py, c), device_id_type=MESH).wait_recv()
        for cp in copies:
            cp.wait_send()
        mine.wait()

    return pl.pallas_call(
        body, name=name, in_specs=[_ANY], out_specs=_ANY, out_shape=S(out_shape, src.dtype),
        scratch_shapes=[pltpu.SemaphoreType.DMA((3,)), pltpu.SemaphoreType.DMA((3,)), pltpu.SemaphoreType.DMA],
    )(src)


def sibling_exchange(src, name):
    def body(src_ref, out_ref, send_sem, recv_sem):
        x, y, c = _coords()
        cp = pltpu.make_async_remote_copy(src_ref=src_ref, dst_ref=out_ref, send_sem=send_sem, recv_sem=recv_sem,
                                          device_id=(x, y, 1 - c), device_id_type=MESH)
        cp.start()
        cp.wait()

    return pl.pallas_call(
        body, name=name, in_specs=[_ANY], out_specs=_ANY, out_shape=S(src.shape, src.dtype),
        scratch_shapes=[pltpu.SemaphoreType.DMA, pltpu.SemaphoreType.DMA],
    )(src)


def _rows(c, half_rows):
    return pl.ds(pl.multiple_of(c * half_rows, 16), half_rows)


def _slot(ref, kind, j, rows=None):
    if kind == "slot":
        return ref.at[j] if rows is None else ref.at[j, rows]
    cs = ref.shape[1] // N_CHIPS
    cols = pl.ds(pl.multiple_of(j * cs, 128), cs)
    return ref.at[:, cols] if rows is None else ref.at[rows, cols]


def _sems(n):
    return [pltpu.SemaphoreType.DMA((n,))]


def ag_layer(shards, kinds, name):
    n = len(shards)
    out_shape = [S((N_CHIPS,) + a.shape if k == "slot" else (a.shape[0], N_CHIPS * a.shape[1]), a.dtype)
                 for a, k in zip(shards, kinds)]

    def body(*refs):
        srcs, outs = refs[:n], refs[n:2 * n]
        send1, recv1, send2, recv2, lsem = refs[2 * n:]
        x, y, c = _coords()
        me = 2 * x + y
        peers = [(1 - x, y), (x, 1 - y), (1 - x, 1 - y)]
        rh = [a.shape[0] // 2 for a in shards]
        started = []
        for t in range(n):
            cp = pltpu.make_async_copy(srcs[t], _slot(outs[t], kinds[t], me), lsem.at[t])
            cp.start()
            started.append(cp)
        sends = []
        for t in range(n):
            for k, (px, py) in enumerate(peers):
                cp = pltpu.make_async_remote_copy(
                    src_ref=srcs[t].at[_rows(c, rh[t])], dst_ref=_slot(outs[t], kinds[t], me, _rows(c, rh[t])),
                    send_sem=send1.at[3 * t + k], recv_sem=recv1.at[3 * t + k], device_id=(px, py, c), device_id_type=MESH)
                cp.start()
                sends.append(cp)
        for k, (px, py) in enumerate(peers):
            for t in range(n):
                landed = _slot(outs[t], kinds[t], 2 * px + py, _rows(c, rh[t]))
                pltpu.make_async_remote_copy(src_ref=landed, dst_ref=landed, send_sem=send1.at[3 * t + k],
                                             recv_sem=recv1.at[3 * t + k], device_id=(px, py, c), device_id_type=MESH).wait_recv()
                cp = pltpu.make_async_remote_copy(src_ref=landed, dst_ref=landed, send_sem=send2.at[3 * t + k],
                                                  recv_sem=recv2.at[3 * t + k], device_id=(x, y, 1 - c), device_id_type=MESH)
                cp.start()
                sends.append(cp)
        for k, (px, py) in enumerate(peers):
            for t in range(n):
                other = _slot(outs[t], kinds[t], 2 * px + py, _rows(1 - c, rh[t]))
                pltpu.make_async_remote_copy(src_ref=other, dst_ref=other, send_sem=send2.at[3 * t + k],
                                             recv_sem=recv2.at[3 * t + k], device_id=(x, y, 1 - c), device_id_type=MESH).wait_recv()
        for cp in sends:
            cp.wait_send()
        for cp in started:
            cp.wait()

    return pl.pallas_call(body, name=name, in_specs=[_ANY] * n, out_specs=[_ANY] * n, out_shape=out_shape,
                          scratch_shapes=_sems(3 * n) * 4 + _sems(n))(*shards)


def rs_pair(grads, kinds, name):
    n = len(grads)

    def half_shape(a, k):
        return (a.shape[0], a.shape[1] // 2, a.shape[2]) if k == "slot" else (a.shape[0] // 2, a.shape[1])

    def half(ref, k, c):
        return ref.at[:, _rows(c, ref.shape[1] // 2)] if k == "slot" else ref.at[_rows(c, ref.shape[0] // 2)]

    shapes = [S(half_shape(a, k), a.dtype) for a, k in zip(grads, kinds)]

    def body(*refs):
        srcs, mine, got = refs[:n], refs[n:2 * n], refs[2 * n:3 * n]
        send, recv, lsem = refs[3 * n:]
        x, y, c = _coords()
        started = []
        for t in range(n):
            cp = pltpu.make_async_copy(half(srcs[t], kinds[t], c), mine[t], lsem.at[t])
            cp.start()
            started.append(cp)
            cp = pltpu.make_async_remote_copy(src_ref=half(srcs[t], kinds[t], 1 - c), dst_ref=got[t], send_sem=send.at[t],
                                              recv_sem=recv.at[t], device_id=(x, y, 1 - c), device_id_type=MESH)
            cp.start()
            started.append(cp)
        for cp in started:
            cp.wait()

    res = pl.pallas_call(body, name=name, in_specs=[_ANY] * n, out_specs=[_ANY] * (2 * n), out_shape=shapes + shapes,
                         scratch_shapes=_sems(n) * 3)(*grads)
    return res[:n], res[n:]


def rs_chips(sums, kinds, name):
    n = len(sums)

    def part_shape(a, k):
        return a.shape[1:] if k == "slot" else (a.shape[0], a.shape[1] // N_CHIPS)

    shapes = [S((N_CHIPS,) + part_shape(a, k), a.dtype) for a, k in zip(sums, kinds)]

    def body(*refs):
        srcs, outs = refs[:n], refs[n:2 * n]
        send, recv, lsem = refs[2 * n:]
        x, y, c = _coords()
        me = 2 * x + y
        peers = [(1 - x, y), (x, 1 - y), (1 - x, 1 - y)]
        started = []
        for t in range(n):
            cp = pltpu.make_async_copy(_slot(srcs[t], kinds[t], me), outs[t].at[me], lsem.at[t])
            cp.start()
            started.append(cp)
        sends = []
        for t in range(n):
            for k, (px, py) in enumerate(peers):
                cp = pltpu.make_async_remote_copy(
                    src_ref=_slot(srcs[t], kinds[t], 2 * px + py), dst_ref=outs[t].at[me],
                    send_sem=send.at[3 * t + k], recv_sem=recv.at[3 * t + k], device_id=(px, py, c), device_id_type=MESH)
                cp.start()
                sends.append(cp)
        for t in range(n):
            for k, (px, py) in enumerate(peers):
                landed = outs[t].at[2 * px + py]
                pltpu.make_async_remote_copy(src_ref=landed, dst_ref=landed, send_sem=send.at[3 * t + k],
                                             recv_sem=recv.at[3 * t + k], device_id=(px, py, c), device_id_type=MESH).wait_recv()
        for cp in sends:
            cp.wait_send()
        for cp in started:
            cp.wait()

    return pl.pallas_call(body, name=name, in_specs=[_ANY] * n, out_specs=[_ANY] * n, out_shape=shapes,
                          scratch_shapes=_sems(3 * n) * 2 + _sems(n))(*sums)


def rs_share(halves, name):
    n = len(halves)
    shapes = [S((2 * a.shape[0], a.shape[1]), a.dtype) for a in halves]

    def body(*refs):
        srcs, outs = refs[:n], refs[n:2 * n]
        send, recv, lsem = refs[2 * n:]
        x, y, c = _coords()
        started = []
        for t in range(n):
            rh = halves[t].shape[0]
            cp = pltpu.make_async_copy(srcs[t], outs[t].at[_rows(c, rh)], lsem.at[t])
            cp.start()
            started.append(cp)
            cp = pltpu.make_async_remote_copy(src_ref=srcs[t], dst_ref=outs[t].at[_rows(c, rh)], send_sem=send.at[t],
                                              recv_sem=recv.at[t], device_id=(x, y, 1 - c), device_id_type=MESH)
            cp.start()
            started.append(cp)
        for cp in started:
            cp.wait()

    return pl.pallas_call(body, name=name, in_specs=[_ANY] * n, out_specs=[_ANY] * n, out_shape=shapes,
                          scratch_shapes=_sems(n) * 3)(*halves)


def add_pair(a, b, name):
    shape = a.shape
    a2, b2 = a.reshape(-1, shape[-1]), b.reshape(-1, shape[-1])
    R, C = a2.shape
    tl = _pick(R, (256, 128, 64, 32, 16, 8))

    def body(a_ref, b_ref, o_ref):
        o_ref[...] = (a_ref[...] + b_ref[...]).astype(o_ref.dtype)

    blk = pl.BlockSpec((tl, C), lambda i: (i, 0))
    out = pl.pallas_call(body, name=name, grid=(R // tl,), in_specs=[blk, blk], out_specs=blk,
                         out_shape=S((R, C), bf16), compiler_params=_cparams(("parallel",)))(a2, b2)
    return out.reshape(shape)


def _block_diag(blocks):
    G, a, b = blocks.shape
    eye = jnp.eye(G, dtype=blocks.dtype)
    return (blocks[:, :, None, :] * eye[:, None, :, None]).reshape(G * a, G * b)


def _flat_pack(arrs, dtype, lanes=1024, row_mult=16):
    flat = jnp.concatenate([a.reshape(-1).astype(dtype) for a in arrs])
    n = flat.shape[0]
    per = lanes * row_mult
    pad = (-n) % per
    if pad:
        flat = jnp.concatenate([flat, jnp.zeros((pad,), dtype)])
    return flat.reshape(-1, lanes)


def _flat_unpack(buf, shapes):
    flat = buf.reshape(-1)
    out, off = [], 0
    for s in shapes:
        n = math.prod(s)
        out.append(flat[off:off + n].reshape(s))
        off += n
    return out


def _layer_weights(p, H):
    w_in = p["w_in"]
    D = w_in.shape[0]
    s5w = p["s5_d"].shape[-1]
    hk = p["dn_proj_w"].shape[0]
    off_z = s5w + 4 * hk
    off_a = off_z + 2 * H
    wp = jnp.concatenate([w_in[:, :off_z], w_in[:, off_a:], jnp.pad(w_in[:, off_z:off_a], ((0, 0), (0, 128 - 2 * H)))], axis=1)
    q = {}
    q["Wu"] = Win(wp, 0, s5w)
    q["Wqkv"] = Win(wp, s5w, 3 * hk)
    q["Wz"] = Win(wp, s5w + 3 * hk, hk)
    q["Wgs"] = Win(wp, off_z, D)
    q["Wgd"] = Win(wp, off_z + D, D)
    q["Wba"] = Win(wp, off_z + 2 * D, 128)
    q["Wga"], q["Wgb"] = Win(p["s5_glu_w"], 0, D), Win(p["s5_glu_w"], D, D)
    F = p["ffn_down"].shape[0]
    q["Wupa"], q["Wupv"] = Win(p["ffn_up"], 0, F), Win(p["ffn_up"], F, F)
    q["cwa"], q["cwv"] = p["ffn_conv_w"][:, :F], p["ffn_conv_w"][:, F:]
    for k in ("dn_proj_w", "w_out", "ffn_down", "dn_conv_w", "mix_norm_w", "ffn_norm_w", "dn_norm_w",
              "dn_a_log", "dn_dt_bias", "s5_d"):
        q[k] = p[k]
    return q


def _s5_params(p, tag):
    G, P = p["s5_a_re"].shape
    HG = p["s5_b_re"].shape[-1]
    col = lambda a: a.reshape(G * P, 1)
    lr, li = col(p["s5_a_re"]), col(p["s5_a_im"])
    logdt = col(jnp.broadcast_to(p["s5_log_dt"][:, None], (G, P)))
    br, bi = p["s5_b_re"].reshape(G * P, HG), p["s5_b_im"].reshape(G * P, HG)
    ar, ai, bbr, bbi = s5_disc(lr, li, logdt, br, bi, "s5_disc")
    bd = lambda m: _block_diag(m.reshape(G, P, HG).transpose(0, 2, 1)).astype(bf16)
    cd = lambda m: _block_diag(m.transpose(0, 2, 1)).astype(bf16)
    return dict(lr=lr, li=li, logdt=logdt, br=br, bi=bi, ar=ar.reshape(1, G * P), ai=ai.reshape(1, G * P),
                Bre=bd(bbr), Bim=bd(bbi), CreT=cd(p["s5_c_re"]), mCimT=cd(-p["s5_c_im"]), G=G, P=P, HG=HG)


def layer_fwd(x, q, s5, H):
    r = {"x": x}
    h1 = rms_fwd(x, q["mix_norm_w"], "rms_mix")
    r["h1"] = h1
    u32, u16 = mm(h1, q["Wu"], "nn", "proj_u", out_dtypes=(f32, bf16), epi=lambda a: (a, a))
    qkv = mm(h1, q["Wqkv"], "nn", "proj_qkv")
    z = mm(h1, q["Wz"], "nn", "proj_z")
    ba = mm(h1, q["Wba"], "nn", "proj_ba")
    gs = mm(h1, q["Wgs"], "nn", "proj_gs")
    gd = mm(h1, q["Wgd"], "nn", "proj_gd")
    r.update(u32=u32, u16=u16, qkv=qkv, z=z, ba=ba, gs=gs, gd=gd)
    HG, P = s5["HG"], s5["P"]
    bur = mm_bd(u16, s5["Bre"], "nn", "s5_bu_re", HG, P)
    bui = mm_bd(u16, s5["Bim"], "nn", "s5_bu_im", HG, P)
    xr, xi = s5_scan_fwd(bur, bui, s5["ar"], s5["ai"], "s5_scan_fwd")
    y1 = mm_bd(xr, s5["CreT"], "nn", "s5_y_re", P, HG)
    ypre, ys5 = mm_bd(xi, s5["mCimT"], "nn", "s5_y_im", P, HG, extras=(y1, u32, q["s5_d"]), epi=_s5_y_epi, out_dtypes=(f32, bf16))
    ga = mm(ys5, q["Wga"], "nn", "glu_a")
    gb = mm(ys5, q["Wgb"], "nn", "glu_b")
    r.update(xr=xr, xi=xi, ypre=ypre, ys5=ys5, ga=ga, gb=gb)
    qn, kn, vv, bg = dn_prep(qkv, q["dn_conv_w"], ba, q["dn_a_log"], q["dn_dt_bias"], H, "dn_prep")
    L = x.shape[0]
    bgt = bg.reshape(L // DN_CHUNK, DN_CHUNK, 2 * H).transpose(0, 2, 1)
    o, ss = dn_chunk_fwd(qn, kn, vv, bg, bgt, H, "dn_chunk_fwd")
    ydn = dn_gate(o, z, q["dn_norm_w"], H, "dn_gate")
    brdn, merged = mm(ydn, q["dn_proj_w"], "nn", "dn_proj", extras=(gs, gd, ga, gb), epi=_merge_epi, out_dtypes=(f32, bf16))
    r.update(qn=qn, kn=kn, vv=vv, bg=bg, bgt=bgt, ss=ss, o=o, ydn=ydn, brdn=brdn, merged=merged)
    x1 = mm(merged, q["w_out"], "nn", "out_proj", extras=(x,), epi=_add)
    h2 = rms_fwd(x1, q["ffn_norm_w"], "rms_ffn")
    ua = mm(h2, q["Wupa"], "nn", "ffn_up_a")
    uv = mm(h2, q["Wupv"], "nn", "ffn_up_v")
    hmid = ffn_mid(ua, uv, q["cwa"], q["cwv"], "ffn_mid")
    x2 = mm(hmid, q["ffn_down"], "nn", "ffn_down", extras=(x1,), epi=_add)
    r.update(x1=x1, h2=h2, ua=ua, uv=uv, hmid=hmid)
    return x2, r


def layer_bwd(dx2, dx2b, r, q, s5, H):
    g = {}
    dhmid = mm(dx2b, q["ffn_down"], "nt", "d_hmid")
    g["ffn_down"] = mm(r["hmid"], dx2b, "tn", "dw_ffn_down")
    dca, dcv, dwa, dwv = ffn_mid_bwd(r["ua"], r["uv"], q["cwa"], q["cwv"], dhmid, "ffn_mid_bwd")
    g["ffn_conv_w"] = jnp.concatenate([dwa, dwv], axis=1)
    dua = conv_t(dca, q["cwa"], "ffn_conv_t_a")
    duv = conv_t(dcv, q["cwv"], "ffn_conv_t_v")
    dh2 = mm(dua, q["Wupa"], "nt", "d_h2_a")
    dh2 = mm(duv, q["Wupv"], "nt", "d_h2_v", extras=(dh2,), epi=_add)
    F = dua.shape[1]
    dwup = mm(r["h2"], dua, "tn", "dw_up_a", out_into=(2 * F, 0, None))
    g["ffn_up"] = mm(r["h2"], duv, "tn", "dw_up_v", out_into=(2 * F, F, dwup))
    dx1, dx1b, dffn_w = rms_bwd(r["x1"], q["ffn_norm_w"], dh2, dx2, "rms_ffn_bwd")
    g["ffn_norm_w"] = dffn_w[0]
    dm = mm(dx1b, q["w_out"], "nt", "d_merged")
    g["w_out"] = mm(r["merged"], dx1b, "tn", "dw_out")
    dgs, dgd, dga, dgb, dbrdn = merge_bwd(dm, r["gs"], r["gd"], r["ga"], r["gb"], r["brdn"], "merge_bwd")
    dydn = mm(dbrdn, q["dn_proj_w"], "nt", "d_ydn")
    g["dn_proj_w"] = mm(r["ydn"], dbrdn, "tn", "dw_dn_proj")
    do, dz, dnw = dn_gate_bwd(r["o"], r["z"], q["dn_norm_w"], dydn, H, "dn_gate_bwd")
    g["dn_norm_w"] = dnw[0]
    dq, dk, dv, dbg, dgt = dn_chunk_bwd(r["qn"], r["kn"], r["vv"], r["bg"], r["bgt"], r["ss"], do, H, "dn_chunk_bwd")
    L = dq.shape[0]
    dbg = dbg + jnp.concatenate([jnp.zeros((L, H), f32), dgt.transpose(0, 2, 1).reshape(L, H)], axis=1)
    dc, dba, dcw, dal, ddt = dn_prep_bwd(r["qkv"], q["dn_conv_w"], r["ba"], q["dn_a_log"], q["dn_dt_bias"],
                                         dq, dk, dv, dbg, H, "dn_prep_bwd")
    g["dn_conv_w"], g["dn_a_log"], g["dn_dt_bias"] = dcw, dal[0], ddt[0]
    dqkv = conv_t(dc, q["dn_conv_w"], "dn_conv_t")
    dys5 = mm(dga, q["Wga"], "nt", "d_ys5_a")
    dys5 = mm(dgb, q["Wgb"], "nt", "d_ys5_b", extras=(dys5,), epi=_add)
    D = dga.shape[1]
    dwglu = mm(r["ys5"], dga, "tn", "dw_glu_a", out_into=(2 * D, 0, None))
    g["s5_glu_w"] = mm(r["ys5"], dgb, "tn", "dw_glu_b", out_into=(2 * D, D, dwglu))
    dyp, du_direct, dd = s5_out_bwd(dys5, r["ypre"], r["u32"], q["s5_d"], "s5_out_bwd")
    g["s5_d"] = dd[0]
    G, P, HG = s5["G"], s5["P"], s5["HG"]
    gdr = mm_bd(dyp, s5["CreT"], "nt", "s5_gx_re", P, HG)
    gdi = mm_bd(dyp, s5["mCimT"], "nt", "s5_gx_im", P, HG)
    dcre = _diag_blocks(mm_bd(r["xr"], dyp, "tn", "dw_s5_c_re", P, HG), G, P, HG)
    dcim = _diag_blocks(mm_bd(r["xi"], dyp, "tn", "dw_s5_c_im", P, HG), G, P, HG)
    g["s5_c_re"], g["s5_c_im"] = dcre.transpose(0, 2, 1), -dcim.transpose(0, 2, 1)
    gxr, gxi, dar, dai = s5_scan_bwd(gdr, gdi, r["xr"], r["xi"], s5["ar"], s5["ai"], "s5_scan_bwd")
    dbre = _diag_blocks(mm_bd(r["u16"], gxr, "tn", "dw_s5_b_re", HG, P), G, HG, P)
    dbim = _diag_blocks(mm_bd(r["u16"], gxi, "tn", "dw_s5_b_im", HG, P), G, HG, P)
    tocol = lambda m: m.transpose(0, 2, 1).reshape(G * P, HG)
    dlr, dli, dlogdt, dbr, dbi = s5_disc_bwd(s5["lr"], s5["li"], s5["logdt"], s5["br"], s5["bi"],
                                             dar.reshape(G * P, 1), dai.reshape(G * P, 1), tocol(dbre), tocol(dbim), "s5_disc_bwd")
    g["s5_a_re"], g["s5_a_im"] = dlr.reshape(G, P), dli.reshape(G, P)
    g["s5_log_dt"] = jnp.sum(dlogdt.reshape(G, P), axis=1)
    g["s5_b_re"], g["s5_b_im"] = dbr.reshape(G, P, HG), dbi.reshape(G, P, HG)
    du = mm_bd(gxr, s5["Bre"], "nt", "d_u_re", HG, P)
    du = mm_bd(gxi, s5["Bim"], "nt", "d_u_im", HG, P, extras=(du, du_direct), epi=lambda a, b, c: (a + b + c,), out_dtypes=(bf16,))
    h1 = r["h1"]
    segs = [("Wu", du), ("Wqkv", dqkv), ("Wz", dz), ("Wba", dba), ("Wgs", dgs), ("Wgd", dgd)]
    dh1 = None
    dws = []
    for name, dseg in segs:
        if dh1 is None:
            dh1 = mm(dseg, q[name], "nt", "d_h1_" + name)
        else:
            dh1 = mm(dseg, q[name], "nt", "d_h1_" + name, extras=(dh1,), epi=_add)
        dw = mm(h1, dseg, "tn", "dw_in_" + name)
        dws.append(dw[:, :2 * H] if name == "Wba" else dw)
    g["w_in"] = jnp.concatenate(dws, axis=1)
    dx, dxb, dmix = rms_bwd(r["x"], q["mix_norm_w"], dh1, dx1, "rms_mix_bwd")
    g["mix_norm_w"] = dmix[0]
    return dx, dxb, g


BIG = ("w_in", "s5_glu_w", "dn_proj_w", "w_out", "ffn_up", "ffn_down")
SHARDED_SMALL = ("dn_conv_w", "ffn_conv_w")
COL_SHARDED = ("w_in", "s5_glu_w", "dn_proj_w", "ffn_up", "dn_conv_w", "ffn_conv_w")
REPL = ("mix_norm_w", "s5_log_dt", "s5_a_re", "s5_a_im", "s5_b_re", "s5_b_im", "s5_c_re", "s5_c_im", "s5_d",
        "dn_a_log", "dn_dt_bias", "dn_norm_w", "ffn_norm_w")
WEIGHTS = ['mix_norm_w', 'w_in', 's5_log_dt', 's5_a_re', 's5_a_im', 's5_b_re', 's5_b_im', 's5_c_re', 's5_c_im', 's5_d',
           's5_glu_w', 'dn_conv_w', 'dn_a_log', 'dn_dt_bias', 'dn_norm_w', 'dn_proj_w', 'w_out', 'ffn_norm_w', 'ffn_up',
           'ffn_conv_w', 'ffn_down', 'final_norm_w']


def _join_shards(name, shards):
    return jnp.concatenate(shards, axis=-1 if name in COL_SHARDED else -2)


KINDS = {"w_in": "slot", "s5_glu_w": "col", "dn_proj_w": "col", "w_out": "slot", "ffn_up": "col", "ffn_down": "slot"}


def gather_layer_weights(shards):
    outs = ag_layer([shards[n].astype(bf16) for n in BIG], [KINDS[n] for n in BIG], "ag_layer")
    full = dict(zip(BIG, outs))
    for n in ("w_out", "ffn_down"):
        full[n] = full[n].reshape(-1, full[n].shape[-1])
    w = full["w_in"]
    full["w_in"] = w.transpose(1, 0, 2).reshape(w.shape[1], -1)
    return full


def gather_small_sharded(shards):
    names = SHARDED_SMALL
    shapes = [shards[n].shape for n in names]
    got = chip_exchange(_flat_pack([shards[n] for n in names], f32, lanes=128, row_mult=8), "ag_small", same=True)
    per_chip = [_flat_unpack(got[j], shapes) for j in range(N_CHIPS)]
    return {n: _join_shards(n, [per_chip[j][k] for j in range(N_CHIPS)]) for k, n in enumerate(names)}


def reduce_scatter_layer_grads(g):
    tens = []
    for n in BIG:
        a = g[n]
        if n == "w_in":
            a = a.reshape(a.shape[0], N_CHIPS, -1).transpose(1, 0, 2)
        elif KINDS[n] == "slot":
            a = a.reshape(N_CHIPS, -1, a.shape[-1])
        tens.append(a)
    kinds = [KINDS[n] for n in BIG]
    mine, theirs = rs_pair(tens, kinds, "rs_pair")
    sums = [add_pair(m, t, "rs_pair_sum_" + n) for n, m, t in zip(BIG, mine, theirs)]
    parts = rs_chips(sums, kinds, "rs_chips")
    halves = [sum_slots(x, "rs_chip_sum_" + n) for n, x in zip(BIG, parts)]
    return dict(zip(BIG, rs_share(halves, "rs_share")))


def all_reduce_small(arrs):
    shapes = [a.shape for a in arrs]
    pack = _flat_pack(arrs, f32, lanes=128, row_mult=8)
    from_chips = chip_exchange(pack, "ar_chips", same=True)
    from_sib = sibling_exchange(from_chips, "ar_sibling")
    c = lax.axis_index("c")
    both = jnp.concatenate([jnp.where(c == 0, from_chips, from_sib), jnp.where(c == 0, from_sib, from_chips)], axis=0)
    return _flat_unpack(sum_slots(both, "ar_sum"), shapes)


def kernel(x, mix_norm_w, w_in, s5_log_dt, s5_a_re, s5_a_im, s5_b_re, s5_b_im, s5_c_re, s5_c_im, s5_d, s5_glu_w, dn_conv_w, dn_a_log, dn_dt_bias, dn_norm_w, dn_proj_w, w_out, ffn_norm_w, ffn_up, ffn_conv_w, ffn_down, final_norm_w, loss_target, m_mix_norm_w, m_w_in, m_s5_log_dt, m_s5_a_re, m_s5_a_im, m_s5_b_re, m_s5_b_im, m_s5_c_re, m_s5_c_im, m_s5_d, m_s5_glu_w, m_dn_conv_w, m_dn_a_log, m_dn_dt_bias, m_dn_norm_w, m_dn_proj_w, m_w_out, m_ffn_norm_w, m_ffn_up, m_ffn_conv_w, m_ffn_down, m_final_norm_w, v_mix_norm_w, v_w_in, v_s5_log_dt, v_s5_a_re, v_s5_a_im, v_s5_b_re, v_s5_b_im, v_s5_c_re, v_s5_c_im, v_s5_d, v_s5_glu_w, v_dn_conv_w, v_dn_a_log, v_dn_dt_bias, v_dn_norm_w, v_dn_proj_w, v_w_out, v_ffn_norm_w, v_ffn_up, v_ffn_conv_w, v_ffn_down, v_final_norm_w):
    args = locals()
    W = {n: args[n] for n in WEIGHTS}
    M = {n: args["m_" + n] for n in WEIGHTS}
    V = {n: args["v_" + n] for n in WEIGHTS}
    depth = mix_norm_w.shape[0]
    H = dn_a_log.shape[1]
    xs = x[0]
    target = loss_target[0]

    layers = []
    conv_full = gather_small_sharded({n: W[n] for n in SHARDED_SMALL})
    for l in range(depth):
        p = gather_layer_weights({n: W[n][l] for n in BIG})
        for n in REPL:
            p[n] = W[n][l]
        for n in SHARDED_SMALL:
            p[n] = conv_full[n][l]
        for n in ("mix_norm_w", "ffn_norm_w", "dn_norm_w", "dn_a_log", "dn_dt_bias", "s5_d"):
            p[n] = p[n].reshape(1, -1)
        layers.append((_layer_weights(p, H), _s5_params(p, l)))
    res = []
    for l in range(depth):
        xs, r = layer_fwd(xs, layers[l][0], layers[l][1], H)
        res.append(r)
    dx, dxb, loss_part, dfinal = loss_head(xs, final_norm_w.reshape(1, -1), target, "loss_head")
    loss = lax.psum(loss_part[0, 0], ("x", "y", "c"))

    grads = [None] * depth
    for l in reversed(range(depth)):
        dx, dxb, grads[l] = layer_bwd(dx, dxb, res[l], layers[l][0], layers[l][1], H)
    grad_x = dx[None]

    G = {}
    sharded = [reduce_scatter_layer_grads(grads[l]) for l in range(depth)]
    for n in BIG:
        G[n] = jnp.stack([sharded[l][n] for l in range(depth)])
    small_names = REPL + SHARDED_SMALL
    small = [jnp.stack([grads[l][n] for l in range(depth)]) for n in small_names] + [dfinal[0]]
    for n, a in zip(small_names + ("final_norm_w",), all_reduce_small(small)):
        G[n] = a
    chip = 2 * lax.axis_index("x") + lax.axis_index("y")
    for n in SHARDED_SMALL:
        cs = W[n].shape[-1]
        G[n] = lax.dynamic_slice_in_dim(G[n], chip * cs, cs, axis=-1)

    delta, new_m, new_v = {}, {}, {}
    for n in WEIGHTS:
        shape = W[n].shape
        size = math.prod(shape)
        if n in BIG + SHARDED_SMALL:
            two_d = (size // shape[-1], shape[-1])
        else:
            two_d = (size // 128, 128) if size % 128 == 0 else (1, size)
        d, nm, nv = adamw(W[n].reshape(two_d), G[n].reshape(two_d), M[n].reshape(two_d), V[n].reshape(two_d), "adamw_" + n)
        delta[n], new_m[n], new_v[n] = d.reshape(shape), nm.reshape(shape), nv.reshape(shape)
        G[n] = G[n].reshape(shape)
    return (loss, grad_x, *[G[n] for n in WEIGHTS], *[delta[n] for n in WEIGHTS],
            *[new_m[n] for n in WEIGHTS], *[new_v[n] for n in WEIGHTS])
```

```python
import functools
import math

import jax
import jax.numpy as jnp
from jax import lax
from jax.experimental import pallas as pl
from jax.experimental.pallas import tpu as pltpu

f32 = jnp.float32
bf16 = jnp.bfloat16
S = jax.ShapeDtypeStruct

NORM_EPS = 1e-6
DN_CHUNK = 64
S5_GROUP = 16
ADAM_LR, ADAM_B1, ADAM_B2, ADAM_EPS, ADAM_WD, ADAM_STEP = 0.001, 0.9, 0.999, 1e-08, 0.01, 10
VMEM_LIMIT_BYTES = 56 * 1024 * 1024
HALO = 8
MESH = pl.DeviceIdType.MESH
N_CHIPS = 4


def _pick(n, cands):
    for c in cands:
        if n % c == 0:
            return c
    return n


MM_VMEM_BUDGET = 40 * 1024 * 1024
MM_MAX_TK = 2816


def _pick_k(K):
    if K <= MM_MAX_TK or K % 128:
        return K
    return max(d for d in range(128, MM_MAX_TK + 1, 128) if K % d == 0)


def _cparams(sem):
    return pltpu.CompilerParams(dimension_semantics=sem, vmem_limit_bytes=VMEM_LIMIT_BYTES)


_DIMS = {"nn": ((1,), (0,)), "nt": ((1,), (1,)), "tn": ((0,), (0,))}


class Win:
    def __init__(self, arr, c0, nc):
        self.arr, self.c0, self.nc = arr, c0, nc


def mm(a, b, mode, name, extras=(), epi=None, out_dtypes=(f32,), out_into=None):
    barr, c0 = (b.arr, b.c0) if isinstance(b, Win) else (b, 0)
    if mode == "tn":
        K, M = a.shape
    else:
        M, K = a.shape
    if mode == "nt":
        N, K2 = barr.shape
        K2 = b.nc if isinstance(b, Win) else K2
        n_off, k_off = 0, c0
    else:
        K2, N = barr.shape
        N = b.nc if isinstance(b, Win) else N
        n_off, k_off = c0, 0
    assert K == K2, (a.shape, barr.shape, mode)
    o_tot, o_off, o_alias = out_into if out_into is not None else (N, 0, None)
    tm = _pick(M, (1024, 512, 256, 128, 64, 32, 16, 8))
    tn = _pick(math.gcd(math.gcd(N, n_off), o_off), (1024, 512, 256, 128))
    tk = _pick_k(math.gcd(K, k_off))

    def vmem_estimate(tm_):
        tiles = tm_ * tk * a.dtype.itemsize + tk * tn * barr.dtype.itemsize
        tiles += sum(tm_ * tn * e.dtype.itemsize for e in extras if e.shape == (M, N))
        tiles += sum(tm_ * tn * jnp.dtype(dt).itemsize for dt in out_dtypes)
        return 2 * tiles + 3 * tm_ * tn * 4

    while vmem_estimate(tm) > MM_VMEM_BUDGET and tm % 16 == 0 and tm > 128:
        tm //= 2
    nk = K // tk
    assert M % tm == 0 and N % tn == 0 and K % tk == 0 and n_off % tn == 0 and k_off % tk == 0 and o_off % tn == 0
    nb, kb, ob = n_off // tn, k_off // tk, o_off // tn
    if mode == "tn":
        a_spec = pl.BlockSpec((tk, tm), lambda j, i, k: (k, i))
    else:
        a_spec = pl.BlockSpec((tm, tk), lambda j, i, k: (i, k))
    if mode == "nt":
        b_spec = pl.BlockSpec((tn, tk), lambda j, i, k: (j, k + kb))
    else:
        b_spec = pl.BlockSpec((tk, tn), lambda j, i, k: (k, j + nb))
    ex_specs = []
    for e in extras:
        if e.shape == (M, N):
            ex_specs.append(pl.BlockSpec((tm, tn), lambda j, i, k: (i, j)))
        elif e.shape == (1, N):
            ex_specs.append(pl.BlockSpec((1, tn), lambda j, i, k: (0, j)))
        elif e.shape == (M, 1):
            ex_specs.append(pl.BlockSpec((tm, 1), lambda j, i, k: (i, 0)))
        else:
            raise ValueError((e.shape, M, N))
    ne, no = len(extras), len(out_dtypes)
    na = 1 if o_alias is not None else 0
    assert out_into is None or no == 1
    dims = (_DIMS[mode], ((), ()))

    def body(a_ref, b_ref, *rest):
        ex, outs = rest[:ne], rest[ne + na:ne + na + no]
        p = lax.dot_general(a_ref[...].astype(bf16), b_ref[...].astype(bf16), dims, preferred_element_type=f32)

        def finish(acc):
            res = epi(acc, *[e[...] for e in ex]) if epi is not None else (acc,)
            for o, r in zip(outs, res):
                o[...] = r.astype(o.dtype)

        if nk == 1:
            finish(p)
        else:
            acc_ref = rest[-1]
            k = pl.program_id(2)

            @pl.when(k == 0)
            def _():
                acc_ref[...] = p

            @pl.when(k > 0)
            def _():
                acc_ref[...] += p

            @pl.when(k == nk - 1)
            def _():
                finish(acc_ref[...])

    outs = pl.pallas_call(
        body,
        name=name,
        grid=(N // tn, M // tm, nk),
        in_specs=[a_spec, b_spec] + ex_specs + [pl.BlockSpec(memory_space=pl.ANY)] * na,
        out_specs=[pl.BlockSpec((tm, tn), lambda j, i, k: (i, j + ob)) for _ in out_dtypes],
        out_shape=[S((M, o_tot), dt) for dt in out_dtypes],
        scratch_shapes=[pltpu.VMEM((tm, tn), f32)] if nk > 1 else [],
        input_output_aliases={2 + ne: 0} if na else {},
        compiler_params=_cparams(("parallel", "parallel", "arbitrary")),
    )(a, barr, *extras, *([o_alias] if na else []))
    return outs[0] if no == 1 else tuple(outs)


def mm_bd(a, b, mode, name, ga, gb, extras=(), epi=None, out_dtypes=(f32,)):
    T = max(1, min(256 // min(ga, gb), 1024 // max(ga, gb)))
    if mode == "tn":
        K, M = a.shape
        N = b.shape[1]
        G = M // ga
        T = min(T, G)
        tm, tn, tk = T * ga, T * gb, _pick_k(K)
        nk = K // tk
        grid = (G // T, 1, nk)
        a_spec = pl.BlockSpec((tk, tm), lambda j, i, k: (k, j))
        b_spec = pl.BlockSpec((tk, tn), lambda j, i, k: (k, j))
        o_spec = pl.BlockSpec((tm, tn), lambda j, i, k: (j, 0))
        out_shape = (M, tn)
    else:
        M = a.shape[0]
        if mode == "nn":
            G = b.shape[0] // ga
            T = min(T, G)
            kw, tn, N = T * ga, T * gb, G * gb
            b_spec = pl.BlockSpec((kw, tn), lambda j, i, k: (j, j))
        else:
            G = b.shape[0] // ga
            T = min(T, G)
            kw, tn, N = T * gb, T * ga, G * ga
            b_spec = pl.BlockSpec((tn, kw), lambda j, i, k: (j, j))
        tm = _pick(M, (1024, 512, 256, 128, 64, 32, 16, 8))
        nk = 1
        grid = (G // T, M // tm, 1)
        a_spec = pl.BlockSpec((tm, kw), lambda j, i, k: (i, j))
        o_spec = pl.BlockSpec((tm, tn), lambda j, i, k: (i, j))
        out_shape = (M, N)
    ex_specs = []
    for e in extras:
        if e.shape == out_shape:
            ex_specs.append(o_spec)
        elif e.shape == (1, out_shape[1]):
            ex_specs.append(pl.BlockSpec((1, tn), lambda j, i, k: (0, j)))
        else:
            raise ValueError((e.shape, out_shape))
    ne, no = len(extras), len(out_dtypes)
    dims = (_DIMS[mode], ((), ()))

    def body(a_ref, b_ref, *rest):
        ex, outs = rest[:ne], rest[ne:ne + no]
        p = lax.dot_general(a_ref[...].astype(bf16), b_ref[...].astype(bf16), dims, preferred_element_type=f32)

        def finish(acc):
            res = epi(acc, *[e[...] for e in ex]) if epi is not None else (acc,)
            for o, r in zip(outs, res):
                o[...] = r.astype(o.dtype)

        if nk == 1:
            finish(p)
        else:
            acc_ref = rest[-1]
            k = pl.program_id(2)

            @pl.when(k == 0)
            def _():
                acc_ref[...] = p

            @pl.when(k > 0)
            def _():
                acc_ref[...] += p

            @pl.when(k == nk - 1)
            def _():
                finish(acc_ref[...])

    outs = pl.pallas_call(
        body, name=name, grid=grid, in_specs=[a_spec, b_spec] + ex_specs, out_specs=[o_spec] * no,
        out_shape=[S(out_shape, dt) for dt in out_dtypes],
        scratch_shapes=[pltpu.VMEM((tm, tn), f32)] if nk > 1 else [],
        compiler_params=_cparams(("parallel", "parallel", "arbitrary")),
    )(a, b, *extras)
    return outs[0] if no == 1 else tuple(outs)


def _diag_blocks(tiles, G, ga, gb):
    T = tiles.shape[1] // gb
    t5 = tiles.reshape(G // T, T, ga, T, gb)
    return jnp.sum(t5 * jnp.eye(T, dtype=tiles.dtype)[None, :, None, :, None], axis=3).reshape(G, ga, gb)


def _add(acc, prev):
    return (acc + prev,)


def rowk(name, fn, L, tl, ncol, ins, outs, accs=()):
    nrow = L // tl
    assert L % tl == 0 and tl % HALO == 0
    hb = tl // HALO

    def cw_of(c_total):
        assert c_total % ncol == 0, (name, c_total, ncol)
        return c_total // ncol

    in_specs = []
    for arr, kind in ins:
        if kind == "rows":
            in_specs.append(pl.BlockSpec((tl, cw_of(arr.shape[1])), lambda j, i: (i, j)))
        elif kind == "prev":
            in_specs.append(pl.BlockSpec((HALO, cw_of(arr.shape[1])), lambda j, i: (jnp.maximum(i * hb - 1, 0), j)))
        elif kind == "next":
            in_specs.append(pl.BlockSpec((HALO, cw_of(arr.shape[1])), lambda j, i: (jnp.minimum((i + 1) * hb, nrow * hb - 1), j)))
        elif kind == "cols":
            in_specs.append(pl.BlockSpec((arr.shape[0], cw_of(arr.shape[1])), lambda j, i: (0, j)))
        elif kind == "const":
            in_specs.append(pl.BlockSpec(arr.shape, lambda j, i: (0,) * arr.ndim))
        else:
            raise ValueError(kind)
    out_specs = [pl.BlockSpec((tl, cw_of(c)), lambda j, i: (i, j)) for c, _ in outs]
    out_shape = [S((L, c), dt) for c, dt in outs]
    out_specs += [pl.BlockSpec((r, cw_of(c)), lambda j, i: (0, j)) for r, c, _ in accs]
    out_shape += [S((r, c), dt) for r, c, dt in accs]
    ni, no, na = len(ins), len(outs), len(accs)

    def body(*refs):
        i = pl.program_id(1)
        res = fn(i, nrow, *[r[...] for r in refs[:ni]])
        for o, r in zip(refs[ni:ni + no], res[:no]):
            o[...] = r.astype(o.dtype)
        for o, r in zip(refs[ni + no:ni + no + na], res[no:]):
            @pl.when(i == 0)
            def _(o=o, r=r):
                o[...] = r.astype(o.dtype)

            @pl.when(i > 0)
            def _(o=o, r=r):
                o[...] += r.astype(o.dtype)

    res = pl.pallas_call(
        body,
        name=name,
        grid=(ncol, nrow),
        in_specs=in_specs,
        out_specs=out_specs,
        out_shape=out_shape,
        compiler_params=_cparams(("parallel", "arbitrary")),
    )(*[a for a, _ in ins])
    return tuple(res)


def _sigmoid(x):
    return 1.0 / (1.0 + jnp.exp(-x))


def _silu(x):
    return x * _sigmoid(x)


def _dsilu(x):
    s = _sigmoid(x)
    return s * (1.0 + x * (1.0 - s))


def _erf(x):
    a = jnp.abs(x)
    t = 1.0 / (1.0 + 0.3275911 * a)
    poly = t * (0.254829592 + t * (-0.284496736 + t * (1.421413741 + t * (-1.453152027 + t * 1.061405429))))
    y = 1.0 - poly * jnp.exp(-a * a)
    return jnp.where(x < 0, -y, y)


def _gelu(x):
    return 0.5 * x * (1.0 + _erf(x * (2.0 ** -0.5)))


def _dgelu(x):
    cdf = 0.5 * (1.0 + _erf(x * (2.0 ** -0.5)))
    pdf = jnp.exp(-0.5 * x * x) * (1.0 / math.sqrt(2.0 * math.pi))
    return cdf + x * pdf


def _rms(x, w):
    return x * lax.rsqrt(jnp.mean(x * x, axis=-1, keepdims=True) + NORM_EPS) * w


def _rms_bwd(x, w, dy):
    d = x.shape[-1]
    r = lax.rsqrt(jnp.mean(x * x, axis=-1, keepdims=True) + NORM_EPS)
    wdy = w * dy
    dx = r * wdy - x * (r * r * r) * (jnp.sum(x * wdy, axis=-1, keepdims=True) / d)
    dw = jnp.sum(x * r * dy, axis=0, keepdims=True)
    return dx, dw


def _from_cols(cols, width):
    tl = cols[0].shape[0]
    lane = lax.broadcasted_iota(jnp.int32, (tl, width), 1)
    out = jnp.zeros((tl, width), f32)
    for n, col in enumerate(cols):
        out = jnp.where(lane == n, col, out)
    return out


def _from_rows(rows):
    c = rows[0].shape[1]
    sub = lax.broadcasted_iota(jnp.int32, (len(rows), c), 0)
    out = jnp.zeros((len(rows), c), f32)
    for n, row in enumerate(rows):
        out = jnp.where(sub == n, row, out)
    return out


def _shift_down(x, halo, s, first):
    if s == 0:
        return x
    tl = x.shape[0]
    halo = jnp.where(first, 0.0, halo)
    xx = jnp.concatenate([halo, x], axis=0)
    return pltpu.roll(xx, s, 0)[HALO:HALO + tl]


def _shift_up(x, halo, s, last):
    if s == 0:
        return x
    tl = x.shape[0]
    halo = jnp.where(last, 0.0, halo)
    xx = jnp.concatenate([x, halo], axis=0)
    return pltpu.roll(xx, tl + HALO - s, 0)[0:tl]


def _causal_conv(x, halo, w, first):
    kw = w.shape[0]
    shifted = [_shift_down(x, halo, kw - 1 - j, first) for j in range(kw)]
    out = shifted[0] * w[0:1]
    for j in range(1, kw):
        out = out + shifted[j] * w[j:j + 1]
    return out, shifted


def rms_fwd(x, w, name):
    L, D = x.shape

    def fn(i, n, xb, wb):
        return (_rms(xb, wb),)

    return rowk(name, fn, L, _pick(L, (256, 128, 64, 32, 16, 8)), 1, [(x, "rows"), (w, "const")], [(D, bf16)])[0]


def rms_bwd(x, w, dh, dres, name):
    L, D = x.shape

    def fn(i, n, xb, wb, dhb, drb):
        dx, dw = _rms_bwd(xb, wb, dhb)
        dx = dx + drb
        return dx, dx, dw

    return rowk(name, fn, L, _pick(L, (256, 128, 64, 32, 16, 8)), 1,
                [(x, "rows"), (w, "const"), (dh, "rows"), (dres, "rows")], [(D, f32), (D, bf16)], [(1, D, f32)])


def loss_head(x, w, target, name):
    L, D = x.shape

    def fn(i, n, xb, wb, tb):
        err = _rms(xb, wb) - tb
        loss = 0.5 * jnp.sum(err * err) / D
        dx, dw = _rms_bwd(xb, wb, err / D)
        return dx, dx, jnp.full((8, 128), loss, f32), dw

    return rowk(name, fn, L, _pick(L, (256, 128, 64, 32, 16, 8)), 1,
                [(x, "rows"), (w, "const"), (target, "rows")], [(D, f32), (D, bf16)], [(8, 128, f32), (1, D, f32)])


def _s5_disc_math(lr, li, logdt, br, bi):
    dt = jnp.exp(logdt)
    mag = jnp.exp(lr * dt)
    ar, ai = mag * jnp.cos(li * dt), mag * jnp.sin(li * dt)
    den = lr * lr + li * li
    nr, ni = ar - 1.0, ai
    cr = (nr * lr + ni * li) / den
    ci = (ni * lr - nr * li) / den
    return ar, ai, cr * br - ci * bi, cr * bi + ci * br


def _disc_call(body, name, ins, out_widths):
    GP = ins[0].shape[0]
    tl = _pick(GP, (512, 256, 128, 64, 32, 16, 8))
    spec = lambda w: pl.BlockSpec((tl, w), lambda i: (i, 0))
    return pl.pallas_call(body, name=name, grid=(GP // tl,),
                          in_specs=[spec(a.shape[1]) for a in ins], out_specs=[spec(w) for w in out_widths],
                          out_shape=[S((GP, w), f32) for w in out_widths], compiler_params=_cparams(("parallel",)))(*ins)


def s5_disc(lr, li, logdt, br, bi, name):
    HG = br.shape[1]

    def body(lr_ref, li_ref, dt_ref, br_ref, bi_ref, ar_ref, ai_ref, bbr_ref, bbi_ref):
        ar, ai, bbr, bbi = _s5_disc_math(lr_ref[...], li_ref[...], dt_ref[...], br_ref[...], bi_ref[...])
        ar_ref[...], ai_ref[...], bbr_ref[...], bbi_ref[...] = ar, ai, bbr, bbi

    return _disc_call(body, name, [lr, li, logdt, br, bi], [1, 1, HG, HG])


def s5_disc_bwd(lr, li, logdt, br, bi, dar, dai, dbbr, dbbi, name):
    HG = br.shape[1]

    def body(lr_ref, li_ref, dt_ref, br_ref, bi_ref, dar_ref, dai_ref, dbbr_ref, dbbi_ref, *outs):
        _, vjp = jax.vjp(_s5_disc_math, lr_ref[...], li_ref[...], dt_ref[...], br_ref[...], bi_ref[...])
        for o, g in zip(outs, vjp((dar_ref[...], dai_ref[...], dbbr_ref[...], dbbi_ref[...]))):
            o[...] = g

    return _disc_call(body, name, [lr, li, logdt, br, bi, dar, dai, dbbr, dbbi], [1, 1, 1, HG, HG])


SCAN_TB = 256


def s5_scan_fwd(bur, bui, ar, ai, name):
    L, GP = bur.shape
    cw = _pick(GP, (1024, 512, 256, 128))
    tb = _pick(L, (SCAN_TB, 128, 64, 32, 16, 8))

    def body(bur_ref, bui_ref, ar_ref, ai_ref, xr_ref, xi_ref, cr_ref, ci_ref):
        @pl.when(pl.program_id(1) == 0)
        def _():
            cr_ref[...] = jnp.zeros_like(cr_ref)
            ci_ref[...] = jnp.zeros_like(ci_ref)

        a_r, a_i = ar_ref[...], ai_ref[...]

        def step(t, carry):
            xr, xi = carry
            row = pl.ds(t, 1)
            nr = a_r * xr - a_i * xi + bur_ref[row, :]
            ni = a_r * xi + a_i * xr + bui_ref[row, :]
            xr_ref[row, :] = nr
            xi_ref[row, :] = ni
            return nr, ni

        xr, xi = lax.fori_loop(0, tb, step, (cr_ref[...], ci_ref[...]), unroll=8)
        cr_ref[...] = xr
        ci_ref[...] = xi

    blk = pl.BlockSpec((tb, cw), lambda j, i: (i, j))
    vec = pl.BlockSpec((1, cw), lambda j, i: (0, j))
    return pl.pallas_call(
        body, name=name, grid=(GP // cw, L // tb),
        in_specs=[blk, blk, vec, vec], out_specs=[blk, blk],
        out_shape=[S((L, GP), f32), S((L, GP), f32)],
        scratch_shapes=[pltpu.VMEM((1, cw), f32), pltpu.VMEM((1, cw), f32)],
        compiler_params=_cparams(("parallel", "arbitrary")),
    )(bur, bui, ar, ai)


def s5_scan_bwd(gr, gi, xr, xi, ar, ai, name):
    L, GP = gr.shape
    cw = _pick(GP, (1024, 512, 256, 128))
    tb = _pick(L, (SCAN_TB, 128, 64, 32, 16, 8))
    nt = L // tb

    def body(gr_ref, gi_ref, xr_ref, xi_ref, ar_ref, ai_ref, gxr_ref, gxi_ref, dar_ref, dai_ref, cr_ref, ci_ref):
        @pl.when(pl.program_id(1) == 0)
        def _():
            cr_ref[...] = jnp.zeros_like(cr_ref)
            ci_ref[...] = jnp.zeros_like(ci_ref)
            dar_ref[...] = jnp.zeros_like(dar_ref)
            dai_ref[...] = jnp.zeros_like(dai_ref)

        a_r, a_i = ar_ref[...], ai_ref[...]

        def step(s, carry):
            cr, ci, dr, di = carry
            row = pl.ds(tb - 1 - s, 1)
            x_r, x_i = xr_ref[row, :], xi_ref[row, :]
            dr = dr + cr * x_r + ci * x_i
            di = di + ci * x_r - cr * x_i
            nr = gr_ref[row, :] + a_r * cr + a_i * ci
            ni = gi_ref[row, :] + a_r * ci - a_i * cr
            gxr_ref[row, :] = nr
            gxi_ref[row, :] = ni
            return nr, ni, dr, di

        cr, ci, dr, di = lax.fori_loop(0, tb, step, (cr_ref[...], ci_ref[...], dar_ref[...], dai_ref[...]), unroll=8)
        cr_ref[...] = cr
        ci_ref[...] = ci
        dar_ref[...] = dr
        dai_ref[...] = di

    blk = pl.BlockSpec((tb, cw), lambda j, i: (nt - 1 - i, j))
    vec = pl.BlockSpec((1, cw), lambda j, i: (0, j))
    return pl.pallas_call(
        body, name=name, grid=(GP // cw, nt),
        in_specs=[blk, blk, blk, blk, vec, vec], out_specs=[blk, blk, vec, vec],
        out_shape=[S((L, GP), f32), S((L, GP), f32), S((1, GP), f32), S((1, GP), f32)],
        scratch_shapes=[pltpu.VMEM((1, cw), f32), pltpu.VMEM((1, cw), f32)],
        compiler_params=_cparams(("parallel", "arbitrary")),
    )(gr, gi, xr, xi, ar, ai)


def _dn_heads_math(cq, ck, cv, braw, araw, alog, dtb, dk):
    q, k, v = _silu(cq), _silu(ck), _silu(cv)
    q = q * lax.rsqrt(jnp.sum(q * q, axis=-1, keepdims=True) + NORM_EPS) * (dk ** -0.5)
    k = k * lax.rsqrt(jnp.sum(k * k, axis=-1, keepdims=True) + NORM_EPS)
    beta = _sigmoid(braw)
    g = -jnp.exp(alog) * jax.nn.softplus(araw + dtb)
    return q, k, v, beta, g


def dn_prep(qkv, convw, ba, alog, dtb, H, name):
    L, W = qkv.shape
    hk = W // 3
    dk = hk // H

    def fn(i, n, xb, hb, wb, bab, alb, dtbb):
        c, _ = _causal_conv(xb, hb, wb, i == 0)
        qs, ks, vs, bs, gs = [], [], [], [], []
        for h in range(H):
            sl = lambda o: c[:, o + h * dk:o + (h + 1) * dk]
            q, k, v, beta, g = _dn_heads_math(sl(0), sl(hk), sl(2 * hk), bab[:, h:h + 1], bab[:, H + h:H + h + 1],
                                              alb[:, h:h + 1], dtbb[:, h:h + 1], dk)
            qs.append(q), ks.append(k), vs.append(v), bs.append(beta), gs.append(g)
        cat = lambda xs: jnp.concatenate(xs, axis=1)
        return cat(qs), cat(ks), cat(vs), _from_cols(bs + gs, 2 * H)

    return rowk(name, fn, L, _pick(L, (128, 64, 32, 16, 8)), 1,
                [(qkv, "rows"), (qkv, "prev"), (convw, "const"), (ba, "rows"), (alog, "const"), (dtb, "const")],
                [(hk, f32), (hk, f32), (hk, f32), (2 * H, f32)])


def dn_prep_bwd(qkv, convw, ba, alog, dtb, dq, dk_, dv, dbg, H, name):
    L, W = qkv.shape
    hk = W // 3
    dk = hk // H
    kw = convw.shape[0]
    nba = ba.shape[1]

    def fn(i, n, xb, hb, wb, bab, alb, dtbb, dqb, dkb, dvb, dbgb):
        c, shifted = _causal_conv(xb, hb, wb, i == 0)
        dcs = [None] * (3 * H)
        dbr, dar, dal, ddt = [], [], [], []
        for h in range(H):
            sl = lambda a, o: a[:, o + h * dk:o + (h + 1) * dk]
            args = (sl(c, 0), sl(c, hk), sl(c, 2 * hk), bab[:, h:h + 1], bab[:, H + h:H + h + 1],
                    alb[:, h:h + 1], dtbb[:, h:h + 1])
            _, vjp = jax.vjp(lambda *a: _dn_heads_math(*a, dk), *args)
            g = vjp((sl(dqb, 0), sl(dkb, 0), sl(dvb, 0), dbgb[:, h:h + 1], dbgb[:, H + h:H + h + 1]))
            dcs[h], dcs[H + h], dcs[2 * H + h] = g[0], g[1], g[2]
            dbr.append(g[3]), dar.append(g[4]), dal.append(g[5]), ddt.append(g[6])
        dc = jnp.concatenate(dcs, axis=1)
        dba = _from_cols(dbr + dar, nba)
        dw = _from_rows([jnp.sum(dc * shifted[j], axis=0, keepdims=True) for j in range(kw)])
        return dc, dba, dw, _from_cols(dal, H), _from_cols(ddt, H)

    return rowk(name, fn, L, _pick(L, (128, 64, 32, 16, 8)), 1,
                [(qkv, "rows"), (qkv, "prev"), (convw, "const"), (ba, "rows"), (alog, "const"), (dtb, "const"),
                 (dq, "rows"), (dk_, "rows"), (dv, "rows"), (dbg, "rows")],
                [(W, f32), (nba, bf16)], [(kw, W, f32), (1, H, f32), (1, H, f32)])


def conv_t(dc, w, name):
    L, C = dc.shape
    kw = w.shape[0]
    ncol = C // _pick(C, (1536, 1408, 1024, 768, 512, 256, 128))

    def fn(i, n, db, hb, wb):
        out = db * wb[kw - 1:kw]
        for j in range(kw - 1):
            out = out + _shift_up(db, hb, kw - 1 - j, i == n - 1) * wb[j:j + 1]
        return (out,)

    return rowk(name, fn, L, _pick(L, (256, 128, 64, 32, 16, 8)), ncol,
                [(dc, "rows"), (dc, "next"), (w, "cols")], [(C, bf16)])[0]


_BDIMS = {"nn": (((2,), (1,)), ((0,), (0,))), "nt": (((2,), (2,)), ((0,), (0,))), "tn": (((1,), (1,)), ((0,), (0,)))}


def _bdot(a, b, mode):
    return lax.dot_general(a.astype(bf16), b.astype(bf16), _BDIMS[mode], preferred_element_type=f32)


def _split16(a):
    hi = a.astype(bf16)
    return hi, (a - hi.astype(f32)).astype(bf16)


def _hdot(a, b, mode):
    ah, al = _split16(a)
    bh, bl = _split16(b)
    d = lambda x, y: lax.dot_general(x, y, _BDIMS[mode], preferred_element_type=f32)
    return d(ah, bh) + (d(ah, bl) + d(al, bh))


def _make_dot(raw):
    @functools.partial(jax.custom_vjp, nondiff_argnums=(2,))
    def dot(a, b, mode):
        return raw(a, b, mode)

    def fwd(a, b, mode):
        return raw(a, b, mode), (a, b)

    def bwd(mode, res, ct):
        a, b = res
        if mode == "nn":
            return raw(ct, b, "nt"), raw(a, ct, "tn")
        if mode == "nt":
            return raw(ct, b, "nn"), raw(ct, a, "tn")
        return raw(b, ct, "nt"), raw(a, ct, "nn")

    dot.defvjp(fwd, bwd)
    return dot


_dot16 = _make_dot(_bdot)
_dot32 = _make_dot(_hdot)


@jax.custom_vjp
def _unit_lower_inv(lmat):
    c = lmat.shape[-1]
    eye = (lax.broadcasted_iota(jnp.int32, (c, c), 0) == lax.broadcasted_iota(jnp.int32, (c, c), 1)).astype(f32)
    p = -lmat
    t = eye + p
    for _ in range(int(math.log2(c)) - 1):
        p = _hdot(p, p, "nn")
        t = t + _hdot(t, p, "nn")
    return t


def _uli_fwd(lmat):
    t = _unit_lower_inv(lmat)
    return t, t


def _uli_bwd(t, dt):
    return (-_hdot(_hdot(t, dt, "tn"), t, "nt"),)


_unit_lower_inv.defvjp(_uli_fwd, _uli_bwd)


def _dn_chunk_math(s_in, q, k, v, gcol, grow, bcol):
    c = q.shape[1]
    ri = lax.broadcasted_iota(jnp.int32, (c, c), 0)
    ci = lax.broadcasted_iota(jnp.int32, (c, c), 1)
    tril = (ri >= ci).astype(f32)
    strict = (ri > ci).astype(f32)
    gc_col = jnp.sum(tril * grow, axis=2, keepdims=True)
    gc_row = jnp.sum((1.0 - strict) * gcol, axis=1, keepdims=True)
    g_last = jnp.sum(gcol, axis=1, keepdims=True)
    decay = jnp.exp((gc_col - gc_row) * tril) * tril
    kb = k * bcol
    vb = v * bcol
    lmat = _dot16(kb, k, "nt") * decay * strict
    t = _unit_lower_inv(lmat)
    u = _dot32(t, vb, "nn")
    w = _dot32(t, kb * jnp.exp(gc_col), "nn")
    attn = _dot16(q, k, "nt") * decay
    v_new = u - _dot16(w, s_in, "nn")
    o = _dot16(q * jnp.exp(gc_col), s_in, "nn") + _dot16(attn, v_new, "nn")
    s_out = s_in * jnp.exp(g_last) + _dot16(k * jnp.exp(g_last - gc_col), v_new, "tn")
    return o, s_out


def _dn_load(q_ref, k_ref, v_ref, bg_ref, bgt_ref, H, dk):
    heads = lambda ref: jnp.stack([ref[:, h * dk:(h + 1) * dk] for h in range(H)])
    bgb, bgtb = bg_ref[...], bgt_ref[0]
    gcol = jnp.stack([bgb[:, H + h:H + h + 1] for h in range(H)])
    bcol = jnp.stack([bgb[:, h:h + 1] for h in range(H)])
    grow = jnp.stack([bgtb[H + h:H + h + 1, :] for h in range(H)])
    return heads(q_ref), heads(k_ref), heads(v_ref), gcol, grow, bcol


def dn_chunk_fwd(qn, kn, vv, bg, bgt, H, name):
    L, hk = qn.shape
    dk = hk // H
    c = DN_CHUNK
    nc = L // c

    def body(q_ref, k_ref, v_ref, bg_ref, bgt_ref, o_ref, ss_ref, s_ref):
        @pl.when(pl.program_id(0) == 0)
        def _():
            s_ref[...] = jnp.zeros_like(s_ref)

        s_in = s_ref[...]
        ss_ref[0] = s_in
        o, s_out = _dn_chunk_math(s_in, *_dn_load(q_ref, k_ref, v_ref, bg_ref, bgt_ref, H, dk))
        for h in range(H):
            o_ref[:, h * dk:(h + 1) * dk] = o[h]
        s_ref[...] = s_out

    row = lambda w: pl.BlockSpec((c, w), lambda n: (n, 0))
    return pl.pallas_call(
        body, name=name, grid=(nc,),
        in_specs=[row(hk), row(hk), row(hk), row(2 * H), pl.BlockSpec((1, 2 * H, c), lambda n: (n, 0, 0))],
        out_specs=[row(hk), pl.BlockSpec((1, H, dk, dk), lambda n: (n, 0, 0, 0))],
        out_shape=[S((L, hk), f32), S((nc, H, dk, dk), f32)],
        scratch_shapes=[pltpu.VMEM((H, dk, dk), f32)],
        compiler_params=_cparams(("arbitrary",)),
    )(qn, kn, vv, bg, bgt)


def dn_chunk_bwd(qn, kn, vv, bg, bgt, ss, do, H, name):
    L, hk = qn.shape
    dk = hk // H
    c = DN_CHUNK
    nc = L // c

    def body(q_ref, k_ref, v_ref, bg_ref, bgt_ref, ss_ref, do_ref, dq_ref, dk_ref, dv_ref, dbg_ref, dgt_ref, ds_ref):
        @pl.when(pl.program_id(0) == 0)
        def _():
            ds_ref[...] = jnp.zeros_like(ds_ref)

        args = (ss_ref[0],) + _dn_load(q_ref, k_ref, v_ref, bg_ref, bgt_ref, H, dk)
        _, vjp = jax.vjp(_dn_chunk_math, *args)
        do = jnp.stack([do_ref[:, h * dk:(h + 1) * dk] for h in range(H)])
        ds, dq, dkk, dv, dgcol, dgrow, dbcol = vjp((do, ds_ref[...]))
        ds_ref[...] = ds
        for h in range(H):
            sl = slice(h * dk, (h + 1) * dk)
            dq_ref[:, sl], dk_ref[:, sl], dv_ref[:, sl] = dq[h], dkk[h], dv[h]
        dbg_ref[...] = _from_cols([dbcol[h] for h in range(H)] + [dgcol[h] for h in range(H)], 2 * H)
        dgt_ref[0] = _from_rows([dgrow[h] for h in range(H)])

    row = lambda w: pl.BlockSpec((c, w), lambda n: (nc - 1 - n, 0))
    return pl.pallas_call(
        body, name=name, grid=(nc,),
        in_specs=[row(hk), row(hk), row(hk), row(2 * H), pl.BlockSpec((1, 2 * H, c), lambda n: (nc - 1 - n, 0, 0)),
                  pl.BlockSpec((1, H, dk, dk), lambda n: (nc - 1 - n, 0, 0, 0)), row(hk)],
        out_specs=[row(hk), row(hk), row(hk), row(2 * H), pl.BlockSpec((1, H, c), lambda n: (nc - 1 - n, 0, 0))],
        out_shape=[S((L, hk), f32), S((L, hk), f32), S((L, hk), f32), S((L, 2 * H), f32), S((nc, H, c), f32)],
        scratch_shapes=[pltpu.VMEM((H, dk, dk), f32)],
        compiler_params=_cparams(("arbitrary",)),
    )(qn, kn, vv, bg, bgt, ss, do)


def _dn_gate_math(o, z, w):
    return _rms(o, w) * _silu(z)


def dn_gate(o, z, w, H, name):
    L, hv = o.shape
    dv = hv // H

    def fn(i, n, ob, zb, wb):
        return (jnp.concatenate([_dn_gate_math(ob[:, h * dv:(h + 1) * dv], zb[:, h * dv:(h + 1) * dv], wb)
                                 for h in range(H)], axis=1),)

    return rowk(name, fn, L, _pick(L, (256, 128, 64, 32, 16, 8)), 1, [(o, "rows"), (z, "rows"), (w, "const")], [(hv, bf16)])[0]


def dn_gate_bwd(o, z, w, dy, H, name):
    L, hv = o.shape
    dv = hv // H

    def fn(i, n, ob, zb, wb, dyb):
        dos, dzs, dw = [], [], 0.0
        for h in range(H):
            sl = slice(h * dv, (h + 1) * dv)
            _, vjp = jax.vjp(_dn_gate_math, ob[:, sl], zb[:, sl], wb)
            a, b, c = vjp(dyb[:, sl])
            dos.append(a), dzs.append(b)
            dw = dw + c
        return jnp.concatenate(dos, axis=1), jnp.concatenate(dzs, axis=1), dw

    return rowk(name, fn, L, _pick(L, (256, 128, 64, 32, 16, 8)), 1,
                [(o, "rows"), (z, "rows"), (w, "const"), (dy, "rows")], [(hv, f32), (hv, bf16)], [(1, dv, f32)])


def ffn_mid(ua, uv, wa, wv, name):
    L, F = ua.shape
    ncol = F // _pick(F, (1408, 1024, 512, 256, 128))

    def fn(i, n, ab, ah, vb, vh, wab, wvb):
        ca, _ = _causal_conv(ab, ah, wab, i == 0)
        cv, _ = _causal_conv(vb, vh, wvb, i == 0)
        return (_silu(ca) * cv,)

    return rowk(name, fn, L, _pick(L, (256, 128, 64, 32, 16, 8)), ncol,
                [(ua, "rows"), (ua, "prev"), (uv, "rows"), (uv, "prev"), (wa, "cols"), (wv, "cols")], [(F, bf16)])[0]


def ffn_mid_bwd(ua, uv, wa, wv, dh, name):
    L, F = ua.shape
    kw = wa.shape[0]
    ncol = F // _pick(F, (1408, 1024, 512, 256, 128))

    def fn(i, n, ab, ah, vb, vh, wab, wvb, dhb):
        ca, sa = _causal_conv(ab, ah, wab, i == 0)
        cv, sv = _causal_conv(vb, vh, wvb, i == 0)
        dca = dhb * cv * _dsilu(ca)
        dcv = dhb * _silu(ca)
        dwa = _from_rows([jnp.sum(dca * sa[j], axis=0, keepdims=True) for j in range(kw)])
        dwv = _from_rows([jnp.sum(dcv * sv[j], axis=0, keepdims=True) for j in range(kw)])
        return dca, dcv, dwa, dwv

    return rowk(name, fn, L, _pick(L, (256, 128, 64, 32, 16, 8)), ncol,
                [(ua, "rows"), (ua, "prev"), (uv, "rows"), (uv, "prev"), (wa, "cols"), (wv, "cols"), (dh, "rows")],
                [(F, f32), (F, f32)], [(kw, F, f32), (kw, F, f32)])


def _merge_epi(acc, gs, gd, ga, gb):
    return acc, _sigmoid(gs) * ga * _sigmoid(gb) + _sigmoid(gd) * acc


def merge_bwd(dm, gs, gd, ga, gb, brdn, name):
    L, D = dm.shape
    ncol = D // _pick(D, (1024, 512, 256, 128))

    def fn(i, n, dmb, gsb, gdb, gab, gbb, brb):
        ss, sd, sb = _sigmoid(gsb), _sigmoid(gdb), _sigmoid(gbb)
        br_s5 = gab * sb
        dbr_s5 = dmb * ss
        return (dmb * br_s5 * ss * (1.0 - ss), dmb * brb * sd * (1.0 - sd), dbr_s5 * sb,
                dbr_s5 * gab * sb * (1.0 - sb), dmb * sd)

    return rowk(name, fn, L, _pick(L, (256, 128, 64, 32, 16, 8)), ncol,
                [(a, "rows") for a in (dm, gs, gd, ga, gb, brdn)], [(D, bf16)] * 5)


def s5_out_bwd(dy, ypre, u, d, name):
    L, W = dy.shape

    def fn(i, n, dyb, yb, ub, db):
        dyp = dyb * _dgelu(yb)
        return dyp, db * dyp, jnp.sum(dyp * ub, axis=0, keepdims=True)

    return rowk(name, fn, L, _pick(L, (256, 128, 64, 32, 16, 8)), 1,
                [(dy, "rows"), (ypre, "rows"), (u, "rows"), (d, "const")], [(W, bf16), (W, f32)], [(1, W, f32)])


def _s5_y_epi(acc, y1, u, d):
    ypre = acc + y1 + d * u
    return ypre, _gelu(ypre)


def adamw(w, g, m, v, name):
    R, C = w.shape
    tl = _pick(R, (256, 128, 64, 32, 16, 8))
    if R * C * 4 <= 2 * 1024 * 1024:
        tl = R

    def body(w_ref, g_ref, m_ref, v_ref, d_ref, nm_ref, nv_ref):
        gg = g_ref[...]
        nm = ADAM_B1 * m_ref[...] + (1.0 - ADAM_B1) * gg
        nv = ADAM_B2 * v_ref[...] + (1.0 - ADAM_B2) * (gg * gg)
        m_hat = nm / (1.0 - ADAM_B1 ** ADAM_STEP)
        v_hat = nv / (1.0 - ADAM_B2 ** ADAM_STEP)
        d_ref[...] = -ADAM_LR * (m_hat / (jnp.sqrt(v_hat) + ADAM_EPS) + ADAM_WD * w_ref[...])
        nm_ref[...] = nm
        nv_ref[...] = nv

    blk = pl.BlockSpec((tl, C), lambda i: (i, 0))
    return pl.pallas_call(body, name=name, grid=(R // tl,), in_specs=[blk] * 4, out_specs=[blk] * 3,
                          out_shape=[S((R, C), f32)] * 3, compiler_params=_cparams(("parallel",)))(w, g, m, v)


def sum_slots(x, name, out_dtype=f32):
    n, R, C = x.shape
    tl = _pick(R, (512, 256, 128, 64, 32, 16, 8))

    def body(x_ref, o_ref):
        acc = x_ref[0].astype(f32)
        for s in range(1, n):
            acc = acc + x_ref[s].astype(f32)
        o_ref[...] = acc.astype(o_ref.dtype)

    return pl.pallas_call(body, name=name, grid=(R // tl,),
                          in_specs=[pl.BlockSpec((n, tl, C), lambda i: (0, i, 0))],
                          out_specs=pl.BlockSpec((tl, C), lambda i: (i, 0)),
                          out_shape=S((R, C), out_dtype), compiler_params=_cparams(("parallel",)))(x)


_ANY = pl.BlockSpec(memory_space=pl.ANY)


def _coords():
    return lax.axis_index("x"), lax.axis_index("y"), lax.axis_index("c")


def chip_exchange(src, name, same=False):
    out_shape = (N_CHIPS,) + src.shape if same else src.shape
    assert out_shape[0] == N_CHIPS

    def body(src_ref, out_ref, send_sems, recv_sems, local_sem):
        x, y, c = _coords()
        me = 2 * x + y
        slot = (lambda j: src_ref) if same else (lambda j: src_ref.at[j])
        mine = pltpu.make_async_copy(slot(me), out_ref.at[me], local_sem)
        mine.start()
        peers = [(1 - x, y), (x, 1 - y), (1 - x, 1 - y)]
        copies = []
        for k, (px, py) in enumerate(peers):
            cp = pltpu.make_async_remote_copy(
                src_ref=slot(2 * px + py), dst_ref=out_ref.at[me],
                send_sem=send_sems.at[k], recv_sem=recv_sems.at[k],
                device_id=(px, py, c), device_id_type=MESH)
            cp.start()
            copies.append(cp)
        for k, (px, py) in enumerate(peers):
            pltpu.make_async_remote_copy(
                src_ref=slot(me), dst_ref=out_ref.at[2 * px + py],
                send_sem=send_sems.at[k], recv_sem=recv_sems.at[k],
                device_id=(px, py, c), device_id_type=MESH).wait_recv()
        for cp in copies:
            cp.wait_send()
        mine.wait()

    return pl.pallas_call(
        body, name=name, in_specs=[_ANY], out_specs=_ANY, out_shape=S(out_shape, src.dtype),
        scratch_shapes=[pltpu.SemaphoreType.DMA((3,)), pltpu.SemaphoreType.DMA((3,)), pltpu.SemaphoreType.DMA],
    )(src)


def sibling_exchange(src, name):
    def body(src_ref, out_ref, send_sem, recv_sem):
        x, y, c = _coords()
        cp = pltpu.make_async_remote_copy(src_ref=src_ref, dst_ref=out_ref, send_sem=send_sem, recv_sem=recv_sem,
                                          device_id=(x, y, 1 - c), device_id_type=MESH)
        cp.start()
        cp.wait()

    return pl.pallas_call(
        body, name=name, in_specs=[_ANY], out_specs=_ANY, out_shape=S(src.shape, src.dtype),
        scratch_shapes=[pltpu.SemaphoreType.DMA, pltpu.SemaphoreType.DMA],
    )(src)


def _rows(c, half_rows):
    return pl.ds(pl.multiple_of(c * half_rows, 16), half_rows)


def _slot(ref, kind, j, rows=None):
    if kind == "slot":
        return ref.at[j] if rows is None else ref.at[j, rows]
    cs = ref.shape[1] // N_CHIPS
    cols = pl.ds(pl.multiple_of(j * cs, 128), cs)
    return ref.at[:, cols] if rows is None else ref.at[rows, cols]


def _sems(n):
    return [pltpu.SemaphoreType.DMA((n,))]


def ag_layer(bufs, kinds, name):
    n = len(bufs)

    def body(*refs):
        outs = refs[n:2 * n]
        send1, recv1, send2, recv2 = refs[2 * n:]
        x, y, c = _coords()
        me = 2 * x + y
        peers = [(1 - x, y), (x, 1 - y), (1 - x, 1 - y)]
        rh = [(a.shape[1] if k == "slot" else a.shape[0]) // 2 for a, k in zip(bufs, kinds)]
        sends = []
        for t in range(n):
            own = _slot(outs[t], kinds[t], me, _rows(c, rh[t]))
            for k, (px, py) in enumerate(peers):
                cp = pltpu.make_async_remote_copy(
                    src_ref=own, dst_ref=own,
                    send_sem=send1.at[3 * t + k], recv_sem=recv1.at[3 * t + k], device_id=(px, py, c), device_id_type=MESH)
                cp.start()
                sends.append(cp)
        for k, (px, py) in enumerate(peers):
            for t in range(n):
                landed = _slot(outs[t], kinds[t], 2 * px + py, _rows(c, rh[t]))
                pltpu.make_async_remote_copy(src_ref=landed, dst_ref=landed, send_sem=send1.at[3 * t + k],
                                             recv_sem=recv1.at[3 * t + k], device_id=(px, py, c), device_id_type=MESH).wait_recv()
                cp = pltpu.make_async_remote_copy(src_ref=landed, dst_ref=landed, send_sem=send2.at[3 * t + k],
                                                  recv_sem=recv2.at[3 * t + k], device_id=(x, y, 1 - c), device_id_type=MESH)
                cp.start()
                sends.append(cp)
        for k, (px, py) in enumerate(peers):
            for t in range(n):
                other = _slot(outs[t], kinds[t], 2 * px + py, _rows(1 - c, rh[t]))
                pltpu.make_async_remote_copy(src_ref=other, dst_ref=other, send_sem=send2.at[3 * t + k],
                                             recv_sem=recv2.at[3 * t + k], device_id=(x, y, 1 - c), device_id_type=MESH).wait_recv()
        for cp in sends:
            cp.wait_send()

    return pl.pallas_call(body, name=name, in_specs=[_ANY] * n, out_specs=[_ANY] * n,
                          out_shape=[S(a.shape, a.dtype) for a in bufs], input_output_aliases={t: t for t in range(n)},
                          scratch_shapes=_sems(3 * n) * 4)(*bufs)


def rs_pair(grads, kinds, name):
    n = len(grads)

    def half_shape(a, k):
        return (a.shape[0], a.shape[1] // 2, a.shape[2]) if k == "slot" else (a.shape[0] // 2, a.shape[1])

    def half(ref, k, c):
        return ref.at[:, _rows(c, ref.shape[1] // 2)] if k == "slot" else ref.at[_rows(c, ref.shape[0] // 2)]

    shapes = [S(half_shape(a, k), a.dtype) for a, k in zip(grads, kinds)]

    def body(*refs):
        srcs, got = refs[:n], refs[n:2 * n]
        send, recv = refs[2 * n:]
        x, y, c = _coords()
        started = []
        for t in range(n):
            cp = pltpu.make_async_remote_copy(src_ref=half(srcs[t], kinds[t], 1 - c), dst_ref=got[t], send_sem=send.at[t],
                                              recv_sem=recv.at[t], device_id=(x, y, 1 - c), device_id_type=MESH)
            cp.start()
            started.append(cp)
        for cp in started:
            cp.wait()

    return pl.pallas_call(body, name=name, in_specs=[_ANY] * n, out_specs=[_ANY] * n, out_shape=shapes,
                          scratch_shapes=_sems(n) * 2)(*grads)


def rs_chips(sums, kinds, name):
    n = len(sums)

    def part_shape(a, k):
        return a.shape[1:] if k == "slot" else (a.shape[0], a.shape[1] // N_CHIPS)

    shapes = [S((N_CHIPS,) + part_shape(a, k), a.dtype) for a, k in zip(sums, kinds)]

    def body(*refs):
        srcs, outs = refs[:n], refs[n:2 * n]
        send, recv, lsem = refs[2 * n:]
        x, y, c = _coords()
        me = 2 * x + y
        peers = [(1 - x, y), (x, 1 - y), (1 - x, 1 - y)]
        started = []
        for t in range(n):
            cp = pltpu.make_async_copy(_slot(srcs[t], kinds[t], me), outs[t].at[me], lsem.at[t])
            cp.start()
            started.append(cp)
        sends = []
        for t in range(n):
            for k, (px, py) in enumerate(peers):
                cp = pltpu.make_async_remote_copy(
                    src_ref=_slot(srcs[t], kinds[t], 2 * px + py), dst_ref=outs[t].at[me],
                    send_sem=send.at[3 * t + k], recv_sem=recv.at[3 * t + k], device_id=(px, py, c), device_id_type=MESH)
                cp.start()
                sends.append(cp)
        for t in range(n):
            for k, (px, py) in enumerate(peers):
                landed = outs[t].at[2 * px + py]
                pltpu.make_async_remote_copy(src_ref=landed, dst_ref=landed, send_sem=send.at[3 * t + k],
                                             recv_sem=recv.at[3 * t + k], device_id=(px, py, c), device_id_type=MESH).wait_recv()
        for cp in sends:
            cp.wait_send()
        for cp in started:
            cp.wait()

    return pl.pallas_call(body, name=name, in_specs=[_ANY] * n, out_specs=[_ANY] * n, out_shape=shapes,
                          scratch_shapes=_sems(3 * n) * 2 + _sems(n))(*sums)


def rs_share(gs, name):
    n = len(gs)

    def body(*refs):
        outs = refs[n:2 * n]
        send, recv = refs[2 * n:]
        x, y, c = _coords()
        started = []
        for t in range(n):
            cp = pltpu.make_async_remote_copy(src_ref=outs[t].at[c], dst_ref=outs[t].at[c], send_sem=send.at[t],
                                              recv_sem=recv.at[t], device_id=(x, y, 1 - c), device_id_type=MESH)
            cp.start()
            started.append(cp)
        for cp in started:
            cp.wait()

    return pl.pallas_call(body, name=name, in_specs=[_ANY] * n, out_specs=[_ANY] * n,
                          out_shape=[S(a.shape, a.dtype) for a in gs], input_output_aliases={t: t for t in range(n)},
                          scratch_shapes=_sems(n) * 2)(*gs)


def _core_index():
    return lax.axis_index("c").astype(jnp.int32).reshape(1)


def add_half(g, got, kind, name):
    g4 = g.reshape((g.shape[0] if kind == "slot" else 1, 2, -1, g.shape[-1]))
    parts, _, rh, C = g4.shape
    got3 = got.reshape(parts, rh, C)
    tl = _pick(rh, (256, 128, 64, 32, 16, 8))

    def body(c_ref, g_ref, got_ref, o_ref):
        o_ref[...] = (g_ref[...] + got_ref[...]).astype(o_ref.dtype)

    blk = pl.BlockSpec((None, tl, C), lambda s, i, c: (s, i, 0))
    out = pl.pallas_call(
        body, name=name,
        grid_spec=pltpu.PrefetchScalarGridSpec(
            num_scalar_prefetch=1, grid=(parts, rh // tl),
            in_specs=[pl.BlockSpec((None, None, tl, C), lambda s, i, c: (s, c[0], i, 0)), blk], out_specs=blk),
        out_shape=S((parts, rh, C), bf16), compiler_params=_cparams(("parallel", "parallel")),
    )(_core_index(), g4, got3)
    return out.reshape(got.shape)


def sum_slots_half(x, name):
    n, rh, C = x.shape
    tl = _pick(rh, (512, 256, 128, 64, 32, 16, 8))

    def body(c_ref, x_ref, o_ref):
        acc = x_ref[0].astype(f32)
        for s in range(1, n):
            acc = acc + x_ref[s].astype(f32)
        o_ref[...] = acc

    return pl.pallas_call(
        body, name=name,
        grid_spec=pltpu.PrefetchScalarGridSpec(
            num_scalar_prefetch=1, grid=(rh // tl,),
            in_specs=[pl.BlockSpec((n, tl, C), lambda i, c: (0, i, 0))],
            out_specs=pl.BlockSpec((None, tl, C), lambda i, c: (c[0], i, 0))),
        out_shape=S((2, rh, C), f32), compiler_params=_cparams(("parallel",)),
    )(_core_index(), x)


def _block_diag(blocks):
    G, a, b = blocks.shape
    eye = jnp.eye(G, dtype=blocks.dtype)
    return (blocks[:, :, None, :] * eye[:, None, :, None]).reshape(G * a, G * b)


def _flat_pack(arrs, dtype, lanes=1024, row_mult=16):
    flat = jnp.concatenate([a.reshape(-1).astype(dtype) for a in arrs])
    n = flat.shape[0]
    per = lanes * row_mult
    pad = (-n) % per
    if pad:
        flat = jnp.concatenate([flat, jnp.zeros((pad,), dtype)])
    return flat.reshape(-1, lanes)


def _flat_unpack(buf, shapes):
    flat = buf.reshape(-1)
    out, off = [], 0
    for s in shapes:
        n = math.prod(s)
        out.append(flat[off:off + n].reshape(s))
        off += n
    return out


def _layer_weights(p, H):
    w_in = p["w_in"]
    D = w_in.shape[0]
    s5w = p["s5_d"].shape[-1]
    hk = p["dn_proj_w"].shape[0]
    off_z = s5w + 4 * hk
    off_a = off_z + 2 * H
    wp = jnp.concatenate([w_in[:, :off_z], w_in[:, off_a:], jnp.pad(w_in[:, off_z:off_a], ((0, 0), (0, 128 - 2 * H)))], axis=1)
    q = {}
    q["Wu"] = Win(wp, 0, s5w)
    q["Wqkv"] = Win(wp, s5w, 3 * hk)
    q["Wz"] = Win(wp, s5w + 3 * hk, hk)
    q["Wgs"] = Win(wp, off_z, D)
    q["Wgd"] = Win(wp, off_z + D, D)
    q["Wba"] = Win(wp, off_z + 2 * D, 128)
    q["Wga"], q["Wgb"] = Win(p["s5_glu_w"], 0, D), Win(p["s5_glu_w"], D, D)
    F = p["ffn_down"].shape[0]
    q["Wupa"], q["Wupv"] = Win(p["ffn_up"], 0, F), Win(p["ffn_up"], F, F)
    q["cwa"], q["cwv"] = p["ffn_conv_w"][:, :F], p["ffn_conv_w"][:, F:]
    for k in ("dn_proj_w", "w_out", "ffn_down", "dn_conv_w", "mix_norm_w", "ffn_norm_w", "dn_norm_w",
              "dn_a_log", "dn_dt_bias", "s5_d"):
        q[k] = p[k]
    return q


def _s5_params(p, tag):
    G, P = p["s5_a_re"].shape
    HG = p["s5_b_re"].shape[-1]
    col = lambda a: a.reshape(G * P, 1)
    lr, li = col(p["s5_a_re"]), col(p["s5_a_im"])
    logdt = col(jnp.broadcast_to(p["s5_log_dt"][:, None], (G, P)))
    br, bi = p["s5_b_re"].reshape(G * P, HG), p["s5_b_im"].reshape(G * P, HG)
    ar, ai, bbr, bbi = s5_disc(lr, li, logdt, br, bi, "s5_disc")
    bd = lambda m: _block_diag(m.reshape(G, P, HG).transpose(0, 2, 1)).astype(bf16)
    cd = lambda m: _block_diag(m.transpose(0, 2, 1)).astype(bf16)
    return dict(lr=lr, li=li, logdt=logdt, br=br, bi=bi, ar=ar.reshape(1, G * P), ai=ai.reshape(1, G * P),
                Bre=bd(bbr), Bim=bd(bbi), CreT=cd(p["s5_c_re"]), mCimT=cd(-p["s5_c_im"]), G=G, P=P, HG=HG)


def layer_fwd(x, q, s5, H):
    r = {"x": x}
    h1 = rms_fwd(x, q["mix_norm_w"], "rms_mix")
    r["h1"] = h1
    u32, u16 = mm(h1, q["Wu"], "nn", "proj_u", out_dtypes=(f32, bf16), epi=lambda a: (a, a))
    qkv = mm(h1, q["Wqkv"], "nn", "proj_qkv")
    z = mm(h1, q["Wz"], "nn", "proj_z")
    ba = mm(h1, q["Wba"], "nn", "proj_ba")
    gs = mm(h1, q["Wgs"], "nn", "proj_gs")
    gd = mm(h1, q["Wgd"], "nn", "proj_gd")
    r.update(u32=u32, u16=u16, qkv=qkv, z=z, ba=ba, gs=gs, gd=gd)
    HG, P = s5["HG"], s5["P"]
    bur = mm_bd(u16, s5["Bre"], "nn", "s5_bu_re", HG, P)
    bui = mm_bd(u16, s5["Bim"], "nn", "s5_bu_im", HG, P)
    xr, xi = s5_scan_fwd(bur, bui, s5["ar"], s5["ai"], "s5_scan_fwd")
    y1 = mm_bd(xr, s5["CreT"], "nn", "s5_y_re", P, HG)
    ypre, ys5 = mm_bd(xi, s5["mCimT"], "nn", "s5_y_im", P, HG, extras=(y1, u32, q["s5_d"]), epi=_s5_y_epi, out_dtypes=(f32, bf16))
    ga = mm(ys5, q["Wga"], "nn", "glu_a")
    gb = mm(ys5, q["Wgb"], "nn", "glu_b")
    r.update(xr=xr, xi=xi, ypre=ypre, ys5=ys5, ga=ga, gb=gb)
    qn, kn, vv, bg = dn_prep(qkv, q["dn_conv_w"], ba, q["dn_a_log"], q["dn_dt_bias"], H, "dn_prep")
    L = x.shape[0]
    bgt = bg.reshape(L // DN_CHUNK, DN_CHUNK, 2 * H).transpose(0, 2, 1)
    o, ss = dn_chunk_fwd(qn, kn, vv, bg, bgt, H, "dn_chunk_fwd")
    ydn = dn_gate(o, z, q["dn_norm_w"], H, "dn_gate")
    brdn, merged = mm(ydn, q["dn_proj_w"], "nn", "dn_proj", extras=(gs, gd, ga, gb), epi=_merge_epi, out_dtypes=(f32, bf16))
    r.update(qn=qn, kn=kn, vv=vv, bg=bg, bgt=bgt, ss=ss, o=o, ydn=ydn, brdn=brdn, merged=merged)
    x1 = mm(merged, q["w_out"], "nn", "out_proj", extras=(x,), epi=_add)
    h2 = rms_fwd(x1, q["ffn_norm_w"], "rms_ffn")
    ua = mm(h2, q["Wupa"], "nn", "ffn_up_a")
    uv = mm(h2, q["Wupv"], "nn", "ffn_up_v")
    hmid = ffn_mid(ua, uv, q["cwa"], q["cwv"], "ffn_mid")
    x2 = mm(hmid, q["ffn_down"], "nn", "ffn_down", extras=(x1,), epi=_add)
    r.update(x1=x1, h2=h2, ua=ua, uv=uv, hmid=hmid)
    return x2, r


def layer_bwd(dx2, dx2b, r, q, s5, H):
    g = {}
    dhmid = mm(dx2b, q["ffn_down"], "nt", "d_hmid")
    g["ffn_down"] = mm(r["hmid"], dx2b, "tn", "dw_ffn_down")
    dca, dcv, dwa, dwv = ffn_mid_bwd(r["ua"], r["uv"], q["cwa"], q["cwv"], dhmid, "ffn_mid_bwd")
    g["ffn_conv_w"] = jnp.concatenate([dwa, dwv], axis=1)
    dua = conv_t(dca, q["cwa"], "ffn_conv_t_a")
    duv = conv_t(dcv, q["cwv"], "ffn_conv_t_v")
    dh2 = mm(dua, q["Wupa"], "nt", "d_h2_a")
    dh2 = mm(duv, q["Wupv"], "nt", "d_h2_v", extras=(dh2,), epi=_add)
    F = dua.shape[1]
    dwup = mm(r["h2"], dua, "tn", "dw_up_a", out_into=(2 * F, 0, None))
    g["ffn_up"] = mm(r["h2"], duv, "tn", "dw_up_v", out_into=(2 * F, F, dwup))
    dx1, dx1b, dffn_w = rms_bwd(r["x1"], q["ffn_norm_w"], dh2, dx2, "rms_ffn_bwd")
    g["ffn_norm_w"] = dffn_w[0]
    dm = mm(dx1b, q["w_out"], "nt", "d_merged")
    g["w_out"] = mm(r["merged"], dx1b, "tn", "dw_out")
    dgs, dgd, dga, dgb, dbrdn = merge_bwd(dm, r["gs"], r["gd"], r["ga"], r["gb"], r["brdn"], "merge_bwd")
    dydn = mm(dbrdn, q["dn_proj_w"], "nt", "d_ydn")
    g["dn_proj_w"] = mm(r["ydn"], dbrdn, "tn", "dw_dn_proj")
    do, dz, dnw = dn_gate_bwd(r["o"], r["z"], q["dn_norm_w"], dydn, H, "dn_gate_bwd")
    g["dn_norm_w"] = dnw[0]
    dq, dk, dv, dbg, dgt = dn_chunk_bwd(r["qn"], r["kn"], r["vv"], r["bg"], r["bgt"], r["ss"], do, H, "dn_chunk_bwd")
    L = dq.shape[0]
    dbg = dbg + jnp.concatenate([jnp.zeros((L, H), f32), dgt.transpose(0, 2, 1).reshape(L, H)], axis=1)
    dc, dba, dcw, dal, ddt = dn_prep_bwd(r["qkv"], q["dn_conv_w"], r["ba"], q["dn_a_log"], q["dn_dt_bias"],
                                         dq, dk, dv, dbg, H, "dn_prep_bwd")
    g["dn_conv_w"], g["dn_a_log"], g["dn_dt_bias"] = dcw, dal[0], ddt[0]
    dqkv = conv_t(dc, q["dn_conv_w"], "dn_conv_t")
    dys5 = mm(dga, q["Wga"], "nt", "d_ys5_a")
    dys5 = mm(dgb, q["Wgb"], "nt", "d_ys5_b", extras=(dys5,), epi=_add)
    D = dga.shape[1]
    dwglu = mm(r["ys5"], dga, "tn", "dw_glu_a", out_into=(2 * D, 0, None))
    g["s5_glu_w"] = mm(r["ys5"], dgb, "tn", "dw_glu_b", out_into=(2 * D, D, dwglu))
    dyp, du_direct, dd = s5_out_bwd(dys5, r["ypre"], r["u32"], q["s5_d"], "s5_out_bwd")
    g["s5_d"] = dd[0]
    G, P, HG = s5["G"], s5["P"], s5["HG"]
    gdr = mm_bd(dyp, s5["CreT"], "nt", "s5_gx_re", P, HG)
    gdi = mm_bd(dyp, s5["mCimT"], "nt", "s5_gx_im", P, HG)
    dcre = _diag_blocks(mm_bd(r["xr"], dyp, "tn", "dw_s5_c_re", P, HG), G, P, HG)
    dcim = _diag_blocks(mm_bd(r["xi"], dyp, "tn", "dw_s5_c_im", P, HG), G, P, HG)
    g["s5_c_re"], g["s5_c_im"] = dcre.transpose(0, 2, 1), -dcim.transpose(0, 2, 1)
    gxr, gxi, dar, dai = s5_scan_bwd(gdr, gdi, r["xr"], r["xi"], s5["ar"], s5["ai"], "s5_scan_bwd")
    dbre = _diag_blocks(mm_bd(r["u16"], gxr, "tn", "dw_s5_b_re", HG, P), G, HG, P)
    dbim = _diag_blocks(mm_bd(r["u16"], gxi, "tn", "dw_s5_b_im", HG, P), G, HG, P)
    tocol = lambda m: m.transpose(0, 2, 1).reshape(G * P, HG)
    dlr, dli, dlogdt, dbr, dbi = s5_disc_bwd(s5["lr"], s5["li"], s5["logdt"], s5["br"], s5["bi"],
                                             dar.reshape(G * P, 1), dai.reshape(G * P, 1), tocol(dbre), tocol(dbim), "s5_disc_bwd")
    g["s5_a_re"], g["s5_a_im"] = dlr.reshape(G, P), dli.reshape(G, P)
    g["s5_log_dt"] = jnp.sum(dlogdt.reshape(G, P), axis=1)
    g["s5_b_re"], g["s5_b_im"] = dbr.reshape(G, P, HG), dbi.reshape(G, P, HG)
    du = mm_bd(gxr, s5["Bre"], "nt", "d_u_re", HG, P)
    du = mm_bd(gxi, s5["Bim"], "nt", "d_u_im", HG, P, extras=(du, du_direct), epi=lambda a, b, c: (a + b + c,), out_dtypes=(bf16,))
    h1 = r["h1"]
    segs = [("Wu", du), ("Wqkv", dqkv), ("Wz", dz), ("Wba", dba), ("Wgs", dgs), ("Wgd", dgd)]
    dh1 = None
    dws = []
    for name, dseg in segs:
        if dh1 is None:
            dh1 = mm(dseg, q[name], "nt", "d_h1_" + name)
        else:
            dh1 = mm(dseg, q[name], "nt", "d_h1_" + name, extras=(dh1,), epi=_add)
        dw = mm(h1, dseg, "tn", "dw_in_" + name)
        dws.append(dw[:, :2 * H] if name == "Wba" else dw)
    g["w_in"] = jnp.concatenate(dws, axis=1)
    dx, dxb, dmix = rms_bwd(r["x"], q["mix_norm_w"], dh1, dx1, "rms_mix_bwd")
    g["mix_norm_w"] = dmix[0]
    return dx, dxb, g


BIG = ("w_in", "s5_glu_w", "dn_proj_w", "w_out", "ffn_up", "ffn_down")
SHARDED_SMALL = ("dn_conv_w", "ffn_conv_w")
COL_SHARDED = ("w_in", "s5_glu_w", "dn_proj_w", "ffn_up", "dn_conv_w", "ffn_conv_w")
REPL = ("mix_norm_w", "s5_log_dt", "s5_a_re", "s5_a_im", "s5_b_re", "s5_b_im", "s5_c_re", "s5_c_im", "s5_d",
        "dn_a_log", "dn_dt_bias", "dn_norm_w", "ffn_norm_w")
WEIGHTS = ['mix_norm_w', 'w_in', 's5_log_dt', 's5_a_re', 's5_a_im', 's5_b_re', 's5_b_im', 's5_c_re', 's5_c_im', 's5_d',
           's5_glu_w', 'dn_conv_w', 'dn_a_log', 'dn_dt_bias', 'dn_norm_w', 'dn_proj_w', 'w_out', 'ffn_norm_w', 'ffn_up',
           'ffn_conv_w', 'ffn_down', 'final_norm_w']


def _join_shards(name, shards):
    return jnp.concatenate(shards, axis=-1 if name in COL_SHARDED else -2)


KINDS = {"w_in": "slot", "s5_glu_w": "col", "dn_proj_w": "col", "w_out": "slot", "ffn_up": "col", "ffn_down": "slot"}


def gather_layer_weights(shards):
    me = 2 * lax.axis_index("x") + lax.axis_index("y")
    bufs = []
    for n in BIG:
        blk = shards[n].astype(bf16)
        if KINDS[n] == "slot":
            bufs.append(lax.dynamic_update_slice(jnp.zeros((N_CHIPS,) + blk.shape, bf16), blk[None], (me, 0, 0)))
        else:
            cs = blk.shape[1]
            bufs.append(lax.dynamic_update_slice(jnp.zeros((blk.shape[0], N_CHIPS * cs), bf16), blk, (0, me * cs)))
    outs = ag_layer(bufs, [KINDS[n] for n in BIG], "ag_layer")
    full = dict(zip(BIG, outs))
    for n in ("w_out", "ffn_down"):
        full[n] = full[n].reshape(-1, full[n].shape[-1])
    w = full["w_in"]
    full["w_in"] = w.transpose(1, 0, 2).reshape(w.shape[1], -1)
    return full


def gather_small_sharded(shards):
    names = SHARDED_SMALL
    shapes = [shards[n].shape for n in names]
    got = chip_exchange(_flat_pack([shards[n] for n in names], f32, lanes=128, row_mult=8), "ag_small", same=True)
    per_chip = [_flat_unpack(got[j], shapes) for j in range(N_CHIPS)]
    return {n: _join_shards(n, [per_chip[j][k] for j in range(N_CHIPS)]) for k, n in enumerate(names)}


def reduce_scatter_layer_grads(g):
    tens = []
    for n in BIG:
        a = g[n]
        if n == "w_in":
            a = a.reshape(a.shape[0], N_CHIPS, -1).transpose(1, 0, 2)
        elif KINDS[n] == "slot":
            a = a.reshape(N_CHIPS, -1, a.shape[-1])
        tens.append(a)
    kinds = [KINDS[n] for n in BIG]
    theirs = rs_pair(tens, kinds, "rs_pair")
    sums = [add_half(a, t, k, "rs_pair_sum_" + n) for n, a, t, k in zip(BIG, tens, theirs, kinds)]
    parts = rs_chips(sums, kinds, "rs_chips")
    halves = [sum_slots_half(x, "rs_chip_sum_" + n) for n, x in zip(BIG, parts)]
    return {n: a.reshape(-1, a.shape[-1]) for n, a in zip(BIG, rs_share(halves, "rs_share"))}


def all_reduce_small(arrs):
    shapes = [a.shape for a in arrs]
    pack = _flat_pack(arrs, f32, lanes=1024, row_mult=64)
    from_chips = chip_exchange(pack, "ar_chips", same=True)
    from_sib = sibling_exchange(from_chips, "ar_sibling")
    c = lax.axis_index("c")
    both = jnp.concatenate([jnp.where(c == 0, from_chips, from_sib), jnp.where(c == 0, from_sib, from_chips)], axis=0)
    return _flat_unpack(sum_slots(both, "ar_sum"), shapes)


def kernel(x, mix_norm_w, w_in, s5_log_dt, s5_a_re, s5_a_im, s5_b_re, s5_b_im, s5_c_re, s5_c_im, s5_d, s5_glu_w, dn_conv_w, dn_a_log, dn_dt_bias, dn_norm_w, dn_proj_w, w_out, ffn_norm_w, ffn_up, ffn_conv_w, ffn_down, final_norm_w, loss_target, m_mix_norm_w, m_w_in, m_s5_log_dt, m_s5_a_re, m_s5_a_im, m_s5_b_re, m_s5_b_im, m_s5_c_re, m_s5_c_im, m_s5_d, m_s5_glu_w, m_dn_conv_w, m_dn_a_log, m_dn_dt_bias, m_dn_norm_w, m_dn_proj_w, m_w_out, m_ffn_norm_w, m_ffn_up, m_ffn_conv_w, m_ffn_down, m_final_norm_w, v_mix_norm_w, v_w_in, v_s5_log_dt, v_s5_a_re, v_s5_a_im, v_s5_b_re, v_s5_b_im, v_s5_c_re, v_s5_c_im, v_s5_d, v_s5_glu_w, v_dn_conv_w, v_dn_a_log, v_dn_dt_bias, v_dn_norm_w, v_dn_proj_w, v_w_out, v_ffn_norm_w, v_ffn_up, v_ffn_conv_w, v_ffn_down, v_final_norm_w):
    args = locals()
    W = {n: args[n] for n in WEIGHTS}
    M = {n: args["m_" + n] for n in WEIGHTS}
    V = {n: args["v_" + n] for n in WEIGHTS}
    depth = mix_norm_w.shape[0]
    H = dn_a_log.shape[1]
    xs = x[0]
    target = loss_target[0]

    layers = []
    conv_full = gather_small_sharded({n: W[n] for n in SHARDED_SMALL})
    for l in range(depth):
        p = gather_layer_weights({n: W[n][l] for n in BIG})
        for n in REPL:
            p[n] = W[n][l]
        for n in SHARDED_SMALL:
            p[n] = conv_full[n][l]
        for n in ("mix_norm_w", "ffn_norm_w", "dn_norm_w", "dn_a_log", "dn_dt_bias", "s5_d"):
            p[n] = p[n].reshape(1, -1)
        layers.append((_layer_weights(p, H), _s5_params(p, l)))
    res = []
    for l in range(depth):
        xs, r = layer_fwd(xs, layers[l][0], layers[l][1], H)
        res.append(r)
    dx, dxb, loss_part, dfinal = loss_head(xs, final_norm_w.reshape(1, -1), target, "loss_head")
    loss = lax.psum(loss_part[0, 0], ("x", "y", "c"))

    grads = [None] * depth
    for l in reversed(range(depth)):
        dx, dxb, grads[l] = layer_bwd(dx, dxb, res[l], layers[l][0], layers[l][1], H)
    grad_x = dx[None]

    G = {}
    sharded = [reduce_scatter_layer_grads(grads[l]) for l in range(depth)]
    for n in BIG:
        G[n] = jnp.stack([sharded[l][n] for l in range(depth)])
    small_names = REPL + SHARDED_SMALL
    small = [jnp.stack([grads[l][n] for l in range(depth)]) for n in small_names] + [dfinal[0]]
    for n, a in zip(small_names + ("final_norm_w",), all_reduce_small(small)):
        G[n] = a
    chip = 2 * lax.axis_index("x") + lax.axis_index("y")
    for n in SHARDED_SMALL:
        cs = W[n].shape[-1]
        G[n] = lax.dynamic_slice_in_dim(G[n], chip * cs, cs, axis=-1)

    delta, new_m, new_v = {}, {}, {}
    for n in WEIGHTS:
        shape = W[n].shape
        size = math.prod(shape)
        if n in BIG + SHARDED_SMALL:
            two_d = (size // shape[-1], shape[-1])
        else:
            two_d = (size // 128, 128) if size % 128 == 0 else (1, size)
        d, nm, nv = adamw(W[n].reshape(two_d), G[n].reshape(two_d), M[n].reshape(two_d), V[n].reshape(two_d), "adamw_" + n)
        delta[n], new_m[n], new_v[n] = d.reshape(shape), nm.reshape(shape), nv.reshape(shape)
        G[n] = G[n].reshape(shape)
    return (loss, grad_x, *[G[n] for n in WEIGHTS], *[delta[n] for n in WEIGHTS],
            *[new_m[n] for n in WEIGHTS], *[new_v[n] for n in WEIGHTS])
```

```python
import functools
import math

import jax
import jax.numpy as jnp
from jax import lax
from jax.experimental import pallas as pl
from jax.experimental.pallas import tpu as pltpu

f32 = jnp.float32
bf16 = jnp.bfloat16
S = jax.ShapeDtypeStruct

NORM_EPS = 1e-6
DN_CHUNK = 64
S5_GROUP = 16
ADAM_LR, ADAM_B1, ADAM_B2, ADAM_EPS, ADAM_WD, ADAM_STEP = 0.001, 0.9, 0.999, 1e-08, 0.01, 10
VMEM_LIMIT_BYTES = 56 * 1024 * 1024
HALO = 8
MESH = pl.DeviceIdType.MESH
N_CHIPS = 4


def _pick(n, cands):
    for c in cands:
        if n % c == 0:
            return c
    return n


MM_VMEM_BUDGET = 40 * 1024 * 1024
MM_MAX_TK = 2816


def _pick_k(K):
    if K <= MM_MAX_TK or K % 128:
        return K
    return max(d for d in range(128, MM_MAX_TK + 1, 128) if K % d == 0)


def _cparams(sem):
    return pltpu.CompilerParams(dimension_semantics=sem, vmem_limit_bytes=VMEM_LIMIT_BYTES)


_DIMS = {"nn": ((1,), (0,)), "nt": ((1,), (1,)), "tn": ((0,), (0,))}


class Win:
    def __init__(self, arr, c0, nc):
        self.arr, self.c0, self.nc = arr, c0, nc


def mm(a, b, mode, name, extras=(), epi=None, out_dtypes=(f32,), out_into=None):
    barr, c0 = (b.arr, b.c0) if isinstance(b, Win) else (b, 0)
    if mode == "tn":
        K, M = a.shape
    else:
        M, K = a.shape
    if mode == "nt":
        N, K2 = barr.shape
        K2 = b.nc if isinstance(b, Win) else K2
        n_off, k_off = 0, c0
    else:
        K2, N = barr.shape
        N = b.nc if isinstance(b, Win) else N
        n_off, k_off = c0, 0
    assert K == K2, (a.shape, barr.shape, mode)
    o_tot, o_off, o_alias = out_into if out_into is not None else (N, 0, None)
    tm = _pick(M, (1024, 512, 256, 128, 64, 32, 16, 8))
    tn = _pick(math.gcd(math.gcd(N, n_off), o_off), (1024, 512, 256, 128))
    tk = _pick_k(math.gcd(K, k_off))

    def vmem_estimate(tm_):
        tiles = tm_ * tk * a.dtype.itemsize + tk * tn * barr.dtype.itemsize
        tiles += sum(tm_ * tn * e.dtype.itemsize for e in extras if e.shape == (M, N))
        tiles += sum(tm_ * tn * jnp.dtype(dt).itemsize for dt in out_dtypes)
        return 2 * tiles + 3 * tm_ * tn * 4

    while vmem_estimate(tm) > MM_VMEM_BUDGET and tm % 16 == 0 and tm > 128:
        tm //= 2
    nk = K // tk
    assert M % tm == 0 and N % tn == 0 and K % tk == 0 and n_off % tn == 0 and k_off % tk == 0 and o_off % tn == 0
    nb, kb, ob = n_off // tn, k_off // tk, o_off // tn
    if mode == "tn":
        a_spec = pl.BlockSpec((tk, tm), lambda j, i, k: (k, i))
    else:
        a_spec = pl.BlockSpec((tm, tk), lambda j, i, k: (i, k))
    if mode == "nt":
        b_spec = pl.BlockSpec((tn, tk), lambda j, i, k: (j, k + kb))
    else:
        b_spec = pl.BlockSpec((tk, tn), lambda j, i, k: (k, j + nb))
    ex_specs = []
    for e in extras:
        if e.shape == (M, N):
            ex_specs.append(pl.BlockSpec((tm, tn), lambda j, i, k: (i, j)))
        elif e.shape == (1, N):
            ex_specs.append(pl.BlockSpec((1, tn), lambda j, i, k: (0, j)))
        elif e.shape == (M, 1):
            ex_specs.append(pl.BlockSpec((tm, 1), lambda j, i, k: (i, 0)))
        else:
            raise ValueError((e.shape, M, N))
    ne, no = len(extras), len(out_dtypes)
    na = 1 if o_alias is not None else 0
    assert out_into is None or no == 1
    dims = (_DIMS[mode], ((), ()))

    def body(a_ref, b_ref, *rest):
        ex, outs = rest[:ne], rest[ne + na:ne + na + no]
        p = lax.dot_general(a_ref[...].astype(bf16), b_ref[...].astype(bf16), dims, preferred_element_type=f32)

        def finish(acc):
            res = epi(acc, *[e[...] for e in ex]) if epi is not None else (acc,)
            for o, r in zip(outs, res):
                o[...] = r.astype(o.dtype)

        if nk == 1:
            finish(p)
        else:
            acc_ref = rest[-1]
            k = pl.program_id(2)

            @pl.when(k == 0)
            def _():
                acc_ref[...] = p

            @pl.when(k > 0)
            def _():
                acc_ref[...] += p

            @pl.when(k == nk - 1)
            def _():
                finish(acc_ref[...])

    outs = pl.pallas_call(
        body,
        name=name,
        grid=(N // tn, M // tm, nk),
        in_specs=[a_spec, b_spec] + ex_specs + [pl.BlockSpec(memory_space=pl.ANY)] * na,
        out_specs=[pl.BlockSpec((tm, tn), lambda j, i, k: (i, j + ob)) for _ in out_dtypes],
        out_shape=[S((M, o_tot), dt) for dt in out_dtypes],
        scratch_shapes=[pltpu.VMEM((tm, tn), f32)] if nk > 1 else [],
        input_output_aliases={2 + ne: 0} if na else {},
        compiler_params=_cparams(("parallel", "parallel", "arbitrary")),
    )(a, barr, *extras, *([o_alias] if na else []))
    return outs[0] if no == 1 else tuple(outs)


def mm_bd(a, b, mode, name, ga, gb, extras=(), epi=None, out_dtypes=(f32,)):
    T = max(1, min(256 // min(ga, gb), 1024 // max(ga, gb)))
    if mode == "tn":
        K, M = a.shape
        N = b.shape[1]
        G = M // ga
        T = min(T, G)
        tm, tn, tk = T * ga, T * gb, _pick_k(K)
        nk = K // tk
        grid = (G // T, 1, nk)
        a_spec = pl.BlockSpec((tk, tm), lambda j, i, k: (k, j))
        b_spec = pl.BlockSpec((tk, tn), lambda j, i, k: (k, j))
        o_spec = pl.BlockSpec((tm, tn), lambda j, i, k: (j, 0))
        out_shape = (M, tn)
    else:
        M = a.shape[0]
        if mode == "nn":
            G = b.shape[0] // ga
            T = min(T, G)
            kw, tn, N = T * ga, T * gb, G * gb
            b_spec = pl.BlockSpec((kw, tn), lambda j, i, k: (j, j))
        else:
            G = b.shape[0] // ga
            T = min(T, G)
            kw, tn, N = T * gb, T * ga, G * ga
            b_spec = pl.BlockSpec((tn, kw), lambda j, i, k: (j, j))
        tm = _pick(M, (1024, 512, 256, 128, 64, 32, 16, 8))
        nk = 1
        grid = (G // T, M // tm, 1)
        a_spec = pl.BlockSpec((tm, kw), lambda j, i, k: (i, j))
        o_spec = pl.BlockSpec((tm, tn), lambda j, i, k: (i, j))
        out_shape = (M, N)
    ex_specs = []
    for e in extras:
        if e.shape == out_shape:
            ex_specs.append(o_spec)
        elif e.shape == (1, out_shape[1]):
            ex_specs.append(pl.BlockSpec((1, tn), lambda j, i, k: (0, j)))
        else:
            raise ValueError((e.shape, out_shape))
    ne, no = len(extras), len(out_dtypes)
    dims = (_DIMS[mode], ((), ()))

    def body(a_ref, b_ref, *rest):
        ex, outs = rest[:ne], rest[ne:ne + no]
        p = lax.dot_general(a_ref[...].astype(bf16), b_ref[...].astype(bf16), dims, preferred_element_type=f32)

        def finish(acc):
            res = epi(acc, *[e[...] for e in ex]) if epi is not None else (acc,)
            for o, r in zip(outs, res):
                o[...] = r.astype(o.dtype)

        if nk == 1:
            finish(p)
        else:
            acc_ref = rest[-1]
            k = pl.program_id(2)

            @pl.when(k == 0)
            def _():
                acc_ref[...] = p

            @pl.when(k > 0)
            def _():
                acc_ref[...] += p

            @pl.when(k == nk - 1)
            def _():
                finish(acc_ref[...])

    outs = pl.pallas_call(
        body, name=name, grid=grid, in_specs=[a_spec, b_spec] + ex_specs, out_specs=[o_spec] * no,
        out_shape=[S(out_shape, dt) for dt in out_dtypes],
        scratch_shapes=[pltpu.VMEM((tm, tn), f32)] if nk > 1 else [],
        compiler_params=_cparams(("parallel", "parallel", "arbitrary")),
    )(a, b, *extras)
    return outs[0] if no == 1 else tuple(outs)


def _diag_blocks(tiles, G, ga, gb):
    T = tiles.shape[1] // gb
    t5 = tiles.reshape(G // T, T, ga, T, gb)
    return jnp.sum(t5 * jnp.eye(T, dtype=tiles.dtype)[None, :, None, :, None], axis=3).reshape(G, ga, gb)


def _add(acc, prev):
    return (acc + prev,)


def rowk(name, fn, L, tl, ncol, ins, outs, accs=()):
    nrow = L // tl
    assert L % tl == 0 and tl % HALO == 0
    hb = tl // HALO

    def cw_of(c_total):
        assert c_total % ncol == 0, (name, c_total, ncol)
        return c_total // ncol

    in_specs = []
    for arr, kind in ins:
        if kind == "rows":
            in_specs.append(pl.BlockSpec((tl, cw_of(arr.shape[1])), lambda j, i: (i, j)))
        elif kind == "prev":
            in_specs.append(pl.BlockSpec((HALO, cw_of(arr.shape[1])), lambda j, i: (jnp.maximum(i * hb - 1, 0), j)))
        elif kind == "next":
            in_specs.append(pl.BlockSpec((HALO, cw_of(arr.shape[1])), lambda j, i: (jnp.minimum((i + 1) * hb, nrow * hb - 1), j)))
        elif kind == "cols":
            in_specs.append(pl.BlockSpec((arr.shape[0], cw_of(arr.shape[1])), lambda j, i: (0, j)))
        elif kind == "const":
            in_specs.append(pl.BlockSpec(arr.shape, lambda j, i: (0,) * arr.ndim))
        else:
            raise ValueError(kind)
    out_specs = [pl.BlockSpec((tl, cw_of(c)), lambda j, i: (i, j)) for c, _ in outs]
    out_shape = [S((L, c), dt) for c, dt in outs]
    out_specs += [pl.BlockSpec((r, cw_of(c)), lambda j, i: (0, j)) for r, c, _ in accs]
    out_shape += [S((r, c), dt) for r, c, dt in accs]
    ni, no, na = len(ins), len(outs), len(accs)

    def body(*refs):
        i = pl.program_id(1)
        res = fn(i, nrow, *[r[...] for r in refs[:ni]])
        for o, r in zip(refs[ni:ni + no], res[:no]):
            o[...] = r.astype(o.dtype)
        for o, r in zip(refs[ni + no:ni + no + na], res[no:]):
            @pl.when(i == 0)
            def _(o=o, r=r):
                o[...] = r.astype(o.dtype)

            @pl.when(i > 0)
            def _(o=o, r=r):
                o[...] += r.astype(o.dtype)

    res = pl.pallas_call(
        body,
        name=name,
        grid=(ncol, nrow),
        in_specs=in_specs,
        out_specs=out_specs,
        out_shape=out_shape,
        compiler_params=_cparams(("parallel", "arbitrary")),
    )(*[a for a, _ in ins])
    return tuple(res)


def _sigmoid(x):
    return 1.0 / (1.0 + jnp.exp(-x))


def _silu(x):
    return x * _sigmoid(x)


def _dsilu(x):
    s = _sigmoid(x)
    return s * (1.0 + x * (1.0 - s))


def _erf(x):
    a = jnp.abs(x)
    t = 1.0 / (1.0 + 0.3275911 * a)
    poly = t * (0.254829592 + t * (-0.284496736 + t * (1.421413741 + t * (-1.453152027 + t * 1.061405429))))
    y = 1.0 - poly * jnp.exp(-a * a)
    return jnp.where(x < 0, -y, y)


def _gelu(x):
    return 0.5 * x * (1.0 + _erf(x * (2.0 ** -0.5)))


def _dgelu(x):
    cdf = 0.5 * (1.0 + _erf(x * (2.0 ** -0.5)))
    pdf = jnp.exp(-0.5 * x * x) * (1.0 / math.sqrt(2.0 * math.pi))
    return cdf + x * pdf


def _rms(x, w):
    return x * lax.rsqrt(jnp.mean(x * x, axis=-1, keepdims=True) + NORM_EPS) * w


def _rms_bwd(x, w, dy):
    d = x.shape[-1]
    r = lax.rsqrt(jnp.mean(x * x, axis=-1, keepdims=True) + NORM_EPS)
    wdy = w * dy
    dx = r * wdy - x * (r * r * r) * (jnp.sum(x * wdy, axis=-1, keepdims=True) / d)
    dw = jnp.sum(x * r * dy, axis=0, keepdims=True)
    return dx, dw


def _from_cols(cols, width):
    tl = cols[0].shape[0]
    lane = lax.broadcasted_iota(jnp.int32, (tl, width), 1)
    out = jnp.zeros((tl, width), f32)
    for n, col in enumerate(cols):
        out = jnp.where(lane == n, col, out)
    return out


def _from_rows(rows):
    c = rows[0].shape[1]
    sub = lax.broadcasted_iota(jnp.int32, (len(rows), c), 0)
    out = jnp.zeros((len(rows), c), f32)
    for n, row in enumerate(rows):
        out = jnp.where(sub == n, row, out)
    return out


def _shift_down(x, halo, s, first):
    if s == 0:
        return x
    tl = x.shape[0]
    halo = jnp.where(first, 0.0, halo)
    xx = jnp.concatenate([halo, x], axis=0)
    return pltpu.roll(xx, s, 0)[HALO:HALO + tl]


def _shift_up(x, halo, s, last):
    if s == 0:
        return x
    tl = x.shape[0]
    halo = jnp.where(last, 0.0, halo)
    xx = jnp.concatenate([x, halo], axis=0)
    return pltpu.roll(xx, tl + HALO - s, 0)[0:tl]


def _causal_conv(x, halo, w, first):
    kw = w.shape[0]
    shifted = [_shift_down(x, halo, kw - 1 - j, first) for j in range(kw)]
    out = shifted[0] * w[0:1]
    for j in range(1, kw):
        out = out + shifted[j] * w[j:j + 1]
    return out, shifted


def rms_fwd(x, w, name):
    L, D = x.shape

    def fn(i, n, xb, wb):
        return (_rms(xb, wb),)

    return rowk(name, fn, L, _pick(L, (256, 128, 64, 32, 16, 8)), 1, [(x, "rows"), (w, "const")], [(D, bf16)])[0]


def rms_bwd(x, w, dh, dres, name):
    L, D = x.shape

    def fn(i, n, xb, wb, dhb, drb):
        dx, dw = _rms_bwd(xb, wb, dhb)
        dx = dx + drb
        return dx, dx, dw

    return rowk(name, fn, L, _pick(L, (256, 128, 64, 32, 16, 8)), 1,
                [(x, "rows"), (w, "const"), (dh, "rows"), (dres, "rows")], [(D, f32), (D, bf16)], [(1, D, f32)])


def loss_head(x, w, target, name):
    L, D = x.shape

    def fn(i, n, xb, wb, tb):
        err = _rms(xb, wb) - tb
        loss = 0.5 * jnp.sum(err * err) / D
        dx, dw = _rms_bwd(xb, wb, err / D)
        return dx, dx, jnp.full((8, 128), loss, f32), dw

    return rowk(name, fn, L, _pick(L, (256, 128, 64, 32, 16, 8)), 1,
                [(x, "rows"), (w, "const"), (target, "rows")], [(D, f32), (D, bf16)], [(8, 128, f32), (1, D, f32)])


def _s5_disc_math(lr, li, logdt, br, bi):
    dt = jnp.exp(logdt)
    mag = jnp.exp(lr * dt)
    ar, ai = mag * jnp.cos(li * dt), mag * jnp.sin(li * dt)
    den = lr * lr + li * li
    nr, ni = ar - 1.0, ai
    cr = (nr * lr + ni * li) / den
    ci = (ni * lr - nr * li) / den
    return ar, ai, cr * br - ci * bi, cr * bi + ci * br


def _disc_call(body, name, ins, out_widths):
    GP = ins[0].shape[0]
    tl = _pick(GP, (512, 256, 128, 64, 32, 16, 8))
    spec = lambda w: pl.BlockSpec((tl, w), lambda i: (i, 0))
    return pl.pallas_call(body, name=name, grid=(GP // tl,),
                          in_specs=[spec(a.shape[1]) for a in ins], out_specs=[spec(w) for w in out_widths],
                          out_shape=[S((GP, w), f32) for w in out_widths], compiler_params=_cparams(("parallel",)))(*ins)


def s5_disc(lr, li, logdt, br, bi, name):
    HG = br.shape[1]

    def body(lr_ref, li_ref, dt_ref, br_ref, bi_ref, ar_ref, ai_ref, bbr_ref, bbi_ref):
        ar, ai, bbr, bbi = _s5_disc_math(lr_ref[...], li_ref[...], dt_ref[...], br_ref[...], bi_ref[...])
        ar_ref[...], ai_ref[...], bbr_ref[...], bbi_ref[...] = ar, ai, bbr, bbi

    return _disc_call(body, name, [lr, li, logdt, br, bi], [1, 1, HG, HG])


def s5_disc_bwd(lr, li, logdt, br, bi, dar, dai, dbbr, dbbi, name):
    HG = br.shape[1]

    def body(lr_ref, li_ref, dt_ref, br_ref, bi_ref, dar_ref, dai_ref, dbbr_ref, dbbi_ref, *outs):
        _, vjp = jax.vjp(_s5_disc_math, lr_ref[...], li_ref[...], dt_ref[...], br_ref[...], bi_ref[...])
        for o, g in zip(outs, vjp((dar_ref[...], dai_ref[...], dbbr_ref[...], dbbi_ref[...]))):
            o[...] = g

    return _disc_call(body, name, [lr, li, logdt, br, bi, dar, dai, dbbr, dbbi], [1, 1, 1, HG, HG])


SCAN_TB = 256


def s5_scan_fwd(bur, bui, ar, ai, name):
    L, GP = bur.shape
    cw = _pick(GP, (1024, 512, 256, 128))
    tb = _pick(L, (SCAN_TB, 128, 64, 32, 16, 8))

    def body(bur_ref, bui_ref, ar_ref, ai_ref, xr_ref, xi_ref, cr_ref, ci_ref):
        @pl.when(pl.program_id(1) == 0)
        def _():
            cr_ref[...] = jnp.zeros_like(cr_ref)
            ci_ref[...] = jnp.zeros_like(ci_ref)

        a_r, a_i = ar_ref[...], ai_ref[...]

        def step(t, carry):
            xr, xi = carry
            row = pl.ds(t, 1)
            nr = a_r * xr - a_i * xi + bur_ref[row, :]
            ni = a_r * xi + a_i * xr + bui_ref[row, :]
            xr_ref[row, :] = nr
            xi_ref[row, :] = ni
            return nr, ni

        xr, xi = lax.fori_loop(0, tb, step, (cr_ref[...], ci_ref[...]), unroll=8)
        cr_ref[...] = xr
        ci_ref[...] = xi

    blk = pl.BlockSpec((tb, cw), lambda j, i: (i, j))
    vec = pl.BlockSpec((1, cw), lambda j, i: (0, j))
    return pl.pallas_call(
        body, name=name, grid=(GP // cw, L // tb),
        in_specs=[blk, blk, vec, vec], out_specs=[blk, blk],
        out_shape=[S((L, GP), f32), S((L, GP), f32)],
        scratch_shapes=[pltpu.VMEM((1, cw), f32), pltpu.VMEM((1, cw), f32)],
        compiler_params=_cparams(("parallel", "arbitrary")),
    )(bur, bui, ar, ai)


def s5_scan_bwd(gr, gi, xr, xi, ar, ai, name):
    L, GP = gr.shape
    cw = _pick(GP, (1024, 512, 256, 128))
    tb = _pick(L, (SCAN_TB, 128, 64, 32, 16, 8))
    nt = L // tb

    def body(gr_ref, gi_ref, xr_ref, xi_ref, ar_ref, ai_ref, gxr_ref, gxi_ref, dar_ref, dai_ref, cr_ref, ci_ref):
        @pl.when(pl.program_id(1) == 0)
        def _():
            cr_ref[...] = jnp.zeros_like(cr_ref)
            ci_ref[...] = jnp.zeros_like(ci_ref)
            dar_ref[...] = jnp.zeros_like(dar_ref)
            dai_ref[...] = jnp.zeros_like(dai_ref)

        a_r, a_i = ar_ref[...], ai_ref[...]

        def step(s, carry):
            cr, ci, dr, di = carry
            row = pl.ds(tb - 1 - s, 1)
            x_r, x_i = xr_ref[row, :], xi_ref[row, :]
            dr = dr + cr * x_r + ci * x_i
            di = di + ci * x_r - cr * x_i
            nr = gr_ref[row, :] + a_r * cr + a_i * ci
            ni = gi_ref[row, :] + a_r * ci - a_i * cr
            gxr_ref[row, :] = nr
            gxi_ref[row, :] = ni
            return nr, ni, dr, di

        cr, ci, dr, di = lax.fori_loop(0, tb, step, (cr_ref[...], ci_ref[...], dar_ref[...], dai_ref[...]), unroll=8)
        cr_ref[...] = cr
        ci_ref[...] = ci
        dar_ref[...] = dr
        dai_ref[...] = di

    blk = pl.BlockSpec((tb, cw), lambda j, i: (nt - 1 - i, j))
    vec = pl.BlockSpec((1, cw), lambda j, i: (0, j))
    return pl.pallas_call(
        body, name=name, grid=(GP // cw, nt),
        in_specs=[blk, blk, blk, blk, vec, vec], out_specs=[blk, blk, vec, vec],
        out_shape=[S((L, GP), f32), S((L, GP), f32), S((1, GP), f32), S((1, GP), f32)],
        scratch_shapes=[pltpu.VMEM((1, cw), f32), pltpu.VMEM((1, cw), f32)],
        compiler_params=_cparams(("parallel", "arbitrary")),
    )(gr, gi, xr, xi, ar, ai)


def _dn_heads_math(cq, ck, cv, braw, araw, alog, dtb, dk):
    q, k, v = _silu(cq), _silu(ck), _silu(cv)
    q = q * lax.rsqrt(jnp.sum(q * q, axis=-1, keepdims=True) + NORM_EPS) * (dk ** -0.5)
    k = k * lax.rsqrt(jnp.sum(k * k, axis=-1, keepdims=True) + NORM_EPS)
    beta = _sigmoid(braw)
    g = -jnp.exp(alog) * jax.nn.softplus(araw + dtb)
    return q, k, v, beta, g


def dn_prep(qkv, convw, ba, alog, dtb, H, name):
    L, W = qkv.shape
    hk = W // 3
    dk = hk // H

    def fn(i, n, xb, hb, wb, bab, alb, dtbb):
        c, _ = _causal_conv(xb, hb, wb, i == 0)
        qs, ks, vs, bs, gs = [], [], [], [], []
        for h in range(H):
            sl = lambda o: c[:, o + h * dk:o + (h + 1) * dk]
            q, k, v, beta, g = _dn_heads_math(sl(0), sl(hk), sl(2 * hk), bab[:, h:h + 1], bab[:, H + h:H + h + 1],
                                              alb[:, h:h + 1], dtbb[:, h:h + 1], dk)
            qs.append(q), ks.append(k), vs.append(v), bs.append(beta), gs.append(g)
        cat = lambda xs: jnp.concatenate(xs, axis=1)
        return cat(qs), cat(ks), cat(vs), _from_cols(bs + gs, 2 * H)

    return rowk(name, fn, L, _pick(L, (128, 64, 32, 16, 8)), 1,
                [(qkv, "rows"), (qkv, "prev"), (convw, "const"), (ba, "rows"), (alog, "const"), (dtb, "const")],
                [(hk, f32), (hk, f32), (hk, f32), (2 * H, f32)])


def dn_prep_bwd(qkv, convw, ba, alog, dtb, dq, dk_, dv, dbg, H, name):
    L, W = qkv.shape
    hk = W // 3
    dk = hk // H
    kw = convw.shape[0]
    nba = ba.shape[1]

    def fn(i, n, xb, hb, wb, bab, alb, dtbb, dqb, dkb, dvb, dbgb):
        c, shifted = _causal_conv(xb, hb, wb, i == 0)
        dcs = [None] * (3 * H)
        dbr, dar, dal, ddt = [], [], [], []
        for h in range(H):
            sl = lambda a, o: a[:, o + h * dk:o + (h + 1) * dk]
            args = (sl(c, 0), sl(c, hk), sl(c, 2 * hk), bab[:, h:h + 1], bab[:, H + h:H + h + 1],
                    alb[:, h:h + 1], dtbb[:, h:h + 1])
            _, vjp = jax.vjp(lambda *a: _dn_heads_math(*a, dk), *args)
            g = vjp((sl(dqb, 0), sl(dkb, 0), sl(dvb, 0), dbgb[:, h:h + 1], dbgb[:, H + h:H + h + 1]))
            dcs[h], dcs[H + h], dcs[2 * H + h] = g[0], g[1], g[2]
            dbr.append(g[3]), dar.append(g[4]), dal.append(g[5]), ddt.append(g[6])
        dc = jnp.concatenate(dcs, axis=1)
        dba = _from_cols(dbr + dar, nba)
        dw = _from_rows([jnp.sum(dc * shifted[j], axis=0, keepdims=True) for j in range(kw)])
        return dc, dba, dw, _from_cols(dal, H), _from_cols(ddt, H)

    return rowk(name, fn, L, _pick(L, (128, 64, 32, 16, 8)), 1,
                [(qkv, "rows"), (qkv, "prev"), (convw, "const"), (ba, "rows"), (alog, "const"), (dtb, "const"),
                 (dq, "rows"), (dk_, "rows"), (dv, "rows"), (dbg, "rows")],
                [(W, f32), (nba, bf16)], [(kw, W, f32), (1, H, f32), (1, H, f32)])


def conv_t(dc, w, name):
    L, C = dc.shape
    kw = w.shape[0]
    ncol = C // _pick(C, (1536, 1408, 1024, 768, 512, 256, 128))

    def fn(i, n, db, hb, wb):
        out = db * wb[kw - 1:kw]
        for j in range(kw - 1):
            out = out + _shift_up(db, hb, kw - 1 - j, i == n - 1) * wb[j:j + 1]
        return (out,)

    return rowk(name, fn, L, _pick(L, (256, 128, 64, 32, 16, 8)), ncol,
                [(dc, "rows"), (dc, "next"), (w, "cols")], [(C, bf16)])[0]


_BDIMS = {"nn": (((2,), (1,)), ((0,), (0,))), "nt": (((2,), (2,)), ((0,), (0,))), "tn": (((1,), (1,)), ((0,), (0,)))}


def _bdot(a, b, mode):
    return lax.dot_general(a.astype(bf16), b.astype(bf16), _BDIMS[mode], preferred_element_type=f32)


def _split16(a):
    hi = a.astype(bf16)
    return hi, (a - hi.astype(f32)).astype(bf16)


def _hdot(a, b, mode):
    ah, al = _split16(a)
    bh, bl = _split16(b)
    d = lambda x, y: lax.dot_general(x, y, _BDIMS[mode], preferred_element_type=f32)
    return d(ah, bh) + (d(ah, bl) + d(al, bh))


def _make_dot(raw):
    @functools.partial(jax.custom_vjp, nondiff_argnums=(2,))
    def dot(a, b, mode):
        return raw(a, b, mode)

    def fwd(a, b, mode):
        return raw(a, b, mode), (a, b)

    def bwd(mode, res, ct):
        a, b = res
        if mode == "nn":
            return raw(ct, b, "nt"), raw(a, ct, "tn")
        if mode == "nt":
            return raw(ct, b, "nn"), raw(ct, a, "tn")
        return raw(b, ct, "nt"), raw(a, ct, "nn")

    dot.defvjp(fwd, bwd)
    return dot


_dot16 = _make_dot(_bdot)
_dot32 = _make_dot(_hdot)


@jax.custom_vjp
def _unit_lower_inv(lmat):
    c = lmat.shape[-1]
    eye = (lax.broadcasted_iota(jnp.int32, (c, c), 0) == lax.broadcasted_iota(jnp.int32, (c, c), 1)).astype(f32)
    p = -lmat
    t = eye + p
    for _ in range(int(math.log2(c)) - 1):
        p = _hdot(p, p, "nn")
        t = t + _hdot(t, p, "nn")
    return t


def _uli_fwd(lmat):
    t = _unit_lower_inv(lmat)
    return t, t


def _uli_bwd(t, dt):
    return (-_hdot(_hdot(t, dt, "tn"), t, "nt"),)


_unit_lower_inv.defvjp(_uli_fwd, _uli_bwd)


def _dn_chunk_math(s_in, q, k, v, gcol, grow, bcol):
    c = q.shape[1]
    ri = lax.broadcasted_iota(jnp.int32, (c, c), 0)
    ci = lax.broadcasted_iota(jnp.int32, (c, c), 1)
    tril = (ri >= ci).astype(f32)
    strict = (ri > ci).astype(f32)
    gc_col = jnp.sum(tril * grow, axis=2, keepdims=True)
    gc_row = jnp.sum((1.0 - strict) * gcol, axis=1, keepdims=True)
    g_last = jnp.sum(gcol, axis=1, keepdims=True)
    decay = jnp.exp((gc_col - gc_row) * tril) * tril
    kb = k * bcol
    vb = v * bcol
    lmat = _dot16(kb, k, "nt") * decay * strict
    t = _unit_lower_inv(lmat)
    u = _dot32(t, vb, "nn")
    w = _dot32(t, kb * jnp.exp(gc_col), "nn")
    attn = _dot16(q, k, "nt") * decay
    v_new = u - _dot16(w, s_in, "nn")
    o = _dot16(q * jnp.exp(gc_col), s_in, "nn") + _dot16(attn, v_new, "nn")
    s_out = s_in * jnp.exp(g_last) + _dot16(k * jnp.exp(g_last - gc_col), v_new, "tn")
    return o, s_out


def _dn_load(q_ref, k_ref, v_ref, bg_ref, bgt_ref, H, dk):
    heads = lambda ref: jnp.stack([ref[:, h * dk:(h + 1) * dk] for h in range(H)])
    bgb, bgtb = bg_ref[...], bgt_ref[0]
    gcol = jnp.stack([bgb[:, H + h:H + h + 1] for h in range(H)])
    bcol = jnp.stack([bgb[:, h:h + 1] for h in range(H)])
    grow = jnp.stack([bgtb[H + h:H + h + 1, :] for h in range(H)])
    return heads(q_ref), heads(k_ref), heads(v_ref), gcol, grow, bcol


def _host(payload, n_in, n_out, n_scratch, nsteps):
    pi = len(payload.arrays) if payload is not None else 0
    po = len(payload.out_shapes) if payload is not None else 0

    def split(refs):
        own = refs[:n_in] + refs[n_in + pi:n_in + pi + n_out] + refs[n_in + pi + n_out + po:n_in + pi + n_out + po + n_scratch]
        theirs = (refs[n_in:n_in + pi], refs[n_in + pi + n_out:n_in + pi + n_out + po], refs[n_in + pi + n_out + po + n_scratch:])
        return own, theirs

    def hooks(theirs):
        if payload is None:
            return
        step = pl.program_id(0)

        @pl.when(step == 0)
        def _():
            payload.start(*theirs)

        @pl.when(step == nsteps - 1)
        def _():
            payload.finish(*theirs)

    extra = dict(in_specs=[_ANY] * pi, out_specs=[_ANY] * po, out_shape=list(payload.out_shapes) if payload else [],
                 scratch=list(payload.sems) if payload else [], arrays=list(payload.arrays) if payload else [],
                 aliases={n_in + a: n_out + b for a, b in payload.aliases.items()} if payload else {})
    return split, hooks, extra


def dn_chunk_fwd(qn, kn, vv, bg, bgt, H, name, payload=None):
    L, hk = qn.shape
    dk = hk // H
    c = DN_CHUNK
    nc = L // c
    split, hooks, extra = _host(payload, 5, 2, 1, nc)

    def body(*refs):
        (q_ref, k_ref, v_ref, bg_ref, bgt_ref, o_ref, ss_ref, s_ref), theirs = split(refs)
        hooks(theirs)

        @pl.when(pl.program_id(0) == 0)
        def _():
            s_ref[...] = jnp.zeros_like(s_ref)

        s_in = s_ref[...]
        ss_ref[0] = s_in
        o, s_out = _dn_chunk_math(s_in, *_dn_load(q_ref, k_ref, v_ref, bg_ref, bgt_ref, H, dk))
        for h in range(H):
            o_ref[:, h * dk:(h + 1) * dk] = o[h]
        s_ref[...] = s_out

    row = lambda w: pl.BlockSpec((c, w), lambda n: (n, 0))
    res = pl.pallas_call(
        body, name=name, grid=(nc,),
        in_specs=[row(hk), row(hk), row(hk), row(2 * H), pl.BlockSpec((1, 2 * H, c), lambda n: (n, 0, 0))] + extra["in_specs"],
        out_specs=[row(hk), pl.BlockSpec((1, H, dk, dk), lambda n: (n, 0, 0, 0))] + extra["out_specs"],
        out_shape=[S((L, hk), f32), S((nc, H, dk, dk), f32)] + extra["out_shape"],
        scratch_shapes=[pltpu.VMEM((H, dk, dk), f32)] + extra["scratch"],
        input_output_aliases=extra["aliases"],
        compiler_params=_cparams(("arbitrary",)),
    )(qn, kn, vv, bg, bgt, *extra["arrays"])
    return res[0], res[1], list(res[2:])


def dn_chunk_bwd(qn, kn, vv, bg, bgt, ss, do, H, name, payload=None):
    L, hk = qn.shape
    dk = hk // H
    c = DN_CHUNK
    nc = L // c
    split, hooks, extra = _host(payload, 7, 5, 1, nc)

    def body(*refs):
        (q_ref, k_ref, v_ref, bg_ref, bgt_ref, ss_ref, do_ref, dq_ref, dk_ref, dv_ref, dbg_ref, dgt_ref, ds_ref), theirs = split(refs)
        hooks(theirs)

        @pl.when(pl.program_id(0) == 0)
        def _():
            ds_ref[...] = jnp.zeros_like(ds_ref)

        args = (ss_ref[0],) + _dn_load(q_ref, k_ref, v_ref, bg_ref, bgt_ref, H, dk)
        _, vjp = jax.vjp(_dn_chunk_math, *args)
        do = jnp.stack([do_ref[:, h * dk:(h + 1) * dk] for h in range(H)])
        ds, dq, dkk, dv, dgcol, dgrow, dbcol = vjp((do, ds_ref[...]))
        ds_ref[...] = ds
        for h in range(H):
            sl = slice(h * dk, (h + 1) * dk)
            dq_ref[:, sl], dk_ref[:, sl], dv_ref[:, sl] = dq[h], dkk[h], dv[h]
        dbg_ref[...] = _from_cols([dbcol[h] for h in range(H)] + [dgcol[h] for h in range(H)], 2 * H)
        dgt_ref[0] = _from_rows([dgrow[h] for h in range(H)])

    row = lambda w: pl.BlockSpec((c, w), lambda n: (nc - 1 - n, 0))
    res = pl.pallas_call(
        body, name=name, grid=(nc,),
        in_specs=[row(hk), row(hk), row(hk), row(2 * H), pl.BlockSpec((1, 2 * H, c), lambda n: (nc - 1 - n, 0, 0)),
                  pl.BlockSpec((1, H, dk, dk), lambda n: (nc - 1 - n, 0, 0, 0)), row(hk)] + extra["in_specs"],
        out_specs=[row(hk), row(hk), row(hk), row(2 * H), pl.BlockSpec((1, H, c), lambda n: (nc - 1 - n, 0, 0))] + extra["out_specs"],
        out_shape=[S((L, hk), f32), S((L, hk), f32), S((L, hk), f32), S((L, 2 * H), f32), S((nc, H, c), f32)] + extra["out_shape"],
        scratch_shapes=[pltpu.VMEM((H, dk, dk), f32)] + extra["scratch"],
        input_output_aliases=extra["aliases"],
        compiler_params=_cparams(("arbitrary",)),
    )(qn, kn, vv, bg, bgt, ss, do, *extra["arrays"])
    return tuple(res[:5]) + (list(res[5:]),)


def _dn_gate_math(o, z, w):
    return _rms(o, w) * _silu(z)


def dn_gate(o, z, w, H, name):
    L, hv = o.shape
    dv = hv // H

    def fn(i, n, ob, zb, wb):
        return (jnp.concatenate([_dn_gate_math(ob[:, h * dv:(h + 1) * dv], zb[:, h * dv:(h + 1) * dv], wb)
                                 for h in range(H)], axis=1),)

    return rowk(name, fn, L, _pick(L, (256, 128, 64, 32, 16, 8)), 1, [(o, "rows"), (z, "rows"), (w, "const")], [(hv, bf16)])[0]


def dn_gate_bwd(o, z, w, dy, H, name):
    L, hv = o.shape
    dv = hv // H

    def fn(i, n, ob, zb, wb, dyb):
        dos, dzs, dw = [], [], 0.0
        for h in range(H):
            sl = slice(h * dv, (h + 1) * dv)
            _, vjp = jax.vjp(_dn_gate_math, ob[:, sl], zb[:, sl], wb)
            a, b, c = vjp(dyb[:, sl])
            dos.append(a), dzs.append(b)
            dw = dw + c
        return jnp.concatenate(dos, axis=1), jnp.concatenate(dzs, axis=1), dw

    return rowk(name, fn, L, _pick(L, (256, 128, 64, 32, 16, 8)), 1,
                [(o, "rows"), (z, "rows"), (w, "const"), (dy, "rows")], [(hv, f32), (hv, bf16)], [(1, dv, f32)])


def ffn_mid(ua, uv, wa, wv, name):
    L, F = ua.shape
    ncol = F // _pick(F, (1408, 1024, 512, 256, 128))

    def fn(i, n, ab, ah, vb, vh, wab, wvb):
        ca, _ = _causal_conv(ab, ah, wab, i == 0)
        cv, _ = _causal_conv(vb, vh, wvb, i == 0)
        return (_silu(ca) * cv,)

    return rowk(name, fn, L, _pick(L, (256, 128, 64, 32, 16, 8)), ncol,
                [(ua, "rows"), (ua, "prev"), (uv, "rows"), (uv, "prev"), (wa, "cols"), (wv, "cols")], [(F, bf16)])[0]


def ffn_mid_bwd(ua, uv, wa, wv, dh, name):
    L, F = ua.shape
    kw = wa.shape[0]
    ncol = F // _pick(F, (1408, 1024, 512, 256, 128))

    def fn(i, n, ab, ah, vb, vh, wab, wvb, dhb):
        ca, sa = _causal_conv(ab, ah, wab, i == 0)
        cv, sv = _causal_conv(vb, vh, wvb, i == 0)
        dca = dhb * cv * _dsilu(ca)
        dcv = dhb * _silu(ca)
        dwa = _from_rows([jnp.sum(dca * sa[j], axis=0, keepdims=True) for j in range(kw)])
        dwv = _from_rows([jnp.sum(dcv * sv[j], axis=0, keepdims=True) for j in range(kw)])
        return dca, dcv, dwa, dwv

    return rowk(name, fn, L, _pick(L, (256, 128, 64, 32, 16, 8)), ncol,
                [(ua, "rows"), (ua, "prev"), (uv, "rows"), (uv, "prev"), (wa, "cols"), (wv, "cols"), (dh, "rows")],
                [(F, f32), (F, f32)], [(kw, F, f32), (kw, F, f32)])


def _merge_epi(acc, gs, gd, ga, gb):
    return acc, _sigmoid(gs) * ga * _sigmoid(gb) + _sigmoid(gd) * acc


def merge_bwd(dm, gs, gd, ga, gb, brdn, name):
    L, D = dm.shape
    ncol = D // _pick(D, (1024, 512, 256, 128))

    def fn(i, n, dmb, gsb, gdb, gab, gbb, brb):
        ss, sd, sb = _sigmoid(gsb), _sigmoid(gdb), _sigmoid(gbb)
        br_s5 = gab * sb
        dbr_s5 = dmb * ss
        return (dmb * br_s5 * ss * (1.0 - ss), dmb * brb * sd * (1.0 - sd), dbr_s5 * sb,
                dbr_s5 * gab * sb * (1.0 - sb), dmb * sd)

    return rowk(name, fn, L, _pick(L, (256, 128, 64, 32, 16, 8)), ncol,
                [(a, "rows") for a in (dm, gs, gd, ga, gb, brdn)], [(D, bf16)] * 5)


def s5_out_bwd(dy, ypre, u, d, name):
    L, W = dy.shape

    def fn(i, n, dyb, yb, ub, db):
        dyp = dyb * _dgelu(yb)
        return dyp, db * dyp, jnp.sum(dyp * ub, axis=0, keepdims=True)

    return rowk(name, fn, L, _pick(L, (256, 128, 64, 32, 16, 8)), 1,
                [(dy, "rows"), (ypre, "rows"), (u, "rows"), (d, "const")], [(W, bf16), (W, f32)], [(1, W, f32)])


def _s5_y_epi(acc, y1, u, d):
    ypre = acc + y1 + d * u
    return ypre, _gelu(ypre)


def adamw(w, g, m, v, name):
    R, C = w.shape
    tl = _pick(R, (256, 128, 64, 32, 16, 8))
    if R * C * 4 <= 2 * 1024 * 1024:
        tl = R

    def body(w_ref, g_ref, m_ref, v_ref, d_ref, nm_ref, nv_ref):
        gg = g_ref[...]
        nm = ADAM_B1 * m_ref[...] + (1.0 - ADAM_B1) * gg
        nv = ADAM_B2 * v_ref[...] + (1.0 - ADAM_B2) * (gg * gg)
        m_hat = nm / (1.0 - ADAM_B1 ** ADAM_STEP)
        v_hat = nv / (1.0 - ADAM_B2 ** ADAM_STEP)
        d_ref[...] = -ADAM_LR * (m_hat / (jnp.sqrt(v_hat) + ADAM_EPS) + ADAM_WD * w_ref[...])
        nm_ref[...] = nm
        nv_ref[...] = nv

    blk = pl.BlockSpec((tl, C), lambda i: (i, 0))
    return pl.pallas_call(body, name=name, grid=(R // tl,), in_specs=[blk] * 4, out_specs=[blk] * 3,
                          out_shape=[S((R, C), f32)] * 3, compiler_params=_cparams(("parallel",)))(w, g, m, v)


def sum_slots(x, name, out_dtype=f32):
    n, R, C = x.shape
    tl = _pick(R, (512, 256, 128, 64, 32, 16, 8))

    def body(x_ref, o_ref):
        acc = x_ref[0].astype(f32)
        for s in range(1, n):
            acc = acc + x_ref[s].astype(f32)
        o_ref[...] = acc.astype(o_ref.dtype)

    return pl.pallas_call(body, name=name, grid=(R // tl,),
                          in_specs=[pl.BlockSpec((n, tl, C), lambda i: (0, i, 0))],
                          out_specs=pl.BlockSpec((tl, C), lambda i: (i, 0)),
                          out_shape=S((R, C), out_dtype), compiler_params=_cparams(("parallel",)))(x)


_ANY = pl.BlockSpec(memory_space=pl.ANY)


def _coords():
    return lax.axis_index("x"), lax.axis_index("y"), lax.axis_index("c")


def chip_exchange(src, name, same=False):
    out_shape = (N_CHIPS,) + src.shape if same else src.shape
    assert out_shape[0] == N_CHIPS

    def body(src_ref, out_ref, send_sems, recv_sems, local_sem):
        x, y, c = _coords()
        me = 2 * x + y
        slot = (lambda j: src_ref) if same else (lambda j: src_ref.at[j])
        mine = pltpu.make_async_copy(slot(me), out_ref.at[me], local_sem)
        mine.start()
        peers = [(1 - x, y), (x, 1 - y), (1 - x, 1 - y)]
        copies = []
        for k, (px, py) in enumerate(peers):
            cp = pltpu.make_async_remote_copy(
                src_ref=slot(2 * px + py), dst_ref=out_ref.at[me],
                send_sem=send_sems.at[k], recv_sem=recv_sems.at[k],
                device_id=(px, py, c), device_id_type=MESH)
            cp.start()
            copies.append(cp)
        for k, (px, py) in enumerate(peers):
            pltpu.make_async_remote_copy(
                src_ref=slot(me), dst_ref=out_ref.at[2 * px + py],
                send_sem=send_sems.at[k], recv_sem=recv_sems.at[k],
                device_id=(px, py, c), device_id_type=MESH).wait_recv()
        for cp in copies:
            cp.wait_send()
        mine.wait()

    return pl.pallas_call(
        body, name=name, in_specs=[_ANY], out_specs=_ANY, out_shape=S(out_shape, src.dtype),
        scratch_shapes=[pltpu.SemaphoreType.DMA((3,)), pltpu.SemaphoreType.DMA((3,)), pltpu.SemaphoreType.DMA],
    )(src)


def sibling_exchange(src, name):
    def body(src_ref, out_ref, send_sem, recv_sem):
        x, y, c = _coords()
        cp = pltpu.make_async_remote_copy(src_ref=src_ref, dst_ref=out_ref, send_sem=send_sem, recv_sem=recv_sem,
                                          device_id=(x, y, 1 - c), device_id_type=MESH)
        cp.start()
        cp.wait()

    return pl.pallas_call(
        body, name=name, in_specs=[_ANY], out_specs=_ANY, out_shape=S(src.shape, src.dtype),
        scratch_shapes=[pltpu.SemaphoreType.DMA, pltpu.SemaphoreType.DMA],
    )(src)


def _rows(c, half_rows):
    return pl.ds(pl.multiple_of(c * half_rows, 16), half_rows)


def _slot(ref, kind, j, rows=None):
    if kind == "slot":
        return ref.at[j] if rows is None else ref.at[j, rows]
    cs = ref.shape[1] // N_CHIPS
    cols = pl.ds(pl.multiple_of(j * cs, 128), cs)
    return ref.at[:, cols] if rows is None else ref.at[rows, cols]


def _sems(n):
    return [pltpu.SemaphoreType.DMA((n,))]


class Payload:
    def __init__(self, arrays, out_shapes, aliases, sems, start, finish):
        self.arrays, self.out_shapes, self.aliases, self.sems = arrays, out_shapes, aliases, sems
        self.start, self.finish = start, finish


def run_payload(p, name):
    ni, no = len(p.arrays), len(p.out_shapes)

    def body(*refs):
        args = (refs[:ni], refs[ni:ni + no], refs[ni + no:])
        p.start(*args)
        p.finish(*args)

    return pl.pallas_call(body, name=name, in_specs=[_ANY] * ni, out_specs=[_ANY] * no, out_shape=p.out_shapes,
                          input_output_aliases=p.aliases, scratch_shapes=p.sems)(*p.arrays)


def _half_rows(bufs, kinds):
    return [(a.shape[1] if k == "slot" else a.shape[0]) // 2 for a, k in zip(bufs, kinds)]


def ag_ici_payload(bufs, kinds):
    n = len(bufs)
    rh = _half_rows(bufs, kinds)

    def copies(outs, sems):
        send, recv = sems
        x, y, c = _coords()
        me = 2 * x + y
        res = []
        for t in range(n):
            own = _slot(outs[t], kinds[t], me, _rows(c, rh[t]))
            for k, (px, py) in enumerate([(1 - x, y), (x, 1 - y), (1 - x, 1 - y)]):
                landed = _slot(outs[t], kinds[t], 2 * px + py, _rows(c, rh[t]))
                sem = dict(send_sem=send.at[3 * t + k], recv_sem=recv.at[3 * t + k], device_id=(px, py, c), device_id_type=MESH)
                res.append((pltpu.make_async_remote_copy(src_ref=own, dst_ref=own, **sem),
                            pltpu.make_async_remote_copy(src_ref=landed, dst_ref=landed, **sem)))
        return res

    def start(ins, outs, sems):
        for mine, _ in copies(outs, sems):
            mine.start()

    def finish(ins, outs, sems):
        both = copies(outs, sems)
        for _, theirs in both:
            theirs.wait_recv()
        for mine, _ in both:
            mine.wait_send()

    return Payload(bufs, [S(a.shape, a.dtype) for a in bufs], {t: t for t in range(n)}, _sems(3 * n) * 2, start, finish)


def ag_d2d(bufs, kinds, name):
    n = len(bufs)
    rh = _half_rows(bufs, kinds)

    def body(*refs):
        outs = refs[n:2 * n]
        send, recv = refs[2 * n:]
        x, y, c = _coords()
        sib = dict(device_id=(x, y, 1 - c), device_id_type=MESH)
        sends = []
        for t in range(n):
            for k, (px, py) in enumerate([(1 - x, y), (x, 1 - y), (1 - x, 1 - y)]):
                landed = _slot(outs[t], kinds[t], 2 * px + py, _rows(c, rh[t]))
                cp = pltpu.make_async_remote_copy(src_ref=landed, dst_ref=landed, send_sem=send.at[3 * t + k],
                                                  recv_sem=recv.at[3 * t + k], **sib)
                cp.start()
                sends.append(cp)
        for t in range(n):
            for k, (px, py) in enumerate([(1 - x, y), (x, 1 - y), (1 - x, 1 - y)]):
                other = _slot(outs[t], kinds[t], 2 * px + py, _rows(1 - c, rh[t]))
                pltpu.make_async_remote_copy(src_ref=other, dst_ref=other, send_sem=send.at[3 * t + k],
                                             recv_sem=recv.at[3 * t + k], **sib).wait_recv()
        for cp in sends:
            cp.wait_send()

    return pl.pallas_call(body, name=name, in_specs=[_ANY] * n, out_specs=[_ANY] * n,
                          out_shape=[S(a.shape, a.dtype) for a in bufs], input_output_aliases={t: t for t in range(n)},
                          scratch_shapes=_sems(3 * n) * 2)(*bufs)


def rs_pair(grads, kinds, name):
    n = len(grads)

    def half_shape(a, k):
        return (a.shape[0], a.shape[1] // 2, a.shape[2]) if k == "slot" else (a.shape[0] // 2, a.shape[1])

    def half(ref, k, c):
        return ref.at[:, _rows(c, ref.shape[1] // 2)] if k == "slot" else ref.at[_rows(c, ref.shape[0] // 2)]

    shapes = [S(half_shape(a, k), a.dtype) for a, k in zip(grads, kinds)]

    def body(*refs):
        srcs, got = refs[:n], refs[n:2 * n]
        send, recv = refs[2 * n:]
        x, y, c = _coords()
        started = []
        for t in range(n):
            cp = pltpu.make_async_remote_copy(src_ref=half(srcs[t], kinds[t], 1 - c), dst_ref=got[t], send_sem=send.at[t],
                                              recv_sem=recv.at[t], device_id=(x, y, 1 - c), device_id_type=MESH)
            cp.start()
            started.append(cp)
        for cp in started:
            cp.wait()

    return pl.pallas_call(body, name=name, in_specs=[_ANY] * n, out_specs=[_ANY] * n, out_shape=shapes,
                          scratch_shapes=_sems(n) * 2)(*grads)


def rs_chips_payload(sums, kinds):
    n = len(sums)

    def part_shape(a, k):
        return a.shape[1:] if k == "slot" else (a.shape[0], a.shape[1] // N_CHIPS)

    shapes = [S((N_CHIPS,) + part_shape(a, k), a.dtype) for a, k in zip(sums, kinds)]

    def copies(srcs, outs, sems):
        send, recv, lsem = sems
        x, y, c = _coords()
        me = 2 * x + y
        local, remote = [], []
        for t in range(n):
            local.append(pltpu.make_async_copy(_slot(srcs[t], kinds[t], me), outs[t].at[me], lsem.at[t]))
            for k, (px, py) in enumerate([(1 - x, y), (x, 1 - y), (1 - x, 1 - y)]):
                landed = outs[t].at[2 * px + py]
                sem = dict(send_sem=send.at[3 * t + k], recv_sem=recv.at[3 * t + k], device_id=(px, py, c), device_id_type=MESH)
                remote.append((pltpu.make_async_remote_copy(src_ref=_slot(srcs[t], kinds[t], 2 * px + py), dst_ref=outs[t].at[me], **sem),
                               pltpu.make_async_remote_copy(src_ref=landed, dst_ref=landed, **sem)))
        return local, remote

    def start(ins, outs, sems):
        local, remote = copies(ins, outs, sems)
        for cp in local:
            cp.start()
        for mine, _ in remote:
            mine.start()

    def finish(ins, outs, sems):
        local, remote = copies(ins, outs, sems)
        for _, theirs in remote:
            theirs.wait_recv()
        for mine, _ in remote:
            mine.wait_send()
        for cp in local:
            cp.wait()

    return Payload(sums, shapes, {}, _sems(3 * n) * 2 + _sems(n), start, finish)


def rs_share(gs, name):
    n = len(gs)

    def body(*refs):
        outs = refs[n:2 * n]
        send, recv = refs[2 * n:]
        x, y, c = _coords()
        started = []
        for t in range(n):
            cp = pltpu.make_async_remote_copy(src_ref=outs[t].at[c], dst_ref=outs[t].at[c], send_sem=send.at[t],
                                              recv_sem=recv.at[t], device_id=(x, y, 1 - c), device_id_type=MESH)
            cp.start()
            started.append(cp)
        for cp in started:
            cp.wait()

    return pl.pallas_call(body, name=name, in_specs=[_ANY] * n, out_specs=[_ANY] * n,
                          out_shape=[S(a.shape, a.dtype) for a in gs], input_output_aliases={t: t for t in range(n)},
                          scratch_shapes=_sems(n) * 2)(*gs)


def _core_index():
    return lax.axis_index("c").astype(jnp.int32).reshape(1)


def add_half(g, got, kind, name):
    g4 = g.reshape((g.shape[0] if kind == "slot" else 1, 2, -1, g.shape[-1]))
    parts, _, rh, C = g4.shape
    got3 = got.reshape(parts, rh, C)
    tl = _pick(rh, (256, 128, 64, 32, 16, 8))

    def body(c_ref, g_ref, got_ref, o_ref):
        o_ref[...] = (g_ref[...] + got_ref[...]).astype(o_ref.dtype)

    blk = pl.BlockSpec((None, tl, C), lambda s, i, c: (s, i, 0))
    out = pl.pallas_call(
        body, name=name,
        grid_spec=pltpu.PrefetchScalarGridSpec(
            num_scalar_prefetch=1, grid=(parts, rh // tl),
            in_specs=[pl.BlockSpec((None, None, tl, C), lambda s, i, c: (s, c[0], i, 0)), blk], out_specs=blk),
        out_shape=S((parts, rh, C), bf16), compiler_params=_cparams(("parallel", "parallel")),
    )(_core_index(), g4, got3)
    return out.reshape(got.shape)


def sum_slots_half(x, name):
    n, rh, C = x.shape
    tl = _pick(rh, (512, 256, 128, 64, 32, 16, 8))

    def body(c_ref, x_ref, o_ref):
        acc = x_ref[0].astype(f32)
        for s in range(1, n):
            acc = acc + x_ref[s].astype(f32)
        o_ref[...] = acc

    return pl.pallas_call(
        body, name=name,
        grid_spec=pltpu.PrefetchScalarGridSpec(
            num_scalar_prefetch=1, grid=(rh // tl,),
            in_specs=[pl.BlockSpec((n, tl, C), lambda i, c: (0, i, 0))],
            out_specs=pl.BlockSpec((None, tl, C), lambda i, c: (c[0], i, 0))),
        out_shape=S((2, rh, C), f32), compiler_params=_cparams(("parallel",)),
    )(_core_index(), x)


def _block_diag(blocks):
    G, a, b = blocks.shape
    eye = jnp.eye(G, dtype=blocks.dtype)
    return (blocks[:, :, None, :] * eye[:, None, :, None]).reshape(G * a, G * b)


def _flat_pack(arrs, dtype, lanes=1024, row_mult=16):
    flat = jnp.concatenate([a.reshape(-1).astype(dtype) for a in arrs])
    n = flat.shape[0]
    per = lanes * row_mult
    pad = (-n) % per
    if pad:
        flat = jnp.concatenate([flat, jnp.zeros((pad,), dtype)])
    return flat.reshape(-1, lanes)


def _flat_unpack(buf, shapes):
    flat = buf.reshape(-1)
    out, off = [], 0
    for s in shapes:
        n = math.prod(s)
        out.append(flat[off:off + n].reshape(s))
        off += n
    return out


def _layer_weights(p, H):
    w_in = p["w_in"]
    D = w_in.shape[0]
    s5w = p["s5_d"].shape[-1]
    hk = p["dn_proj_w"].shape[0]
    off_z = s5w + 4 * hk
    off_a = off_z + 2 * H
    wp = jnp.concatenate([w_in[:, :off_z], w_in[:, off_a:], jnp.pad(w_in[:, off_z:off_a], ((0, 0), (0, 128 - 2 * H)))], axis=1)
    q = {}
    q["Wu"] = Win(wp, 0, s5w)
    q["Wqkv"] = Win(wp, s5w, 3 * hk)
    q["Wz"] = Win(wp, s5w + 3 * hk, hk)
    q["Wgs"] = Win(wp, off_z, D)
    q["Wgd"] = Win(wp, off_z + D, D)
    q["Wba"] = Win(wp, off_z + 2 * D, 128)
    q["Wga"], q["Wgb"] = Win(p["s5_glu_w"], 0, D), Win(p["s5_glu_w"], D, D)
    F = p["ffn_down"].shape[0]
    q["Wupa"], q["Wupv"] = Win(p["ffn_up"], 0, F), Win(p["ffn_up"], F, F)
    q["cwa"], q["cwv"] = p["ffn_conv_w"][:, :F], p["ffn_conv_w"][:, F:]
    for k in ("dn_proj_w", "w_out", "ffn_down", "dn_conv_w", "mix_norm_w", "ffn_norm_w", "dn_norm_w",
              "dn_a_log", "dn_dt_bias", "s5_d"):
        q[k] = p[k]
    return q


def _s5_params(p, tag):
    G, P = p["s5_a_re"].shape
    HG = p["s5_b_re"].shape[-1]
    col = lambda a: a.reshape(G * P, 1)
    lr, li = col(p["s5_a_re"]), col(p["s5_a_im"])
    logdt = col(jnp.broadcast_to(p["s5_log_dt"][:, None], (G, P)))
    br, bi = p["s5_b_re"].reshape(G * P, HG), p["s5_b_im"].reshape(G * P, HG)
    ar, ai, bbr, bbi = s5_disc(lr, li, logdt, br, bi, "s5_disc")
    bd = lambda m: _block_diag(m.reshape(G, P, HG).transpose(0, 2, 1)).astype(bf16)
    cd = lambda m: _block_diag(m.transpose(0, 2, 1)).astype(bf16)
    return dict(lr=lr, li=li, logdt=logdt, br=br, bi=bi, ar=ar.reshape(1, G * P), ai=ai.reshape(1, G * P),
                Bre=bd(bbr), Bim=bd(bbi), CreT=cd(p["s5_c_re"]), mCimT=cd(-p["s5_c_im"]), G=G, P=P, HG=HG)


def layer_fwd(x, q, s5, H, payload=None):
    r = {"x": x}
    h1 = rms_fwd(x, q["mix_norm_w"], "rms_mix")
    r["h1"] = h1
    u32, u16 = mm(h1, q["Wu"], "nn", "proj_u", out_dtypes=(f32, bf16), epi=lambda a: (a, a))
    qkv = mm(h1, q["Wqkv"], "nn", "proj_qkv")
    z = mm(h1, q["Wz"], "nn", "proj_z")
    ba = mm(h1, q["Wba"], "nn", "proj_ba")
    gs = mm(h1, q["Wgs"], "nn", "proj_gs")
    gd = mm(h1, q["Wgd"], "nn", "proj_gd")
    r.update(u32=u32, u16=u16, qkv=qkv, z=z, ba=ba, gs=gs, gd=gd)
    HG, P = s5["HG"], s5["P"]
    bur = mm_bd(u16, s5["Bre"], "nn", "s5_bu_re", HG, P)
    bui = mm_bd(u16, s5["Bim"], "nn", "s5_bu_im", HG, P)
    xr, xi = s5_scan_fwd(bur, bui, s5["ar"], s5["ai"], "s5_scan_fwd")
    y1 = mm_bd(xr, s5["CreT"], "nn", "s5_y_re", P, HG)
    ypre, ys5 = mm_bd(xi, s5["mCimT"], "nn", "s5_y_im", P, HG, extras=(y1, u32, q["s5_d"]), epi=_s5_y_epi, out_dtypes=(f32, bf16))
    ga = mm(ys5, q["Wga"], "nn", "glu_a")
    gb = mm(ys5, q["Wgb"], "nn", "glu_b")
    r.update(xr=xr, xi=xi, ypre=ypre, ys5=ys5, ga=ga, gb=gb)
    qn, kn, vv, bg = dn_prep(qkv, q["dn_conv_w"], ba, q["dn_a_log"], q["dn_dt_bias"], H, "dn_prep")
    L = x.shape[0]
    bgt = bg.reshape(L // DN_CHUNK, DN_CHUNK, 2 * H).transpose(0, 2, 1)
    o, ss, carried = dn_chunk_fwd(qn, kn, vv, bg, bgt, H, "dn_chunk_fwd", payload)
    ydn = dn_gate(o, z, q["dn_norm_w"], H, "dn_gate")
    brdn, merged = mm(ydn, q["dn_proj_w"], "nn", "dn_proj", extras=(gs, gd, ga, gb), epi=_merge_epi, out_dtypes=(f32, bf16))
    r.update(qn=qn, kn=kn, vv=vv, bg=bg, bgt=bgt, ss=ss, o=o, ydn=ydn, brdn=brdn, merged=merged)
    x1 = mm(merged, q["w_out"], "nn", "out_proj", extras=(x,), epi=_add)
    h2 = rms_fwd(x1, q["ffn_norm_w"], "rms_ffn")
    ua = mm(h2, q["Wupa"], "nn", "ffn_up_a")
    uv = mm(h2, q["Wupv"], "nn", "ffn_up_v")
    hmid = ffn_mid(ua, uv, q["cwa"], q["cwv"], "ffn_mid")
    x2 = mm(hmid, q["ffn_down"], "nn", "ffn_down", extras=(x1,), epi=_add)
    r.update(x1=x1, h2=h2, ua=ua, uv=uv, hmid=hmid)
    return x2, r, carried


def layer_bwd(dx2, dx2b, r, q, s5, H, payload=None):
    g = {}
    dhmid = mm(dx2b, q["ffn_down"], "nt", "d_hmid")
    g["ffn_down"] = mm(r["hmid"], dx2b, "tn", "dw_ffn_down")
    dca, dcv, dwa, dwv = ffn_mid_bwd(r["ua"], r["uv"], q["cwa"], q["cwv"], dhmid, "ffn_mid_bwd")
    g["ffn_conv_w"] = jnp.concatenate([dwa, dwv], axis=1)
    dua = conv_t(dca, q["cwa"], "ffn_conv_t_a")
    duv = conv_t(dcv, q["cwv"], "ffn_conv_t_v")
    dh2 = mm(dua, q["Wupa"], "nt", "d_h2_a")
    dh2 = mm(duv, q["Wupv"], "nt", "d_h2_v", extras=(dh2,), epi=_add)
    F = dua.shape[1]
    dwup = mm(r["h2"], dua, "tn", "dw_up_a", out_into=(2 * F, 0, None))
    g["ffn_up"] = mm(r["h2"], duv, "tn", "dw_up_v", out_into=(2 * F, F, dwup))
    dx1, dx1b, dffn_w = rms_bwd(r["x1"], q["ffn_norm_w"], dh2, dx2, "rms_ffn_bwd")
    g["ffn_norm_w"] = dffn_w[0]
    dm = mm(dx1b, q["w_out"], "nt", "d_merged")
    g["w_out"] = mm(r["merged"], dx1b, "tn", "dw_out")
    dgs, dgd, dga, dgb, dbrdn = merge_bwd(dm, r["gs"], r["gd"], r["ga"], r["gb"], r["brdn"], "merge_bwd")
    dydn = mm(dbrdn, q["dn_proj_w"], "nt", "d_ydn")
    g["dn_proj_w"] = mm(r["ydn"], dbrdn, "tn", "dw_dn_proj")
    do, dz, dnw = dn_gate_bwd(r["o"], r["z"], q["dn_norm_w"], dydn, H, "dn_gate_bwd")
    g["dn_norm_w"] = dnw[0]
    dq, dk, dv, dbg, dgt, carried = dn_chunk_bwd(r["qn"], r["kn"], r["vv"], r["bg"], r["bgt"], r["ss"], do, H,
                                                 "dn_chunk_bwd", payload)
    L = dq.shape[0]
    dbg = dbg + jnp.concatenate([jnp.zeros((L, H), f32), dgt.transpose(0, 2, 1).reshape(L, H)], axis=1)
    dc, dba, dcw, dal, ddt = dn_prep_bwd(r["qkv"], q["dn_conv_w"], r["ba"], q["dn_a_log"], q["dn_dt_bias"],
                                         dq, dk, dv, dbg, H, "dn_prep_bwd")
    g["dn_conv_w"], g["dn_a_log"], g["dn_dt_bias"] = dcw, dal[0], ddt[0]
    dqkv = conv_t(dc, q["dn_conv_w"], "dn_conv_t")
    dys5 = mm(dga, q["Wga"], "nt", "d_ys5_a")
    dys5 = mm(dgb, q["Wgb"], "nt", "d_ys5_b", extras=(dys5,), epi=_add)
    D = dga.shape[1]
    dwglu = mm(r["ys5"], dga, "tn", "dw_glu_a", out_into=(2 * D, 0, None))
    g["s5_glu_w"] = mm(r["ys5"], dgb, "tn", "dw_glu_b", out_into=(2 * D, D, dwglu))
    dyp, du_direct, dd = s5_out_bwd(dys5, r["ypre"], r["u32"], q["s5_d"], "s5_out_bwd")
    g["s5_d"] = dd[0]
    G, P, HG = s5["G"], s5["P"], s5["HG"]
    gdr = mm_bd(dyp, s5["CreT"], "nt", "s5_gx_re", P, HG)
    gdi = mm_bd(dyp, s5["mCimT"], "nt", "s5_gx_im", P, HG)
    dcre = _diag_blocks(mm_bd(r["xr"], dyp, "tn", "dw_s5_c_re", P, HG), G, P, HG)
    dcim = _diag_blocks(mm_bd(r["xi"], dyp, "tn", "dw_s5_c_im", P, HG), G, P, HG)
    g["s5_c_re"], g["s5_c_im"] = dcre.transpose(0, 2, 1), -dcim.transpose(0, 2, 1)
    gxr, gxi, dar, dai = s5_scan_bwd(gdr, gdi, r["xr"], r["xi"], s5["ar"], s5["ai"], "s5_scan_bwd")
    dbre = _diag_blocks(mm_bd(r["u16"], gxr, "tn", "dw_s5_b_re", HG, P), G, HG, P)
    dbim = _diag_blocks(mm_bd(r["u16"], gxi, "tn", "dw_s5_b_im", HG, P), G, HG, P)
    tocol = lambda m: m.transpose(0, 2, 1).reshape(G * P, HG)
    dlr, dli, dlogdt, dbr, dbi = s5_disc_bwd(s5["lr"], s5["li"], s5["logdt"], s5["br"], s5["bi"],
                                             dar.reshape(G * P, 1), dai.reshape(G * P, 1), tocol(dbre), tocol(dbim), "s5_disc_bwd")
    g["s5_a_re"], g["s5_a_im"] = dlr.reshape(G, P), dli.reshape(G, P)
    g["s5_log_dt"] = jnp.sum(dlogdt.reshape(G, P), axis=1)
    g["s5_b_re"], g["s5_b_im"] = dbr.reshape(G, P, HG), dbi.reshape(G, P, HG)
    du = mm_bd(gxr, s5["Bre"], "nt", "d_u_re", HG, P)
    du = mm_bd(gxi, s5["Bim"], "nt", "d_u_im", HG, P, extras=(du, du_direct), epi=lambda a, b, c: (a + b + c,), out_dtypes=(bf16,))
    h1 = r["h1"]
    segs = [("Wu", du), ("Wqkv", dqkv), ("Wz", dz), ("Wba", dba), ("Wgs", dgs), ("Wgd", dgd)]
    dh1 = None
    dws = []
    for name, dseg in segs:
        if dh1 is None:
            dh1 = mm(dseg, q[name], "nt", "d_h1_" + name)
        else:
            dh1 = mm(dseg, q[name], "nt", "d_h1_" + name, extras=(dh1,), epi=_add)
        dw = mm(h1, dseg, "tn", "dw_in_" + name)
        dws.append(dw[:, :2 * H] if name == "Wba" else dw)
    g["w_in"] = jnp.concatenate(dws, axis=1)
    dx, dxb, dmix = rms_bwd(r["x"], q["mix_norm_w"], dh1, dx1, "rms_mix_bwd")
    g["mix_norm_w"] = dmix[0]
    return dx, dxb, g, carried


BIG = ("w_in", "s5_glu_w", "dn_proj_w", "w_out", "ffn_up", "ffn_down")
SHARDED_SMALL = ("dn_conv_w", "ffn_conv_w")
COL_SHARDED = ("w_in", "s5_glu_w", "dn_proj_w", "ffn_up", "dn_conv_w", "ffn_conv_w")
REPL = ("mix_norm_w", "s5_log_dt", "s5_a_re", "s5_a_im", "s5_b_re", "s5_b_im", "s5_c_re", "s5_c_im", "s5_d",
        "dn_a_log", "dn_dt_bias", "dn_norm_w", "ffn_norm_w")
WEIGHTS = ['mix_norm_w', 'w_in', 's5_log_dt', 's5_a_re', 's5_a_im', 's5_b_re', 's5_b_im', 's5_c_re', 's5_c_im', 's5_d',
           's5_glu_w', 'dn_conv_w', 'dn_a_log', 'dn_dt_bias', 'dn_norm_w', 'dn_proj_w', 'w_out', 'ffn_norm_w', 'ffn_up',
           'ffn_conv_w', 'ffn_down', 'final_norm_w']


def _join_shards(name, shards):
    return jnp.concatenate(shards, axis=-1 if name in COL_SHARDED else -2)


KINDS = {"w_in": "slot", "s5_glu_w": "col", "dn_proj_w": "col", "w_out": "slot", "ffn_up": "col", "ffn_down": "slot"}


BIG_KINDS = [KINDS[n] for n in BIG]


def gather_begin(shards):
    me = 2 * lax.axis_index("x") + lax.axis_index("y")
    bufs = []
    for n in BIG:
        blk = shards[n].astype(bf16)
        if KINDS[n] == "slot":
            bufs.append(lax.dynamic_update_slice(jnp.zeros((N_CHIPS,) + blk.shape, bf16), blk[None], (me, 0, 0)))
        else:
            cs = blk.shape[1]
            bufs.append(lax.dynamic_update_slice(jnp.zeros((blk.shape[0], N_CHIPS * cs), bf16), blk, (0, me * cs)))
    return ag_ici_payload(bufs, BIG_KINDS)


def gather_finish(bufs):
    full = dict(zip(BIG, ag_d2d(bufs, BIG_KINDS, "ag_d2d")))
    for n in ("w_out", "ffn_down"):
        full[n] = full[n].reshape(-1, full[n].shape[-1])
    w = full["w_in"]
    full["w_in"] = w.transpose(1, 0, 2).reshape(w.shape[1], -1)
    return full


def gather_small_sharded(shards):
    names = SHARDED_SMALL
    shapes = [shards[n].shape for n in names]
    got = chip_exchange(_flat_pack([shards[n] for n in names], f32, lanes=128, row_mult=8), "ag_small", same=True)
    per_chip = [_flat_unpack(got[j], shapes) for j in range(N_CHIPS)]
    return {n: _join_shards(n, [per_chip[j][k] for j in range(N_CHIPS)]) for k, n in enumerate(names)}


def reduce_begin(g):
    tens = []
    for n in BIG:
        a = g[n]
        if n == "w_in":
            a = a.reshape(a.shape[0], N_CHIPS, -1).transpose(1, 0, 2)
        elif KINDS[n] == "slot":
            a = a.reshape(N_CHIPS, -1, a.shape[-1])
        tens.append(a)
    theirs = rs_pair(tens, BIG_KINDS, "rs_pair")
    sums = [add_half(a, t, k, "rs_pair_sum_" + n) for n, a, t, k in zip(BIG, tens, theirs, BIG_KINDS)]
    return rs_chips_payload(sums, BIG_KINDS)


def reduce_finish(parts):
    halves = [sum_slots_half(x, "rs_chip_sum_" + n) for n, x in zip(BIG, parts)]
    return {n: a.reshape(-1, a.shape[-1]) for n, a in zip(BIG, rs_share(halves, "rs_share"))}


def all_reduce_small(arrs):
    shapes = [a.shape for a in arrs]
    pack = _flat_pack(arrs, f32, lanes=1024, row_mult=64)
    from_chips = chip_exchange(pack, "ar_chips", same=True)
    from_sib = sibling_exchange(from_chips, "ar_sibling")
    c = lax.axis_index("c")
    both = jnp.concatenate([jnp.where(c == 0, from_chips, from_sib), jnp.where(c == 0, from_sib, from_chips)], axis=0)
    return _flat_unpack(sum_slots(both, "ar_sum"), shapes)


def kernel(x, mix_norm_w, w_in, s5_log_dt, s5_a_re, s5_a_im, s5_b_re, s5_b_im, s5_c_re, s5_c_im, s5_d, s5_glu_w, dn_conv_w, dn_a_log, dn_dt_bias, dn_norm_w, dn_proj_w, w_out, ffn_norm_w, ffn_up, ffn_conv_w, ffn_down, final_norm_w, loss_target, m_mix_norm_w, m_w_in, m_s5_log_dt, m_s5_a_re, m_s5_a_im, m_s5_b_re, m_s5_b_im, m_s5_c_re, m_s5_c_im, m_s5_d, m_s5_glu_w, m_dn_conv_w, m_dn_a_log, m_dn_dt_bias, m_dn_norm_w, m_dn_proj_w, m_w_out, m_ffn_norm_w, m_ffn_up, m_ffn_conv_w, m_ffn_down, m_final_norm_w, v_mix_norm_w, v_w_in, v_s5_log_dt, v_s5_a_re, v_s5_a_im, v_s5_b_re, v_s5_b_im, v_s5_c_re, v_s5_c_im, v_s5_d, v_s5_glu_w, v_dn_conv_w, v_dn_a_log, v_dn_dt_bias, v_dn_norm_w, v_dn_proj_w, v_w_out, v_ffn_norm_w, v_ffn_up, v_ffn_conv_w, v_ffn_down, v_final_norm_w):
    args = locals()
    W = {n: args[n] for n in WEIGHTS}
    M = {n: args["m_" + n] for n in WEIGHTS}
    V = {n: args["v_" + n] for n in WEIGHTS}
    depth = mix_norm_w.shape[0]
    H = dn_a_log.shape[1]
    xs = x[0]
    target = loss_target[0]

    conv_full = gather_small_sharded({n: W[n] for n in SHARDED_SMALL})

    def layer_params(l, gathered):
        p = gather_finish(gathered)
        for n in REPL:
            p[n] = W[n][l]
        for n in SHARDED_SMALL:
            p[n] = conv_full[n][l]
        for n in ("mix_norm_w", "ffn_norm_w", "dn_norm_w", "dn_a_log", "dn_dt_bias", "s5_d"):
            p[n] = p[n].reshape(1, -1)
        return _layer_weights(p, H), _s5_params(p, l)

    layers, res = [], []
    gathered = run_payload(gather_begin({n: W[n][0] for n in BIG}), "ag_ici")
    for l in range(depth):
        layers.append(layer_params(l, gathered))
        nxt = gather_begin({n: W[n][l + 1] for n in BIG}) if l + 1 < depth else None
        xs, r, gathered = layer_fwd(xs, layers[l][0], layers[l][1], H, nxt)
        res.append(r)
    dx, dxb, loss_part, dfinal = loss_head(xs, final_norm_w.reshape(1, -1), target, "loss_head")
    loss = lax.psum(loss_part[0, 0], ("x", "y", "c"))

    grads, sharded = [None] * depth, [None] * depth
    pending = None
    for l in reversed(range(depth)):
        dx, dxb, grads[l], parts = layer_bwd(dx, dxb, res[l], layers[l][0], layers[l][1], H, pending)
        if pending is not None:
            sharded[l + 1] = reduce_finish(parts)
        pending = reduce_begin(grads[l])
    sharded[0] = reduce_finish(run_payload(pending, "rs_chips"))
    grad_x = dx[None]

    G = {}
    for n in BIG:
        G[n] = jnp.stack([sharded[l][n] for l in range(depth)])
    small_names = REPL + SHARDED_SMALL
    small = [jnp.stack([grads[l][n] for l in range(depth)]) for n in small_names] + [dfinal[0]]
    for n, a in zip(small_names + ("final_norm_w",), all_reduce_small(small)):
        G[n] = a
    chip = 2 * lax.axis_index("x") + lax.axis_index("y")
    for n in SHARDED_SMALL:
        cs = W[n].shape[-1]
        G[n] = lax.dynamic_slice_in_dim(G[n], chip * cs, cs, axis=-1)

    delta, new_m, new_v = {}, {}, {}
    for n in WEIGHTS:
        shape = W[n].shape
        size = math.prod(shape)
        if n in BIG + SHARDED_SMALL:
            two_d = (size // shape[-1], shape[-1])
        else:
            two_d = (size // 128, 128) if size % 128 == 0 else (1, size)
        d, nm, nv = adamw(W[n].reshape(two_d), G[n].reshape(two_d), M[n].reshape(two_d), V[n].reshape(two_d), "adamw_" + n)
        delta[n], new_m[n], new_v[n] = d.reshape(shape), nm.reshape(shape), nv.reshape(shape)
        G[n] = G[n].reshape(shape)
    return (loss, grad_x, *[G[n] for n in WEIGHTS], *[delta[n] for n in WEIGHTS],
            *[new_m[n] for n in WEIGHTS], *[new_v[n] for n in WEIGHTS])
```

```python
import functools
import math

import jax
import jax.numpy as jnp
from jax import lax
from jax.experimental import pallas as pl
from jax.experimental.pallas import tpu as pltpu

f32 = jnp.float32
bf16 = jnp.bfloat16
S = jax.ShapeDtypeStruct

NORM_EPS = 1e-6
DN_CHUNK = 64
S5_GROUP = 16
ADAM_LR, ADAM_B1, ADAM_B2, ADAM_EPS, ADAM_WD, ADAM_STEP = 0.001, 0.9, 0.999, 1e-08, 0.01, 10
VMEM_LIMIT_BYTES = 56 * 1024 * 1024
HALO = 8
MESH = pl.DeviceIdType.MESH
N_CHIPS = 4


def _pick(n, cands):
    for c in cands:
        if n % c == 0:
            return c
    return n


MM_VMEM_BUDGET = 40 * 1024 * 1024
MM_MAX_TK = 2816


def _pick_k(K):
    if K <= MM_MAX_TK or K % 128:
        return K
    return max(d for d in range(128, MM_MAX_TK + 1, 128) if K % d == 0)


MM_MAX_TN = 1536


def _pick_n(N):
    if N <= 1024 or N % 128:
        return N
    return max(d for d in range(128, MM_MAX_TN + 1, 128) if N % d == 0)


def _cparams(sem):
    return pltpu.CompilerParams(dimension_semantics=sem, vmem_limit_bytes=VMEM_LIMIT_BYTES)


_DIMS = {"nn": ((1,), (0,)), "nt": ((1,), (1,)), "tn": ((0,), (0,))}


class Win:
    def __init__(self, arr, c0, nc):
        self.arr, self.c0, self.nc = arr, c0, nc


def mm(a, b, mode, name, extras=(), epi=None, out_dtypes=(f32,), out_into=None):
    barr, c0 = (b.arr, b.c0) if isinstance(b, Win) else (b, 0)
    if mode == "tn":
        K, M = a.shape
    else:
        M, K = a.shape
    if mode == "nt":
        N, K2 = barr.shape
        K2 = b.nc if isinstance(b, Win) else K2
        n_off, k_off = 0, c0
    else:
        K2, N = barr.shape
        N = b.nc if isinstance(b, Win) else N
        n_off, k_off = c0, 0
    assert K == K2, (a.shape, barr.shape, mode)
    o_tot, o_off, o_alias = out_into if out_into is not None else (N, 0, None)
    tm = _pick(M, (1024, 512, 256, 128, 64, 32, 16, 8))
    tn = _pick_n(math.gcd(math.gcd(N, n_off), o_off))
    tk = _pick_k(math.gcd(K, k_off))

    def vmem_estimate(tm_):
        tiles = tm_ * tk * a.dtype.itemsize + tk * tn * barr.dtype.itemsize
        tiles += sum(tm_ * tn * e.dtype.itemsize for e in extras if e.shape == (M, N))
        tiles += sum(tm_ * tn * jnp.dtype(dt).itemsize for dt in out_dtypes)
        return 2 * tiles + 3 * tm_ * tn * 4

    while vmem_estimate(tm) > MM_VMEM_BUDGET and tm % 16 == 0 and tm > 128:
        tm //= 2
    nk = K // tk
    assert M % tm == 0 and N % tn == 0 and K % tk == 0 and n_off % tn == 0 and k_off % tk == 0 and o_off % tn == 0
    nb, kb, ob = n_off // tn, k_off // tk, o_off // tn
    if mode == "tn":
        a_spec = pl.BlockSpec((tk, tm), lambda j, i, k: (k, i))
    else:
        a_spec = pl.BlockSpec((tm, tk), lambda j, i, k: (i, k))
    if mode == "nt":
        b_spec = pl.BlockSpec((tn, tk), lambda j, i, k: (j, k + kb))
    else:
        b_spec = pl.BlockSpec((tk, tn), lambda j, i, k: (k, j + nb))
    ex_specs = []
    for e in extras:
        if e.shape == (M, N):
            ex_specs.append(pl.BlockSpec((tm, tn), lambda j, i, k: (i, j)))
        elif e.shape == (1, N):
            ex_specs.append(pl.BlockSpec((1, tn), lambda j, i, k: (0, j)))
        elif e.shape == (M, 1):
            ex_specs.append(pl.BlockSpec((tm, 1), lambda j, i, k: (i, 0)))
        else:
            raise ValueError((e.shape, M, N))
    ne, no = len(extras), len(out_dtypes)
    na = 1 if o_alias is not None else 0
    assert out_into is None or no == 1
    dims = (_DIMS[mode], ((), ()))

    def body(a_ref, b_ref, *rest):
        ex, outs = rest[:ne], rest[ne + na:ne + na + no]
        p = lax.dot_general(a_ref[...].astype(bf16), b_ref[...].astype(bf16), dims, preferred_element_type=f32)

        def finish(acc):
            res = epi(acc, *[e[...] for e in ex]) if epi is not None else (acc,)
            for o, r in zip(outs, res):
                o[...] = r.astype(o.dtype)

        if nk == 1:
            finish(p)
        else:
            acc_ref = rest[-1]
            k = pl.program_id(2)

            @pl.when(k == 0)
            def _():
                acc_ref[...] = p

            @pl.when(k > 0)
            def _():
                acc_ref[...] += p

            @pl.when(k == nk - 1)
            def _():
                finish(acc_ref[...])

    outs = pl.pallas_call(
        body,
        name=name,
        grid=(N // tn, M // tm, nk),
        in_specs=[a_spec, b_spec] + ex_specs + [pl.BlockSpec(memory_space=pl.ANY)] * na,
        out_specs=[pl.BlockSpec((tm, tn), lambda j, i, k: (i, j + ob)) for _ in out_dtypes],
        out_shape=[S((M, o_tot), dt) for dt in out_dtypes],
        scratch_shapes=[pltpu.VMEM((tm, tn), f32)] if nk > 1 else [],
        input_output_aliases={2 + ne: 0} if na else {},
        compiler_params=_cparams(("parallel", "parallel", "arbitrary")),
    )(a, barr, *extras, *([o_alias] if na else []))
    return outs[0] if no == 1 else tuple(outs)


def mm_bd(a, b, mode, name, ga, gb, extras=(), epi=None, out_dtypes=(f32,)):
    T = max(1, min(256 // min(ga, gb), 1024 // max(ga, gb)))
    if mode == "tn":
        K, M = a.shape
        N = b.shape[1]
        G = M // ga
        T = min(T, G)
        tm, tn, tk = T * ga, T * gb, _pick_k(K)
        nk = K // tk
        grid = (G // T, 1, nk)
        a_spec = pl.BlockSpec((tk, tm), lambda j, i, k: (k, j))
        b_spec = pl.BlockSpec((tk, tn), lambda j, i, k: (k, j))
        o_spec = pl.BlockSpec((tm, tn), lambda j, i, k: (j, 0))
        out_shape = (M, tn)
    else:
        M = a.shape[0]
        if mode == "nn":
            G = b.shape[0] // ga
            T = min(T, G)
            kw, tn, N = T * ga, T * gb, G * gb
            b_spec = pl.BlockSpec((kw, tn), lambda j, i, k: (j, j))
        else:
            G = b.shape[0] // ga
            T = min(T, G)
            kw, tn, N = T * gb, T * ga, G * ga
            b_spec = pl.BlockSpec((tn, kw), lambda j, i, k: (j, j))
        tm = _pick(M, (1024, 512, 256, 128, 64, 32, 16, 8))
        nk = 1
        grid = (G // T, M // tm, 1)
        a_spec = pl.BlockSpec((tm, kw), lambda j, i, k: (i, j))
        o_spec = pl.BlockSpec((tm, tn), lambda j, i, k: (i, j))
        out_shape = (M, N)
    ex_specs = []
    for e in extras:
        if e.shape == out_shape:
            ex_specs.append(o_spec)
        elif e.shape == (1, out_shape[1]):
            ex_specs.append(pl.BlockSpec((1, tn), lambda j, i, k: (0, j)))
        else:
            raise ValueError((e.shape, out_shape))
    ne, no = len(extras), len(out_dtypes)
    dims = (_DIMS[mode], ((), ()))

    def body(a_ref, b_ref, *rest):
        ex, outs = rest[:ne], rest[ne:ne + no]
        p = lax.dot_general(a_ref[...].astype(bf16), b_ref[...].astype(bf16), dims, preferred_element_type=f32)

        def finish(acc):
            res = epi(acc, *[e[...] for e in ex]) if epi is not None else (acc,)
            for o, r in zip(outs, res):
                o[...] = r.astype(o.dtype)

        if nk == 1:
            finish(p)
        else:
            acc_ref = rest[-1]
            k = pl.program_id(2)

            @pl.when(k == 0)
            def _():
                acc_ref[...] = p

            @pl.when(k > 0)
            def _():
                acc_ref[...] += p

            @pl.when(k == nk - 1)
            def _():
                finish(acc_ref[...])

    outs = pl.pallas_call(
        body, name=name, grid=grid, in_specs=[a_spec, b_spec] + ex_specs, out_specs=[o_spec] * no,
        out_shape=[S(out_shape, dt) for dt in out_dtypes],
        scratch_shapes=[pltpu.VMEM((tm, tn), f32)] if nk > 1 else [],
        compiler_params=_cparams(("parallel", "parallel", "arbitrary")),
    )(a, b, *extras)
    return outs[0] if no == 1 else tuple(outs)


def _diag_blocks(tiles, G, ga, gb):
    T = tiles.shape[1] // gb
    t5 = tiles.reshape(G // T, T, ga, T, gb)
    return jnp.sum(t5 * jnp.eye(T, dtype=tiles.dtype)[None, :, None, :, None], axis=3).reshape(G, ga, gb)


def _add(acc, prev):
    return (acc + prev,)


def rowk(name, fn, L, tl, ncol, ins, outs, accs=()):
    nrow = L // tl
    assert L % tl == 0 and tl % HALO == 0
    hb = tl // HALO

    def cw_of(c_total):
        assert c_total % ncol == 0, (name, c_total, ncol)
        return c_total // ncol

    in_specs = []
    for arr, kind in ins:
        if kind == "rows":
            in_specs.append(pl.BlockSpec((tl, cw_of(arr.shape[1])), lambda j, i: (i, j)))
        elif kind == "prev":
            in_specs.append(pl.BlockSpec((HALO, cw_of(arr.shape[1])), lambda j, i: (jnp.maximum(i * hb - 1, 0), j)))
        elif kind == "next":
            in_specs.append(pl.BlockSpec((HALO, cw_of(arr.shape[1])), lambda j, i: (jnp.minimum((i + 1) * hb, nrow * hb - 1), j)))
        elif kind == "cols":
            in_specs.append(pl.BlockSpec((arr.shape[0], cw_of(arr.shape[1])), lambda j, i: (0, j)))
        elif kind == "const":
            in_specs.append(pl.BlockSpec(arr.shape, lambda j, i: (0,) * arr.ndim))
        else:
            raise ValueError(kind)
    out_specs = [pl.BlockSpec((tl, cw_of(c)), lambda j, i: (i, j)) for c, _ in outs]
    out_shape = [S((L, c), dt) for c, dt in outs]
    out_specs += [pl.BlockSpec((r, cw_of(c)), lambda j, i: (0, j)) for r, c, _ in accs]
    out_shape += [S((r, c), dt) for r, c, dt in accs]
    ni, no, na = len(ins), len(outs), len(accs)

    def body(*refs):
        i = pl.program_id(1)
        res = fn(i, nrow, *[r[...] for r in refs[:ni]])
        for o, r in zip(refs[ni:ni + no], res[:no]):
            o[...] = r.astype(o.dtype)
        for o, r in zip(refs[ni + no:ni + no + na], res[no:]):
            @pl.when(i == 0)
            def _(o=o, r=r):
                o[...] = r.astype(o.dtype)

            @pl.when(i > 0)
            def _(o=o, r=r):
                o[...] += r.astype(o.dtype)

    res = pl.pallas_call(
        body,
        name=name,
        grid=(ncol, nrow),
        in_specs=in_specs,
        out_specs=out_specs,
        out_shape=out_shape,
        compiler_params=_cparams(("parallel", "arbitrary")),
    )(*[a for a, _ in ins])
    return tuple(res)


def _sigmoid(x):
    return 1.0 / (1.0 + jnp.exp(-x))


def _silu(x):
    return x * _sigmoid(x)


def _dsilu(x):
    s = _sigmoid(x)
    return s * (1.0 + x * (1.0 - s))


def _erf(x):
    a = jnp.abs(x)
    t = 1.0 / (1.0 + 0.3275911 * a)
    poly = t * (0.254829592 + t * (-0.284496736 + t * (1.421413741 + t * (-1.453152027 + t * 1.061405429))))
    y = 1.0 - poly * jnp.exp(-a * a)
    return jnp.where(x < 0, -y, y)


def _gelu(x):
    return 0.5 * x * (1.0 + _erf(x * (2.0 ** -0.5)))


def _dgelu(x):
    cdf = 0.5 * (1.0 + _erf(x * (2.0 ** -0.5)))
    pdf = jnp.exp(-0.5 * x * x) * (1.0 / math.sqrt(2.0 * math.pi))
    return cdf + x * pdf


def _rms(x, w):
    return x * lax.rsqrt(jnp.mean(x * x, axis=-1, keepdims=True) + NORM_EPS) * w


def _rms_bwd(x, w, dy):
    d = x.shape[-1]
    r = lax.rsqrt(jnp.mean(x * x, axis=-1, keepdims=True) + NORM_EPS)
    wdy = w * dy
    dx = r * wdy - x * (r * r * r) * (jnp.sum(x * wdy, axis=-1, keepdims=True) / d)
    dw = jnp.sum(x * r * dy, axis=0, keepdims=True)
    return dx, dw


def _from_cols(cols, width):
    tl = cols[0].shape[0]
    lane = lax.broadcasted_iota(jnp.int32, (tl, width), 1)
    out = jnp.zeros((tl, width), f32)
    for n, col in enumerate(cols):
        out = jnp.where(lane == n, col, out)
    return out


def _from_rows(rows):
    c = rows[0].shape[1]
    sub = lax.broadcasted_iota(jnp.int32, (len(rows), c), 0)
    out = jnp.zeros((len(rows), c), f32)
    for n, row in enumerate(rows):
        out = jnp.where(sub == n, row, out)
    return out


def _shift_down(x, halo, s, first):
    if s == 0:
        return x
    tl = x.shape[0]
    halo = jnp.where(first, 0.0, halo)
    xx = jnp.concatenate([halo, x], axis=0)
    return pltpu.roll(xx, s, 0)[HALO:HALO + tl]


def _shift_up(x, halo, s, last):
    if s == 0:
        return x
    tl = x.shape[0]
    halo = jnp.where(last, 0.0, halo)
    xx = jnp.concatenate([x, halo], axis=0)
    return pltpu.roll(xx, tl + HALO - s, 0)[0:tl]


def _causal_conv(x, halo, w, first):
    kw = w.shape[0]
    shifted = [_shift_down(x, halo, kw - 1 - j, first) for j in range(kw)]
    out = shifted[0] * w[0:1]
    for j in range(1, kw):
        out = out + shifted[j] * w[j:j + 1]
    return out, shifted


def rms_fwd(x, w, name):
    L, D = x.shape

    def fn(i, n, xb, wb):
        return (_rms(xb, wb),)

    return rowk(name, fn, L, _pick(L, (256, 128, 64, 32, 16, 8)), 1, [(x, "rows"), (w, "const")], [(D, bf16)])[0]


def rms_bwd(x, w, dh, dres, name):
    L, D = x.shape

    def fn(i, n, xb, wb, dhb, drb):
        dx, dw = _rms_bwd(xb, wb, dhb)
        dx = dx + drb
        return dx, dx, dw

    return rowk(name, fn, L, _pick(L, (256, 128, 64, 32, 16, 8)), 1,
                [(x, "rows"), (w, "const"), (dh, "rows"), (dres, "rows")], [(D, f32), (D, bf16)], [(1, D, f32)])


def loss_head(x, w, target, name):
    L, D = x.shape

    def fn(i, n, xb, wb, tb):
        err = _rms(xb, wb) - tb
        loss = 0.5 * jnp.sum(err * err) / D
        dx, dw = _rms_bwd(xb, wb, err / D)
        return dx, dx, jnp.full((8, 128), loss, f32), dw

    return rowk(name, fn, L, _pick(L, (256, 128, 64, 32, 16, 8)), 1,
                [(x, "rows"), (w, "const"), (target, "rows")], [(D, f32), (D, bf16)], [(8, 128, f32), (1, D, f32)])


def _s5_disc_math(lr, li, logdt, br, bi):
    dt = jnp.exp(logdt)
    mag = jnp.exp(lr * dt)
    ar, ai = mag * jnp.cos(li * dt), mag * jnp.sin(li * dt)
    den = lr * lr + li * li
    nr, ni = ar - 1.0, ai
    cr = (nr * lr + ni * li) / den
    ci = (ni * lr - nr * li) / den
    return ar, ai, cr * br - ci * bi, cr * bi + ci * br


def _disc_call(body, name, ins, out_widths):
    GP = ins[0].shape[0]
    tl = _pick(GP, (512, 256, 128, 64, 32, 16, 8))
    spec = lambda w: pl.BlockSpec((tl, w), lambda i: (i, 0))
    return pl.pallas_call(body, name=name, grid=(GP // tl,),
                          in_specs=[spec(a.shape[1]) for a in ins], out_specs=[spec(w) for w in out_widths],
                          out_shape=[S((GP, w), f32) for w in out_widths], compiler_params=_cparams(("parallel",)))(*ins)


def s5_disc(lr, li, logdt, br, bi, name):
    HG = br.shape[1]

    def body(lr_ref, li_ref, dt_ref, br_ref, bi_ref, ar_ref, ai_ref, bbr_ref, bbi_ref):
        ar, ai, bbr, bbi = _s5_disc_math(lr_ref[...], li_ref[...], dt_ref[...], br_ref[...], bi_ref[...])
        ar_ref[...], ai_ref[...], bbr_ref[...], bbi_ref[...] = ar, ai, bbr, bbi

    return _disc_call(body, name, [lr, li, logdt, br, bi], [1, 1, HG, HG])


def s5_disc_bwd(lr, li, logdt, br, bi, dar, dai, dbbr, dbbi, name):
    HG = br.shape[1]

    def body(lr_ref, li_ref, dt_ref, br_ref, bi_ref, dar_ref, dai_ref, dbbr_ref, dbbi_ref, *outs):
        _, vjp = jax.vjp(_s5_disc_math, lr_ref[...], li_ref[...], dt_ref[...], br_ref[...], bi_ref[...])
        for o, g in zip(outs, vjp((dar_ref[...], dai_ref[...], dbbr_ref[...], dbbi_ref[...]))):
            o[...] = g

    return _disc_call(body, name, [lr, li, logdt, br, bi, dar, dai, dbbr, dbbi], [1, 1, 1, HG, HG])


SCAN_TB = 256


def s5_scan_fwd(bur, bui, ar, ai, name):
    L, GP = bur.shape
    cw = _pick(GP, (1024, 512, 256, 128))
    tb = _pick(L, (SCAN_TB, 128, 64, 32, 16, 8))

    def body(bur_ref, bui_ref, ar_ref, ai_ref, xr_ref, xi_ref, cr_ref, ci_ref):
        @pl.when(pl.program_id(1) == 0)
        def _():
            cr_ref[...] = jnp.zeros_like(cr_ref)
            ci_ref[...] = jnp.zeros_like(ci_ref)

        a_r, a_i = ar_ref[...], ai_ref[...]

        def step(t, carry):
            xr, xi = carry
            row = pl.ds(t, 1)
            nr = a_r * xr - a_i * xi + bur_ref[row, :]
            ni = a_r * xi + a_i * xr + bui_ref[row, :]
            xr_ref[row, :] = nr
            xi_ref[row, :] = ni
            return nr, ni

        xr, xi = lax.fori_loop(0, tb, step, (cr_ref[...], ci_ref[...]), unroll=8)
        cr_ref[...] = xr
        ci_ref[...] = xi

    blk = pl.BlockSpec((tb, cw), lambda j, i: (i, j))
    vec = pl.BlockSpec((1, cw), lambda j, i: (0, j))
    return pl.pallas_call(
        body, name=name, grid=(GP // cw, L // tb),
        in_specs=[blk, blk, vec, vec], out_specs=[blk, blk],
        out_shape=[S((L, GP), f32), S((L, GP), f32)],
        scratch_shapes=[pltpu.VMEM((1, cw), f32), pltpu.VMEM((1, cw), f32)],
        compiler_params=_cparams(("parallel", "arbitrary")),
    )(bur, bui, ar, ai)


def s5_scan_bwd(gr, gi, xr, xi, ar, ai, name):
    L, GP = gr.shape
    cw = _pick(GP, (1024, 512, 256, 128))
    tb = _pick(L, (SCAN_TB, 128, 64, 32, 16, 8))
    nt = L // tb

    def body(gr_ref, gi_ref, xr_ref, xi_ref, ar_ref, ai_ref, gxr_ref, gxi_ref, dar_ref, dai_ref, cr_ref, ci_ref):
        @pl.when(pl.program_id(1) == 0)
        def _():
            cr_ref[...] = jnp.zeros_like(cr_ref)
            ci_ref[...] = jnp.zeros_like(ci_ref)
            dar_ref[...] = jnp.zeros_like(dar_ref)
            dai_ref[...] = jnp.zeros_like(dai_ref)

        a_r, a_i = ar_ref[...], ai_ref[...]

        def step(s, carry):
            cr, ci, dr, di = carry
            row = pl.ds(tb - 1 - s, 1)
            x_r, x_i = xr_ref[row, :], xi_ref[row, :]
            dr = dr + cr * x_r + ci * x_i
            di = di + ci * x_r - cr * x_i
            nr = gr_ref[row, :] + a_r * cr + a_i * ci
            ni = gi_ref[row, :] + a_r * ci - a_i * cr
            gxr_ref[row, :] = nr
            gxi_ref[row, :] = ni
            return nr, ni, dr, di

        cr, ci, dr, di = lax.fori_loop(0, tb, step, (cr_ref[...], ci_ref[...], dar_ref[...], dai_ref[...]), unroll=8)
        cr_ref[...] = cr
        ci_ref[...] = ci
        dar_ref[...] = dr
        dai_ref[...] = di

    blk = pl.BlockSpec((tb, cw), lambda j, i: (nt - 1 - i, j))
    vec = pl.BlockSpec((1, cw), lambda j, i: (0, j))
    return pl.pallas_call(
        body, name=name, grid=(GP // cw, nt),
        in_specs=[blk, blk, blk, blk, vec, vec], out_specs=[blk, blk, vec, vec],
        out_shape=[S((L, GP), f32), S((L, GP), f32), S((1, GP), f32), S((1, GP), f32)],
        scratch_shapes=[pltpu.VMEM((1, cw), f32), pltpu.VMEM((1, cw), f32)],
        compiler_params=_cparams(("parallel", "arbitrary")),
    )(gr, gi, xr, xi, ar, ai)


def _dn_heads_math(cq, ck, cv, braw, araw, alog, dtb, dk):
    q, k, v = _silu(cq), _silu(ck), _silu(cv)
    q = q * lax.rsqrt(jnp.sum(q * q, axis=-1, keepdims=True) + NORM_EPS) * (dk ** -0.5)
    k = k * lax.rsqrt(jnp.sum(k * k, axis=-1, keepdims=True) + NORM_EPS)
    beta = _sigmoid(braw)
    g = -jnp.exp(alog) * jax.nn.softplus(araw + dtb)
    return q, k, v, beta, g


def dn_prep(qkv, convw, ba, alog, dtb, H, name):
    L, W = qkv.shape
    hk = W // 3
    dk = hk // H

    def fn(i, n, xb, hb, wb, bab, alb, dtbb):
        c, _ = _causal_conv(xb, hb, wb, i == 0)
        qs, ks, vs, bs, gs = [], [], [], [], []
        for h in range(H):
            sl = lambda o: c[:, o + h * dk:o + (h + 1) * dk]
            q, k, v, beta, g = _dn_heads_math(sl(0), sl(hk), sl(2 * hk), bab[:, h:h + 1], bab[:, H + h:H + h + 1],
                                              alb[:, h:h + 1], dtbb[:, h:h + 1], dk)
            qs.append(q), ks.append(k), vs.append(v), bs.append(beta), gs.append(g)
        cat = lambda xs: jnp.concatenate(xs, axis=1)
        return cat(qs), cat(ks), cat(vs), _from_cols(bs + gs, 2 * H)

    return rowk(name, fn, L, _pick(L, (128, 64, 32, 16, 8)), 1,
                [(qkv, "rows"), (qkv, "prev"), (convw, "const"), (ba, "rows"), (alog, "const"), (dtb, "const")],
                [(hk, f32), (hk, f32), (hk, f32), (2 * H, f32)])


def dn_prep_bwd(qkv, convw, ba, alog, dtb, dq, dk_, dv, dbg, H, name):
    L, W = qkv.shape
    hk = W // 3
    dk = hk // H
    kw = convw.shape[0]
    nba = ba.shape[1]

    def fn(i, n, xb, hb, wb, bab, alb, dtbb, dqb, dkb, dvb, dbgb):
        c, shifted = _causal_conv(xb, hb, wb, i == 0)
        dcs = [None] * (3 * H)
        dbr, dar, dal, ddt = [], [], [], []
        for h in range(H):
            sl = lambda a, o: a[:, o + h * dk:o + (h + 1) * dk]
            args = (sl(c, 0), sl(c, hk), sl(c, 2 * hk), bab[:, h:h + 1], bab[:, H + h:H + h + 1],
                    alb[:, h:h + 1], dtbb[:, h:h + 1])
            _, vjp = jax.vjp(lambda *a: _dn_heads_math(*a, dk), *args)
            g = vjp((sl(dqb, 0), sl(dkb, 0), sl(dvb, 0), dbgb[:, h:h + 1], dbgb[:, H + h:H + h + 1]))
            dcs[h], dcs[H + h], dcs[2 * H + h] = g[0], g[1], g[2]
            dbr.append(g[3]), dar.append(g[4]), dal.append(g[5]), ddt.append(g[6])
        dc = jnp.concatenate(dcs, axis=1)
        dba = _from_cols(dbr + dar, nba)
        dw = _from_rows([jnp.sum(dc * shifted[j], axis=0, keepdims=True) for j in range(kw)])
        return dc, dba, dw, _from_cols(dal, H), _from_cols(ddt, H)

    return rowk(name, fn, L, _pick(L, (128, 64, 32, 16, 8)), 1,
                [(qkv, "rows"), (qkv, "prev"), (convw, "const"), (ba, "rows"), (alog, "const"), (dtb, "const"),
                 (dq, "rows"), (dk_, "rows"), (dv, "rows"), (dbg, "rows")],
                [(W, f32), (nba, bf16)], [(kw, W, f32), (1, H, f32), (1, H, f32)])


def conv_t(dc, w, name):
    L, C = dc.shape
    kw = w.shape[0]
    ncol = C // _pick(C, (1536, 1408, 1024, 768, 512, 256, 128))

    def fn(i, n, db, hb, wb):
        out = db * wb[kw - 1:kw]
        for j in range(kw - 1):
            out = out + _shift_up(db, hb, kw - 1 - j, i == n - 1) * wb[j:j + 1]
        return (out,)

    return rowk(name, fn, L, _pick(L, (256, 128, 64, 32, 16, 8)), ncol,
                [(dc, "rows"), (dc, "next"), (w, "cols")], [(C, bf16)])[0]


_BDIMS = {"nn": (((2,), (1,)), ((0,), (0,))), "nt": (((2,), (2,)), ((0,), (0,))), "tn": (((1,), (1,)), ((0,), (0,)))}


def _bdot(a, b, mode):
    return lax.dot_general(a.astype(bf16), b.astype(bf16), _BDIMS[mode], preferred_element_type=f32)


def _split16(a):
    hi = a.astype(bf16)
    return hi, (a - hi.astype(f32)).astype(bf16)


def _hdot(a, b, mode):
    ah, al = _split16(a)
    bh, bl = _split16(b)
    d = lambda x, y: lax.dot_general(x, y, _BDIMS[mode], preferred_element_type=f32)
    return d(ah, bh) + (d(ah, bl) + d(al, bh))


def _make_dot(raw):
    @functools.partial(jax.custom_vjp, nondiff_argnums=(2,))
    def dot(a, b, mode):
        return raw(a, b, mode)

    def fwd(a, b, mode):
        return raw(a, b, mode), (a, b)

    def bwd(mode, res, ct):
        a, b = res
        if mode == "nn":
            return raw(ct, b, "nt"), raw(a, ct, "tn")
        if mode == "nt":
            return raw(ct, b, "nn"), raw(ct, a, "tn")
        return raw(b, ct, "nt"), raw(a, ct, "nn")

    dot.defvjp(fwd, bwd)
    return dot


_dot16 = _make_dot(_bdot)
_dot32 = _make_dot(_hdot)


@jax.custom_vjp
def _unit_lower_inv(lmat):
    c = lmat.shape[-1]
    eye = (lax.broadcasted_iota(jnp.int32, (c, c), 0) == lax.broadcasted_iota(jnp.int32, (c, c), 1)).astype(f32)
    p = -lmat
    t = eye + p
    for _ in range(int(math.log2(c)) - 1):
        p = _hdot(p, p, "nn")
        t = t + _hdot(t, p, "nn")
    return t


def _uli_fwd(lmat):
    t = _unit_lower_inv(lmat)
    return t, t


def _uli_bwd(t, dt):
    return (-_hdot(_hdot(t, dt, "tn"), t, "nt"),)


_unit_lower_inv.defvjp(_uli_fwd, _uli_bwd)


def _dn_chunk_math(s_in, q, k, v, gcol, grow, bcol):
    c = q.shape[1]
    ri = lax.broadcasted_iota(jnp.int32, (c, c), 0)
    ci = lax.broadcasted_iota(jnp.int32, (c, c), 1)
    tril = (ri >= ci).astype(f32)
    strict = (ri > ci).astype(f32)
    gc_col = jnp.sum(tril * grow, axis=2, keepdims=True)
    gc_row = jnp.sum((1.0 - strict) * gcol, axis=1, keepdims=True)
    g_last = jnp.sum(gcol, axis=1, keepdims=True)
    decay = jnp.exp((gc_col - gc_row) * tril) * tril
    kb = k * bcol
    vb = v * bcol
    lmat = _dot16(kb, k, "nt") * decay * strict
    t = _unit_lower_inv(lmat)
    u = _dot32(t, vb, "nn")
    w = _dot32(t, kb * jnp.exp(gc_col), "nn")
    attn = _dot16(q, k, "nt") * decay
    v_new = u - _dot16(w, s_in, "nn")
    o = _dot16(q * jnp.exp(gc_col), s_in, "nn") + _dot16(attn, v_new, "nn")
    s_out = s_in * jnp.exp(g_last) + _dot16(k * jnp.exp(g_last - gc_col), v_new, "tn")
    return o, s_out


def _dn_load(q_ref, k_ref, v_ref, bg_ref, bgt_ref, H, dk):
    heads = lambda ref: jnp.stack([ref[:, h * dk:(h + 1) * dk] for h in range(H)])
    bgb, bgtb = bg_ref[...], bgt_ref[0]
    gcol = jnp.stack([bgb[:, H + h:H + h + 1] for h in range(H)])
    bcol = jnp.stack([bgb[:, h:h + 1] for h in range(H)])
    grow = jnp.stack([bgtb[H + h:H + h + 1, :] for h in range(H)])
    return heads(q_ref), heads(k_ref), heads(v_ref), gcol, grow, bcol


def _host(payload, n_in, n_out, n_scratch, nsteps):
    pi = len(payload.arrays) if payload is not None else 0
    po = len(payload.out_shapes) if payload is not None else 0

    def split(refs):
        own = refs[:n_in] + refs[n_in + pi:n_in + pi + n_out] + refs[n_in + pi + n_out + po:n_in + pi + n_out + po + n_scratch]
        theirs = (refs[n_in:n_in + pi], refs[n_in + pi + n_out:n_in + pi + n_out + po], refs[n_in + pi + n_out + po + n_scratch:])
        return own, theirs

    def hooks(theirs):
        if payload is None:
            return
        step = pl.program_id(0)

        @pl.when(step == 0)
        def _():
            payload.start(*theirs)

        @pl.when(step == nsteps - 1)
        def _():
            payload.finish(*theirs)

    extra = dict(in_specs=[_ANY] * pi, out_specs=[_ANY] * po, out_shape=list(payload.out_shapes) if payload else [],
                 scratch=list(payload.sems) if payload else [], arrays=list(payload.arrays) if payload else [],
                 aliases={n_in + a: n_out + b for a, b in payload.aliases.items()} if payload else {})
    return split, hooks, extra


def dn_chunk_fwd(qn, kn, vv, bg, bgt, H, name, payload=None):
    L, hk = qn.shape
    dk = hk // H
    c = DN_CHUNK
    nc = L // c
    split, hooks, extra = _host(payload, 5, 2, 1, nc)

    def body(*refs):
        (q_ref, k_ref, v_ref, bg_ref, bgt_ref, o_ref, ss_ref, s_ref), theirs = split(refs)
        hooks(theirs)

        @pl.when(pl.program_id(0) == 0)
        def _():
            s_ref[...] = jnp.zeros_like(s_ref)

        s_in = s_ref[...]
        ss_ref[0] = s_in
        o, s_out = _dn_chunk_math(s_in, *_dn_load(q_ref, k_ref, v_ref, bg_ref, bgt_ref, H, dk))
        for h in range(H):
            o_ref[:, h * dk:(h + 1) * dk] = o[h]
        s_ref[...] = s_out

    row = lambda w: pl.BlockSpec((c, w), lambda n: (n, 0))
    res = pl.pallas_call(
        body, name=name, grid=(nc,),
        in_specs=[row(hk), row(hk), row(hk), row(2 * H), pl.BlockSpec((1, 2 * H, c), lambda n: (n, 0, 0))] + extra["in_specs"],
        out_specs=[row(hk), pl.BlockSpec((1, H, dk, dk), lambda n: (n, 0, 0, 0))] + extra["out_specs"],
        out_shape=[S((L, hk), f32), S((nc, H, dk, dk), f32)] + extra["out_shape"],
        scratch_shapes=[pltpu.VMEM((H, dk, dk), f32)] + extra["scratch"],
        input_output_aliases=extra["aliases"],
        compiler_params=_cparams(("arbitrary",)),
    )(qn, kn, vv, bg, bgt, *extra["arrays"])
    return res[0], res[1], list(res[2:])


def dn_chunk_bwd(qn, kn, vv, bg, bgt, ss, do, H, name, payload=None):
    L, hk = qn.shape
    dk = hk // H
    c = DN_CHUNK
    nc = L // c
    split, hooks, extra = _host(payload, 7, 5, 1, nc)

    def body(*refs):
        (q_ref, k_ref, v_ref, bg_ref, bgt_ref, ss_ref, do_ref, dq_ref, dk_ref, dv_ref, dbg_ref, dgt_ref, ds_ref), theirs = split(refs)
        hooks(theirs)

        @pl.when(pl.program_id(0) == 0)
        def _():
            ds_ref[...] = jnp.zeros_like(ds_ref)

        args = (ss_ref[0],) + _dn_load(q_ref, k_ref, v_ref, bg_ref, bgt_ref, H, dk)
        _, vjp = jax.vjp(_dn_chunk_math, *args)
        do = jnp.stack([do_ref[:, h * dk:(h + 1) * dk] for h in range(H)])
        ds, dq, dkk, dv, dgcol, dgrow, dbcol = vjp((do, ds_ref[...]))
        ds_ref[...] = ds
        for h in range(H):
            sl = slice(h * dk, (h + 1) * dk)
            dq_ref[:, sl], dk_ref[:, sl], dv_ref[:, sl] = dq[h], dkk[h], dv[h]
        dbg_ref[...] = _from_cols([dbcol[h] for h in range(H)] + [dgcol[h] for h in range(H)], 2 * H)
        dgt_ref[0] = _from_rows([dgrow[h] for h in range(H)])

    row = lambda w: pl.BlockSpec((c, w), lambda n: (nc - 1 - n, 0))
    res = pl.pallas_call(
        body, name=name, grid=(nc,),
        in_specs=[row(hk), row(hk), row(hk), row(2 * H), pl.BlockSpec((1, 2 * H, c), lambda n: (nc - 1 - n, 0, 0)),
                  pl.BlockSpec((1, H, dk, dk), lambda n: (nc - 1 - n, 0, 0, 0)), row(hk)] + extra["in_specs"],
        out_specs=[row(hk), row(hk), row(hk), row(2 * H), pl.BlockSpec((1, H, c), lambda n: (nc - 1 - n, 0, 0))] + extra["out_specs"],
        out_shape=[S((L, hk), f32), S((L, hk), f32), S((L, hk), f32), S((L, 2 * H), f32), S((nc, H, c), f32)] + extra["out_shape"],
        scratch_shapes=[pltpu.VMEM((H, dk, dk), f32)] + extra["scratch"],
        input_output_aliases=extra["aliases"],
        compiler_params=_cparams(("arbitrary",)),
    )(qn, kn, vv, bg, bgt, ss, do, *extra["arrays"])
    return tuple(res[:5]) + (list(res[5:]),)


def _dn_gate_math(o, z, w):
    return _rms(o, w) * _silu(z)


def dn_gate(o, z, w, H, name):
    L, hv = o.shape
    dv = hv // H

    def fn(i, n, ob, zb, wb):
        return (jnp.concatenate([_dn_gate_math(ob[:, h * dv:(h + 1) * dv], zb[:, h * dv:(h + 1) * dv], wb)
                                 for h in range(H)], axis=1),)

    return rowk(name, fn, L, _pick(L, (256, 128, 64, 32, 16, 8)), 1, [(o, "rows"), (z, "rows"), (w, "const")], [(hv, bf16)])[0]


def dn_gate_bwd(o, z, w, dy, H, name):
    L, hv = o.shape
    dv = hv // H

    def fn(i, n, ob, zb, wb, dyb):
        dos, dzs, dw = [], [], 0.0
        for h in range(H):
            sl = slice(h * dv, (h + 1) * dv)
            _, vjp = jax.vjp(_dn_gate_math, ob[:, sl], zb[:, sl], wb)
            a, b, c = vjp(dyb[:, sl])
            dos.append(a), dzs.append(b)
            dw = dw + c
        return jnp.concatenate(dos, axis=1), jnp.concatenate(dzs, axis=1), dw

    return rowk(name, fn, L, _pick(L, (256, 128, 64, 32, 16, 8)), 1,
                [(o, "rows"), (z, "rows"), (w, "const"), (dy, "rows")], [(hv, f32), (hv, bf16)], [(1, dv, f32)])


def ffn_mid(ua, uv, wa, wv, name):
    L, F = ua.shape
    ncol = F // _pick(F, (1408, 1024, 512, 256, 128))

    def fn(i, n, ab, ah, vb, vh, wab, wvb):
        ca, _ = _causal_conv(ab, ah, wab, i == 0)
        cv, _ = _causal_conv(vb, vh, wvb, i == 0)
        return (_silu(ca) * cv,)

    return rowk(name, fn, L, _pick(L, (256, 128, 64, 32, 16, 8)), ncol,
                [(ua, "rows"), (ua, "prev"), (uv, "rows"), (uv, "prev"), (wa, "cols"), (wv, "cols")], [(F, bf16)])[0]


def ffn_mid_bwd(ua, uv, wa, wv, dh, name):
    L, F = ua.shape
    kw = wa.shape[0]
    ncol = F // _pick(F, (1408, 1024, 512, 256, 128))

    def fn(i, n, ab, ap, an, vb, vp, vn, wab, wvb, dhb, dhn):
        tl = ab.shape[0]
        last = i == n - 1
        ext = lambda blk, nxt: jnp.concatenate([blk, jnp.where(last, 0.0, nxt)], axis=0)
        dhe = ext(dhb, dhn)
        ca, sa = _causal_conv(ext(ab, an), ap, wab, i == 0)
        cv, sv = _causal_conv(ext(vb, vn), vp, wvb, i == 0)
        dca = dhe * cv * _dsilu(ca)
        dcv = dhe * _silu(ca)
        dwa = _from_rows([jnp.sum(dca[:tl] * sa[j][:tl], axis=0, keepdims=True) for j in range(kw)])
        dwv = _from_rows([jnp.sum(dcv[:tl] * sv[j][:tl], axis=0, keepdims=True) for j in range(kw)])

        def conv_t_rows(dc, w):
            out = dc[:tl] * w[kw - 1:kw]
            for j in range(kw - 1):
                out = out + pltpu.roll(dc, tl + HALO - (kw - 1 - j), 0)[:tl] * w[j:j + 1]
            return out

        return conv_t_rows(dca, wab), conv_t_rows(dcv, wvb), dwa, dwv

    return rowk(name, fn, L, _pick(L, (256, 128, 64, 32, 16, 8)), ncol,
                [(ua, "rows"), (ua, "prev"), (ua, "next"), (uv, "rows"), (uv, "prev"), (uv, "next"),
                 (wa, "cols"), (wv, "cols"), (dh, "rows"), (dh, "next")],
                [(F, bf16), (F, bf16)], [(kw, F, f32), (kw, F, f32)])


def _merge_epi(acc, gs, gd, ga, gb):
    return acc, _sigmoid(gs) * ga * _sigmoid(gb) + _sigmoid(gd) * acc


def merge_bwd(dm, gs, gd, ga, gb, brdn, name):
    L, D = dm.shape
    ncol = D // _pick(D, (1024, 512, 256, 128))

    def fn(i, n, dmb, gsb, gdb, gab, gbb, brb):
        ss, sd, sb = _sigmoid(gsb), _sigmoid(gdb), _sigmoid(gbb)
        br_s5 = gab * sb
        dbr_s5 = dmb * ss
        return (dmb * br_s5 * ss * (1.0 - ss), dmb * brb * sd * (1.0 - sd), dbr_s5 * sb,
                dbr_s5 * gab * sb * (1.0 - sb), dmb * sd)

    return rowk(name, fn, L, _pick(L, (256, 128, 64, 32, 16, 8)), ncol,
                [(a, "rows") for a in (dm, gs, gd, ga, gb, brdn)], [(D, bf16)] * 5)


def s5_out_bwd(dy, ypre, u, d, name):
    L, W = dy.shape

    def fn(i, n, dyb, yb, ub, db):
        dyp = dyb * _dgelu(yb)
        return dyp, db * dyp, jnp.sum(dyp * ub, axis=0, keepdims=True)

    return rowk(name, fn, L, _pick(L, (256, 128, 64, 32, 16, 8)), 1,
                [(dy, "rows"), (ypre, "rows"), (u, "rows"), (d, "const")], [(W, bf16), (W, f32)], [(1, W, f32)])


def _s5_y_epi(acc, y1, u, d):
    ypre = acc + y1 + d * u
    return ypre, _gelu(ypre)


def adamw(w, g, m, v, name):
    R, C = w.shape
    tl = _pick(R, (256, 128, 64, 32, 16, 8))
    if R * C * 4 <= 2 * 1024 * 1024:
        tl = R

    def body(w_ref, g_ref, m_ref, v_ref, d_ref, nm_ref, nv_ref):
        gg = g_ref[...]
        nm = ADAM_B1 * m_ref[...] + (1.0 - ADAM_B1) * gg
        nv = ADAM_B2 * v_ref[...] + (1.0 - ADAM_B2) * (gg * gg)
        m_hat = nm / (1.0 - ADAM_B1 ** ADAM_STEP)
        v_hat = nv / (1.0 - ADAM_B2 ** ADAM_STEP)
        d_ref[...] = -ADAM_LR * (m_hat / (jnp.sqrt(v_hat) + ADAM_EPS) + ADAM_WD * w_ref[...])
        nm_ref[...] = nm
        nv_ref[...] = nv

    blk = pl.BlockSpec((tl, C), lambda i: (i, 0))
    return pl.pallas_call(body, name=name, grid=(R // tl,), in_specs=[blk] * 4, out_specs=[blk] * 3,
                          out_shape=[S((R, C), f32)] * 3, compiler_params=_cparams(("parallel",)))(w, g, m, v)


def sum_slots(x, name, out_dtype=f32):
    n, R, C = x.shape
    tl = _pick(R, (512, 256, 128, 64, 32, 16, 8))

    def body(x_ref, o_ref):
        acc = x_ref[0].astype(f32)
        for s in range(1, n):
            acc = acc + x_ref[s].astype(f32)
        o_ref[...] = acc.astype(o_ref.dtype)

    return pl.pallas_call(body, name=name, grid=(R // tl,),
                          in_specs=[pl.BlockSpec((n, tl, C), lambda i: (0, i, 0))],
                          out_specs=pl.BlockSpec((tl, C), lambda i: (i, 0)),
                          out_shape=S((R, C), out_dtype), compiler_params=_cparams(("parallel",)))(x)


_ANY = pl.BlockSpec(memory_space=pl.ANY)


def _coords():
    return lax.axis_index("x"), lax.axis_index("y"), lax.axis_index("c")


def chip_exchange(src, name, same=False):
    out_shape = (N_CHIPS,) + src.shape if same else src.shape
    assert out_shape[0] == N_CHIPS

    def body(src_ref, out_ref, send_sems, recv_sems, local_sem):
        x, y, c = _coords()
        me = 2 * x + y
        slot = (lambda j: src_ref) if same else (lambda j: src_ref.at[j])
        mine = pltpu.make_async_copy(slot(me), out_ref.at[me], local_sem)
        mine.start()
        peers = [(1 - x, y), (x, 1 - y), (1 - x, 1 - y)]
        copies = []
        for k, (px, py) in enumerate(peers):
            cp = pltpu.make_async_remote_copy(
                src_ref=slot(2 * px + py), dst_ref=out_ref.at[me],
                send_sem=send_sems.at[k], recv_sem=recv_sems.at[k],
                device_id=(px, py, c), device_id_type=MESH)
            cp.start()
            copies.append(cp)
        for k, (px, py) in enumerate(peers):
            pltpu.make_async_remote_copy(
                src_ref=slot(me), dst_ref=out_ref.at[2 * px + py],
                send_sem=send_sems.at[k], recv_sem=recv_sems.at[k],
                device_id=(px, py, c), device_id_type=MESH).wait_recv()
        for cp in copies:
            cp.wait_send()
        mine.wait()

    return pl.pallas_call(
        body, name=name, in_specs=[_ANY], out_specs=_ANY, out_shape=S(out_shape, src.dtype),
        scratch_shapes=[pltpu.SemaphoreType.DMA((3,)), pltpu.SemaphoreType.DMA((3,)), pltpu.SemaphoreType.DMA],
    )(src)


def sibling_exchange(src, name):
    def body(src_ref, out_ref, send_sem, recv_sem):
        x, y, c = _coords()
        cp = pltpu.make_async_remote_copy(src_ref=src_ref, dst_ref=out_ref, send_sem=send_sem, recv_sem=recv_sem,
                                          device_id=(x, y, 1 - c), device_id_type=MESH)
        cp.start()
        cp.wait()

    return pl.pallas_call(
        body, name=name, in_specs=[_ANY], out_specs=_ANY, out_shape=S(src.shape, src.dtype),
        scratch_shapes=[pltpu.SemaphoreType.DMA, pltpu.SemaphoreType.DMA],
    )(src)


def _rows(c, half_rows):
    return pl.ds(pl.multiple_of(c * half_rows, 16), half_rows)


def _slot(ref, kind, j, rows=None):
    if kind == "slot":
        return ref.at[j] if rows is None else ref.at[j, rows]
    cs = ref.shape[1] // N_CHIPS
    cols = pl.ds(pl.multiple_of(j * cs, 128), cs)
    return ref.at[:, cols] if rows is None else ref.at[rows, cols]


def _sems(n):
    return [pltpu.SemaphoreType.DMA((n,))]


class Payload:
    def __init__(self, arrays, out_shapes, aliases, sems, start, finish):
        self.arrays, self.out_shapes, self.aliases, self.sems = arrays, out_shapes, aliases, sems
        self.start, self.finish = start, finish


def run_payload(p, name):
    ni, no = len(p.arrays), len(p.out_shapes)

    def body(*refs):
        args = (refs[:ni], refs[ni:ni + no], refs[ni + no:])
        p.start(*args)
        p.finish(*args)

    return pl.pallas_call(body, name=name, in_specs=[_ANY] * ni, out_specs=[_ANY] * no, out_shape=p.out_shapes,
                          input_output_aliases=p.aliases, scratch_shapes=p.sems)(*p.arrays)


def _half_rows(bufs, kinds):
    return [(a.shape[1] if k == "slot" else a.shape[0]) // 2 for a, k in zip(bufs, kinds)]


def ag_ici_payload(bufs, kinds):
    n = len(bufs)
    rh = _half_rows(bufs, kinds)

    def copies(outs, sems):
        send, recv = sems
        x, y, c = _coords()
        me = 2 * x + y
        res = []
        for t in range(n):
            own = _slot(outs[t], kinds[t], me, _rows(c, rh[t]))
            for k, (px, py) in enumerate([(1 - x, y), (x, 1 - y), (1 - x, 1 - y)]):
                landed = _slot(outs[t], kinds[t], 2 * px + py, _rows(c, rh[t]))
                sem = dict(send_sem=send.at[3 * t + k], recv_sem=recv.at[3 * t + k], device_id=(px, py, c), device_id_type=MESH)
                res.append((pltpu.make_async_remote_copy(src_ref=own, dst_ref=own, **sem),
                            pltpu.make_async_remote_copy(src_ref=landed, dst_ref=landed, **sem)))
        return res

    def start(ins, outs, sems):
        for mine, _ in copies(outs, sems):
            mine.start()

    def finish(ins, outs, sems):
        both = copies(outs, sems)
        for _, theirs in both:
            theirs.wait_recv()
        for mine, _ in both:
            mine.wait_send()

    return Payload(bufs, [S(a.shape, a.dtype) for a in bufs], {t: t for t in range(n)}, _sems(3 * n) * 2, start, finish)


def ag_d2d(bufs, kinds, name):
    n = len(bufs)
    rh = _half_rows(bufs, kinds)

    def body(*refs):
        outs = refs[n:2 * n]
        send, recv = refs[2 * n:]
        x, y, c = _coords()
        sib = dict(device_id=(x, y, 1 - c), device_id_type=MESH)
        sends = []
        for t in range(n):
            for k, (px, py) in enumerate([(1 - x, y), (x, 1 - y), (1 - x, 1 - y)]):
                landed = _slot(outs[t], kinds[t], 2 * px + py, _rows(c, rh[t]))
                cp = pltpu.make_async_remote_copy(src_ref=landed, dst_ref=landed, send_sem=send.at[3 * t + k],
                                                  recv_sem=recv.at[3 * t + k], **sib)
                cp.start()
                sends.append(cp)
        for t in range(n):
            for k, (px, py) in enumerate([(1 - x, y), (x, 1 - y), (1 - x, 1 - y)]):
                other = _slot(outs[t], kinds[t], 2 * px + py, _rows(1 - c, rh[t]))
                pltpu.make_async_remote_copy(src_ref=other, dst_ref=other, send_sem=send.at[3 * t + k],
                                             recv_sem=recv.at[3 * t + k], **sib).wait_recv()
        for cp in sends:
            cp.wait_send()

    return pl.pallas_call(body, name=name, in_specs=[_ANY] * n, out_specs=[_ANY] * n,
                          out_shape=[S(a.shape, a.dtype) for a in bufs], input_output_aliases={t: t for t in range(n)},
                          scratch_shapes=_sems(3 * n) * 2)(*bufs)


def rs_pair(grads, kinds, name):
    n = len(grads)

    def half_shape(a, k):
        return (a.shape[0], a.shape[1] // 2, a.shape[2]) if k == "slot" else (a.shape[0] // 2, a.shape[1])

    def half(ref, k, c):
        return ref.at[:, _rows(c, ref.shape[1] // 2)] if k == "slot" else ref.at[_rows(c, ref.shape[0] // 2)]

    shapes = [S(half_shape(a, k), a.dtype) for a, k in zip(grads, kinds)]

    def body(*refs):
        srcs, got = refs[:n], refs[n:2 * n]
        send, recv = refs[2 * n:]
        x, y, c = _coords()
        started = []
        for t in range(n):
            cp = pltpu.make_async_remote_copy(src_ref=half(srcs[t], kinds[t], 1 - c), dst_ref=got[t], send_sem=send.at[t],
                                              recv_sem=recv.at[t], device_id=(x, y, 1 - c), device_id_type=MESH)
            cp.start()
            started.append(cp)
        for cp in started:
            cp.wait()

    return pl.pallas_call(body, name=name, in_specs=[_ANY] * n, out_specs=[_ANY] * n, out_shape=shapes,
                          scratch_shapes=_sems(n) * 2)(*grads)


def rs_chips_payload(sums, kinds):
    n = len(sums)

    def part_shape(a, k):
        return a.shape[1:] if k == "slot" else (a.shape[0], a.shape[1] // N_CHIPS)

    shapes = [S((N_CHIPS,) + part_shape(a, k), a.dtype) for a, k in zip(sums, kinds)]

    def copies(srcs, outs, sems):
        send, recv, lsem = sems
        x, y, c = _coords()
        me = 2 * x + y
        local, remote = [], []
        for t in range(n):
            local.append(pltpu.make_async_copy(_slot(srcs[t], kinds[t], me), outs[t].at[me], lsem.at[t]))
            for k, (px, py) in enumerate([(1 - x, y), (x, 1 - y), (1 - x, 1 - y)]):
                landed = outs[t].at[2 * px + py]
                sem = dict(send_sem=send.at[3 * t + k], recv_sem=recv.at[3 * t + k], device_id=(px, py, c), device_id_type=MESH)
                remote.append((pltpu.make_async_remote_copy(src_ref=_slot(srcs[t], kinds[t], 2 * px + py), dst_ref=outs[t].at[me], **sem),
                               pltpu.make_async_remote_copy(src_ref=landed, dst_ref=landed, **sem)))
        return local, remote

    def start(ins, outs, sems):
        local, remote = copies(ins, outs, sems)
        for cp in local:
            cp.start()
        for mine, _ in remote:
            mine.start()

    def finish(ins, outs, sems):
        local, remote = copies(ins, outs, sems)
        for _, theirs in remote:
            theirs.wait_recv()
        for mine, _ in remote:
            mine.wait_send()
        for cp in local:
            cp.wait()

    return Payload(sums, shapes, {}, _sems(3 * n) * 2 + _sems(n), start, finish)


def rs_share(gs, name):
    n = len(gs)

    def body(*refs):
        outs = refs[n:2 * n]
        send, recv = refs[2 * n:]
        x, y, c = _coords()
        started = []
        for t in range(n):
            cp = pltpu.make_async_remote_copy(src_ref=outs[t].at[c], dst_ref=outs[t].at[c], send_sem=send.at[t],
                                              recv_sem=recv.at[t], device_id=(x, y, 1 - c), device_id_type=MESH)
            cp.start()
            started.append(cp)
        for cp in started:
            cp.wait()

    return pl.pallas_call(body, name=name, in_specs=[_ANY] * n, out_specs=[_ANY] * n,
                          out_shape=[S(a.shape, a.dtype) for a in gs], input_output_aliases={t: t for t in range(n)},
                          scratch_shapes=_sems(n) * 2)(*gs)


def _core_index():
    return lax.axis_index("c").astype(jnp.int32).reshape(1)


def place_block(blk, kind, name):
    R, cs = blk.shape
    tl = _pick(R, (256, 128, 64, 32, 16, 8))

    def body(me_ref, b_ref, o_ref):
        o_ref[...] = b_ref[...].astype(o_ref.dtype)

    if kind == "slot":
        out_spec, out_shape = pl.BlockSpec((None, tl, cs), lambda i, me: (me[0], i, 0)), (N_CHIPS, R, cs)
    else:
        out_spec, out_shape = pl.BlockSpec((tl, cs), lambda i, me: (i, me[0])), (R, N_CHIPS * cs)
    me = (2 * lax.axis_index("x") + lax.axis_index("y")).astype(jnp.int32).reshape(1)
    return pl.pallas_call(
        body, name=name,
        grid_spec=pltpu.PrefetchScalarGridSpec(num_scalar_prefetch=1, grid=(R // tl,),
                                               in_specs=[pl.BlockSpec((tl, cs), lambda i, me: (i, 0))], out_specs=out_spec),
        out_shape=S(out_shape, bf16), compiler_params=_cparams(("parallel",)),
    )(me, blk)


def add_half(g, got, kind, name):
    g4 = g.reshape((g.shape[0] if kind == "slot" else 1, 2, -1, g.shape[-1]))
    parts, _, rh, C = g4.shape
    got3 = got.reshape(parts, rh, C)
    tl = _pick(rh, (256, 128, 64, 32, 16, 8))

    def body(c_ref, g_ref, got_ref, o_ref):
        o_ref[...] = (g_ref[...] + got_ref[...]).astype(o_ref.dtype)

    blk = pl.BlockSpec((None, tl, C), lambda s, i, c: (s, i, 0))
    out = pl.pallas_call(
        body, name=name,
        grid_spec=pltpu.PrefetchScalarGridSpec(
            num_scalar_prefetch=1, grid=(parts, rh // tl),
            in_specs=[pl.BlockSpec((None, None, tl, C), lambda s, i, c: (s, c[0], i, 0)), blk], out_specs=blk),
        out_shape=S((parts, rh, C), bf16), compiler_params=_cparams(("parallel", "parallel")),
    )(_core_index(), g4, got3)
    return out.reshape(got.shape)


def sum_slots_half(x, name):
    n, rh, C = x.shape
    tl = _pick(rh, (512, 256, 128, 64, 32, 16, 8))

    def body(c_ref, x_ref, o_ref):
        acc = x_ref[0].astype(f32)
        for s in range(1, n):
            acc = acc + x_ref[s].astype(f32)
        o_ref[...] = acc

    return pl.pallas_call(
        body, name=name,
        grid_spec=pltpu.PrefetchScalarGridSpec(
            num_scalar_prefetch=1, grid=(rh // tl,),
            in_specs=[pl.BlockSpec((n, tl, C), lambda i, c: (0, i, 0))],
            out_specs=pl.BlockSpec((None, tl, C), lambda i, c: (c[0], i, 0))),
        out_shape=S((2, rh, C), f32), compiler_params=_cparams(("parallel",)),
    )(_core_index(), x)


def _block_diag(blocks):
    G, a, b = blocks.shape
    eye = jnp.eye(G, dtype=blocks.dtype)
    return (blocks[:, :, None, :] * eye[:, None, :, None]).reshape(G * a, G * b)


def _flat_pack(arrs, dtype, lanes=1024, row_mult=16):
    flat = jnp.concatenate([a.reshape(-1).astype(dtype) for a in arrs])
    n = flat.shape[0]
    per = lanes * row_mult
    pad = (-n) % per
    if pad:
        flat = jnp.concatenate([flat, jnp.zeros((pad,), dtype)])
    return flat.reshape(-1, lanes)


def _flat_unpack(buf, shapes):
    flat = buf.reshape(-1)
    out, off = [], 0
    for s in shapes:
        n = math.prod(s)
        out.append(flat[off:off + n].reshape(s))
        off += n
    return out


def _layer_weights(p, H):
    w_in = p["w_in"]
    D = w_in.shape[0]
    s5w = p["s5_d"].shape[-1]
    hk = p["dn_proj_w"].shape[0]
    off_z = s5w + 4 * hk
    off_a = off_z + 2 * H
    wp = jnp.concatenate([w_in[:, :off_z], w_in[:, off_a:], jnp.pad(w_in[:, off_z:off_a], ((0, 0), (0, 128 - 2 * H)))], axis=1)
    q = {}
    q["Wu"] = Win(wp, 0, s5w)
    q["Wqkv"] = Win(wp, s5w, 3 * hk)
    q["Wz"] = Win(wp, s5w + 3 * hk, hk)
    q["Wgs"] = Win(wp, off_z, D)
    q["Wgd"] = Win(wp, off_z + D, D)
    q["Wba"] = Win(wp, off_z + 2 * D, 128)
    q["Wga"], q["Wgb"] = Win(p["s5_glu_w"], 0, D), Win(p["s5_glu_w"], D, D)
    F = p["ffn_down"].shape[0]
    q["Wupa"], q["Wupv"] = Win(p["ffn_up"], 0, F), Win(p["ffn_up"], F, F)
    q["cwa"], q["cwv"] = p["ffn_conv_w"][:, :F], p["ffn_conv_w"][:, F:]
    for k in ("dn_proj_w", "w_out", "ffn_down", "dn_conv_w", "mix_norm_w", "ffn_norm_w", "dn_norm_w",
              "dn_a_log", "dn_dt_bias", "s5_d"):
        q[k] = p[k]
    return q


def _s5_params(p, tag):
    G, P = p["s5_a_re"].shape
    HG = p["s5_b_re"].shape[-1]
    col = lambda a: a.reshape(G * P, 1)
    lr, li = col(p["s5_a_re"]), col(p["s5_a_im"])
    logdt = col(jnp.broadcast_to(p["s5_log_dt"][:, None], (G, P)))
    br, bi = p["s5_b_re"].reshape(G * P, HG), p["s5_b_im"].reshape(G * P, HG)
    ar, ai, bbr, bbi = s5_disc(lr, li, logdt, br, bi, "s5_disc")
    bd = lambda m: _block_diag(m.reshape(G, P, HG).transpose(0, 2, 1)).astype(bf16)
    cd = lambda m: _block_diag(m.transpose(0, 2, 1)).astype(bf16)
    return dict(lr=lr, li=li, logdt=logdt, br=br, bi=bi, ar=ar.reshape(1, G * P), ai=ai.reshape(1, G * P),
                Bre=bd(bbr), Bim=bd(bbi), CreT=cd(p["s5_c_re"]), mCimT=cd(-p["s5_c_im"]), G=G, P=P, HG=HG)


def layer_fwd(x, q, s5, H, payload=None):
    r = {"x": x}
    h1 = rms_fwd(x, q["mix_norm_w"], "rms_mix")
    r["h1"] = h1
    u32, u16 = mm(h1, q["Wu"], "nn", "proj_u", out_dtypes=(f32, bf16), epi=lambda a: (a, a))
    qkv = mm(h1, q["Wqkv"], "nn", "proj_qkv")
    z = mm(h1, q["Wz"], "nn", "proj_z")
    ba = mm(h1, q["Wba"], "nn", "proj_ba")
    gs = mm(h1, q["Wgs"], "nn", "proj_gs")
    gd = mm(h1, q["Wgd"], "nn", "proj_gd")
    r.update(u32=u32, u16=u16, qkv=qkv, z=z, ba=ba, gs=gs, gd=gd)
    HG, P = s5["HG"], s5["P"]
    bur = mm_bd(u16, s5["Bre"], "nn", "s5_bu_re", HG, P)
    bui = mm_bd(u16, s5["Bim"], "nn", "s5_bu_im", HG, P)
    xr, xi = s5_scan_fwd(bur, bui, s5["ar"], s5["ai"], "s5_scan_fwd")
    y1 = mm_bd(xr, s5["CreT"], "nn", "s5_y_re", P, HG)
    ypre, ys5 = mm_bd(xi, s5["mCimT"], "nn", "s5_y_im", P, HG, extras=(y1, u32, q["s5_d"]), epi=_s5_y_epi, out_dtypes=(f32, bf16))
    ga = mm(ys5, q["Wga"], "nn", "glu_a")
    gb = mm(ys5, q["Wgb"], "nn", "glu_b")
    r.update(xr=xr, xi=xi, ypre=ypre, ys5=ys5, ga=ga, gb=gb)
    qn, kn, vv, bg = dn_prep(qkv, q["dn_conv_w"], ba, q["dn_a_log"], q["dn_dt_bias"], H, "dn_prep")
    L = x.shape[0]
    bgt = bg.reshape(L // DN_CHUNK, DN_CHUNK, 2 * H).transpose(0, 2, 1)
    o, ss, carried = dn_chunk_fwd(qn, kn, vv, bg, bgt, H, "dn_chunk_fwd", payload)
    ydn = dn_gate(o, z, q["dn_norm_w"], H, "dn_gate")
    brdn, merged = mm(ydn, q["dn_proj_w"], "nn", "dn_proj", extras=(gs, gd, ga, gb), epi=_merge_epi, out_dtypes=(f32, bf16))
    r.update(qn=qn, kn=kn, vv=vv, bg=bg, bgt=bgt, ss=ss, o=o, ydn=ydn, brdn=brdn, merged=merged)
    x1 = mm(merged, q["w_out"], "nn", "out_proj", extras=(x,), epi=_add)
    h2 = rms_fwd(x1, q["ffn_norm_w"], "rms_ffn")
    ua = mm(h2, q["Wupa"], "nn", "ffn_up_a")
    uv = mm(h2, q["Wupv"], "nn", "ffn_up_v")
    hmid = ffn_mid(ua, uv, q["cwa"], q["cwv"], "ffn_mid")
    x2 = mm(hmid, q["ffn_down"], "nn", "ffn_down", extras=(x1,), epi=_add)
    r.update(x1=x1, h2=h2, ua=ua, uv=uv, hmid=hmid)
    return x2, r, carried


def layer_bwd(dx2, dx2b, r, q, s5, H, payload=None):
    g = {}
    dhmid = mm(dx2b, q["ffn_down"], "nt", "d_hmid")
    g["ffn_down"] = mm(r["hmid"], dx2b, "tn", "dw_ffn_down")
    dua, duv, dwa, dwv = ffn_mid_bwd(r["ua"], r["uv"], q["cwa"], q["cwv"], dhmid, "ffn_mid_bwd")
    g["ffn_conv_w"] = jnp.concatenate([dwa, dwv], axis=1)
    dh2 = mm(dua, q["Wupa"], "nt", "d_h2_a")
    dh2 = mm(duv, q["Wupv"], "nt", "d_h2_v", extras=(dh2,), epi=_add)
    F = dua.shape[1]
    dwup = mm(r["h2"], dua, "tn", "dw_up_a", out_into=(2 * F, 0, None))
    g["ffn_up"] = mm(r["h2"], duv, "tn", "dw_up_v", out_into=(2 * F, F, dwup))
    dx1, dx1b, dffn_w = rms_bwd(r["x1"], q["ffn_norm_w"], dh2, dx2, "rms_ffn_bwd")
    g["ffn_norm_w"] = dffn_w[0]
    dm = mm(dx1b, q["w_out"], "nt", "d_merged")
    g["w_out"] = mm(r["merged"], dx1b, "tn", "dw_out")
    dgs, dgd, dga, dgb, dbrdn = merge_bwd(dm, r["gs"], r["gd"], r["ga"], r["gb"], r["brdn"], "merge_bwd")
    dydn = mm(dbrdn, q["dn_proj_w"], "nt", "d_ydn")
    g["dn_proj_w"] = mm(r["ydn"], dbrdn, "tn", "dw_dn_proj")
    do, dz, dnw = dn_gate_bwd(r["o"], r["z"], q["dn_norm_w"], dydn, H, "dn_gate_bwd")
    g["dn_norm_w"] = dnw[0]
    dq, dk, dv, dbg, dgt, carried = dn_chunk_bwd(r["qn"], r["kn"], r["vv"], r["bg"], r["bgt"], r["ss"], do, H,
                                                 "dn_chunk_bwd", payload)
    L = dq.shape[0]
    dbg = dbg + jnp.concatenate([jnp.zeros((L, H), f32), dgt.transpose(0, 2, 1).reshape(L, H)], axis=1)
    dc, dba, dcw, dal, ddt = dn_prep_bwd(r["qkv"], q["dn_conv_w"], r["ba"], q["dn_a_log"], q["dn_dt_bias"],
                                         dq, dk, dv, dbg, H, "dn_prep_bwd")
    g["dn_conv_w"], g["dn_a_log"], g["dn_dt_bias"] = dcw, dal[0], ddt[0]
    dqkv = conv_t(dc, q["dn_conv_w"], "dn_conv_t")
    dys5 = mm(dga, q["Wga"], "nt", "d_ys5_a")
    dys5 = mm(dgb, q["Wgb"], "nt", "d_ys5_b", extras=(dys5,), epi=_add)
    D = dga.shape[1]
    dwglu = mm(r["ys5"], dga, "tn", "dw_glu_a", out_into=(2 * D, 0, None))
    g["s5_glu_w"] = mm(r["ys5"], dgb, "tn", "dw_glu_b", out_into=(2 * D, D, dwglu))
    dyp, du_direct, dd = s5_out_bwd(dys5, r["ypre"], r["u32"], q["s5_d"], "s5_out_bwd")
    g["s5_d"] = dd[0]
    G, P, HG = s5["G"], s5["P"], s5["HG"]
    gdr = mm_bd(dyp, s5["CreT"], "nt", "s5_gx_re", P, HG)
    gdi = mm_bd(dyp, s5["mCimT"], "nt", "s5_gx_im", P, HG)
    dcre = _diag_blocks(mm_bd(r["xr"], dyp, "tn", "dw_s5_c_re", P, HG), G, P, HG)
    dcim = _diag_blocks(mm_bd(r["xi"], dyp, "tn", "dw_s5_c_im", P, HG), G, P, HG)
    g["s5_c_re"], g["s5_c_im"] = dcre.transpose(0, 2, 1), -dcim.transpose(0, 2, 1)
    gxr, gxi, dar, dai = s5_scan_bwd(gdr, gdi, r["xr"], r["xi"], s5["ar"], s5["ai"], "s5_scan_bwd")
    dbre = _diag_blocks(mm_bd(r["u16"], gxr, "tn", "dw_s5_b_re", HG, P), G, HG, P)
    dbim = _diag_blocks(mm_bd(r["u16"], gxi, "tn", "dw_s5_b_im", HG, P), G, HG, P)
    tocol = lambda m: m.transpose(0, 2, 1).reshape(G * P, HG)
    dlr, dli, dlogdt, dbr, dbi = s5_disc_bwd(s5["lr"], s5["li"], s5["logdt"], s5["br"], s5["bi"],
                                             dar.reshape(G * P, 1), dai.reshape(G * P, 1), tocol(dbre), tocol(dbim), "s5_disc_bwd")
    g["s5_a_re"], g["s5_a_im"] = dlr.reshape(G, P), dli.reshape(G, P)
    g["s5_log_dt"] = jnp.sum(dlogdt.reshape(G, P), axis=1)
    g["s5_b_re"], g["s5_b_im"] = dbr.reshape(G, P, HG), dbi.reshape(G, P, HG)
    du = mm_bd(gxr, s5["Bre"], "nt", "d_u_re", HG, P)
    du = mm_bd(gxi, s5["Bim"], "nt", "d_u_im", HG, P, extras=(du, du_direct), epi=lambda a, b, c: (a + b + c,), out_dtypes=(bf16,))
    h1 = r["h1"]
    segs = [("Wu", du), ("Wqkv", dqkv), ("Wz", dz), ("Wba", dba), ("Wgs", dgs), ("Wgd", dgd)]
    dh1 = None
    dws = []
    for name, dseg in segs:
        if dh1 is None:
            dh1 = mm(dseg, q[name], "nt", "d_h1_" + name)
        else:
            dh1 = mm(dseg, q[name], "nt", "d_h1_" + name, extras=(dh1,), epi=_add)
        dw = mm(h1, dseg, "tn", "dw_in_" + name)
        dws.append(dw[:, :2 * H] if name == "Wba" else dw)
    g["w_in"] = jnp.concatenate(dws, axis=1)
    dx, dxb, dmix = rms_bwd(r["x"], q["mix_norm_w"], dh1, dx1, "rms_mix_bwd")
    g["mix_norm_w"] = dmix[0]
    return dx, dxb, g, carried


BIG = ("w_in", "s5_glu_w", "dn_proj_w", "w_out", "ffn_up", "ffn_down")
SHARDED_SMALL = ("dn_conv_w", "ffn_conv_w")
COL_SHARDED = ("w_in", "s5_glu_w", "dn_proj_w", "ffn_up", "dn_conv_w", "ffn_conv_w")
REPL = ("mix_norm_w", "s5_log_dt", "s5_a_re", "s5_a_im", "s5_b_re", "s5_b_im", "s5_c_re", "s5_c_im", "s5_d",
        "dn_a_log", "dn_dt_bias", "dn_norm_w", "ffn_norm_w")
WEIGHTS = ['mix_norm_w', 'w_in', 's5_log_dt', 's5_a_re', 's5_a_im', 's5_b_re', 's5_b_im', 's5_c_re', 's5_c_im', 's5_d',
           's5_glu_w', 'dn_conv_w', 'dn_a_log', 'dn_dt_bias', 'dn_norm_w', 'dn_proj_w', 'w_out', 'ffn_norm_w', 'ffn_up',
           'ffn_conv_w', 'ffn_down', 'final_norm_w']


def _join_shards(name, shards):
    return jnp.concatenate(shards, axis=-1 if name in COL_SHARDED else -2)


KINDS = {"w_in": "slot", "s5_glu_w": "col", "dn_proj_w": "col", "w_out": "slot", "ffn_up": "col", "ffn_down": "slot"}


BIG_KINDS = [KINDS[n] for n in BIG]


def gather_begin(shards):
    return ag_ici_payload([place_block(shards[n], KINDS[n], "ag_place_" + n) for n in BIG], BIG_KINDS)


def gather_finish(bufs):
    full = dict(zip(BIG, ag_d2d(bufs, BIG_KINDS, "ag_d2d")))
    for n in ("w_out", "ffn_down"):
        full[n] = full[n].reshape(-1, full[n].shape[-1])
    w = full["w_in"]
    full["w_in"] = w.transpose(1, 0, 2).reshape(w.shape[1], -1)
    return full


def gather_small_sharded(shards):
    names = SHARDED_SMALL
    shapes = [shards[n].shape for n in names]
    got = chip_exchange(_flat_pack([shards[n] for n in names], f32, lanes=128, row_mult=8), "ag_small", same=True)
    per_chip = [_flat_unpack(got[j], shapes) for j in range(N_CHIPS)]
    return {n: _join_shards(n, [per_chip[j][k] for j in range(N_CHIPS)]) for k, n in enumerate(names)}


def reduce_begin(g):
    tens = []
    for n in BIG:
        a = g[n]
        if n == "w_in":
            a = a.reshape(a.shape[0], N_CHIPS, -1).transpose(1, 0, 2)
        elif KINDS[n] == "slot":
            a = a.reshape(N_CHIPS, -1, a.shape[-1])
        tens.append(a)
    theirs = rs_pair(tens, BIG_KINDS, "rs_pair")
    sums = [add_half(a, t, k, "rs_pair_sum_" + n) for n, a, t, k in zip(BIG, tens, theirs, BIG_KINDS)]
    return rs_chips_payload(sums, BIG_KINDS)


def reduce_finish(parts):
    halves = [sum_slots_half(x, "rs_chip_sum_" + n) for n, x in zip(BIG, parts)]
    return {n: a.reshape(-1, a.shape[-1]) for n, a in zip(BIG, rs_share(halves, "rs_share"))}


def all_reduce_small(arrs):
    shapes = [a.shape for a in arrs]
    pack = _flat_pack(arrs, f32, lanes=1024, row_mult=64)
    from_chips = chip_exchange(pack, "ar_chips", same=True)
    from_sib = sibling_exchange(from_chips, "ar_sibling")
    c = lax.axis_index("c")
    both = jnp.concatenate([jnp.where(c == 0, from_chips, from_sib), jnp.where(c == 0, from_sib, from_chips)], axis=0)
    return _flat_unpack(sum_slots(both, "ar_sum"), shapes)


def kernel(x, mix_norm_w, w_in, s5_log_dt, s5_a_re, s5_a_im, s5_b_re, s5_b_im, s5_c_re, s5_c_im, s5_d, s5_glu_w, dn_conv_w, dn_a_log, dn_dt_bias, dn_norm_w, dn_proj_w, w_out, ffn_norm_w, ffn_up, ffn_conv_w, ffn_down, final_norm_w, loss_target, m_mix_norm_w, m_w_in, m_s5_log_dt, m_s5_a_re, m_s5_a_im, m_s5_b_re, m_s5_b_im, m_s5_c_re, m_s5_c_im, m_s5_d, m_s5_glu_w, m_dn_conv_w, m_dn_a_log, m_dn_dt_bias, m_dn_norm_w, m_dn_proj_w, m_w_out, m_ffn_norm_w, m_ffn_up, m_ffn_conv_w, m_ffn_down, m_final_norm_w, v_mix_norm_w, v_w_in, v_s5_log_dt, v_s5_a_re, v_s5_a_im, v_s5_b_re, v_s5_b_im, v_s5_c_re, v_s5_c_im, v_s5_d, v_s5_glu_w, v_dn_conv_w, v_dn_a_log, v_dn_dt_bias, v_dn_norm_w, v_dn_proj_w, v_w_out, v_ffn_norm_w, v_ffn_up, v_ffn_conv_w, v_ffn_down, v_final_norm_w):
    args = locals()
    W = {n: args[n] for n in WEIGHTS}
    M = {n: args["m_" + n] for n in WEIGHTS}
    V = {n: args["v_" + n] for n in WEIGHTS}
    depth = mix_norm_w.shape[0]
    H = dn_a_log.shape[1]
    xs = x[0]
    target = loss_target[0]

    conv_full = gather_small_sharded({n: W[n] for n in SHARDED_SMALL})

    def layer_params(l, gathered):
        p = gather_finish(gathered)
        for n in REPL:
            p[n] = W[n][l]
        for n in SHARDED_SMALL:
            p[n] = conv_full[n][l]
        for n in ("mix_norm_w", "ffn_norm_w", "dn_norm_w", "dn_a_log", "dn_dt_bias", "s5_d"):
            p[n] = p[n].reshape(1, -1)
        return _layer_weights(p, H), _s5_params(p, l)

    layers, res = [], []
    gathered = run_payload(gather_begin({n: W[n][0] for n in BIG}), "ag_ici")
    for l in range(depth):
        layers.append(layer_params(l, gathered))
        nxt = gather_begin({n: W[n][l + 1] for n in BIG}) if l + 1 < depth else None
        xs, r, gathered = layer_fwd(xs, layers[l][0], layers[l][1], H, nxt)
        res.append(r)
    dx, dxb, loss_part, dfinal = loss_head(xs, final_norm_w.reshape(1, -1), target, "loss_head")
    loss = lax.psum(loss_part[0, 0], ("x", "y", "c"))

    grads, sharded = [None] * depth, [None] * depth
    pending = None
    for l in reversed(range(depth)):
        dx, dxb, grads[l], parts = layer_bwd(dx, dxb, res[l], layers[l][0], layers[l][1], H, pending)
        if pending is not None:
            sharded[l + 1] = reduce_finish(parts)
        pending = reduce_begin(grads[l])
    sharded[0] = reduce_finish(run_payload(pending, "rs_chips"))
    grad_x = dx[None]

    G = {}
    for n in BIG:
        G[n] = jnp.stack([sharded[l][n] for l in range(depth)])
    small_names = REPL + SHARDED_SMALL
    small = [jnp.stack([grads[l][n] for l in range(depth)]) for n in small_names] + [dfinal[0]]
    for n, a in zip(small_names + ("final_norm_w",), all_reduce_small(small)):
        G[n] = a
    chip = 2 * lax.axis_index("x") + lax.axis_index("y")
    for n in SHARDED_SMALL:
        cs = W[n].shape[-1]
        G[n] = lax.dynamic_slice_in_dim(G[n], chip * cs, cs, axis=-1)

    delta, new_m, new_v = {}, {}, {}
    for n in WEIGHTS:
        shape = W[n].shape
        size = math.prod(shape)
        if n in BIG + SHARDED_SMALL:
            two_d = (size // shape[-1], shape[-1])
        else:
            two_d = (size // 128, 128) if size % 128 == 0 else (1, size)
        d, nm, nv = adamw(W[n].reshape(two_d), G[n].reshape(two_d), M[n].reshape(two_d), V[n].reshape(two_d), "adamw_" + n)
        delta[n], new_m[n], new_v[n] = d.reshape(shape), nm.reshape(shape), nv.reshape(shape)
        G[n] = G[n].reshape(shape)
    return (loss, grad_x, *[G[n] for n in WEIGHTS], *[delta[n] for n in WEIGHTS],
            *[new_m[n] for n in WEIGHTS], *[new_v[n] for n in WEIGHTS])
```

```python
import functools
import math

import jax
import jax.numpy as jnp
from jax import lax
from jax.experimental import pallas as pl
from jax.experimental.pallas import tpu as pltpu

f32 = jnp.float32
bf16 = jnp.bfloat16
S = jax.ShapeDtypeStruct

NORM_EPS = 1e-6
DN_CHUNK = 64
S5_GROUP = 16
ADAM_LR, ADAM_B1, ADAM_B2, ADAM_EPS, ADAM_WD, ADAM_STEP = 0.001, 0.9, 0.999, 1e-08, 0.01, 10
VMEM_LIMIT_BYTES = 56 * 1024 * 1024
HALO = 8
MESH = pl.DeviceIdType.MESH
N_CHIPS = 4


def _pick(n, cands):
    for c in cands:
        if n % c == 0:
            return c
    return n


MM_VMEM_BUDGET = 40 * 1024 * 1024
MM_MAX_TK = 2816


def _pick_k(K):
    if K <= MM_MAX_TK or K % 128:
        return K
    return max(d for d in range(128, MM_MAX_TK + 1, 128) if K % d == 0)


MM_MAX_TN = 1536


def _pick_n(N):
    if N <= 1024 or N % 128:
        return N
    return max(d for d in range(128, MM_MAX_TN + 1, 128) if N % d == 0)


def _cparams(sem):
    return pltpu.CompilerParams(dimension_semantics=sem, vmem_limit_bytes=VMEM_LIMIT_BYTES)


_DIMS = {"nn": ((1,), (0,)), "nt": ((1,), (1,)), "tn": ((0,), (0,))}


class Win:
    def __init__(self, arr, c0, nc):
        self.arr, self.c0, self.nc = arr, c0, nc


def mm(a, b, mode, name, extras=(), epi=None, out_dtypes=(f32,), out_into=None, payload=None):
    barr, c0 = (b.arr, b.c0) if isinstance(b, Win) else (b, 0)
    if mode == "tn":
        K, M = a.shape
    else:
        M, K = a.shape
    if mode == "nt":
        N, K2 = barr.shape
        K2 = b.nc if isinstance(b, Win) else K2
        n_off, k_off = 0, c0
    else:
        K2, N = barr.shape
        N = b.nc if isinstance(b, Win) else N
        n_off, k_off = c0, 0
    assert K == K2, (a.shape, barr.shape, mode)
    o_tot, o_off, o_alias = out_into if out_into is not None else (N, 0, None)
    tm = _pick_n(M)
    tn = _pick_n(math.gcd(math.gcd(N, n_off), o_off))
    tk = _pick_k(math.gcd(K, k_off))

    def vmem_estimate(tm_):
        tiles = tm_ * tk * a.dtype.itemsize + tk * tn * barr.dtype.itemsize
        tiles += sum(tm_ * tn * e.dtype.itemsize for e in extras if e.shape == (M, N))
        tiles += sum(tm_ * tn * jnp.dtype(dt).itemsize for dt in out_dtypes)
        return 2 * tiles + 3 * tm_ * tn * 4

    while vmem_estimate(tm) > MM_VMEM_BUDGET:
        smaller = [d for d in range(128, tm, 128) if M % d == 0]
        if not smaller:
            break
        tm = smaller[-1]
    nk = K // tk
    assert M % tm == 0 and N % tn == 0 and K % tk == 0 and n_off % tn == 0 and k_off % tk == 0 and o_off % tn == 0
    nb, kb, ob = n_off // tn, k_off // tk, o_off // tn
    if mode == "tn":
        a_spec = pl.BlockSpec((tk, tm), lambda j, i, k: (k, i))
    else:
        a_spec = pl.BlockSpec((tm, tk), lambda j, i, k: (i, k))
    if mode == "nt":
        b_spec = pl.BlockSpec((tn, tk), lambda j, i, k: (j, k + kb))
    else:
        b_spec = pl.BlockSpec((tk, tn), lambda j, i, k: (k, j + nb))
    ex_specs = []
    for e in extras:
        if e.shape == (M, N):
            ex_specs.append(pl.BlockSpec((tm, tn), lambda j, i, k: (i, j)))
        elif e.shape == (1, N):
            ex_specs.append(pl.BlockSpec((1, tn), lambda j, i, k: (0, j)))
        elif e.shape == (M, 1):
            ex_specs.append(pl.BlockSpec((tm, 1), lambda j, i, k: (i, 0)))
        else:
            raise ValueError((e.shape, M, N))
    ne, no = len(extras), len(out_dtypes)
    na = 1 if o_alias is not None else 0
    assert out_into is None or no == 1
    dims = (_DIMS[mode], ((), ()))
    nj, ni = N // tn, M // tm
    split, hooks, extra = _host(payload, 2 + ne + na, no, 1 if nk > 1 else 0)

    def body(*refs):
        own, theirs = split(refs)
        a_ref, b_ref, rest = own[0], own[1], own[2:]
        ex, outs = rest[:ne], rest[ne + na:ne + na + no]
        j, i, k = pl.program_id(0), pl.program_id(1), pl.program_id(2)
        hooks(theirs, (j == 0) & (i == 0) & (k == 0), (j == nj - 1) & (i == ni - 1) & (k == nk - 1))
        p = lax.dot_general(a_ref[...].astype(bf16), b_ref[...].astype(bf16), dims, preferred_element_type=f32)

        def finish(acc):
            res = epi(acc, *[e[...] for e in ex]) if epi is not None else (acc,)
            for o, r in zip(outs, res):
                o[...] = r.astype(o.dtype)

        if nk == 1:
            finish(p)
        else:
            acc_ref = rest[-1]

            @pl.when(k == 0)
            def _():
                acc_ref[...] = p

            @pl.when(k > 0)
            def _():
                acc_ref[...] += p

            @pl.when(k == nk - 1)
            def _():
                finish(acc_ref[...])

    aliases = dict(extra["aliases"])
    if na:
        aliases[2 + ne] = 0
    res = pl.pallas_call(
        body,
        name=name,
        grid=(nj, ni, nk),
        in_specs=[a_spec, b_spec] + ex_specs + [pl.BlockSpec(memory_space=pl.ANY)] * na + extra["in_specs"],
        out_specs=[pl.BlockSpec((tm, tn), lambda j, i, k: (i, j + ob)) for _ in out_dtypes] + extra["out_specs"],
        out_shape=[S((M, o_tot), dt) for dt in out_dtypes] + extra["out_shape"],
        scratch_shapes=([pltpu.VMEM((tm, tn), f32)] if nk > 1 else []) + extra["scratch"],
        input_output_aliases=aliases,
        compiler_params=_cparams(("arbitrary",) * 3 if payload is not None else ("parallel", "parallel", "arbitrary")),
    )(a, barr, *extras, *([o_alias] if na else []), *extra["arrays"])
    outs = res[:no]
    if payload is not None:
        return tuple(outs) + (list(res[no:]),)
    return outs[0] if no == 1 else tuple(outs)


def mm_bd(a, b, mode, name, ga, gb, extras=(), epi=None, out_dtypes=(f32,)):
    T = max(1, min(256 // min(ga, gb), 1024 // max(ga, gb)))
    if mode == "tn":
        K, M = a.shape
        N = b.shape[1]
        G = M // ga
        T = min(T, G)
        tm, tn, tk = T * ga, T * gb, _pick_k(K)
        nk = K // tk
        grid = (G // T, 1, nk)
        a_spec = pl.BlockSpec((tk, tm), lambda j, i, k: (k, j))
        b_spec = pl.BlockSpec((tk, tn), lambda j, i, k: (k, j))
        o_spec = pl.BlockSpec((tm, tn), lambda j, i, k: (j, 0))
        out_shape = (M, tn)
    else:
        M = a.shape[0]
        if mode == "nn":
            G = b.shape[0] // ga
            T = min(T, G)
            kw, tn, N = T * ga, T * gb, G * gb
            b_spec = pl.BlockSpec((kw, tn), lambda j, i, k: (j, j))
        else:
            G = b.shape[0] // ga
            T = min(T, G)
            kw, tn, N = T * gb, T * ga, G * ga
            b_spec = pl.BlockSpec((tn, kw), lambda j, i, k: (j, j))
        tm = _pick(M, (1024, 512, 256, 128, 64, 32, 16, 8))
        nk = 1
        grid = (G // T, M // tm, 1)
        a_spec = pl.BlockSpec((tm, kw), lambda j, i, k: (i, j))
        o_spec = pl.BlockSpec((tm, tn), lambda j, i, k: (i, j))
        out_shape = (M, N)
    ex_specs = []
    for e in extras:
        if e.shape == out_shape:
            ex_specs.append(o_spec)
        elif e.shape == (1, out_shape[1]):
            ex_specs.append(pl.BlockSpec((1, tn), lambda j, i, k: (0, j)))
        else:
            raise ValueError((e.shape, out_shape))
    ne, no = len(extras), len(out_dtypes)
    dims = (_DIMS[mode], ((), ()))

    def body(a_ref, b_ref, *rest):
        ex, outs = rest[:ne], rest[ne:ne + no]
        p = lax.dot_general(a_ref[...].astype(bf16), b_ref[...].astype(bf16), dims, preferred_element_type=f32)

        def finish(acc):
            res = epi(acc, *[e[...] for e in ex]) if epi is not None else (acc,)
            for o, r in zip(outs, res):
                o[...] = r.astype(o.dtype)

        if nk == 1:
            finish(p)
        else:
            acc_ref = rest[-1]
            k = pl.program_id(2)

            @pl.when(k == 0)
            def _():
                acc_ref[...] = p

            @pl.when(k > 0)
            def _():
                acc_ref[...] += p

            @pl.when(k == nk - 1)
            def _():
                finish(acc_ref[...])

    outs = pl.pallas_call(
        body, name=name, grid=grid, in_specs=[a_spec, b_spec] + ex_specs, out_specs=[o_spec] * no,
        out_shape=[S(out_shape, dt) for dt in out_dtypes],
        scratch_shapes=[pltpu.VMEM((tm, tn), f32)] if nk > 1 else [],
        compiler_params=_cparams(("parallel", "parallel", "arbitrary")),
    )(a, b, *extras)
    return outs[0] if no == 1 else tuple(outs)


def _diag_blocks(tiles, G, ga, gb):
    T = tiles.shape[1] // gb
    t5 = tiles.reshape(G // T, T, ga, T, gb)
    return jnp.sum(t5 * jnp.eye(T, dtype=tiles.dtype)[None, :, None, :, None], axis=3).reshape(G, ga, gb)


def _add(acc, prev):
    return (acc + prev,)


def rowk(name, fn, L, tl, ncol, ins, outs, accs=()):
    nrow = L // tl
    assert L % tl == 0 and tl % HALO == 0
    hb = tl // HALO

    def cw_of(c_total):
        assert c_total % ncol == 0, (name, c_total, ncol)
        return c_total // ncol

    in_specs = []
    for arr, kind in ins:
        if kind == "rows":
            in_specs.append(pl.BlockSpec((tl, cw_of(arr.shape[1])), lambda j, i: (i, j)))
        elif kind == "prev":
            in_specs.append(pl.BlockSpec((HALO, cw_of(arr.shape[1])), lambda j, i: (jnp.maximum(i * hb - 1, 0), j)))
        elif kind == "next":
            in_specs.append(pl.BlockSpec((HALO, cw_of(arr.shape[1])), lambda j, i: (jnp.minimum((i + 1) * hb, nrow * hb - 1), j)))
        elif kind == "cols":
            in_specs.append(pl.BlockSpec((arr.shape[0], cw_of(arr.shape[1])), lambda j, i: (0, j)))
        elif kind == "const":
            in_specs.append(pl.BlockSpec(arr.shape, lambda j, i: (0,) * arr.ndim))
        else:
            raise ValueError(kind)
    out_specs = [pl.BlockSpec((tl, cw_of(c)), lambda j, i: (i, j)) for c, _ in outs]
    out_shape = [S((L, c), dt) for c, dt in outs]
    out_specs += [pl.BlockSpec((r, cw_of(c)), lambda j, i: (0, j)) for r, c, _ in accs]
    out_shape += [S((r, c), dt) for r, c, dt in accs]
    ni, no, na = len(ins), len(outs), len(accs)

    def body(*refs):
        i = pl.program_id(1)
        res = fn(i, nrow, *[r[...] for r in refs[:ni]])
        for o, r in zip(refs[ni:ni + no], res[:no]):
            o[...] = r.astype(o.dtype)
        for o, r in zip(refs[ni + no:ni + no + na], res[no:]):
            @pl.when(i == 0)
            def _(o=o, r=r):
                o[...] = r.astype(o.dtype)

            @pl.when(i > 0)
            def _(o=o, r=r):
                o[...] += r.astype(o.dtype)

    res = pl.pallas_call(
        body,
        name=name,
        grid=(ncol, nrow),
        in_specs=in_specs,
        out_specs=out_specs,
        out_shape=out_shape,
        compiler_params=_cparams(("parallel", "arbitrary")),
    )(*[a for a, _ in ins])
    return tuple(res)


def _sigmoid(x):
    return 1.0 / (1.0 + jnp.exp(-x))


def _silu(x):
    return x * _sigmoid(x)


def _dsilu(x):
    s = _sigmoid(x)
    return s * (1.0 + x * (1.0 - s))


def _erf(x):
    a = jnp.abs(x)
    t = 1.0 / (1.0 + 0.3275911 * a)
    poly = t * (0.254829592 + t * (-0.284496736 + t * (1.421413741 + t * (-1.453152027 + t * 1.061405429))))
    y = 1.0 - poly * jnp.exp(-a * a)
    return jnp.where(x < 0, -y, y)


def _gelu(x):
    return 0.5 * x * (1.0 + _erf(x * (2.0 ** -0.5)))


def _dgelu(x):
    cdf = 0.5 * (1.0 + _erf(x * (2.0 ** -0.5)))
    pdf = jnp.exp(-0.5 * x * x) * (1.0 / math.sqrt(2.0 * math.pi))
    return cdf + x * pdf


def _rms(x, w):
    return x * lax.rsqrt(jnp.mean(x * x, axis=-1, keepdims=True) + NORM_EPS) * w


def _rms_bwd(x, w, dy):
    d = x.shape[-1]
    r = lax.rsqrt(jnp.mean(x * x, axis=-1, keepdims=True) + NORM_EPS)
    wdy = w * dy
    dx = r * wdy - x * (r * r * r) * (jnp.sum(x * wdy, axis=-1, keepdims=True) / d)
    dw = jnp.sum(x * r * dy, axis=0, keepdims=True)
    return dx, dw


def _from_cols(cols, width):
    tl = cols[0].shape[0]
    lane = lax.broadcasted_iota(jnp.int32, (tl, width), 1)
    out = jnp.zeros((tl, width), f32)
    for n, col in enumerate(cols):
        out = jnp.where(lane == n, col, out)
    return out


def _from_rows(rows):
    c = rows[0].shape[1]
    sub = lax.broadcasted_iota(jnp.int32, (len(rows), c), 0)
    out = jnp.zeros((len(rows), c), f32)
    for n, row in enumerate(rows):
        out = jnp.where(sub == n, row, out)
    return out


def _shift_down(x, halo, s, first):
    if s == 0:
        return x
    tl = x.shape[0]
    halo = jnp.where(first, 0.0, halo)
    xx = jnp.concatenate([halo, x], axis=0)
    return pltpu.roll(xx, s, 0)[HALO:HALO + tl]


def _shift_up(x, halo, s, last):
    if s == 0:
        return x
    tl = x.shape[0]
    halo = jnp.where(last, 0.0, halo)
    xx = jnp.concatenate([x, halo], axis=0)
    return pltpu.roll(xx, tl + HALO - s, 0)[0:tl]


def _causal_conv(x, halo, w, first):
    kw = w.shape[0]
    shifted = [_shift_down(x, halo, kw - 1 - j, first) for j in range(kw)]
    out = shifted[0] * w[0:1]
    for j in range(1, kw):
        out = out + shifted[j] * w[j:j + 1]
    return out, shifted


def rms_fwd(x, w, name):
    L, D = x.shape

    def fn(i, n, xb, wb):
        return (_rms(xb, wb),)

    return rowk(name, fn, L, _pick(L, (256, 128, 64, 32, 16, 8)), 1, [(x, "rows"), (w, "const")], [(D, bf16)])[0]


def rms_bwd(x, w, dh, dres, name):
    L, D = x.shape

    def fn(i, n, xb, wb, dhb, drb):
        dx, dw = _rms_bwd(xb, wb, dhb)
        dx = dx + drb
        return dx, dx, dw

    return rowk(name, fn, L, _pick(L, (256, 128, 64, 32, 16, 8)), 1,
                [(x, "rows"), (w, "const"), (dh, "rows"), (dres, "rows")], [(D, f32), (D, bf16)], [(1, D, f32)])


def loss_head(x, w, target, name):
    L, D = x.shape

    def fn(i, n, xb, wb, tb):
        err = _rms(xb, wb) - tb
        loss = 0.5 * jnp.sum(err * err) / D
        dx, dw = _rms_bwd(xb, wb, err / D)
        return dx, dx, jnp.full((8, 128), loss, f32), dw

    return rowk(name, fn, L, _pick(L, (256, 128, 64, 32, 16, 8)), 1,
                [(x, "rows"), (w, "const"), (target, "rows")], [(D, f32), (D, bf16)], [(8, 128, f32), (1, D, f32)])


def _s5_disc_math(lr, li, logdt, br, bi):
    dt = jnp.exp(logdt)
    mag = jnp.exp(lr * dt)
    ar, ai = mag * jnp.cos(li * dt), mag * jnp.sin(li * dt)
    den = lr * lr + li * li
    nr, ni = ar - 1.0, ai
    cr = (nr * lr + ni * li) / den
    ci = (ni * lr - nr * li) / den
    return ar, ai, cr * br - ci * bi, cr * bi + ci * br


def _disc_call(body, name, ins, out_widths):
    GP = ins[0].shape[0]
    tl = _pick(GP, (512, 256, 128, 64, 32, 16, 8))
    spec = lambda w: pl.BlockSpec((tl, w), lambda i: (i, 0))
    return pl.pallas_call(body, name=name, grid=(GP // tl,),
                          in_specs=[spec(a.shape[1]) for a in ins], out_specs=[spec(w) for w in out_widths],
                          out_shape=[S((GP, w), f32) for w in out_widths], compiler_params=_cparams(("parallel",)))(*ins)


def s5_disc(lr, li, logdt, br, bi, name):
    HG = br.shape[1]

    def body(lr_ref, li_ref, dt_ref, br_ref, bi_ref, ar_ref, ai_ref, bbr_ref, bbi_ref):
        ar, ai, bbr, bbi = _s5_disc_math(lr_ref[...], li_ref[...], dt_ref[...], br_ref[...], bi_ref[...])
        ar_ref[...], ai_ref[...], bbr_ref[...], bbi_ref[...] = ar, ai, bbr, bbi

    return _disc_call(body, name, [lr, li, logdt, br, bi], [1, 1, HG, HG])


def s5_disc_bwd(lr, li, logdt, br, bi, dar, dai, dbbr, dbbi, name):
    HG = br.shape[1]

    def body(lr_ref, li_ref, dt_ref, br_ref, bi_ref, dar_ref, dai_ref, dbbr_ref, dbbi_ref, *outs):
        _, vjp = jax.vjp(_s5_disc_math, lr_ref[...], li_ref[...], dt_ref[...], br_ref[...], bi_ref[...])
        for o, g in zip(outs, vjp((dar_ref[...], dai_ref[...], dbbr_ref[...], dbbi_ref[...]))):
            o[...] = g

    return _disc_call(body, name, [lr, li, logdt, br, bi, dar, dai, dbbr, dbbi], [1, 1, 1, HG, HG])


SCAN_TB = 256


def s5_scan_fwd(bur, bui, ar, ai, name):
    L, GP = bur.shape
    cw = _pick(GP, (1024, 512, 256, 128))
    tb = _pick(L, (SCAN_TB, 128, 64, 32, 16, 8))

    def body(bur_ref, bui_ref, ar_ref, ai_ref, xr_ref, xi_ref, cr_ref, ci_ref):
        @pl.when(pl.program_id(1) == 0)
        def _():
            cr_ref[...] = jnp.zeros_like(cr_ref)
            ci_ref[...] = jnp.zeros_like(ci_ref)

        a_r, a_i = ar_ref[...], ai_ref[...]

        def step(t, carry):
            xr, xi = carry
            row = pl.ds(t, 1)
            nr = a_r * xr - a_i * xi + bur_ref[row, :]
            ni = a_r * xi + a_i * xr + bui_ref[row, :]
            xr_ref[row, :] = nr
            xi_ref[row, :] = ni
            return nr, ni

        xr, xi = lax.fori_loop(0, tb, step, (cr_ref[...], ci_ref[...]), unroll=8)
        cr_ref[...] = xr
        ci_ref[...] = xi

    blk = pl.BlockSpec((tb, cw), lambda j, i: (i, j))
    vec = pl.BlockSpec((1, cw), lambda j, i: (0, j))
    return pl.pallas_call(
        body, name=name, grid=(GP // cw, L // tb),
        in_specs=[blk, blk, vec, vec], out_specs=[blk, blk],
        out_shape=[S((L, GP), f32), S((L, GP), f32)],
        scratch_shapes=[pltpu.VMEM((1, cw), f32), pltpu.VMEM((1, cw), f32)],
        compiler_params=_cparams(("parallel", "arbitrary")),
    )(bur, bui, ar, ai)


def s5_scan_bwd(gr, gi, xr, xi, ar, ai, name):
    L, GP = gr.shape
    cw = _pick(GP, (1024, 512, 256, 128))
    tb = _pick(L, (SCAN_TB, 128, 64, 32, 16, 8))
    nt = L // tb

    def body(gr_ref, gi_ref, xr_ref, xi_ref, ar_ref, ai_ref, gxr_ref, gxi_ref, dar_ref, dai_ref, cr_ref, ci_ref):
        @pl.when(pl.program_id(1) == 0)
        def _():
            cr_ref[...] = jnp.zeros_like(cr_ref)
            ci_ref[...] = jnp.zeros_like(ci_ref)
            dar_ref[...] = jnp.zeros_like(dar_ref)
            dai_ref[...] = jnp.zeros_like(dai_ref)

        a_r, a_i = ar_ref[...], ai_ref[...]

        def step(s, carry):
            cr, ci, dr, di = carry
            row = pl.ds(tb - 1 - s, 1)
            x_r, x_i = xr_ref[row, :], xi_ref[row, :]
            dr = dr + cr * x_r + ci * x_i
            di = di + ci * x_r - cr * x_i
            nr = gr_ref[row, :] + a_r * cr + a_i * ci
            ni = gi_ref[row, :] + a_r * ci - a_i * cr
            gxr_ref[row, :] = nr
            gxi_ref[row, :] = ni
            return nr, ni, dr, di

        cr, ci, dr, di = lax.fori_loop(0, tb, step, (cr_ref[...], ci_ref[...], dar_ref[...], dai_ref[...]), unroll=8)
        cr_ref[...] = cr
        ci_ref[...] = ci
        dar_ref[...] = dr
        dai_ref[...] = di

    blk = pl.BlockSpec((tb, cw), lambda j, i: (nt - 1 - i, j))
    vec = pl.BlockSpec((1, cw), lambda j, i: (0, j))
    return pl.pallas_call(
        body, name=name, grid=(GP // cw, nt),
        in_specs=[blk, blk, blk, blk, vec, vec], out_specs=[blk, blk, vec, vec],
        out_shape=[S((L, GP), f32), S((L, GP), f32), S((1, GP), f32), S((1, GP), f32)],
        scratch_shapes=[pltpu.VMEM((1, cw), f32), pltpu.VMEM((1, cw), f32)],
        compiler_params=_cparams(("parallel", "arbitrary")),
    )(gr, gi, xr, xi, ar, ai)


def _dn_heads_math(cq, ck, cv, braw, araw, alog, dtb, dk):
    q, k, v = _silu(cq), _silu(ck), _silu(cv)
    q = q * lax.rsqrt(jnp.sum(q * q, axis=-1, keepdims=True) + NORM_EPS) * (dk ** -0.5)
    k = k * lax.rsqrt(jnp.sum(k * k, axis=-1, keepdims=True) + NORM_EPS)
    beta = _sigmoid(braw)
    g = -jnp.exp(alog) * jax.nn.softplus(araw + dtb)
    return q, k, v, beta, g


def dn_prep(qkv, convw, ba, alog, dtb, H, name):
    L, W = qkv.shape
    hk = W // 3
    dk = hk // H

    def fn(i, n, xb, hb, wb, bab, alb, dtbb):
        c, _ = _causal_conv(xb, hb, wb, i == 0)
        qs, ks, vs, bs, gs = [], [], [], [], []
        for h in range(H):
            sl = lambda o: c[:, o + h * dk:o + (h + 1) * dk]
            q, k, v, beta, g = _dn_heads_math(sl(0), sl(hk), sl(2 * hk), bab[:, h:h + 1], bab[:, H + h:H + h + 1],
                                              alb[:, h:h + 1], dtbb[:, h:h + 1], dk)
            qs.append(q), ks.append(k), vs.append(v), bs.append(beta), gs.append(g)
        cat = lambda xs: jnp.concatenate(xs, axis=1)
        return cat(qs), cat(ks), cat(vs), _from_cols(bs + gs, 2 * H)

    return rowk(name, fn, L, _pick(L, (128, 64, 32, 16, 8)), 1,
                [(qkv, "rows"), (qkv, "prev"), (convw, "const"), (ba, "rows"), (alog, "const"), (dtb, "const")],
                [(hk, f32), (hk, f32), (hk, f32), (2 * H, f32)])


def dn_prep_bwd(qkv, convw, ba, alog, dtb, dq, dk_, dv, dbg, H, name):
    L, W = qkv.shape
    hk = W // 3
    dk = hk // H
    kw = convw.shape[0]
    nba = ba.shape[1]

    def fn(i, n, xb, hb, wb, bab, alb, dtbb, dqb, dkb, dvb, dbgb):
        c, shifted = _causal_conv(xb, hb, wb, i == 0)
        dcs = [None] * (3 * H)
        dbr, dar, dal, ddt = [], [], [], []
        for h in range(H):
            sl = lambda a, o: a[:, o + h * dk:o + (h + 1) * dk]
            args = (sl(c, 0), sl(c, hk), sl(c, 2 * hk), bab[:, h:h + 1], bab[:, H + h:H + h + 1],
                    alb[:, h:h + 1], dtbb[:, h:h + 1])
            _, vjp = jax.vjp(lambda *a: _dn_heads_math(*a, dk), *args)
            g = vjp((sl(dqb, 0), sl(dkb, 0), sl(dvb, 0), dbgb[:, h:h + 1], dbgb[:, H + h:H + h + 1]))
            dcs[h], dcs[H + h], dcs[2 * H + h] = g[0], g[1], g[2]
            dbr.append(g[3]), dar.append(g[4]), dal.append(g[5]), ddt.append(g[6])
        dc = jnp.concatenate(dcs, axis=1)
        dba = _from_cols(dbr + dar, nba)
        dw = _from_rows([jnp.sum(dc * shifted[j], axis=0, keepdims=True) for j in range(kw)])
        return dc, dba, dw, _from_cols(dal, H), _from_cols(ddt, H)

    return rowk(name, fn, L, _pick(L, (128, 64, 32, 16, 8)), 1,
                [(qkv, "rows"), (qkv, "prev"), (convw, "const"), (ba, "rows"), (alog, "const"), (dtb, "const"),
                 (dq, "rows"), (dk_, "rows"), (dv, "rows"), (dbg, "rows")],
                [(W, f32), (nba, bf16)], [(kw, W, f32), (1, H, f32), (1, H, f32)])


def conv_t(dc, w, name):
    L, C = dc.shape
    kw = w.shape[0]
    ncol = C // _pick(C, (1536, 1408, 1024, 768, 512, 256, 128))

    def fn(i, n, db, hb, wb):
        out = db * wb[kw - 1:kw]
        for j in range(kw - 1):
            out = out + _shift_up(db, hb, kw - 1 - j, i == n - 1) * wb[j:j + 1]
        return (out,)

    return rowk(name, fn, L, _pick(L, (256, 128, 64, 32, 16, 8)), ncol,
                [(dc, "rows"), (dc, "next"), (w, "cols")], [(C, bf16)])[0]


_BDIMS = {"nn": (((2,), (1,)), ((0,), (0,))), "nt": (((2,), (2,)), ((0,), (0,))), "tn": (((1,), (1,)), ((0,), (0,)))}


def _bdot(a, b, mode):
    return lax.dot_general(a.astype(bf16), b.astype(bf16), _BDIMS[mode], preferred_element_type=f32)


def _split16(a):
    hi = a.astype(bf16)
    return hi, (a - hi.astype(f32)).astype(bf16)


def _hdot(a, b, mode):
    ah, al = _split16(a)
    bh, bl = _split16(b)
    d = lambda x, y: lax.dot_general(x, y, _BDIMS[mode], preferred_element_type=f32)
    return d(ah, bh) + (d(ah, bl) + d(al, bh))


def _make_dot(raw):
    @functools.partial(jax.custom_vjp, nondiff_argnums=(2,))
    def dot(a, b, mode):
        return raw(a, b, mode)

    def fwd(a, b, mode):
        return raw(a, b, mode), (a, b)

    def bwd(mode, res, ct):
        a, b = res
        if mode == "nn":
            return raw(ct, b, "nt"), raw(a, ct, "tn")
        if mode == "nt":
            return raw(ct, b, "nn"), raw(ct, a, "tn")
        return raw(b, ct, "nt"), raw(a, ct, "nn")

    dot.defvjp(fwd, bwd)
    return dot


_dot16 = _make_dot(_bdot)
_dot32 = _make_dot(_hdot)


@jax.custom_vjp
def _unit_lower_inv(lmat):
    c = lmat.shape[-1]
    eye = (lax.broadcasted_iota(jnp.int32, (c, c), 0) == lax.broadcasted_iota(jnp.int32, (c, c), 1)).astype(f32)
    p = -lmat
    t = eye + p
    for _ in range(int(math.log2(c)) - 1):
        p = _hdot(p, p, "nn")
        t = t + _hdot(t, p, "nn")
    return t


def _uli_fwd(lmat):
    t = _unit_lower_inv(lmat)
    return t, t


def _uli_bwd(t, dt):
    return (-_hdot(_hdot(t, dt, "tn"), t, "nt"),)


_unit_lower_inv.defvjp(_uli_fwd, _uli_bwd)


def _dn_chunk_math(s_in, q, k, v, gcol, grow, bcol):
    c = q.shape[1]
    ri = lax.broadcasted_iota(jnp.int32, (c, c), 0)
    ci = lax.broadcasted_iota(jnp.int32, (c, c), 1)
    tril = (ri >= ci).astype(f32)
    strict = (ri > ci).astype(f32)
    gc_col = jnp.sum(tril * grow, axis=2, keepdims=True)
    gc_row = jnp.sum((1.0 - strict) * gcol, axis=1, keepdims=True)
    g_last = jnp.sum(gcol, axis=1, keepdims=True)
    decay = jnp.exp((gc_col - gc_row) * tril) * tril
    kb = k * bcol
    vb = v * bcol
    lmat = _dot16(kb, k, "nt") * decay * strict
    t = _unit_lower_inv(lmat)
    u = _dot32(t, vb, "nn")
    w = _dot32(t, kb * jnp.exp(gc_col), "nn")
    attn = _dot16(q, k, "nt") * decay
    v_new = u - _dot16(w, s_in, "nn")
    o = _dot16(q * jnp.exp(gc_col), s_in, "nn") + _dot16(attn, v_new, "nn")
    s_out = s_in * jnp.exp(g_last) + _dot16(k * jnp.exp(g_last - gc_col), v_new, "tn")
    return o, s_out


def _dn_load(q_ref, k_ref, v_ref, bg_ref, bgt_ref, H, dk):
    heads = lambda ref: jnp.stack([ref[:, h * dk:(h + 1) * dk] for h in range(H)])
    bgb, bgtb = bg_ref[...], bgt_ref[0]
    gcol = jnp.stack([bgb[:, H + h:H + h + 1] for h in range(H)])
    bcol = jnp.stack([bgb[:, h:h + 1] for h in range(H)])
    grow = jnp.stack([bgtb[H + h:H + h + 1, :] for h in range(H)])
    return heads(q_ref), heads(k_ref), heads(v_ref), gcol, grow, bcol


def _host(payload, n_in, n_out, n_scratch):
    pi = len(payload.arrays) if payload is not None else 0
    po = len(payload.out_shapes) if payload is not None else 0

    def split(refs):
        own = refs[:n_in] + refs[n_in + pi:n_in + pi + n_out] + refs[n_in + pi + n_out + po:n_in + pi + n_out + po + n_scratch]
        theirs = (refs[n_in:n_in + pi], refs[n_in + pi + n_out:n_in + pi + n_out + po], refs[n_in + pi + n_out + po + n_scratch:])
        return own, theirs

    def hooks(theirs, first, last):
        if payload is None:
            return

        @pl.when(first)
        def _():
            payload.start(*theirs)

        @pl.when(last)
        def _():
            payload.finish(*theirs)

    extra = dict(in_specs=[_ANY] * pi, out_specs=[_ANY] * po, out_shape=list(payload.out_shapes) if payload else [],
                 scratch=list(payload.sems) if payload else [], arrays=list(payload.arrays) if payload else [],
                 aliases={n_in + a: n_out + b for a, b in payload.aliases.items()} if payload else {})
    return split, hooks, extra


def dn_chunk_fwd(qn, kn, vv, bg, bgt, H, name, payload=None):
    L, hk = qn.shape
    dk = hk // H
    c = DN_CHUNK
    nc = L // c
    split, hooks, extra = _host(payload, 5, 2, 1)

    def body(*refs):
        (q_ref, k_ref, v_ref, bg_ref, bgt_ref, o_ref, ss_ref, s_ref), theirs = split(refs)
        hooks(theirs, pl.program_id(0) == 0, pl.program_id(0) == nc - 1)

        @pl.when(pl.program_id(0) == 0)
        def _():
            s_ref[...] = jnp.zeros_like(s_ref)

        s_in = s_ref[...]
        ss_ref[0] = s_in
        o, s_out = _dn_chunk_math(s_in, *_dn_load(q_ref, k_ref, v_ref, bg_ref, bgt_ref, H, dk))
        for h in range(H):
            o_ref[:, h * dk:(h + 1) * dk] = o[h]
        s_ref[...] = s_out

    row = lambda w: pl.BlockSpec((c, w), lambda n: (n, 0))
    res = pl.pallas_call(
        body, name=name, grid=(nc,),
        in_specs=[row(hk), row(hk), row(hk), row(2 * H), pl.BlockSpec((1, 2 * H, c), lambda n: (n, 0, 0))] + extra["in_specs"],
        out_specs=[row(hk), pl.BlockSpec((1, H, dk, dk), lambda n: (n, 0, 0, 0))] + extra["out_specs"],
        out_shape=[S((L, hk), f32), S((nc, H, dk, dk), f32)] + extra["out_shape"],
        scratch_shapes=[pltpu.VMEM((H, dk, dk), f32)] + extra["scratch"],
        input_output_aliases=extra["aliases"],
        compiler_params=_cparams(("arbitrary",)),
    )(qn, kn, vv, bg, bgt, *extra["arrays"])
    return res[0], res[1], list(res[2:])


def dn_chunk_bwd(qn, kn, vv, bg, bgt, ss, do, H, name, payload=None):
    L, hk = qn.shape
    dk = hk // H
    c = DN_CHUNK
    nc = L // c
    split, hooks, extra = _host(payload, 7, 5, 1)

    def body(*refs):
        (q_ref, k_ref, v_ref, bg_ref, bgt_ref, ss_ref, do_ref, dq_ref, dk_ref, dv_ref, dbg_ref, dgt_ref, ds_ref), theirs = split(refs)
        hooks(theirs, pl.program_id(0) == 0, pl.program_id(0) == nc - 1)

        @pl.when(pl.program_id(0) == 0)
        def _():
            ds_ref[...] = jnp.zeros_like(ds_ref)

        args = (ss_ref[0],) + _dn_load(q_ref, k_ref, v_ref, bg_ref, bgt_ref, H, dk)
        _, vjp = jax.vjp(_dn_chunk_math, *args)
        do = jnp.stack([do_ref[:, h * dk:(h + 1) * dk] for h in range(H)])
        ds, dq, dkk, dv, dgcol, dgrow, dbcol = vjp((do, ds_ref[...]))
        ds_ref[...] = ds
        for h in range(H):
            sl = slice(h * dk, (h + 1) * dk)
            dq_ref[:, sl], dk_ref[:, sl], dv_ref[:, sl] = dq[h], dkk[h], dv[h]
        dbg_ref[...] = _from_cols([dbcol[h] for h in range(H)] + [dgcol[h] for h in range(H)], 2 * H)
        dgt_ref[0] = _from_rows([dgrow[h] for h in range(H)])

    row = lambda w: pl.BlockSpec((c, w), lambda n: (nc - 1 - n, 0))
    res = pl.pallas_call(
        body, name=name, grid=(nc,),
        in_specs=[row(hk), row(hk), row(hk), row(2 * H), pl.BlockSpec((1, 2 * H, c), lambda n: (nc - 1 - n, 0, 0)),
                  pl.BlockSpec((1, H, dk, dk), lambda n: (nc - 1 - n, 0, 0, 0)), row(hk)] + extra["in_specs"],
        out_specs=[row(hk), row(hk), row(hk), row(2 * H), pl.BlockSpec((1, H, c), lambda n: (nc - 1 - n, 0, 0))] + extra["out_specs"],
        out_shape=[S((L, hk), f32), S((L, hk), f32), S((L, hk), f32), S((L, 2 * H), f32), S((nc, H, c), f32)] + extra["out_shape"],
        scratch_shapes=[pltpu.VMEM((H, dk, dk), f32)] + extra["scratch"],
        input_output_aliases=extra["aliases"],
        compiler_params=_cparams(("arbitrary",)),
    )(qn, kn, vv, bg, bgt, ss, do, *extra["arrays"])
    return tuple(res[:5]) + (list(res[5:]),)


def _dn_gate_math(o, z, w):
    return _rms(o, w) * _silu(z)


def dn_gate(o, z, w, H, name):
    L, hv = o.shape
    dv = hv // H

    def fn(i, n, ob, zb, wb):
        return (jnp.concatenate([_dn_gate_math(ob[:, h * dv:(h + 1) * dv], zb[:, h * dv:(h + 1) * dv], wb)
                                 for h in range(H)], axis=1),)

    return rowk(name, fn, L, _pick(L, (256, 128, 64, 32, 16, 8)), 1, [(o, "rows"), (z, "rows"), (w, "const")], [(hv, bf16)])[0]


def dn_gate_bwd(o, z, w, dy, H, name):
    L, hv = o.shape
    dv = hv // H

    def fn(i, n, ob, zb, wb, dyb):
        dos, dzs, dw = [], [], 0.0
        for h in range(H):
            sl = slice(h * dv, (h + 1) * dv)
            _, vjp = jax.vjp(_dn_gate_math, ob[:, sl], zb[:, sl], wb)
            a, b, c = vjp(dyb[:, sl])
            dos.append(a), dzs.append(b)
            dw = dw + c
        return jnp.concatenate(dos, axis=1), jnp.concatenate(dzs, axis=1), dw

    return rowk(name, fn, L, _pick(L, (256, 128, 64, 32, 16, 8)), 1,
                [(o, "rows"), (z, "rows"), (w, "const"), (dy, "rows")], [(hv, f32), (hv, bf16)], [(1, dv, f32)])


def ffn_mid(ua, uv, wa, wv, name):
    L, F = ua.shape
    ncol = F // _pick(F, (1408, 1024, 512, 256, 128))

    def fn(i, n, ab, ah, vb, vh, wab, wvb):
        ca, _ = _causal_conv(ab, ah, wab, i == 0)
        cv, _ = _causal_conv(vb, vh, wvb, i == 0)
        return (_silu(ca) * cv,)

    return rowk(name, fn, L, _pick(L, (256, 128, 64, 32, 16, 8)), ncol,
                [(ua, "rows"), (ua, "prev"), (uv, "rows"), (uv, "prev"), (wa, "cols"), (wv, "cols")], [(F, bf16)])[0]


def ffn_mid_bwd(ua, uv, wa, wv, dh, name):
    L, F = ua.shape
    kw = wa.shape[0]
    ncol = F // _pick(F, (1408, 1024, 512, 256, 128))

    def fn(i, n, ab, ap, an, vb, vp, vn, wab, wvb, dhb, dhn):
        tl = ab.shape[0]
        last = i == n - 1
        ext = lambda blk, nxt: jnp.concatenate([blk, jnp.where(last, 0.0, nxt)], axis=0)
        dhe = ext(dhb, dhn)
        ca, sa = _causal_conv(ext(ab, an), ap, wab, i == 0)
        cv, sv = _causal_conv(ext(vb, vn), vp, wvb, i == 0)
        dca = dhe * cv * _dsilu(ca)
        dcv = dhe * _silu(ca)
        dwa = _from_rows([jnp.sum(dca[:tl] * sa[j][:tl], axis=0, keepdims=True) for j in range(kw)])
        dwv = _from_rows([jnp.sum(dcv[:tl] * sv[j][:tl], axis=0, keepdims=True) for j in range(kw)])

        def conv_t_rows(dc, w):
            out = dc[:tl] * w[kw - 1:kw]
            for j in range(kw - 1):
                out = out + pltpu.roll(dc, tl + HALO - (kw - 1 - j), 0)[:tl] * w[j:j + 1]
            return out

        return conv_t_rows(dca, wab), conv_t_rows(dcv, wvb), dwa, dwv

    return rowk(name, fn, L, _pick(L, (256, 128, 64, 32, 16, 8)), ncol,
                [(ua, "rows"), (ua, "prev"), (ua, "next"), (uv, "rows"), (uv, "prev"), (uv, "next"),
                 (wa, "cols"), (wv, "cols"), (dh, "rows"), (dh, "next")],
                [(F, bf16), (F, bf16)], [(kw, F, f32), (kw, F, f32)])


def _merge_epi(acc, gs, gd, ga, gb):
    return acc, _sigmoid(gs) * ga * _sigmoid(gb) + _sigmoid(gd) * acc


def merge_bwd(dm, gs, gd, ga, gb, brdn, name):
    L, D = dm.shape
    ncol = D // _pick(D, (1024, 512, 256, 128))

    def fn(i, n, dmb, gsb, gdb, gab, gbb, brb):
        ss, sd, sb = _sigmoid(gsb), _sigmoid(gdb), _sigmoid(gbb)
        br_s5 = gab * sb
        dbr_s5 = dmb * ss
        return (dmb * br_s5 * ss * (1.0 - ss), dmb * brb * sd * (1.0 - sd), dbr_s5 * sb,
                dbr_s5 * gab * sb * (1.0 - sb), dmb * sd)

    return rowk(name, fn, L, _pick(L, (256, 128, 64, 32, 16, 8)), ncol,
                [(a, "rows") for a in (dm, gs, gd, ga, gb, brdn)], [(D, bf16)] * 5)


def s5_out_bwd(dy, ypre, u, d, name):
    L, W = dy.shape

    def fn(i, n, dyb, yb, ub, db):
        dyp = dyb * _dgelu(yb)
        return dyp, db * dyp, jnp.sum(dyp * ub, axis=0, keepdims=True)

    return rowk(name, fn, L, _pick(L, (256, 128, 64, 32, 16, 8)), 1,
                [(dy, "rows"), (ypre, "rows"), (u, "rows"), (d, "const")], [(W, bf16), (W, f32)], [(1, W, f32)])


def _s5_y_epi(acc, y1, u, d):
    ypre = acc + y1 + d * u
    return ypre, _gelu(ypre)


def adamw(w, g, m, v, name):
    R, C = w.shape
    tl = _pick(R, (256, 128, 64, 32, 16, 8))
    if R * C * 4 <= 2 * 1024 * 1024:
        tl = R

    def body(w_ref, g_ref, m_ref, v_ref, d_ref, nm_ref, nv_ref):
        gg = g_ref[...]
        nm = ADAM_B1 * m_ref[...] + (1.0 - ADAM_B1) * gg
        nv = ADAM_B2 * v_ref[...] + (1.0 - ADAM_B2) * (gg * gg)
        m_hat = nm / (1.0 - ADAM_B1 ** ADAM_STEP)
        v_hat = nv / (1.0 - ADAM_B2 ** ADAM_STEP)
        d_ref[...] = -ADAM_LR * (m_hat / (jnp.sqrt(v_hat) + ADAM_EPS) + ADAM_WD * w_ref[...])
        nm_ref[...] = nm
        nv_ref[...] = nv

    blk = pl.BlockSpec((tl, C), lambda i: (i, 0))
    return pl.pallas_call(body, name=name, grid=(R // tl,), in_specs=[blk] * 4, out_specs=[blk] * 3,
                          out_shape=[S((R, C), f32)] * 3, compiler_params=_cparams(("parallel",)))(w, g, m, v)


def sum_slots(x, name, out_dtype=f32):
    n, R, C = x.shape
    tl = _pick(R, (512, 256, 128, 64, 32, 16, 8))

    def body(x_ref, o_ref):
        acc = x_ref[0].astype(f32)
        for s in range(1, n):
            acc = acc + x_ref[s].astype(f32)
        o_ref[...] = acc.astype(o_ref.dtype)

    return pl.pallas_call(body, name=name, grid=(R // tl,),
                          in_specs=[pl.BlockSpec((n, tl, C), lambda i: (0, i, 0))],
                          out_specs=pl.BlockSpec((tl, C), lambda i: (i, 0)),
                          out_shape=S((R, C), out_dtype), compiler_params=_cparams(("parallel",)))(x)


_ANY = pl.BlockSpec(memory_space=pl.ANY)


def _coords():
    return lax.axis_index("x"), lax.axis_index("y"), lax.axis_index("c")


def chip_exchange(src, name, same=False):
    out_shape = (N_CHIPS,) + src.shape if same else src.shape
    assert out_shape[0] == N_CHIPS

    def body(src_ref, out_ref, send_sems, recv_sems, local_sem):
        x, y, c = _coords()
        me = 2 * x + y
        slot = (lambda j: src_ref) if same else (lambda j: src_ref.at[j])
        mine = pltpu.make_async_copy(slot(me), out_ref.at[me], local_sem)
        mine.start()
        peers = [(1 - x, y), (x, 1 - y), (1 - x, 1 - y)]
        copies = []
        for k, (px, py) in enumerate(peers):
            cp = pltpu.make_async_remote_copy(
                src_ref=slot(2 * px + py), dst_ref=out_ref.at[me],
                send_sem=send_sems.at[k], recv_sem=recv_sems.at[k],
                device_id=(px, py, c), device_id_type=MESH)
            cp.start()
            copies.append(cp)
        for k, (px, py) in enumerate(peers):
            pltpu.make_async_remote_copy(
                src_ref=slot(me), dst_ref=out_ref.at[2 * px + py],
                send_sem=send_sems.at[k], recv_sem=recv_sems.at[k],
                device_id=(px, py, c), device_id_type=MESH).wait_recv()
        for cp in copies:
            cp.wait_send()
        mine.wait()

    return pl.pallas_call(
        body, name=name, in_specs=[_ANY], out_specs=_ANY, out_shape=S(out_shape, src.dtype),
        scratch_shapes=[pltpu.SemaphoreType.DMA((3,)), pltpu.SemaphoreType.DMA((3,)), pltpu.SemaphoreType.DMA],
    )(src)


def sibling_exchange(src, name):
    def body(src_ref, out_ref, send_sem, recv_sem):
        x, y, c = _coords()
        cp = pltpu.make_async_remote_copy(src_ref=src_ref, dst_ref=out_ref, send_sem=send_sem, recv_sem=recv_sem,
                                          device_id=(x, y, 1 - c), device_id_type=MESH)
        cp.start()
        cp.wait()

    return pl.pallas_call(
        body, name=name, in_specs=[_ANY], out_specs=_ANY, out_shape=S(src.shape, src.dtype),
        scratch_shapes=[pltpu.SemaphoreType.DMA, pltpu.SemaphoreType.DMA],
    )(src)


def _rows(c, half_rows):
    return pl.ds(pl.multiple_of(c * half_rows, 16), half_rows)


def _slot(ref, kind, j, rows=None):
    if kind == "slot":
        return ref.at[j] if rows is None else ref.at[j, rows]
    cs = ref.shape[1] // N_CHIPS
    cols = pl.ds(pl.multiple_of(j * cs, 128), cs)
    return ref.at[:, cols] if rows is None else ref.at[rows, cols]


def _sems(n):
    return [pltpu.SemaphoreType.DMA((n,))]


class Payload:
    def __init__(self, arrays, out_shapes, aliases, sems, start, finish):
        self.arrays, self.out_shapes, self.aliases, self.sems = arrays, out_shapes, aliases, sems
        self.start, self.finish = start, finish


def run_payload(p, name):
    ni, no = len(p.arrays), len(p.out_shapes)

    def body(*refs):
        args = (refs[:ni], refs[ni:ni + no], refs[ni + no:])
        p.start(*args)
        p.finish(*args)

    return pl.pallas_call(body, name=name, in_specs=[_ANY] * ni, out_specs=[_ANY] * no, out_shape=p.out_shapes,
                          input_output_aliases=p.aliases, scratch_shapes=p.sems)(*p.arrays)


def _half_rows(bufs, kinds):
    return [(a.shape[1] if k == "slot" else a.shape[0]) // 2 for a, k in zip(bufs, kinds)]


def ag_ici_payload(bufs, kinds):
    n = len(bufs)
    rh = _half_rows(bufs, kinds)

    def copies(outs, sems):
        send, recv = sems
        x, y, c = _coords()
        me = 2 * x + y
        res = []
        for t in range(n):
            own = _slot(outs[t], kinds[t], me, _rows(c, rh[t]))
            for k, (px, py) in enumerate([(1 - x, y), (x, 1 - y), (1 - x, 1 - y)]):
                landed = _slot(outs[t], kinds[t], 2 * px + py, _rows(c, rh[t]))
                sem = dict(send_sem=send.at[3 * t + k], recv_sem=recv.at[3 * t + k], device_id=(px, py, c), device_id_type=MESH)
                res.append((pltpu.make_async_remote_copy(src_ref=own, dst_ref=own, **sem),
                            pltpu.make_async_remote_copy(src_ref=landed, dst_ref=landed, **sem)))
        return res

    def start(ins, outs, sems):
        for mine, _ in copies(outs, sems):
            mine.start()

    def finish(ins, outs, sems):
        both = copies(outs, sems)
        for _, theirs in both:
            theirs.wait_recv()
        for mine, _ in both:
            mine.wait_send()

    return Payload(bufs, [S(a.shape, a.dtype) for a in bufs], {t: t for t in range(n)}, _sems(3 * n) * 2, start, finish)


def ag_d2d(bufs, kinds, name):
    n = len(bufs)
    rh = _half_rows(bufs, kinds)

    def body(*refs):
        outs = refs[n:2 * n]
        send, recv = refs[2 * n:]
        x, y, c = _coords()
        sib = dict(device_id=(x, y, 1 - c), device_id_type=MESH)
        sends = []
        for t in range(n):
            for k, (px, py) in enumerate([(1 - x, y), (x, 1 - y), (1 - x, 1 - y)]):
                landed = _slot(outs[t], kinds[t], 2 * px + py, _rows(c, rh[t]))
                cp = pltpu.make_async_remote_copy(src_ref=landed, dst_ref=landed, send_sem=send.at[3 * t + k],
                                                  recv_sem=recv.at[3 * t + k], **sib)
                cp.start()
                sends.append(cp)
        for t in range(n):
            for k, (px, py) in enumerate([(1 - x, y), (x, 1 - y), (1 - x, 1 - y)]):
                other = _slot(outs[t], kinds[t], 2 * px + py, _rows(1 - c, rh[t]))
                pltpu.make_async_remote_copy(src_ref=other, dst_ref=other, send_sem=send.at[3 * t + k],
                                             recv_sem=recv.at[3 * t + k], **sib).wait_recv()
        for cp in sends:
            cp.wait_send()

    return pl.pallas_call(body, name=name, in_specs=[_ANY] * n, out_specs=[_ANY] * n,
                          out_shape=[S(a.shape, a.dtype) for a in bufs], input_output_aliases={t: t for t in range(n)},
                          scratch_shapes=_sems(3 * n) * 2)(*bufs)


def rs_pair_payload(grads, kinds):
    n = len(grads)

    def half_shape(a, k):
        return (a.shape[0], a.shape[1] // 2, a.shape[2]) if k == "slot" else (a.shape[0] // 2, a.shape[1])

    def half(ref, k, c):
        return ref.at[:, _rows(c, ref.shape[1] // 2)] if k == "slot" else ref.at[_rows(c, ref.shape[0] // 2)]

    shapes = [S(half_shape(a, k), a.dtype) for a, k in zip(grads, kinds)]

    def copies(srcs, got, sems):
        send, recv = sems
        x, y, c = _coords()
        return [pltpu.make_async_remote_copy(src_ref=half(srcs[t], kinds[t], 1 - c), dst_ref=got[t], send_sem=send.at[t],
                                             recv_sem=recv.at[t], device_id=(x, y, 1 - c), device_id_type=MESH)
                for t in range(n)]

    def start(ins, outs, sems):
        for cp in copies(ins, outs, sems):
            cp.start()

    def finish(ins, outs, sems):
        for cp in copies(ins, outs, sems):
            cp.wait()

    return Payload(grads, shapes, {}, _sems(n) * 2, start, finish)


def rs_chips_payload(sums, kinds):
    n = len(sums)

    def part_shape(a, k):
        return a.shape[1:] if k == "slot" else (a.shape[0], a.shape[1] // N_CHIPS)

    shapes = [S((N_CHIPS,) + part_shape(a, k), a.dtype) for a, k in zip(sums, kinds)]

    def copies(srcs, outs, sems):
        send, recv, lsem = sems
        x, y, c = _coords()
        me = 2 * x + y
        local, remote = [], []
        for t in range(n):
            local.append(pltpu.make_async_copy(_slot(srcs[t], kinds[t], me), outs[t].at[me], lsem.at[t]))
            for k, (px, py) in enumerate([(1 - x, y), (x, 1 - y), (1 - x, 1 - y)]):
                landed = outs[t].at[2 * px + py]
                sem = dict(send_sem=send.at[3 * t + k], recv_sem=recv.at[3 * t + k], device_id=(px, py, c), device_id_type=MESH)
                remote.append((pltpu.make_async_remote_copy(src_ref=_slot(srcs[t], kinds[t], 2 * px + py), dst_ref=outs[t].at[me], **sem),
                               pltpu.make_async_remote_copy(src_ref=landed, dst_ref=landed, **sem)))
        return local, remote

    def start(ins, outs, sems):
        local, remote = copies(ins, outs, sems)
        for cp in local:
            cp.start()
        for mine, _ in remote:
            mine.start()

    def finish(ins, outs, sems):
        local, remote = copies(ins, outs, sems)
        for _, theirs in remote:
            theirs.wait_recv()
        for mine, _ in remote:
            mine.wait_send()
        for cp in local:
            cp.wait()

    return Payload(sums, shapes, {}, _sems(3 * n) * 2 + _sems(n), start, finish)


def rs_share(gs, name):
    n = len(gs)

    def body(*refs):
        outs = refs[n:2 * n]
        send, recv = refs[2 * n:]
        x, y, c = _coords()
        started = []
        for t in range(n):
            cp = pltpu.make_async_remote_copy(src_ref=outs[t].at[c], dst_ref=outs[t].at[c], send_sem=send.at[t],
                                              recv_sem=recv.at[t], device_id=(x, y, 1 - c), device_id_type=MESH)
            cp.start()
            started.append(cp)
        for cp in started:
            cp.wait()

    return pl.pallas_call(body, name=name, in_specs=[_ANY] * n, out_specs=[_ANY] * n,
                          out_shape=[S(a.shape, a.dtype) for a in gs], input_output_aliases={t: t for t in range(n)},
                          scratch_shapes=_sems(n) * 2)(*gs)


def _core_index():
    return lax.axis_index("c").astype(jnp.int32).reshape(1)


def place_block(blk, kind, name):
    R, cs = blk.shape
    tl = _pick(R, (256, 128, 64, 32, 16, 8))

    def body(me_ref, b_ref, o_ref):
        o_ref[...] = b_ref[...].astype(o_ref.dtype)

    if kind == "slot":
        out_spec, out_shape = pl.BlockSpec((None, tl, cs), lambda i, me: (me[0], i, 0)), (N_CHIPS, R, cs)
    else:
        out_spec, out_shape = pl.BlockSpec((tl, cs), lambda i, me: (i, me[0])), (R, N_CHIPS * cs)
    me = (2 * lax.axis_index("x") + lax.axis_index("y")).astype(jnp.int32).reshape(1)
    return pl.pallas_call(
        body, name=name,
        grid_spec=pltpu.PrefetchScalarGridSpec(num_scalar_prefetch=1, grid=(R // tl,),
                                               in_specs=[pl.BlockSpec((tl, cs), lambda i, me: (i, 0))], out_specs=out_spec),
        out_shape=S(out_shape, bf16), compiler_params=_cparams(("parallel",)),
    )(me, blk)


def add_half(g, got, kind, name):
    g4 = g.reshape((g.shape[0] if kind == "slot" else 1, 2, -1, g.shape[-1]))
    parts, _, rh, C = g4.shape
    got3 = got.reshape(parts, rh, C)
    tl = _pick(rh, (256, 128, 64, 32, 16, 8))

    def body(c_ref, g_ref, got_ref, o_ref):
        o_ref[...] = (g_ref[...] + got_ref[...]).astype(o_ref.dtype)

    blk = pl.BlockSpec((None, tl, C), lambda s, i, c: (s, i, 0))
    out = pl.pallas_call(
        body, name=name,
        grid_spec=pltpu.PrefetchScalarGridSpec(
            num_scalar_prefetch=1, grid=(parts, rh // tl),
            in_specs=[pl.BlockSpec((None, None, tl, C), lambda s, i, c: (s, c[0], i, 0)), blk], out_specs=blk),
        out_shape=S((parts, rh, C), bf16), compiler_params=_cparams(("parallel", "parallel")),
    )(_core_index(), g4, got3)
    return out.reshape(got.shape)


def sum_slots_half(x, name):
    n, rh, C = x.shape
    tl = _pick(rh, (512, 256, 128, 64, 32, 16, 8))

    def body(c_ref, x_ref, o_ref):
        acc = x_ref[0].astype(f32)
        for s in range(1, n):
            acc = acc + x_ref[s].astype(f32)
        o_ref[...] = acc

    return pl.pallas_call(
        body, name=name,
        grid_spec=pltpu.PrefetchScalarGridSpec(
            num_scalar_prefetch=1, grid=(rh // tl,),
            in_specs=[pl.BlockSpec((n, tl, C), lambda i, c: (0, i, 0))],
            out_specs=pl.BlockSpec((None, tl, C), lambda i, c: (c[0], i, 0))),
        out_shape=S((2, rh, C), f32), compiler_params=_cparams(("parallel",)),
    )(_core_index(), x)


def _block_diag(blocks):
    G, a, b = blocks.shape
    eye = jnp.eye(G, dtype=blocks.dtype)
    return (blocks[:, :, None, :] * eye[:, None, :, None]).reshape(G * a, G * b)


def _flat_pack(arrs, dtype, lanes=1024, row_mult=16):
    flat = jnp.concatenate([a.reshape(-1).astype(dtype) for a in arrs])
    n = flat.shape[0]
    per = lanes * row_mult
    pad = (-n) % per
    if pad:
        flat = jnp.concatenate([flat, jnp.zeros((pad,), dtype)])
    return flat.reshape(-1, lanes)


def _flat_unpack(buf, shapes):
    flat = buf.reshape(-1)
    out, off = [], 0
    for s in shapes:
        n = math.prod(s)
        out.append(flat[off:off + n].reshape(s))
        off += n
    return out


def _layer_weights(p, H):
    w_in = p["w_in"]
    D = w_in.shape[0]
    s5w = p["s5_d"].shape[-1]
    hk = p["dn_proj_w"].shape[0]
    off_z = s5w + 4 * hk
    off_a = off_z + 2 * H
    wp = jnp.concatenate([w_in[:, :off_z], w_in[:, off_a:], jnp.pad(w_in[:, off_z:off_a], ((0, 0), (0, 128 - 2 * H)))], axis=1)
    q = {}
    q["Wu"] = Win(wp, 0, s5w)
    q["Wqkv"] = Win(wp, s5w, 3 * hk)
    q["Wz"] = Win(wp, s5w + 3 * hk, hk)
    q["Wgs"] = Win(wp, off_z, D)
    q["Wgd"] = Win(wp, off_z + D, D)
    q["Wba"] = Win(wp, off_z + 2 * D, 128)
    q["Wga"], q["Wgb"] = Win(p["s5_glu_w"], 0, D), Win(p["s5_glu_w"], D, D)
    F = p["ffn_down"].shape[0]
    q["Wupa"], q["Wupv"] = Win(p["ffn_up"], 0, F), Win(p["ffn_up"], F, F)
    q["cwa"], q["cwv"] = p["ffn_conv_w"][:, :F], p["ffn_conv_w"][:, F:]
    for k in ("dn_proj_w", "w_out", "ffn_down", "dn_conv_w", "mix_norm_w", "ffn_norm_w", "dn_norm_w",
              "dn_a_log", "dn_dt_bias", "s5_d"):
        q[k] = p[k]
    return q


def _s5_params(p, tag):
    G, P = p["s5_a_re"].shape
    HG = p["s5_b_re"].shape[-1]
    col = lambda a: a.reshape(G * P, 1)
    lr, li = col(p["s5_a_re"]), col(p["s5_a_im"])
    logdt = col(jnp.broadcast_to(p["s5_log_dt"][:, None], (G, P)))
    br, bi = p["s5_b_re"].reshape(G * P, HG), p["s5_b_im"].reshape(G * P, HG)
    ar, ai, bbr, bbi = s5_disc(lr, li, logdt, br, bi, "s5_disc")
    bd = lambda m: _block_diag(m.reshape(G, P, HG).transpose(0, 2, 1)).astype(bf16)
    cd = lambda m: _block_diag(m.transpose(0, 2, 1)).astype(bf16)
    return dict(lr=lr, li=li, logdt=logdt, br=br, bi=bi, ar=ar.reshape(1, G * P), ai=ai.reshape(1, G * P),
                Bre=bd(bbr), Bim=bd(bbi), CreT=cd(p["s5_c_re"]), mCimT=cd(-p["s5_c_im"]), G=G, P=P, HG=HG)


def layer_fwd(x, q, s5, H, payloads=(None, None, None)):
    r = {"x": x}
    h1 = rms_fwd(x, q["mix_norm_w"], "rms_mix")
    r["h1"] = h1
    u32, u16 = mm(h1, q["Wu"], "nn", "proj_u", out_dtypes=(f32, bf16), epi=lambda a: (a, a))
    qkv = mm(h1, q["Wqkv"], "nn", "proj_qkv")
    z = mm(h1, q["Wz"], "nn", "proj_z")
    ba = mm(h1, q["Wba"], "nn", "proj_ba")
    gs = mm(h1, q["Wgs"], "nn", "proj_gs")
    gd = mm(h1, q["Wgd"], "nn", "proj_gd")
    r.update(u32=u32, u16=u16, qkv=qkv, z=z, ba=ba, gs=gs, gd=gd)
    HG, P = s5["HG"], s5["P"]
    bur = mm_bd(u16, s5["Bre"], "nn", "s5_bu_re", HG, P)
    bui = mm_bd(u16, s5["Bim"], "nn", "s5_bu_im", HG, P)
    xr, xi = s5_scan_fwd(bur, bui, s5["ar"], s5["ai"], "s5_scan_fwd")
    y1 = mm_bd(xr, s5["CreT"], "nn", "s5_y_re", P, HG)
    ypre, ys5 = mm_bd(xi, s5["mCimT"], "nn", "s5_y_im", P, HG, extras=(y1, u32, q["s5_d"]), epi=_s5_y_epi, out_dtypes=(f32, bf16))
    ga = mm(ys5, q["Wga"], "nn", "glu_a")
    gb = mm(ys5, q["Wgb"], "nn", "glu_b")
    r.update(xr=xr, xi=xi, ypre=ypre, ys5=ys5, ga=ga, gb=gb)
    qn, kn, vv, bg = dn_prep(qkv, q["dn_conv_w"], ba, q["dn_a_log"], q["dn_dt_bias"], H, "dn_prep")
    L = x.shape[0]
    bgt = bg.reshape(L // DN_CHUNK, DN_CHUNK, 2 * H).transpose(0, 2, 1)
    o, ss, carried = dn_chunk_fwd(qn, kn, vv, bg, bgt, H, "dn_chunk_fwd", payloads[0])
    ydn = dn_gate(o, z, q["dn_norm_w"], H, "dn_gate")
    brdn, merged = mm(ydn, q["dn_proj_w"], "nn", "dn_proj", extras=(gs, gd, ga, gb), epi=_merge_epi, out_dtypes=(f32, bf16))
    r.update(qn=qn, kn=kn, vv=vv, bg=bg, bgt=bgt, ss=ss, o=o, ydn=ydn, brdn=brdn, merged=merged)
    x1 = mm(merged, q["w_out"], "nn", "out_proj", extras=(x,), epi=_add)
    h2 = rms_fwd(x1, q["ffn_norm_w"], "rms_ffn")
    ua = mm(h2, q["Wupa"], "nn", "ffn_up_a", payload=payloads[1])
    uv = mm(h2, q["Wupv"], "nn", "ffn_up_v", payload=payloads[2])
    if payloads[1] is not None:
        (ua, got_a), (uv, got_v) = ua, uv
        carried = [carried, got_a, got_v]
    hmid = ffn_mid(ua, uv, q["cwa"], q["cwv"], "ffn_mid")
    x2 = mm(hmid, q["ffn_down"], "nn", "ffn_down", extras=(x1,), epi=_add)
    r.update(x1=x1, h2=h2, ua=ua, uv=uv, hmid=hmid)
    return x2, r, carried


def layer_bwd(dx2, dx2b, r, q, s5, H, reduction=None):
    g = {}
    payload = None
    if reduction is None:
        dhmid = mm(dx2b, q["ffn_down"], "nt", "d_hmid")
    else:
        dhmid, theirs = mm(dx2b, q["ffn_down"], "nt", "d_hmid", payload=reduction.pair)
        payload = reduction.chips(theirs)
    g["ffn_down"] = mm(r["hmid"], dx2b, "tn", "dw_ffn_down")
    dua, duv, dwa, dwv = ffn_mid_bwd(r["ua"], r["uv"], q["cwa"], q["cwv"], dhmid, "ffn_mid_bwd")
    g["ffn_conv_w"] = jnp.concatenate([dwa, dwv], axis=1)
    dh2 = mm(dua, q["Wupa"], "nt", "d_h2_a")
    dh2 = mm(duv, q["Wupv"], "nt", "d_h2_v", extras=(dh2,), epi=_add)
    F = dua.shape[1]
    dwup = mm(r["h2"], dua, "tn", "dw_up_a", out_into=(2 * F, 0, None))
    g["ffn_up"] = mm(r["h2"], duv, "tn", "dw_up_v", out_into=(2 * F, F, dwup))
    dx1, dx1b, dffn_w = rms_bwd(r["x1"], q["ffn_norm_w"], dh2, dx2, "rms_ffn_bwd")
    g["ffn_norm_w"] = dffn_w[0]
    dm = mm(dx1b, q["w_out"], "nt", "d_merged")
    g["w_out"] = mm(r["merged"], dx1b, "tn", "dw_out")
    dgs, dgd, dga, dgb, dbrdn = merge_bwd(dm, r["gs"], r["gd"], r["ga"], r["gb"], r["brdn"], "merge_bwd")
    dydn = mm(dbrdn, q["dn_proj_w"], "nt", "d_ydn")
    g["dn_proj_w"] = mm(r["ydn"], dbrdn, "tn", "dw_dn_proj")
    do, dz, dnw = dn_gate_bwd(r["o"], r["z"], q["dn_norm_w"], dydn, H, "dn_gate_bwd")
    g["dn_norm_w"] = dnw[0]
    dq, dk, dv, dbg, dgt, carried = dn_chunk_bwd(r["qn"], r["kn"], r["vv"], r["bg"], r["bgt"], r["ss"], do, H,
                                                 "dn_chunk_bwd", payload)
    L = dq.shape[0]
    dbg = dbg + jnp.concatenate([jnp.zeros((L, H), f32), dgt.transpose(0, 2, 1).reshape(L, H)], axis=1)
    dc, dba, dcw, dal, ddt = dn_prep_bwd(r["qkv"], q["dn_conv_w"], r["ba"], q["dn_a_log"], q["dn_dt_bias"],
                                         dq, dk, dv, dbg, H, "dn_prep_bwd")
    g["dn_conv_w"], g["dn_a_log"], g["dn_dt_bias"] = dcw, dal[0], ddt[0]
    dqkv = conv_t(dc, q["dn_conv_w"], "dn_conv_t")
    dys5 = mm(dga, q["Wga"], "nt", "d_ys5_a")
    dys5 = mm(dgb, q["Wgb"], "nt", "d_ys5_b", extras=(dys5,), epi=_add)
    D = dga.shape[1]
    dwglu = mm(r["ys5"], dga, "tn", "dw_glu_a", out_into=(2 * D, 0, None))
    g["s5_glu_w"] = mm(r["ys5"], dgb, "tn", "dw_glu_b", out_into=(2 * D, D, dwglu))
    dyp, du_direct, dd = s5_out_bwd(dys5, r["ypre"], r["u32"], q["s5_d"], "s5_out_bwd")
    g["s5_d"] = dd[0]
    G, P, HG = s5["G"], s5["P"], s5["HG"]
    gdr = mm_bd(dyp, s5["CreT"], "nt", "s5_gx_re", P, HG)
    gdi = mm_bd(dyp, s5["mCimT"], "nt", "s5_gx_im", P, HG)
    dcre = _diag_blocks(mm_bd(r["xr"], dyp, "tn", "dw_s5_c_re", P, HG), G, P, HG)
    dcim = _diag_blocks(mm_bd(r["xi"], dyp, "tn", "dw_s5_c_im", P, HG), G, P, HG)
    g["s5_c_re"], g["s5_c_im"] = dcre.transpose(0, 2, 1), -dcim.transpose(0, 2, 1)
    gxr, gxi, dar, dai = s5_scan_bwd(gdr, gdi, r["xr"], r["xi"], s5["ar"], s5["ai"], "s5_scan_bwd")
    dbre = _diag_blocks(mm_bd(r["u16"], gxr, "tn", "dw_s5_b_re", HG, P), G, HG, P)
    dbim = _diag_blocks(mm_bd(r["u16"], gxi, "tn", "dw_s5_b_im", HG, P), G, HG, P)
    tocol = lambda m: m.transpose(0, 2, 1).reshape(G * P, HG)
    dlr, dli, dlogdt, dbr, dbi = s5_disc_bwd(s5["lr"], s5["li"], s5["logdt"], s5["br"], s5["bi"],
                                             dar.reshape(G * P, 1), dai.reshape(G * P, 1), tocol(dbre), tocol(dbim), "s5_disc_bwd")
    g["s5_a_re"], g["s5_a_im"] = dlr.reshape(G, P), dli.reshape(G, P)
    g["s5_log_dt"] = jnp.sum(dlogdt.reshape(G, P), axis=1)
    g["s5_b_re"], g["s5_b_im"] = dbr.reshape(G, P, HG), dbi.reshape(G, P, HG)
    du = mm_bd(gxr, s5["Bre"], "nt", "d_u_re", HG, P)
    du = mm_bd(gxi, s5["Bim"], "nt", "d_u_im", HG, P, extras=(du, du_direct), epi=lambda a, b, c: (a + b + c,), out_dtypes=(bf16,))
    h1 = r["h1"]
    segs = [("Wu", du), ("Wqkv", dqkv), ("Wz", dz), ("Wba", dba), ("Wgs", dgs), ("Wgd", dgd)]
    dh1 = None
    dws = []
    for name, dseg in segs:
        if dh1 is None:
            dh1 = mm(dseg, q[name], "nt", "d_h1_" + name)
        else:
            dh1 = mm(dseg, q[name], "nt", "d_h1_" + name, extras=(dh1,), epi=_add)
        dw = mm(h1, dseg, "tn", "dw_in_" + name)
        dws.append(dw[:, :2 * H] if name == "Wba" else dw)
    g["w_in"] = jnp.concatenate(dws, axis=1)
    dx, dxb, dmix = rms_bwd(r["x"], q["mix_norm_w"], dh1, dx1, "rms_mix_bwd")
    g["mix_norm_w"] = dmix[0]
    return dx, dxb, g, carried


BIG = ("w_in", "s5_glu_w", "dn_proj_w", "w_out", "ffn_up", "ffn_down")
SHARDED_SMALL = ("dn_conv_w", "ffn_conv_w")
COL_SHARDED = ("w_in", "s5_glu_w", "dn_proj_w", "ffn_up", "dn_conv_w", "ffn_conv_w")
REPL = ("mix_norm_w", "s5_log_dt", "s5_a_re", "s5_a_im", "s5_b_re", "s5_b_im", "s5_c_re", "s5_c_im", "s5_d",
        "dn_a_log", "dn_dt_bias", "dn_norm_w", "ffn_norm_w")
WEIGHTS = ['mix_norm_w', 'w_in', 's5_log_dt', 's5_a_re', 's5_a_im', 's5_b_re', 's5_b_im', 's5_c_re', 's5_c_im', 's5_d',
           's5_glu_w', 'dn_conv_w', 'dn_a_log', 'dn_dt_bias', 'dn_norm_w', 'dn_proj_w', 'w_out', 'ffn_norm_w', 'ffn_up',
           'ffn_conv_w', 'ffn_down', 'final_norm_w']


def _join_shards(name, shards):
    return jnp.concatenate(shards, axis=-1 if name in COL_SHARDED else -2)


KINDS = {"w_in": "slot", "s5_glu_w": "col", "dn_proj_w": "col", "w_out": "slot", "ffn_up": "col", "ffn_down": "slot"}


BIG_KINDS = [KINDS[n] for n in BIG]


AG_GROUPS = (("w_in", "s5_glu_w", "dn_proj_w", "w_out"), ("ffn_up",), ("ffn_down",))


def gather_begin(shards, names):
    return ag_ici_payload([place_block(shards[n], KINDS[n], "ag_place_" + n) for n in names], [KINDS[n] for n in names])


def gather_finish(bufs):
    full = dict(zip(BIG, ag_d2d([bufs[n] for n in BIG], BIG_KINDS, "ag_d2d")))
    for n in ("w_out", "ffn_down"):
        full[n] = full[n].reshape(-1, full[n].shape[-1])
    w = full["w_in"]
    full["w_in"] = w.transpose(1, 0, 2).reshape(w.shape[1], -1)
    return full


def gather_small_sharded(shards):
    names = SHARDED_SMALL
    shapes = [shards[n].shape for n in names]
    got = chip_exchange(_flat_pack([shards[n] for n in names], f32, lanes=128, row_mult=8), "ag_small", same=True)
    per_chip = [_flat_unpack(got[j], shapes) for j in range(N_CHIPS)]
    return {n: _join_shards(n, [per_chip[j][k] for j in range(N_CHIPS)]) for k, n in enumerate(names)}


class Reduction:
    def __init__(self, tens):
        self.tens = tens
        self.pair = rs_pair_payload(tens, BIG_KINDS)

    def chips(self, theirs):
        sums = [add_half(a, t, k, "rs_pair_sum_" + n) for n, a, t, k in zip(BIG, self.tens, theirs, BIG_KINDS)]
        return rs_chips_payload(sums, BIG_KINDS)


def reduce_begin(g):
    tens = []
    for n in BIG:
        a = g[n]
        if n == "w_in":
            a = a.reshape(a.shape[0], N_CHIPS, -1).transpose(1, 0, 2)
        elif KINDS[n] == "slot":
            a = a.reshape(N_CHIPS, -1, a.shape[-1])
        tens.append(a)
    return Reduction(tens)


def reduce_finish(parts):
    halves = [sum_slots_half(x, "rs_chip_sum_" + n) for n, x in zip(BIG, parts)]
    return {n: a.reshape(-1, a.shape[-1]) for n, a in zip(BIG, rs_share(halves, "rs_share"))}


def all_reduce_small(arrs):
    shapes = [a.shape for a in arrs]
    pack = _flat_pack(arrs, f32, lanes=1024, row_mult=64)
    from_chips = chip_exchange(pack, "ar_chips", same=True)
    from_sib = sibling_exchange(from_chips, "ar_sibling")
    c = lax.axis_index("c")
    both = jnp.concatenate([jnp.where(c == 0, from_chips, from_sib), jnp.where(c == 0, from_sib, from_chips)], axis=0)
    return _flat_unpack(sum_slots(both, "ar_sum"), shapes)


def kernel(x, mix_norm_w, w_in, s5_log_dt, s5_a_re, s5_a_im, s5_b_re, s5_b_im, s5_c_re, s5_c_im, s5_d, s5_glu_w, dn_conv_w, dn_a_log, dn_dt_bias, dn_norm_w, dn_proj_w, w_out, ffn_norm_w, ffn_up, ffn_conv_w, ffn_down, final_norm_w, loss_target, m_mix_norm_w, m_w_in, m_s5_log_dt, m_s5_a_re, m_s5_a_im, m_s5_b_re, m_s5_b_im, m_s5_c_re, m_s5_c_im, m_s5_d, m_s5_glu_w, m_dn_conv_w, m_dn_a_log, m_dn_dt_bias, m_dn_norm_w, m_dn_proj_w, m_w_out, m_ffn_norm_w, m_ffn_up, m_ffn_conv_w, m_ffn_down, m_final_norm_w, v_mix_norm_w, v_w_in, v_s5_log_dt, v_s5_a_re, v_s5_a_im, v_s5_b_re, v_s5_b_im, v_s5_c_re, v_s5_c_im, v_s5_d, v_s5_glu_w, v_dn_conv_w, v_dn_a_log, v_dn_dt_bias, v_dn_norm_w, v_dn_proj_w, v_w_out, v_ffn_norm_w, v_ffn_up, v_ffn_conv_w, v_ffn_down, v_final_norm_w):
    args = locals()
    W = {n: args[n] for n in WEIGHTS}
    M = {n: args["m_" + n] for n in WEIGHTS}
    V = {n: args["v_" + n] for n in WEIGHTS}
    depth = mix_norm_w.shape[0]
    H = dn_a_log.shape[1]
    xs = x[0]
    target = loss_target[0]

    conv_full = gather_small_sharded({n: W[n] for n in SHARDED_SMALL})

    def layer_params(l, gathered):
        p = gather_finish(gathered)
        for n in REPL:
            p[n] = W[n][l]
        for n in SHARDED_SMALL:
            p[n] = conv_full[n][l]
        for n in ("mix_norm_w", "ffn_norm_w", "dn_norm_w", "dn_a_log", "dn_dt_bias", "s5_d"):
            p[n] = p[n].reshape(1, -1)
        return _layer_weights(p, H), _s5_params(p, l)

    def named(groups_results):
        return {n: buf for names, bufs in zip(AG_GROUPS, groups_results) for n, buf in zip(names, bufs)}

    layers, res = [], []
    begun = [gather_begin({n: W[n][0] for n in BIG}, names) for names in AG_GROUPS]
    gathered = [run_payload(p, "ag_ici_%d" % k) for k, p in enumerate(begun)]
    for l in range(depth):
        layers.append(layer_params(l, named(gathered)))
        nxt = (None, None, None)
        if l + 1 < depth:
            nxt = [gather_begin({n: W[n][l + 1] for n in BIG}, names) for names in AG_GROUPS]
        xs, r, gathered = layer_fwd(xs, layers[l][0], layers[l][1], H, nxt)
        res.append(r)
    dx, dxb, loss_part, dfinal = loss_head(xs, final_norm_w.reshape(1, -1), target, "loss_head")
    loss = lax.psum(loss_part[0, 0], ("x", "y", "c"))

    grads, sharded = [None] * depth, [None] * depth
    pending = None
    for l in reversed(range(depth)):
        dx, dxb, grads[l], parts = layer_bwd(dx, dxb, res[l], layers[l][0], layers[l][1], H, pending)
        if pending is not None:
            sharded[l + 1] = reduce_finish(parts)
        pending = reduce_begin(grads[l])
    theirs = run_payload(pending.pair, "rs_pair")
    sharded[0] = reduce_finish(run_payload(pending.chips(theirs), "rs_chips"))
    grad_x = dx[None]

    G = {}
    for n in BIG:
        G[n] = jnp.stack([sharded[l][n] for l in range(depth)])
    small_names = REPL + SHARDED_SMALL
    small = [jnp.stack([grads[l][n] for l in range(depth)]) for n in small_names] + [dfinal[0]]
    for n, a in zip(small_names + ("final_norm_w",), all_reduce_small(small)):
        G[n] = a
    chip = 2 * lax.axis_index("x") + lax.axis_index("y")
    for n in SHARDED_SMALL:
        cs = W[n].shape[-1]
        G[n] = lax.dynamic_slice_in_dim(G[n], chip * cs, cs, axis=-1)

    delta, new_m, new_v = {}, {}, {}
    for n in WEIGHTS:
        shape = W[n].shape
        size = math.prod(shape)
        if n in BIG + SHARDED_SMALL:
            two_d = (size // shape[-1], shape[-1])
        else:
            two_d = (size // 128, 128) if size % 128 == 0 else (1, size)
        d, nm, nv = adamw(W[n].reshape(two_d), G[n].reshape(two_d), M[n].reshape(two_d), V[n].reshape(two_d), "adamw_" + n)
        delta[n], new_m[n], new_v[n] = d.reshape(shape), nm.reshape(shape), nv.reshape(shape)
        G[n] = G[n].reshape(shape)
    return (loss, grad_x, *[G[n] for n in WEIGHTS], *[delta[n] for n in WEIGHTS],
            *[new_m[n] for n in WEIGHTS], *[new_v[n] for n in WEIGHTS])
```

```python
import functools
import math

import jax
import jax.numpy as jnp
from jax import lax
from jax.experimental import pallas as pl
from jax.experimental.pallas import tpu as pltpu

f32 = jnp.float32
bf16 = jnp.bfloat16
S = jax.ShapeDtypeStruct

NORM_EPS = 1e-6
DN_CHUNK = 64
S5_GROUP = 16
ADAM_LR, ADAM_B1, ADAM_B2, ADAM_EPS, ADAM_WD, ADAM_STEP = 0.001, 0.9, 0.999, 1e-08, 0.01, 10
VMEM_LIMIT_BYTES = 56 * 1024 * 1024
HALO = 8
MESH = pl.DeviceIdType.MESH
N_CHIPS = 4


def _pick(n, cands):
    for c in cands:
        if n % c == 0:
            return c
    return n


MM_VMEM_BUDGET = 40 * 1024 * 1024
MM_MAX_TK = 2816


def _pick_k(K):
    if K <= MM_MAX_TK or K % 128:
        return K
    return max(d for d in range(128, MM_MAX_TK + 1, 128) if K % d == 0)


MM_MAX_TN = 1536


def _pick_n(N):
    if N <= 1024 or N % 128:
        return N
    return max(d for d in range(128, MM_MAX_TN + 1, 128) if N % d == 0)


def _cparams(sem):
    return pltpu.CompilerParams(dimension_semantics=sem, vmem_limit_bytes=VMEM_LIMIT_BYTES)


_DIMS = {"nn": ((1,), (0,)), "nt": ((1,), (1,)), "tn": ((0,), (0,))}


class Win:
    def __init__(self, arr, c0, nc):
        self.arr, self.c0, self.nc = arr, c0, nc


def mm(a, b, mode, name, extras=(), epi=None, out_dtypes=(f32,), out_into=None, payload=None):
    barr, c0 = (b.arr, b.c0) if isinstance(b, Win) else (b, 0)
    if mode == "tn":
        K, M = a.shape
    else:
        M, K = a.shape
    if mode == "nt":
        N, K2 = barr.shape
        K2 = b.nc if isinstance(b, Win) else K2
        n_off, k_off = 0, c0
    else:
        K2, N = barr.shape
        N = b.nc if isinstance(b, Win) else N
        n_off, k_off = c0, 0
    assert K == K2, (a.shape, barr.shape, mode)
    o_tot, o_off, o_alias = out_into if out_into is not None else (N, 0, None)
    tm = _pick(M, (1024, 512, 256, 128, 64, 32, 16, 8))
    tn = _pick_n(math.gcd(math.gcd(N, n_off), o_off))
    tk = _pick_k(math.gcd(K, k_off))

    def vmem_estimate(tm_):
        tiles = tm_ * tk * a.dtype.itemsize + tk * tn * barr.dtype.itemsize
        tiles += sum(tm_ * tn * e.dtype.itemsize for e in extras if e.shape == (M, N))
        tiles += sum(tm_ * tn * jnp.dtype(dt).itemsize for dt in out_dtypes)
        return 2 * tiles + 3 * tm_ * tn * 4

    while vmem_estimate(tm) > MM_VMEM_BUDGET:
        smaller = [d for d in range(128, tm, 128) if M % d == 0]
        if not smaller:
            break
        tm = smaller[-1]
    nk = K // tk
    assert M % tm == 0 and N % tn == 0 and K % tk == 0 and n_off % tn == 0 and k_off % tk == 0 and o_off % tn == 0
    nb, kb, ob = n_off // tn, k_off // tk, o_off // tn
    if mode == "tn":
        a_spec = pl.BlockSpec((tk, tm), lambda j, i, k: (k, i))
    else:
        a_spec = pl.BlockSpec((tm, tk), lambda j, i, k: (i, k))
    if mode == "nt":
        b_spec = pl.BlockSpec((tn, tk), lambda j, i, k: (j, k + kb))
    else:
        b_spec = pl.BlockSpec((tk, tn), lambda j, i, k: (k, j + nb))
    ex_specs = []
    for e in extras:
        if e.shape == (M, N):
            ex_specs.append(pl.BlockSpec((tm, tn), lambda j, i, k: (i, j)))
        elif e.shape == (1, N):
            ex_specs.append(pl.BlockSpec((1, tn), lambda j, i, k: (0, j)))
        elif e.shape == (M, 1):
            ex_specs.append(pl.BlockSpec((tm, 1), lambda j, i, k: (i, 0)))
        else:
            raise ValueError((e.shape, M, N))
    ne, no = len(extras), len(out_dtypes)
    na = 1 if o_alias is not None else 0
    assert out_into is None or no == 1
    dims = (_DIMS[mode], ((), ()))
    nj, ni = N // tn, M // tm
    split, hooks, extra = _host(payload, 2 + ne + na, no, 1 if nk > 1 else 0)

    def body(*refs):
        own, theirs = split(refs)
        a_ref, b_ref, rest = own[0], own[1], own[2:]
        ex, outs = rest[:ne], rest[ne + na:ne + na + no]
        j, i, k = pl.program_id(0), pl.program_id(1), pl.program_id(2)
        hooks(theirs, (j == 0) & (i == 0) & (k == 0), (j == nj - 1) & (i == ni - 1) & (k == nk - 1))
        p = lax.dot_general(a_ref[...].astype(bf16), b_ref[...].astype(bf16), dims, preferred_element_type=f32)

        def finish(acc):
            res = epi(acc, *[e[...] for e in ex]) if epi is not None else (acc,)
            for o, r in zip(outs, res):
                o[...] = r.astype(o.dtype)

        if nk == 1:
            finish(p)
        else:
            acc_ref = rest[-1]

            @pl.when(k == 0)
            def _():
                acc_ref[...] = p

            @pl.when(k > 0)
            def _():
                acc_ref[...] += p

            @pl.when(k == nk - 1)
            def _():
                finish(acc_ref[...])

    aliases = dict(extra["aliases"])
    if na:
        aliases[2 + ne] = 0
    res = pl.pallas_call(
        body,
        name=name,
        grid=(nj, ni, nk),
        in_specs=[a_spec, b_spec] + ex_specs + [pl.BlockSpec(memory_space=pl.ANY)] * na + extra["in_specs"],
        out_specs=[pl.BlockSpec((tm, tn), lambda j, i, k: (i, j + ob)) for _ in out_dtypes] + extra["out_specs"],
        out_shape=[S((M, o_tot), dt) for dt in out_dtypes] + extra["out_shape"],
        scratch_shapes=([pltpu.VMEM((tm, tn), f32)] if nk > 1 else []) + extra["scratch"],
        input_output_aliases=aliases,
        compiler_params=_cparams(("arbitrary",) * 3 if payload is not None else ("parallel", "parallel", "arbitrary")),
    )(a, barr, *extras, *([o_alias] if na else []), *extra["arrays"])
    outs = res[:no]
    if payload is not None:
        return tuple(outs) + (list(res[no:]),)
    return outs[0] if no == 1 else tuple(outs)


def mm_bd(a, b, mode, name, ga, gb, extras=(), epi=None, out_dtypes=(f32,)):
    T = max(1, min(256 // min(ga, gb), 1024 // max(ga, gb)))
    if mode == "tn":
        K, M = a.shape
        N = b.shape[1]
        G = M // ga
        T = min(T, G)
        tm, tn, tk = T * ga, T * gb, _pick_k(K)
        nk = K // tk
        grid = (G // T, 1, nk)
        a_spec = pl.BlockSpec((tk, tm), lambda j, i, k: (k, j))
        b_spec = pl.BlockSpec((tk, tn), lambda j, i, k: (k, j))
        o_spec = pl.BlockSpec((tm, tn), lambda j, i, k: (j, 0))
        out_shape = (M, tn)
    else:
        M = a.shape[0]
        if mode == "nn":
            G = b.shape[0] // ga
            T = min(T, G)
            kw, tn, N = T * ga, T * gb, G * gb
            b_spec = pl.BlockSpec((kw, tn), lambda j, i, k: (j, j))
        else:
            G = b.shape[0] // ga
            T = min(T, G)
            kw, tn, N = T * gb, T * ga, G * ga
            b_spec = pl.BlockSpec((tn, kw), lambda j, i, k: (j, j))
        tm = _pick(M, (1024, 512, 256, 128, 64, 32, 16, 8))
        nk = 1
        grid = (G // T, M // tm, 1)
        a_spec = pl.BlockSpec((tm, kw), lambda j, i, k: (i, j))
        o_spec = pl.BlockSpec((tm, tn), lambda j, i, k: (i, j))
        out_shape = (M, N)
    ex_specs = []
    for e in extras:
        if e.shape == out_shape:
            ex_specs.append(o_spec)
        elif e.shape == (1, out_shape[1]):
            ex_specs.append(pl.BlockSpec((1, tn), lambda j, i, k: (0, j)))
        else:
            raise ValueError((e.shape, out_shape))
    ne, no = len(extras), len(out_dtypes)
    dims = (_DIMS[mode], ((), ()))

    def body(a_ref, b_ref, *rest):
        ex, outs = rest[:ne], rest[ne:ne + no]
        p = lax.dot_general(a_ref[...].astype(bf16), b_ref[...].astype(bf16), dims, preferred_element_type=f32)

        def finish(acc):
            res = epi(acc, *[e[...] for e in ex]) if epi is not None else (acc,)
            for o, r in zip(outs, res):
                o[...] = r.astype(o.dtype)

        if nk == 1:
            finish(p)
        else:
            acc_ref = rest[-1]
            k = pl.program_id(2)

            @pl.when(k == 0)
            def _():
                acc_ref[...] = p

            @pl.when(k > 0)
            def _():
                acc_ref[...] += p

            @pl.when(k == nk - 1)
            def _():
                finish(acc_ref[...])

    outs = pl.pallas_call(
        body, name=name, grid=grid, in_specs=[a_spec, b_spec] + ex_specs, out_specs=[o_spec] * no,
        out_shape=[S(out_shape, dt) for dt in out_dtypes],
        scratch_shapes=[pltpu.VMEM((tm, tn), f32)] if nk > 1 else [],
        compiler_params=_cparams(("parallel", "parallel", "arbitrary")),
    )(a, b, *extras)
    return outs[0] if no == 1 else tuple(outs)


def _diag_blocks(tiles, G, ga, gb):
    T = tiles.shape[1] // gb
    t5 = tiles.reshape(G // T, T, ga, T, gb)
    return jnp.sum(t5 * jnp.eye(T, dtype=tiles.dtype)[None, :, None, :, None], axis=3).reshape(G, ga, gb)


def _add(acc, prev):
    return (acc + prev,)


def rowk(name, fn, L, tl, ncol, ins, outs, accs=()):
    nrow = L // tl
    assert L % tl == 0 and tl % HALO == 0
    hb = tl // HALO

    def cw_of(c_total):
        assert c_total % ncol == 0, (name, c_total, ncol)
        return c_total // ncol

    in_specs = []
    for arr, kind in ins:
        if kind == "rows":
            in_specs.append(pl.BlockSpec((tl, cw_of(arr.shape[1])), lambda j, i: (i, j)))
        elif kind == "prev":
            in_specs.append(pl.BlockSpec((HALO, cw_of(arr.shape[1])), lambda j, i: (jnp.maximum(i * hb - 1, 0), j)))
        elif kind == "next":
            in_specs.append(pl.BlockSpec((HALO, cw_of(arr.shape[1])), lambda j, i: (jnp.minimum((i + 1) * hb, nrow * hb - 1), j)))
        elif kind == "cols":
            in_specs.append(pl.BlockSpec((arr.shape[0], cw_of(arr.shape[1])), lambda j, i: (0, j)))
        elif kind == "const":
            in_specs.append(pl.BlockSpec(arr.shape, lambda j, i: (0,) * arr.ndim))
        else:
            raise ValueError(kind)
    out_specs = [pl.BlockSpec((tl, cw_of(c)), lambda j, i: (i, j)) for c, _ in outs]
    out_shape = [S((L, c), dt) for c, dt in outs]
    out_specs += [pl.BlockSpec((r, cw_of(c)), lambda j, i: (0, j)) for r, c, _ in accs]
    out_shape += [S((r, c), dt) for r, c, dt in accs]
    ni, no, na = len(ins), len(outs), len(accs)

    def body(*refs):
        i = pl.program_id(1)
        res = fn(i, nrow, *[r[...] for r in refs[:ni]])
        for o, r in zip(refs[ni:ni + no], res[:no]):
            o[...] = r.astype(o.dtype)
        for o, r in zip(refs[ni + no:ni + no + na], res[no:]):
            @pl.when(i == 0)
            def _(o=o, r=r):
                o[...] = r.astype(o.dtype)

            @pl.when(i > 0)
            def _(o=o, r=r):
                o[...] += r.astype(o.dtype)

    res = pl.pallas_call(
        body,
        name=name,
        grid=(ncol, nrow),
        in_specs=in_specs,
        out_specs=out_specs,
        out_shape=out_shape,
        compiler_params=_cparams(("parallel", "arbitrary")),
    )(*[a for a, _ in ins])
    return tuple(res)


def _sigmoid(x):
    return 1.0 / (1.0 + jnp.exp(-x))


def _silu(x):
    return x * _sigmoid(x)


def _dsilu(x):
    s = _sigmoid(x)
    return s * (1.0 + x * (1.0 - s))


def _erf(x):
    a = jnp.abs(x)
    t = 1.0 / (1.0 + 0.3275911 * a)
    poly = t * (0.254829592 + t * (-0.284496736 + t * (1.421413741 + t * (-1.453152027 + t * 1.061405429))))
    y = 1.0 - poly * jnp.exp(-a * a)
    return jnp.where(x < 0, -y, y)


def _gelu(x):
    return 0.5 * x * (1.0 + _erf(x * (2.0 ** -0.5)))


def _dgelu(x):
    cdf = 0.5 * (1.0 + _erf(x * (2.0 ** -0.5)))
    pdf = jnp.exp(-0.5 * x * x) * (1.0 / math.sqrt(2.0 * math.pi))
    return cdf + x * pdf


def _rms(x, w):
    return x * lax.rsqrt(jnp.mean(x * x, axis=-1, keepdims=True) + NORM_EPS) * w


def _rms_bwd(x, w, dy):
    d = x.shape[-1]
    r = lax.rsqrt(jnp.mean(x * x, axis=-1, keepdims=True) + NORM_EPS)
    wdy = w * dy
    dx = r * wdy - x * (r * r * r) * (jnp.sum(x * wdy, axis=-1, keepdims=True) / d)
    dw = jnp.sum(x * r * dy, axis=0, keepdims=True)
    return dx, dw


def _from_cols(cols, width):
    tl = cols[0].shape[0]
    lane = lax.broadcasted_iota(jnp.int32, (tl, width), 1)
    out = jnp.zeros((tl, width), f32)
    for n, col in enumerate(cols):
        out = jnp.where(lane == n, col, out)
    return out


def _from_rows(rows):
    c = rows[0].shape[1]
    sub = lax.broadcasted_iota(jnp.int32, (len(rows), c), 0)
    out = jnp.zeros((len(rows), c), f32)
    for n, row in enumerate(rows):
        out = jnp.where(sub == n, row, out)
    return out


def _shift_down(x, halo, s, first):
    if s == 0:
        return x
    tl = x.shape[0]
    halo = jnp.where(first, 0.0, halo)
    xx = jnp.concatenate([halo, x], axis=0)
    return pltpu.roll(xx, s, 0)[HALO:HALO + tl]


def _causal_conv(x, halo, w, first):
    kw = w.shape[0]
    shifted = [_shift_down(x, halo, kw - 1 - j, first) for j in range(kw)]
    out = shifted[0] * w[0:1]
    for j in range(1, kw):
        out = out + shifted[j] * w[j:j + 1]
    return out, shifted


def rms_fwd(x, w, name):
    L, D = x.shape

    def fn(i, n, xb, wb):
        return (_rms(xb, wb),)

    return rowk(name, fn, L, _pick(L, (256, 128, 64, 32, 16, 8)), 1, [(x, "rows"), (w, "const")], [(D, bf16)])[0]


def rms_bwd(x, w, dh, dres, name):
    L, D = x.shape

    def fn(i, n, xb, wb, dhb, drb):
        dx, dw = _rms_bwd(xb, wb, dhb)
        dx = dx + drb
        return dx, dx, dw

    return rowk(name, fn, L, _pick(L, (256, 128, 64, 32, 16, 8)), 1,
                [(x, "rows"), (w, "const"), (dh, "rows"), (dres, "rows")], [(D, f32), (D, bf16)], [(1, D, f32)])


def loss_head(x, w, target, name):
    L, D = x.shape

    def fn(i, n, xb, wb, tb):
        err = _rms(xb, wb) - tb
        loss = 0.5 * jnp.sum(err * err) / D
        dx, dw = _rms_bwd(xb, wb, err / D)
        return dx, dx, jnp.full((8, 128), loss, f32), dw

    return rowk(name, fn, L, _pick(L, (256, 128, 64, 32, 16, 8)), 1,
                [(x, "rows"), (w, "const"), (target, "rows")], [(D, f32), (D, bf16)], [(8, 128, f32), (1, D, f32)])


def _s5_disc_math(lr, li, logdt, br, bi):
    dt = jnp.exp(logdt)
    mag = jnp.exp(lr * dt)
    ar, ai = mag * jnp.cos(li * dt), mag * jnp.sin(li * dt)
    den = lr * lr + li * li
    nr, ni = ar - 1.0, ai
    cr = (nr * lr + ni * li) / den
    ci = (ni * lr - nr * li) / den
    return ar, ai, cr * br - ci * bi, cr * bi + ci * br


def _disc_call(body, name, ins, out_widths):
    GP = ins[0].shape[0]
    tl = _pick(GP, (512, 256, 128, 64, 32, 16, 8))
    spec = lambda w: pl.BlockSpec((tl, w), lambda i: (i, 0))
    return pl.pallas_call(body, name=name, grid=(GP // tl,),
                          in_specs=[spec(a.shape[1]) for a in ins], out_specs=[spec(w) for w in out_widths],
                          out_shape=[S((GP, w), f32) for w in out_widths], compiler_params=_cparams(("parallel",)))(*ins)


def s5_disc(lr, li, logdt, br, bi, name):
    HG = br.shape[1]

    def body(lr_ref, li_ref, dt_ref, br_ref, bi_ref, ar_ref, ai_ref, bbr_ref, bbi_ref):
        ar, ai, bbr, bbi = _s5_disc_math(lr_ref[...], li_ref[...], dt_ref[...], br_ref[...], bi_ref[...])
        ar_ref[...], ai_ref[...], bbr_ref[...], bbi_ref[...] = ar, ai, bbr, bbi

    return _disc_call(body, name, [lr, li, logdt, br, bi], [1, 1, HG, HG])


def s5_disc_bwd(lr, li, logdt, br, bi, dar, dai, dbbr, dbbi, name):
    HG = br.shape[1]

    def body(lr_ref, li_ref, dt_ref, br_ref, bi_ref, dar_ref, dai_ref, dbbr_ref, dbbi_ref, *outs):
        _, vjp = jax.vjp(_s5_disc_math, lr_ref[...], li_ref[...], dt_ref[...], br_ref[...], bi_ref[...])
        for o, g in zip(outs, vjp((dar_ref[...], dai_ref[...], dbbr_ref[...], dbbi_ref[...]))):
            o[...] = g

    return _disc_call(body, name, [lr, li, logdt, br, bi, dar, dai, dbbr, dbbi], [1, 1, 1, HG, HG])


SCAN_TB = 256


def s5_scan_fwd(bur, bui, ar, ai, name):
    L, GP = bur.shape
    cw = _pick(GP, (1024, 512, 256, 128))
    tb = _pick(L, (SCAN_TB, 128, 64, 32, 16, 8))

    def body(bur_ref, bui_ref, ar_ref, ai_ref, xr_ref, xi_ref, cr_ref, ci_ref):
        @pl.when(pl.program_id(1) == 0)
        def _():
            cr_ref[...] = jnp.zeros_like(cr_ref)
            ci_ref[...] = jnp.zeros_like(ci_ref)

        a_r, a_i = ar_ref[...], ai_ref[...]

        def step(t, carry):
            xr, xi = carry
            row = pl.ds(t, 1)
            nr = a_r * xr - a_i * xi + bur_ref[row, :]
            ni = a_r * xi + a_i * xr + bui_ref[row, :]
            xr_ref[row, :] = nr
            xi_ref[row, :] = ni
            return nr, ni

        xr, xi = lax.fori_loop(0, tb, step, (cr_ref[...], ci_ref[...]), unroll=8)
        cr_ref[...] = xr
        ci_ref[...] = xi

    blk = pl.BlockSpec((tb, cw), lambda j, i: (i, j))
    vec = pl.BlockSpec((1, cw), lambda j, i: (0, j))
    return pl.pallas_call(
        body, name=name, grid=(GP // cw, L // tb),
        in_specs=[blk, blk, vec, vec], out_specs=[blk, blk],
        out_shape=[S((L, GP), f32), S((L, GP), f32)],
        scratch_shapes=[pltpu.VMEM((1, cw), f32), pltpu.VMEM((1, cw), f32)],
        compiler_params=_cparams(("parallel", "arbitrary")),
    )(bur, bui, ar, ai)


def s5_scan_bwd(gr, gi, xr, xi, ar, ai, name):
    L, GP = gr.shape
    cw = _pick(GP, (1024, 512, 256, 128))
    tb = _pick(L, (SCAN_TB, 128, 64, 32, 16, 8))
    nt = L // tb

    def body(gr_ref, gi_ref, xr_ref, xi_ref, ar_ref, ai_ref, gxr_ref, gxi_ref, dar_ref, dai_ref, cr_ref, ci_ref):
        @pl.when(pl.program_id(1) == 0)
        def _():
            cr_ref[...] = jnp.zeros_like(cr_ref)
            ci_ref[...] = jnp.zeros_like(ci_ref)
            dar_ref[...] = jnp.zeros_like(dar_ref)
            dai_ref[...] = jnp.zeros_like(dai_ref)

        a_r, a_i = ar_ref[...], ai_ref[...]

        def step(s, carry):
            cr, ci, dr, di = carry
            row = pl.ds(tb - 1 - s, 1)
            x_r, x_i = xr_ref[row, :], xi_ref[row, :]
            dr = dr + cr * x_r + ci * x_i
            di = di + ci * x_r - cr * x_i
            nr = gr_ref[row, :] + a_r * cr + a_i * ci
            ni = gi_ref[row, :] + a_r * ci - a_i * cr
            gxr_ref[row, :] = nr
            gxi_ref[row, :] = ni
            return nr, ni, dr, di

        cr, ci, dr, di = lax.fori_loop(0, tb, step, (cr_ref[...], ci_ref[...], dar_ref[...], dai_ref[...]), unroll=8)
        cr_ref[...] = cr
        ci_ref[...] = ci
        dar_ref[...] = dr
        dai_ref[...] = di

    blk = pl.BlockSpec((tb, cw), lambda j, i: (nt - 1 - i, j))
    vec = pl.BlockSpec((1, cw), lambda j, i: (0, j))
    return pl.pallas_call(
        body, name=name, grid=(GP // cw, nt),
        in_specs=[blk, blk, blk, blk, vec, vec], out_specs=[blk, blk, vec, vec],
        out_shape=[S((L, GP), f32), S((L, GP), f32), S((1, GP), f32), S((1, GP), f32)],
        scratch_shapes=[pltpu.VMEM((1, cw), f32), pltpu.VMEM((1, cw), f32)],
        compiler_params=_cparams(("parallel", "arbitrary")),
    )(gr, gi, xr, xi, ar, ai)


def _dn_heads_math(cq, ck, cv, braw, araw, alog, dtb, dk):
    q, k, v = _silu(cq), _silu(ck), _silu(cv)
    q = q * lax.rsqrt(jnp.sum(q * q, axis=-1, keepdims=True) + NORM_EPS) * (dk ** -0.5)
    k = k * lax.rsqrt(jnp.sum(k * k, axis=-1, keepdims=True) + NORM_EPS)
    beta = _sigmoid(braw)
    g = -jnp.exp(alog) * jax.nn.softplus(araw + dtb)
    return q, k, v, beta, g


def dn_prep(qkv, convw, ba, alog, dtb, H, name):
    L, W = qkv.shape
    hk = W // 3
    dk = hk // H

    def fn(i, n, xb, hb, wb, bab, alb, dtbb):
        c, _ = _causal_conv(xb, hb, wb, i == 0)
        qs, ks, vs, bs, gs = [], [], [], [], []
        for h in range(H):
            sl = lambda o: c[:, o + h * dk:o + (h + 1) * dk]
            q, k, v, beta, g = _dn_heads_math(sl(0), sl(hk), sl(2 * hk), bab[:, h:h + 1], bab[:, H + h:H + h + 1],
                                              alb[:, h:h + 1], dtbb[:, h:h + 1], dk)
            qs.append(q), ks.append(k), vs.append(v), bs.append(beta), gs.append(g)
        cat = lambda xs: jnp.concatenate(xs, axis=1)
        return cat(qs), cat(ks), cat(vs), _from_cols(bs + gs, 2 * H)

    return rowk(name, fn, L, _pick(L, (128, 64, 32, 16, 8)), 1,
                [(qkv, "rows"), (qkv, "prev"), (convw, "const"), (ba, "rows"), (alog, "const"), (dtb, "const")],
                [(hk, f32), (hk, f32), (hk, f32), (2 * H, f32)])


def _conv_t_rows(dc, w, tl):
    kw = w.shape[0]
    out = dc[:tl] * w[kw - 1:kw]
    for j in range(kw - 1):
        out = out + pltpu.roll(dc, tl + HALO - (kw - 1 - j), 0)[:tl] * w[j:j + 1]
    return out


def dn_prep_bwd(qkv, convw, ba, alog, dtb, dq, dk_, dv, dbg, H, name):
    L, W = qkv.shape
    hk = W // 3
    dk = hk // H
    kw = convw.shape[0]
    nba = ba.shape[1]

    def fn(i, n, xb, hb, xn, wb, bab, alb, dtbb, dqb, dqn, dkb, dkn, dvb, dvn, dbgb):
        tl = xb.shape[0]
        last = i == n - 1
        ext = lambda blk, nxt: jnp.concatenate([blk, jnp.where(last, 0.0, nxt)], axis=0)
        pad = lambda blk: jnp.concatenate([blk, jnp.zeros((HALO, blk.shape[1]), f32)], axis=0)
        c, shifted = _causal_conv(ext(xb, xn), hb, wb, i == 0)
        bae, dbge = pad(bab), pad(dbgb)
        dqe, dke, dve = ext(dqb, dqn), ext(dkb, dkn), ext(dvb, dvn)
        dcs = [None] * (3 * H)
        dbr, dar, dal, ddt = [], [], [], []
        for h in range(H):
            sl = lambda a, o: a[:, o + h * dk:o + (h + 1) * dk]
            args = (sl(c, 0), sl(c, hk), sl(c, 2 * hk), bae[:, h:h + 1], bae[:, H + h:H + h + 1],
                    alb[:, h:h + 1], dtbb[:, h:h + 1])
            _, vjp = jax.vjp(lambda *a: _dn_heads_math(*a, dk), *args)
            g = vjp((sl(dqe, 0), sl(dke, 0), sl(dve, 0), dbge[:, h:h + 1], dbge[:, H + h:H + h + 1]))
            dcs[h], dcs[H + h], dcs[2 * H + h] = g[0], g[1], g[2]
            dbr.append(g[3][:tl]), dar.append(g[4][:tl]), dal.append(g[5]), ddt.append(g[6])
        dc = jnp.concatenate(dcs, axis=1)
        dba = _from_cols(dbr + dar, nba)
        dw = _from_rows([jnp.sum(dc[:tl] * shifted[j][:tl], axis=0, keepdims=True) for j in range(kw)])
        return _conv_t_rows(dc, wb, tl), dba, dw, _from_cols(dal, H), _from_cols(ddt, H)

    return rowk(name, fn, L, _pick(L, (128, 64, 32, 16, 8)), 1,
                [(qkv, "rows"), (qkv, "prev"), (qkv, "next"), (convw, "const"), (ba, "rows"), (alog, "const"), (dtb, "const"),
                 (dq, "rows"), (dq, "next"), (dk_, "rows"), (dk_, "next"), (dv, "rows"), (dv, "next"), (dbg, "rows")],
                [(W, bf16), (nba, bf16)], [(kw, W, f32), (1, H, f32), (1, H, f32)])


_BDIMS = {"nn": (((2,), (1,)), ((0,), (0,))), "nt": (((2,), (2,)), ((0,), (0,))), "tn": (((1,), (1,)), ((0,), (0,)))}


def _bdot(a, b, mode):
    return lax.dot_general(a.astype(bf16), b.astype(bf16), _BDIMS[mode], preferred_element_type=f32)


def _split16(a):
    hi = a.astype(bf16)
    return hi, (a - hi.astype(f32)).astype(bf16)


def _hdot(a, b, mode):
    ah, al = _split16(a)
    bh, bl = _split16(b)
    d = lambda x, y: lax.dot_general(x, y, _BDIMS[mode], preferred_element_type=f32)
    return d(ah, bh) + (d(ah, bl) + d(al, bh))


def _make_dot(raw):
    @functools.partial(jax.custom_vjp, nondiff_argnums=(2,))
    def dot(a, b, mode):
        return raw(a, b, mode)

    def fwd(a, b, mode):
        return raw(a, b, mode), (a, b)

    def bwd(mode, res, ct):
        a, b = res
        if mode == "nn":
            return raw(ct, b, "nt"), raw(a, ct, "tn")
        if mode == "nt":
            return raw(ct, b, "nn"), raw(ct, a, "tn")
        return raw(b, ct, "nt"), raw(a, ct, "nn")

    dot.defvjp(fwd, bwd)
    return dot


_dot16 = _make_dot(_bdot)
_dot32 = _make_dot(_hdot)


@jax.custom_vjp
def _unit_lower_inv(lmat):
    c = lmat.shape[-1]
    eye = (lax.broadcasted_iota(jnp.int32, (c, c), 0) == lax.broadcasted_iota(jnp.int32, (c, c), 1)).astype(f32)
    p = -lmat
    t = eye + p
    for _ in range(int(math.log2(c)) - 1):
        p = _hdot(p, p, "nn")
        t = t + _hdot(t, p, "nn")
    return t


def _uli_fwd(lmat):
    t = _unit_lower_inv(lmat)
    return t, t


def _uli_bwd(t, dt):
    return (-_hdot(_hdot(t, dt, "tn"), t, "nt"),)


_unit_lower_inv.defvjp(_uli_fwd, _uli_bwd)


def _dn_chunk_math(s_in, q, k, v, gcol, grow, bcol):
    c = q.shape[1]
    ri = lax.broadcasted_iota(jnp.int32, (c, c), 0)
    ci = lax.broadcasted_iota(jnp.int32, (c, c), 1)
    tril = (ri >= ci).astype(f32)
    strict = (ri > ci).astype(f32)
    gc_col = jnp.sum(tril * grow, axis=2, keepdims=True)
    gc_row = jnp.sum((1.0 - strict) * gcol, axis=1, keepdims=True)
    g_last = jnp.sum(gcol, axis=1, keepdims=True)
    decay = jnp.exp((gc_col - gc_row) * tril) * tril
    kb = k * bcol
    vb = v * bcol
    lmat = _dot16(kb, k, "nt") * decay * strict
    t = _unit_lower_inv(lmat)
    u = _dot32(t, vb, "nn")
    w = _dot32(t, kb * jnp.exp(gc_col), "nn")
    attn = _dot16(q, k, "nt") * decay
    v_new = u - _dot16(w, s_in, "nn")
    o = _dot16(q * jnp.exp(gc_col), s_in, "nn") + _dot16(attn, v_new, "nn")
    s_out = s_in * jnp.exp(g_last) + _dot16(k * jnp.exp(g_last - gc_col), v_new, "tn")
    return o, s_out


def _dn_load(q_ref, k_ref, v_ref, bg_ref, bgt_ref, H, dk):
    heads = lambda ref: jnp.stack([ref[:, h * dk:(h + 1) * dk] for h in range(H)])
    bgb, bgtb = bg_ref[...], bgt_ref[0]
    gcol = jnp.stack([bgb[:, H + h:H + h + 1] for h in range(H)])
    bcol = jnp.stack([bgb[:, h:h + 1] for h in range(H)])
    grow = jnp.stack([bgtb[H + h:H + h + 1, :] for h in range(H)])
    return heads(q_ref), heads(k_ref), heads(v_ref), gcol, grow, bcol


def _host(payload, n_in, n_out, n_scratch):
    pi = len(payload.arrays) if payload is not None else 0
    po = len(payload.out_shapes) if payload is not None else 0

    def split(refs):
        own = refs[:n_in] + refs[n_in + pi:n_in + pi + n_out] + refs[n_in + pi + n_out + po:n_in + pi + n_out + po + n_scratch]
        theirs = (refs[n_in:n_in + pi], refs[n_in + pi + n_out:n_in + pi + n_out + po], refs[n_in + pi + n_out + po + n_scratch:])
        return own, theirs

    def hooks(theirs, first, last):
        if payload is None:
            return

        @pl.when(first)
        def _():
            payload.start(*theirs)

        @pl.when(last)
        def _():
            payload.finish(*theirs)

    extra = dict(in_specs=[_ANY] * pi, out_specs=[_ANY] * po, out_shape=list(payload.out_shapes) if payload else [],
                 scratch=list(payload.sems) if payload else [], arrays=list(payload.arrays) if payload else [],
                 aliases={n_in + a: n_out + b for a, b in payload.aliases.items()} if payload else {})
    return split, hooks, extra


def dn_chunk_fwd(qn, kn, vv, bg, bgt, H, name, payload=None):
    L, hk = qn.shape
    dk = hk // H
    c = DN_CHUNK
    nc = L // c
    split, hooks, extra = _host(payload, 5, 2, 1)

    def body(*refs):
        (q_ref, k_ref, v_ref, bg_ref, bgt_ref, o_ref, ss_ref, s_ref), theirs = split(refs)
        hooks(theirs, pl.program_id(0) == 0, pl.program_id(0) == nc - 1)

        @pl.when(pl.program_id(0) == 0)
        def _():
            s_ref[...] = jnp.zeros_like(s_ref)

        s_in = s_ref[...]
        ss_ref[0] = s_in
        o, s_out = _dn_chunk_math(s_in, *_dn_load(q_ref, k_ref, v_ref, bg_ref, bgt_ref, H, dk))
        for h in range(H):
            o_ref[:, h * dk:(h + 1) * dk] = o[h]
        s_ref[...] = s_out

    row = lambda w: pl.BlockSpec((c, w), lambda n: (n, 0))
    res = pl.pallas_call(
        body, name=name, grid=(nc,),
        in_specs=[row(hk), row(hk), row(hk), row(2 * H), pl.BlockSpec((1, 2 * H, c), lambda n: (n, 0, 0))] + extra["in_specs"],
        out_specs=[row(hk), pl.BlockSpec((1, H, dk, dk), lambda n: (n, 0, 0, 0))] + extra["out_specs"],
        out_shape=[S((L, hk), f32), S((nc, H, dk, dk), f32)] + extra["out_shape"],
        scratch_shapes=[pltpu.VMEM((H, dk, dk), f32)] + extra["scratch"],
        input_output_aliases=extra["aliases"],
        compiler_params=_cparams(("arbitrary",)),
    )(qn, kn, vv, bg, bgt, *extra["arrays"])
    return res[0], res[1], list(res[2:])


def dn_chunk_bwd(qn, kn, vv, bg, bgt, ss, do, H, name, payload=None):
    L, hk = qn.shape
    dk = hk // H
    c = DN_CHUNK
    nc = L // c
    split, hooks, extra = _host(payload, 7, 5, 1)

    def body(*refs):
        (q_ref, k_ref, v_ref, bg_ref, bgt_ref, ss_ref, do_ref, dq_ref, dk_ref, dv_ref, dbg_ref, dgt_ref, ds_ref), theirs = split(refs)
        hooks(theirs, pl.program_id(0) == 0, pl.program_id(0) == nc - 1)

        @pl.when(pl.program_id(0) == 0)
        def _():
            ds_ref[...] = jnp.zeros_like(ds_ref)

        args = (ss_ref[0],) + _dn_load(q_ref, k_ref, v_ref, bg_ref, bgt_ref, H, dk)
        _, vjp = jax.vjp(_dn_chunk_math, *args)
        do = jnp.stack([do_ref[:, h * dk:(h + 1) * dk] for h in range(H)])
        ds, dq, dkk, dv, dgcol, dgrow, dbcol = vjp((do, ds_ref[...]))
        ds_ref[...] = ds
        for h in range(H):
            sl = slice(h * dk, (h + 1) * dk)
            dq_ref[:, sl], dk_ref[:, sl], dv_ref[:, sl] = dq[h], dkk[h], dv[h]
        dbg_ref[...] = _from_cols([dbcol[h] for h in range(H)] + [dgcol[h] for h in range(H)], 2 * H)
        dgt_ref[0] = _from_rows([dgrow[h] for h in range(H)])

    row = lambda w: pl.BlockSpec((c, w), lambda n: (nc - 1 - n, 0))
    res = pl.pallas_call(
        body, name=name, grid=(nc,),
        in_specs=[row(hk), row(hk), row(hk), row(2 * H), pl.BlockSpec((1, 2 * H, c), lambda n: (nc - 1 - n, 0, 0)),
                  pl.BlockSpec((1, H, dk, dk), lambda n: (nc - 1 - n, 0, 0, 0)), row(hk)] + extra["in_specs"],
        out_specs=[row(hk), row(hk), row(hk), row(2 * H), pl.BlockSpec((1, H, c), lambda n: (nc - 1 - n, 0, 0))] + extra["out_specs"],
        out_shape=[S((L, hk), f32), S((L, hk), f32), S((L, hk), f32), S((L, 2 * H), f32), S((nc, H, c), f32)] + extra["out_shape"],
        scratch_shapes=[pltpu.VMEM((H, dk, dk), f32)] + extra["scratch"],
        input_output_aliases=extra["aliases"],
        compiler_params=_cparams(("arbitrary",)),
    )(qn, kn, vv, bg, bgt, ss, do, *extra["arrays"])
    return tuple(res[:5]) + (list(res[5:]),)


def _dn_gate_math(o, z, w):
    return _rms(o, w) * _silu(z)


def dn_gate(o, z, w, H, name):
    L, hv = o.shape
    dv = hv // H

    def fn(i, n, ob, zb, wb):
        return (jnp.concatenate([_dn_gate_math(ob[:, h * dv:(h + 1) * dv], zb[:, h * dv:(h + 1) * dv], wb)
                                 for h in range(H)], axis=1),)

    return rowk(name, fn, L, _pick(L, (256, 128, 64, 32, 16, 8)), 1, [(o, "rows"), (z, "rows"), (w, "const")], [(hv, bf16)])[0]


def dn_gate_bwd(o, z, w, dy, H, name):
    L, hv = o.shape
    dv = hv // H

    def fn(i, n, ob, zb, wb, dyb):
        dos, dzs, dw = [], [], 0.0
        for h in range(H):
            sl = slice(h * dv, (h + 1) * dv)
            _, vjp = jax.vjp(_dn_gate_math, ob[:, sl], zb[:, sl], wb)
            a, b, c = vjp(dyb[:, sl])
            dos.append(a), dzs.append(b)
            dw = dw + c
        return jnp.concatenate(dos, axis=1), jnp.concatenate(dzs, axis=1), dw

    return rowk(name, fn, L, _pick(L, (256, 128, 64, 32, 16, 8)), 1,
                [(o, "rows"), (z, "rows"), (w, "const"), (dy, "rows")], [(hv, f32), (hv, bf16)], [(1, dv, f32)])


def ffn_mid(ua, uv, wa, wv, name):
    L, F = ua.shape
    ncol = F // _pick(F, (1408, 1024, 512, 256, 128))

    def fn(i, n, ab, ah, vb, vh, wab, wvb):
        ca, _ = _causal_conv(ab, ah, wab, i == 0)
        cv, _ = _causal_conv(vb, vh, wvb, i == 0)
        return (_silu(ca) * cv,)

    return rowk(name, fn, L, _pick(L, (256, 128, 64, 32, 16, 8)), ncol,
                [(ua, "rows"), (ua, "prev"), (uv, "rows"), (uv, "prev"), (wa, "cols"), (wv, "cols")], [(F, bf16)])[0]


def ffn_mid_bwd(ua, uv, wa, wv, dh, name):
    L, F = ua.shape
    kw = wa.shape[0]
    ncol = F // _pick(F, (1408, 1024, 512, 256, 128))

    def fn(i, n, ab, ap, an, vb, vp, vn, wab, wvb, dhb, dhn):
        tl = ab.shape[0]
        last = i == n - 1
        ext = lambda blk, nxt: jnp.concatenate([blk, jnp.where(last, 0.0, nxt)], axis=0)
        dhe = ext(dhb, dhn)
        ca, sa = _causal_conv(ext(ab, an), ap, wab, i == 0)
        cv, sv = _causal_conv(ext(vb, vn), vp, wvb, i == 0)
        dca = dhe * cv * _dsilu(ca)
        dcv = dhe * _silu(ca)
        dwa = _from_rows([jnp.sum(dca[:tl] * sa[j][:tl], axis=0, keepdims=True) for j in range(kw)])
        dwv = _from_rows([jnp.sum(dcv[:tl] * sv[j][:tl], axis=0, keepdims=True) for j in range(kw)])

        return _conv_t_rows(dca, wab, tl), _conv_t_rows(dcv, wvb, tl), dwa, dwv

    return rowk(name, fn, L, _pick(L, (256, 128, 64, 32, 16, 8)), ncol,
                [(ua, "rows"), (ua, "prev"), (ua, "next"), (uv, "rows"), (uv, "prev"), (uv, "next"),
                 (wa, "cols"), (wv, "cols"), (dh, "rows"), (dh, "next")],
                [(F, bf16), (F, bf16)], [(kw, F, f32), (kw, F, f32)])


def _merge_epi(acc, gs, gd, ga, gb):
    return acc, _sigmoid(gs) * ga * _sigmoid(gb) + _sigmoid(gd) * acc


def merge_bwd(dm, gs, gd, ga, gb, brdn, name):
    L, D = dm.shape
    ncol = D // _pick(D, (1024, 512, 256, 128))

    def fn(i, n, dmb, gsb, gdb, gab, gbb, brb):
        ss, sd, sb = _sigmoid(gsb), _sigmoid(gdb), _sigmoid(gbb)
        br_s5 = gab * sb
        dbr_s5 = dmb * ss
        return (dmb * br_s5 * ss * (1.0 - ss), dmb * brb * sd * (1.0 - sd), dbr_s5 * sb,
                dbr_s5 * gab * sb * (1.0 - sb), dmb * sd)

    return rowk(name, fn, L, _pick(L, (256, 128, 64, 32, 16, 8)), ncol,
                [(a, "rows") for a in (dm, gs, gd, ga, gb, brdn)], [(D, bf16)] * 5)


def s5_out_bwd(dy, ypre, u, d, name):
    L, W = dy.shape

    def fn(i, n, dyb, yb, ub, db):
        dyp = dyb * _dgelu(yb)
        return dyp, db * dyp, jnp.sum(dyp * ub, axis=0, keepdims=True)

    return rowk(name, fn, L, _pick(L, (256, 128, 64, 32, 16, 8)), 1,
                [(dy, "rows"), (ypre, "rows"), (u, "rows"), (d, "const")], [(W, bf16), (W, f32)], [(1, W, f32)])


def _s5_y_epi(acc, y1, u, d):
    ypre = acc + y1 + d * u
    return ypre, _gelu(ypre)


def adamw(w, g, m, v, name):
    R, C = w.shape
    tl = _pick(R, (256, 128, 64, 32, 16, 8))
    if R * C * 4 <= 2 * 1024 * 1024:
        tl = R

    def body(w_ref, g_ref, m_ref, v_ref, d_ref, nm_ref, nv_ref):
        gg = g_ref[...]
        nm = ADAM_B1 * m_ref[...] + (1.0 - ADAM_B1) * gg
        nv = ADAM_B2 * v_ref[...] + (1.0 - ADAM_B2) * (gg * gg)
        m_hat = nm / (1.0 - ADAM_B1 ** ADAM_STEP)
        v_hat = nv / (1.0 - ADAM_B2 ** ADAM_STEP)
        d_ref[...] = -ADAM_LR * (m_hat / (jnp.sqrt(v_hat) + ADAM_EPS) + ADAM_WD * w_ref[...])
        nm_ref[...] = nm
        nv_ref[...] = nv

    blk = pl.BlockSpec((tl, C), lambda i: (i, 0))
    return pl.pallas_call(body, name=name, grid=(R // tl,), in_specs=[blk] * 4, out_specs=[blk] * 3,
                          out_shape=[S((R, C), f32)] * 3, compiler_params=_cparams(("parallel",)))(w, g, m, v)


def sum_slots(x, name, out_dtype=f32):
    n, R, C = x.shape
    tl = _pick(R, (512, 256, 128, 64, 32, 16, 8))

    def body(x_ref, o_ref):
        acc = x_ref[0].astype(f32)
        for s in range(1, n):
            acc = acc + x_ref[s].astype(f32)
        o_ref[...] = acc.astype(o_ref.dtype)

    return pl.pallas_call(body, name=name, grid=(R // tl,),
                          in_specs=[pl.BlockSpec((n, tl, C), lambda i: (0, i, 0))],
                          out_specs=pl.BlockSpec((tl, C), lambda i: (i, 0)),
                          out_shape=S((R, C), out_dtype), compiler_params=_cparams(("parallel",)))(x)


_ANY = pl.BlockSpec(memory_space=pl.ANY)


def _coords():
    return lax.axis_index("x"), lax.axis_index("y"), lax.axis_index("c")


def chip_exchange(src, name, same=False):
    out_shape = (N_CHIPS,) + src.shape if same else src.shape
    assert out_shape[0] == N_CHIPS

    def body(src_ref, out_ref, send_sems, recv_sems, local_sem):
        x, y, c = _coords()
        me = 2 * x + y
        slot = (lambda j: src_ref) if same else (lambda j: src_ref.at[j])
        mine = pltpu.make_async_copy(slot(me), out_ref.at[me], local_sem)
        mine.start()
        peers = [(1 - x, y), (x, 1 - y), (1 - x, 1 - y)]
        copies = []
        for k, (px, py) in enumerate(peers):
            cp = pltpu.make_async_remote_copy(
                src_ref=slot(2 * px + py), dst_ref=out_ref.at[me],
                send_sem=send_sems.at[k], recv_sem=recv_sems.at[k],
                device_id=(px, py, c), device_id_type=MESH)
            cp.start()
            copies.append(cp)
        for k, (px, py) in enumerate(peers):
            pltpu.make_async_remote_copy(
                src_ref=slot(me), dst_ref=out_ref.at[2 * px + py],
                send_sem=send_sems.at[k], recv_sem=recv_sems.at[k],
                device_id=(px, py, c), device_id_type=MESH).wait_recv()
        for cp in copies:
            cp.wait_send()
        mine.wait()

    return pl.pallas_call(
        body, name=name, in_specs=[_ANY], out_specs=_ANY, out_shape=S(out_shape, src.dtype),
        scratch_shapes=[pltpu.SemaphoreType.DMA((3,)), pltpu.SemaphoreType.DMA((3,)), pltpu.SemaphoreType.DMA],
    )(src)


def sibling_exchange(src, name):
    def body(src_ref, out_ref, send_sem, recv_sem):
        x, y, c = _coords()
        cp = pltpu.make_async_remote_copy(src_ref=src_ref, dst_ref=out_ref, send_sem=send_sem, recv_sem=recv_sem,
                                          device_id=(x, y, 1 - c), device_id_type=MESH)
        cp.start()
        cp.wait()

    return pl.pallas_call(
        body, name=name, in_specs=[_ANY], out_specs=_ANY, out_shape=S(src.shape, src.dtype),
        scratch_shapes=[pltpu.SemaphoreType.DMA, pltpu.SemaphoreType.DMA],
    )(src)


def _rows(c, half_rows):
    return pl.ds(pl.multiple_of(c * half_rows, 16), half_rows)


def _slot(ref, kind, j, rows=None):
    if kind == "slot":
        return ref.at[j] if rows is None else ref.at[j, rows]
    cs = ref.shape[1] // N_CHIPS
    cols = pl.ds(pl.multiple_of(j * cs, 128), cs)
    return ref.at[:, cols] if rows is None else ref.at[rows, cols]


def _sems(n):
    return [pltpu.SemaphoreType.DMA((n,))]


class Payload:
    def __init__(self, arrays, out_shapes, aliases, sems, start, finish):
        self.arrays, self.out_shapes, self.aliases, self.sems = arrays, out_shapes, aliases, sems
        self.start, self.finish = start, finish


def run_payload(p, name):
    ni, no = len(p.arrays), len(p.out_shapes)

    def body(*refs):
        args = (refs[:ni], refs[ni:ni + no], refs[ni + no:])
        p.start(*args)
        p.finish(*args)

    return pl.pallas_call(body, name=name, in_specs=[_ANY] * ni, out_specs=[_ANY] * no, out_shape=p.out_shapes,
                          input_output_aliases=p.aliases, scratch_shapes=p.sems)(*p.arrays)


def _half_rows(bufs, kinds):
    return [(a.shape[1] if k == "slot" else a.shape[0]) // 2 for a, k in zip(bufs, kinds)]


def ag_ici_payload(bufs, kinds):
    n = len(bufs)
    rh = _half_rows(bufs, kinds)

    def copies(outs, sems):
        send, recv = sems
        x, y, c = _coords()
        me = 2 * x + y
        res = []
        for t in range(n):
            own = _slot(outs[t], kinds[t], me, _rows(c, rh[t]))
            for k, (px, py) in enumerate([(1 - x, y), (x, 1 - y), (1 - x, 1 - y)]):
                landed = _slot(outs[t], kinds[t], 2 * px + py, _rows(c, rh[t]))
                sem = dict(send_sem=send.at[3 * t + k], recv_sem=recv.at[3 * t + k], device_id=(px, py, c), device_id_type=MESH)
                res.append((pltpu.make_async_remote_copy(src_ref=own, dst_ref=own, **sem),
                            pltpu.make_async_remote_copy(src_ref=landed, dst_ref=landed, **sem)))
        return res

    def start(ins, outs, sems):
        for mine, _ in copies(outs, sems):
            mine.start()

    def finish(ins, outs, sems):
        both = copies(outs, sems)
        for _, theirs in both:
            theirs.wait_recv()
        for mine, _ in both:
            mine.wait_send()

    return Payload(bufs, [S(a.shape, a.dtype) for a in bufs], {t: t for t in range(n)}, _sems(3 * n) * 2, start, finish)


def ag_d2d(bufs, kinds, name):
    n = len(bufs)
    rh = _half_rows(bufs, kinds)

    def body(*refs):
        outs = refs[n:2 * n]
        send, recv = refs[2 * n:]
        x, y, c = _coords()
        sib = dict(device_id=(x, y, 1 - c), device_id_type=MESH)
        sends = []
        for t in range(n):
            for k, (px, py) in enumerate([(1 - x, y), (x, 1 - y), (1 - x, 1 - y)]):
                landed = _slot(outs[t], kinds[t], 2 * px + py, _rows(c, rh[t]))
                cp = pltpu.make_async_remote_copy(src_ref=landed, dst_ref=landed, send_sem=send.at[3 * t + k],
                                                  recv_sem=recv.at[3 * t + k], **sib)
                cp.start()
                sends.append(cp)
        for t in range(n):
            for k, (px, py) in enumerate([(1 - x, y), (x, 1 - y), (1 - x, 1 - y)]):
                other = _slot(outs[t], kinds[t], 2 * px + py, _rows(1 - c, rh[t]))
                pltpu.make_async_remote_copy(src_ref=other, dst_ref=other, send_sem=send.at[3 * t + k],
                                             recv_sem=recv.at[3 * t + k], **sib).wait_recv()
        for cp in sends:
            cp.wait_send()

    return pl.pallas_call(body, name=name, in_specs=[_ANY] * n, out_specs=[_ANY] * n,
                          out_shape=[S(a.shape, a.dtype) for a in bufs], input_output_aliases={t: t for t in range(n)},
                          scratch_shapes=_sems(3 * n) * 2)(*bufs)


def rs_pair_payload(grads, kinds):
    n = len(grads)

    def half_shape(a, k):
        return (a.shape[0], a.shape[1] // 2, a.shape[2]) if k == "slot" else (a.shape[0] // 2, a.shape[1])

    def half(ref, k, c):
        return ref.at[:, _rows(c, ref.shape[1] // 2)] if k == "slot" else ref.at[_rows(c, ref.shape[0] // 2)]

    shapes = [S(half_shape(a, k), a.dtype) for a, k in zip(grads, kinds)]

    def copies(srcs, got, sems):
        send, recv = sems
        x, y, c = _coords()
        return [pltpu.make_async_remote_copy(src_ref=half(srcs[t], kinds[t], 1 - c), dst_ref=got[t], send_sem=send.at[t],
                                             recv_sem=recv.at[t], device_id=(x, y, 1 - c), device_id_type=MESH)
                for t in range(n)]

    def start(ins, outs, sems):
        for cp in copies(ins, outs, sems):
            cp.start()

    def finish(ins, outs, sems):
        for cp in copies(ins, outs, sems):
            cp.wait()

    return Payload(grads, shapes, {}, _sems(n) * 2, start, finish)


def rs_chips_payload(sums, kinds):
    n = len(sums)

    def part_shape(a, k):
        return a.shape[1:] if k == "slot" else (a.shape[0], a.shape[1] // N_CHIPS)

    shapes = [S((N_CHIPS,) + part_shape(a, k), a.dtype) for a, k in zip(sums, kinds)]

    def copies(srcs, outs, sems):
        send, recv, lsem = sems
        x, y, c = _coords()
        me = 2 * x + y
        local, remote = [], []
        for t in range(n):
            local.append(pltpu.make_async_copy(_slot(srcs[t], kinds[t], me), outs[t].at[me], lsem.at[t]))
            for k, (px, py) in enumerate([(1 - x, y), (x, 1 - y), (1 - x, 1 - y)]):
                landed = outs[t].at[2 * px + py]
                sem = dict(send_sem=send.at[3 * t + k], recv_sem=recv.at[3 * t + k], device_id=(px, py, c), device_id_type=MESH)
                remote.append((pltpu.make_async_remote_copy(src_ref=_slot(srcs[t], kinds[t], 2 * px + py), dst_ref=outs[t].at[me], **sem),
                               pltpu.make_async_remote_copy(src_ref=landed, dst_ref=landed, **sem)))
        return local, remote

    def start(ins, outs, sems):
        local, remote = copies(ins, outs, sems)
        for cp in local:
            cp.start()
        for mine, _ in remote:
            mine.start()

    def finish(ins, outs, sems):
        local, remote = copies(ins, outs, sems)
        for _, theirs in remote:
            theirs.wait_recv()
        for mine, _ in remote:
            mine.wait_send()
        for cp in local:
            cp.wait()

    return Payload(sums, shapes, {}, _sems(3 * n) * 2 + _sems(n), start, finish)


def rs_share(gs, name):
    n = len(gs)

    def body(*refs):
        outs = refs[n:2 * n]
        send, recv = refs[2 * n:]
        x, y, c = _coords()
        started = []
        for t in range(n):
            cp = pltpu.make_async_remote_copy(src_ref=outs[t].at[c], dst_ref=outs[t].at[c], send_sem=send.at[t],
                                              recv_sem=recv.at[t], device_id=(x, y, 1 - c), device_id_type=MESH)
            cp.start()
            started.append(cp)
        for cp in started:
            cp.wait()

    return pl.pallas_call(body, name=name, in_specs=[_ANY] * n, out_specs=[_ANY] * n,
                          out_shape=[S(a.shape, a.dtype) for a in gs], input_output_aliases={t: t for t in range(n)},
                          scratch_shapes=_sems(n) * 2)(*gs)


def _core_index():
    return lax.axis_index("c").astype(jnp.int32).reshape(1)


def place_block(blk, kind, name):
    R, cs = blk.shape
    tl = _pick(R, (256, 128, 64, 32, 16, 8))

    def body(me_ref, b_ref, o_ref):
        o_ref[...] = b_ref[...].astype(o_ref.dtype)

    if kind == "slot":
        out_spec, out_shape = pl.BlockSpec((None, tl, cs), lambda i, me: (me[0], i, 0)), (N_CHIPS, R, cs)
    else:
        out_spec, out_shape = pl.BlockSpec((tl, cs), lambda i, me: (i, me[0])), (R, N_CHIPS * cs)
    me = (2 * lax.axis_index("x") + lax.axis_index("y")).astype(jnp.int32).reshape(1)
    return pl.pallas_call(
        body, name=name,
        grid_spec=pltpu.PrefetchScalarGridSpec(num_scalar_prefetch=1, grid=(R // tl,),
                                               in_specs=[pl.BlockSpec((tl, cs), lambda i, me: (i, 0))], out_specs=out_spec),
        out_shape=S(out_shape, bf16), compiler_params=_cparams(("parallel",)),
    )(me, blk)


def add_half(g, got, kind, name):
    g4 = g.reshape((g.shape[0] if kind == "slot" else 1, 2, -1, g.shape[-1]))
    parts, _, rh, C = g4.shape
    got3 = got.reshape(parts, rh, C)
    tl = _pick(rh, (256, 128, 64, 32, 16, 8))

    def body(c_ref, g_ref, got_ref, o_ref):
        o_ref[...] = (g_ref[...] + got_ref[...]).astype(o_ref.dtype)

    blk = pl.BlockSpec((None, tl, C), lambda s, i, c: (s, i, 0))
    out = pl.pallas_call(
        body, name=name,
        grid_spec=pltpu.PrefetchScalarGridSpec(
            num_scalar_prefetch=1, grid=(parts, rh // tl),
            in_specs=[pl.BlockSpec((None, None, tl, C), lambda s, i, c: (s, c[0], i, 0)), blk], out_specs=blk),
        out_shape=S((parts, rh, C), bf16), compiler_params=_cparams(("parallel", "parallel")),
    )(_core_index(), g4, got3)
    return out.reshape(got.shape)


def sum_slots_half(x, name):
    n, rh, C = x.shape
    tl = _pick(rh, (512, 256, 128, 64, 32, 16, 8))

    def body(c_ref, x_ref, o_ref):
        acc = x_ref[0].astype(f32)
        for s in range(1, n):
            acc = acc + x_ref[s].astype(f32)
        o_ref[...] = acc

    return pl.pallas_call(
        body, name=name,
        grid_spec=pltpu.PrefetchScalarGridSpec(
            num_scalar_prefetch=1, grid=(rh // tl,),
            in_specs=[pl.BlockSpec((n, tl, C), lambda i, c: (0, i, 0))],
            out_specs=pl.BlockSpec((None, tl, C), lambda i, c: (c[0], i, 0))),
        out_shape=S((2, rh, C), f32), compiler_params=_cparams(("parallel",)),
    )(_core_index(), x)


def _block_diag(blocks):
    G, a, b = blocks.shape
    eye = jnp.eye(G, dtype=blocks.dtype)
    return (blocks[:, :, None, :] * eye[:, None, :, None]).reshape(G * a, G * b)


def _flat_pack(arrs, dtype, lanes=1024, row_mult=16):
    flat = jnp.concatenate([a.reshape(-1).astype(dtype) for a in arrs])
    n = flat.shape[0]
    per = lanes * row_mult
    pad = (-n) % per
    if pad:
        flat = jnp.concatenate([flat, jnp.zeros((pad,), dtype)])
    return flat.reshape(-1, lanes)


def _flat_unpack(buf, shapes):
    flat = buf.reshape(-1)
    out, off = [], 0
    for s in shapes:
        n = math.prod(s)
        out.append(flat[off:off + n].reshape(s))
        off += n
    return out


def _layer_weights(p, H):
    w_in = p["w_in"]
    D = w_in.shape[0]
    s5w = p["s5_d"].shape[-1]
    hk = p["dn_proj_w"].shape[0]
    off_z = s5w + 4 * hk
    off_a = off_z + 2 * H
    wp = jnp.concatenate([w_in[:, :off_z], w_in[:, off_a:], jnp.pad(w_in[:, off_z:off_a], ((0, 0), (0, 128 - 2 * H)))], axis=1)
    q = {}
    q["Wu"] = Win(wp, 0, s5w)
    q["Wqkv"] = Win(wp, s5w, 3 * hk)
    q["Wz"] = Win(wp, s5w + 3 * hk, hk)
    q["Wgs"] = Win(wp, off_z, D)
    q["Wgd"] = Win(wp, off_z + D, D)
    q["Wba"] = Win(wp, off_z + 2 * D, 128)
    q["Wga"], q["Wgb"] = Win(p["s5_glu_w"], 0, D), Win(p["s5_glu_w"], D, D)
    F = p["ffn_down"].shape[0]
    q["Wupa"], q["Wupv"] = Win(p["ffn_up"], 0, F), Win(p["ffn_up"], F, F)
    q["cwa"], q["cwv"] = p["ffn_conv_w"][:, :F], p["ffn_conv_w"][:, F:]
    for k in ("dn_proj_w", "w_out", "ffn_down", "dn_conv_w", "mix_norm_w", "ffn_norm_w", "dn_norm_w",
              "dn_a_log", "dn_dt_bias", "s5_d"):
        q[k] = p[k]
    return q


def _s5_params(p, tag):
    G, P = p["s5_a_re"].shape
    HG = p["s5_b_re"].shape[-1]
    col = lambda a: a.reshape(G * P, 1)
    lr, li = col(p["s5_a_re"]), col(p["s5_a_im"])
    logdt = col(jnp.broadcast_to(p["s5_log_dt"][:, None], (G, P)))
    br, bi = p["s5_b_re"].reshape(G * P, HG), p["s5_b_im"].reshape(G * P, HG)
    ar, ai, bbr, bbi = s5_disc(lr, li, logdt, br, bi, "s5_disc")
    bd = lambda m: _block_diag(m.reshape(G, P, HG).transpose(0, 2, 1).astype(bf16))
    cd = lambda m: _block_diag(m.transpose(0, 2, 1).astype(bf16))
    return dict(lr=lr, li=li, logdt=logdt, br=br, bi=bi, ar=ar.reshape(1, G * P), ai=ai.reshape(1, G * P),
                Bre=bd(bbr), Bim=bd(bbi), CreT=cd(p["s5_c_re"]), mCimT=cd(-p["s5_c_im"]), G=G, P=P, HG=HG)


def layer_fwd(x, q, s5, H, payloads=(None, None, None)):
    r = {"x": x}
    h1 = rms_fwd(x, q["mix_norm_w"], "rms_mix")
    r["h1"] = h1
    u32, u16 = mm(h1, q["Wu"], "nn", "proj_u", out_dtypes=(f32, bf16), epi=lambda a: (a, a))
    qkv = mm(h1, q["Wqkv"], "nn", "proj_qkv")
    z = mm(h1, q["Wz"], "nn", "proj_z")
    ba = mm(h1, q["Wba"], "nn", "proj_ba")
    gs = mm(h1, q["Wgs"], "nn", "proj_gs")
    gd = mm(h1, q["Wgd"], "nn", "proj_gd")
    r.update(u32=u32, u16=u16, qkv=qkv, z=z, ba=ba, gs=gs, gd=gd)
    HG, P = s5["HG"], s5["P"]
    bur = mm_bd(u16, s5["Bre"], "nn", "s5_bu_re", HG, P)
    bui = mm_bd(u16, s5["Bim"], "nn", "s5_bu_im", HG, P)
    xr, xi = s5_scan_fwd(bur, bui, s5["ar"], s5["ai"], "s5_scan_fwd")
    y1 = mm_bd(xr, s5["CreT"], "nn", "s5_y_re", P, HG)
    ypre, ys5 = mm_bd(xi, s5["mCimT"], "nn", "s5_y_im", P, HG, extras=(y1, u32, q["s5_d"]), epi=_s5_y_epi, out_dtypes=(f32, bf16))
    ga = mm(ys5, q["Wga"], "nn", "glu_a")
    gb = mm(ys5, q["Wgb"], "nn", "glu_b")
    r.update(xr=xr, xi=xi, ypre=ypre, ys5=ys5, ga=ga, gb=gb)
    qn, kn, vv, bg = dn_prep(qkv, q["dn_conv_w"], ba, q["dn_a_log"], q["dn_dt_bias"], H, "dn_prep")
    L = x.shape[0]
    bgt = bg.reshape(L // DN_CHUNK, DN_CHUNK, 2 * H).transpose(0, 2, 1)
    o, ss, carried = dn_chunk_fwd(qn, kn, vv, bg, bgt, H, "dn_chunk_fwd", payloads[0])
    ydn = dn_gate(o, z, q["dn_norm_w"], H, "dn_gate")
    brdn, merged = mm(ydn, q["dn_proj_w"], "nn", "dn_proj", extras=(gs, gd, ga, gb), epi=_merge_epi, out_dtypes=(f32, bf16))
    r.update(qn=qn, kn=kn, vv=vv, bg=bg, bgt=bgt, ss=ss, o=o, ydn=ydn, brdn=brdn, merged=merged)
    x1 = mm(merged, q["w_out"], "nn", "out_proj", extras=(x,), epi=_add)
    h2 = rms_fwd(x1, q["ffn_norm_w"], "rms_ffn")
    ua = mm(h2, q["Wupa"], "nn", "ffn_up_a", payload=payloads[1])
    uv = mm(h2, q["Wupv"], "nn", "ffn_up_v", payload=payloads[2])
    if payloads[1] is not None:
        (ua, got_a), (uv, got_v) = ua, uv
        carried = [carried, got_a, got_v]
    hmid = ffn_mid(ua, uv, q["cwa"], q["cwv"], "ffn_mid")
    x2 = mm(hmid, q["ffn_down"], "nn", "ffn_down", extras=(x1,), epi=_add)
    r.update(x1=x1, h2=h2, ua=ua, uv=uv, hmid=hmid)
    return x2, r, carried


def layer_bwd(dx2, dx2b, r, q, s5, H, reduction=None):
    g = {}
    payload = None
    if reduction is None:
        dhmid = mm(dx2b, q["ffn_down"], "nt", "d_hmid")
    else:
        dhmid, theirs = mm(dx2b, q["ffn_down"], "nt", "d_hmid", payload=reduction.pair)
        payload = reduction.chips(theirs)
    g["ffn_down"] = mm(r["hmid"], dx2b, "tn", "dw_ffn_down")
    dua, duv, dwa, dwv = ffn_mid_bwd(r["ua"], r["uv"], q["cwa"], q["cwv"], dhmid, "ffn_mid_bwd")
    g["ffn_conv_w"] = jnp.concatenate([dwa, dwv], axis=1)
    dh2 = mm(dua, q["Wupa"], "nt", "d_h2_a")
    dh2 = mm(duv, q["Wupv"], "nt", "d_h2_v", extras=(dh2,), epi=_add)
    F = dua.shape[1]
    dwup = mm(r["h2"], dua, "tn", "dw_up_a", out_into=(2 * F, 0, None))
    g["ffn_up"] = mm(r["h2"], duv, "tn", "dw_up_v", out_into=(2 * F, F, dwup))
    dx1, dx1b, dffn_w = rms_bwd(r["x1"], q["ffn_norm_w"], dh2, dx2, "rms_ffn_bwd")
    g["ffn_norm_w"] = dffn_w[0]
    dm = mm(dx1b, q["w_out"], "nt", "d_merged")
    g["w_out"] = mm(r["merged"], dx1b, "tn", "dw_out")
    dgs, dgd, dga, dgb, dbrdn = merge_bwd(dm, r["gs"], r["gd"], r["ga"], r["gb"], r["brdn"], "merge_bwd")
    dydn = mm(dbrdn, q["dn_proj_w"], "nt", "d_ydn")
    g["dn_proj_w"] = mm(r["ydn"], dbrdn, "tn", "dw_dn_proj")
    do, dz, dnw = dn_gate_bwd(r["o"], r["z"], q["dn_norm_w"], dydn, H, "dn_gate_bwd")
    g["dn_norm_w"] = dnw[0]
    dq, dk, dv, dbg, dgt, carried = dn_chunk_bwd(r["qn"], r["kn"], r["vv"], r["bg"], r["bgt"], r["ss"], do, H,
                                                 "dn_chunk_bwd", payload)
    L = dq.shape[0]
    dbg = dbg + jnp.concatenate([jnp.zeros((L, H), f32), dgt.transpose(0, 2, 1).reshape(L, H)], axis=1)
    dqkv, dba, dcw, dal, ddt = dn_prep_bwd(r["qkv"], q["dn_conv_w"], r["ba"], q["dn_a_log"], q["dn_dt_bias"],
                                           dq, dk, dv, dbg, H, "dn_prep_bwd")
    g["dn_conv_w"], g["dn_a_log"], g["dn_dt_bias"] = dcw, dal[0], ddt[0]
    dys5 = mm(dga, q["Wga"], "nt", "d_ys5_a")
    dys5 = mm(dgb, q["Wgb"], "nt", "d_ys5_b", extras=(dys5,), epi=_add)
    D = dga.shape[1]
    dwglu = mm(r["ys5"], dga, "tn", "dw_glu_a", out_into=(2 * D, 0, None))
    g["s5_glu_w"] = mm(r["ys5"], dgb, "tn", "dw_glu_b", out_into=(2 * D, D, dwglu))
    dyp, du_direct, dd = s5_out_bwd(dys5, r["ypre"], r["u32"], q["s5_d"], "s5_out_bwd")
    g["s5_d"] = dd[0]
    G, P, HG = s5["G"], s5["P"], s5["HG"]
    gdr = mm_bd(dyp, s5["CreT"], "nt", "s5_gx_re", P, HG)
    gdi = mm_bd(dyp, s5["mCimT"], "nt", "s5_gx_im", P, HG)
    dcre = _diag_blocks(mm_bd(r["xr"], dyp, "tn", "dw_s5_c_re", P, HG), G, P, HG)
    dcim = _diag_blocks(mm_bd(r["xi"], dyp, "tn", "dw_s5_c_im", P, HG), G, P, HG)
    g["s5_c_re"], g["s5_c_im"] = dcre.transpose(0, 2, 1), -dcim.transpose(0, 2, 1)
    gxr, gxi, dar, dai = s5_scan_bwd(gdr, gdi, r["xr"], r["xi"], s5["ar"], s5["ai"], "s5_scan_bwd")
    dbre = _diag_blocks(mm_bd(r["u16"], gxr, "tn", "dw_s5_b_re", HG, P), G, HG, P)
    dbim = _diag_blocks(mm_bd(r["u16"], gxi, "tn", "dw_s5_b_im", HG, P), G, HG, P)
    tocol = lambda m: m.transpose(0, 2, 1).reshape(G * P, HG)
    dlr, dli, dlogdt, dbr, dbi = s5_disc_bwd(s5["lr"], s5["li"], s5["logdt"], s5["br"], s5["bi"],
                                             dar.reshape(G * P, 1), dai.reshape(G * P, 1), tocol(dbre), tocol(dbim), "s5_disc_bwd")
    g["s5_a_re"], g["s5_a_im"] = dlr.reshape(G, P), dli.reshape(G, P)
    g["s5_log_dt"] = jnp.sum(dlogdt.reshape(G, P), axis=1)
    g["s5_b_re"], g["s5_b_im"] = dbr.reshape(G, P, HG), dbi.reshape(G, P, HG)
    du = mm_bd(gxr, s5["Bre"], "nt", "d_u_re", HG, P)
    du = mm_bd(gxi, s5["Bim"], "nt", "d_u_im", HG, P, extras=(du, du_direct), epi=lambda a, b, c: (a + b + c,), out_dtypes=(bf16,))
    h1 = r["h1"]
    segs = [("Wu", du), ("Wqkv", dqkv), ("Wz", dz), ("Wba", dba), ("Wgs", dgs), ("Wgd", dgd)]
    dh1 = None
    dws = []
    for name, dseg in segs:
        if dh1 is None:
            dh1 = mm(dseg, q[name], "nt", "d_h1_" + name)
        else:
            dh1 = mm(dseg, q[name], "nt", "d_h1_" + name, extras=(dh1,), epi=_add)
        dw = mm(h1, dseg, "tn", "dw_in_" + name)
        dws.append(dw[:, :2 * H] if name == "Wba" else dw)
    g["w_in"] = jnp.concatenate(dws, axis=1)
    dx, dxb, dmix = rms_bwd(r["x"], q["mix_norm_w"], dh1, dx1, "rms_mix_bwd")
    g["mix_norm_w"] = dmix[0]
    return dx, dxb, g, carried


BIG = ("w_in", "s5_glu_w", "dn_proj_w", "w_out", "ffn_up", "ffn_down")
SHARDED_SMALL = ("dn_conv_w", "ffn_conv_w")
COL_SHARDED = ("w_in", "s5_glu_w", "dn_proj_w", "ffn_up", "dn_conv_w", "ffn_conv_w")
REPL = ("mix_norm_w", "s5_log_dt", "s5_a_re", "s5_a_im", "s5_b_re", "s5_b_im", "s5_c_re", "s5_c_im", "s5_d",
        "dn_a_log", "dn_dt_bias", "dn_norm_w", "ffn_norm_w")
WEIGHTS = ['mix_norm_w', 'w_in', 's5_log_dt', 's5_a_re', 's5_a_im', 's5_b_re', 's5_b_im', 's5_c_re', 's5_c_im', 's5_d',
           's5_glu_w', 'dn_conv_w', 'dn_a_log', 'dn_dt_bias', 'dn_norm_w', 'dn_proj_w', 'w_out', 'ffn_norm_w', 'ffn_up',
           'ffn_conv_w', 'ffn_down', 'final_norm_w']


def _join_shards(name, shards):
    return jnp.concatenate(shards, axis=-1 if name in COL_SHARDED else -2)


KINDS = {"w_in": "slot", "s5_glu_w": "col", "dn_proj_w": "col", "w_out": "slot", "ffn_up": "col", "ffn_down": "slot"}


BIG_KINDS = [KINDS[n] for n in BIG]


AG_GROUPS = (("w_in", "s5_glu_w", "dn_proj_w", "w_out"), ("ffn_up",), ("ffn_down",))


def gather_begin(shards, names):
    return ag_ici_payload([place_block(shards[n], KINDS[n], "ag_place_" + n) for n in names], [KINDS[n] for n in names])


def gather_finish(bufs):
    full = dict(zip(BIG, ag_d2d([bufs[n] for n in BIG], BIG_KINDS, "ag_d2d")))
    for n in ("w_out", "ffn_down"):
        full[n] = full[n].reshape(-1, full[n].shape[-1])
    w = full["w_in"]
    full["w_in"] = w.transpose(1, 0, 2).reshape(w.shape[1], -1)
    return full


def gather_small_sharded(shards):
    names = SHARDED_SMALL
    shapes = [shards[n].shape for n in names]
    got = chip_exchange(_flat_pack([shards[n] for n in names], f32, lanes=128, row_mult=8), "ag_small", same=True)
    per_chip = [_flat_unpack(got[j], shapes) for j in range(N_CHIPS)]
    return {n: _join_shards(n, [per_chip[j][k] for j in range(N_CHIPS)]) for k, n in enumerate(names)}


class Reduction:
    def __init__(self, tens):
        self.tens = tens
        self.pair = rs_pair_payload(tens, BIG_KINDS)

    def chips(self, theirs):
        sums = [add_half(a, t, k, "rs_pair_sum_" + n) for n, a, t, k in zip(BIG, self.tens, theirs, BIG_KINDS)]
        return rs_chips_payload(sums, BIG_KINDS)


def reduce_begin(g):
    tens = []
    for n in BIG:
        a = g[n]
        if n == "w_in":
            a = a.reshape(a.shape[0], N_CHIPS, -1).transpose(1, 0, 2)
        elif KINDS[n] == "slot":
            a = a.reshape(N_CHIPS, -1, a.shape[-1])
        tens.append(a)
    return Reduction(tens)


def reduce_finish(parts):
    halves = [sum_slots_half(x, "rs_chip_sum_" + n) for n, x in zip(BIG, parts)]
    return {n: a.reshape(-1, a.shape[-1]) for n, a in zip(BIG, rs_share(halves, "rs_share"))}


def all_reduce_small(arrs):
    shapes = [a.shape for a in arrs]
    pack = _flat_pack(arrs, f32, lanes=1024, row_mult=64)
    from_chips = chip_exchange(pack, "ar_chips", same=True)
    from_sib = sibling_exchange(from_chips, "ar_sibling")
    c = lax.axis_index("c")
    both = jnp.concatenate([jnp.where(c == 0, from_chips, from_sib), jnp.where(c == 0, from_sib, from_chips)], axis=0)
    return _flat_unpack(sum_slots(both, "ar_sum"), shapes)


def kernel(x, mix_norm_w, w_in, s5_log_dt, s5_a_re, s5_a_im, s5_b_re, s5_b_im, s5_c_re, s5_c_im, s5_d, s5_glu_w, dn_conv_w, dn_a_log, dn_dt_bias, dn_norm_w, dn_proj_w, w_out, ffn_norm_w, ffn_up, ffn_conv_w, ffn_down, final_norm_w, loss_target, m_mix_norm_w, m_w_in, m_s5_log_dt, m_s5_a_re, m_s5_a_im, m_s5_b_re, m_s5_b_im, m_s5_c_re, m_s5_c_im, m_s5_d, m_s5_glu_w, m_dn_conv_w, m_dn_a_log, m_dn_dt_bias, m_dn_norm_w, m_dn_proj_w, m_w_out, m_ffn_norm_w, m_ffn_up, m_ffn_conv_w, m_ffn_down, m_final_norm_w, v_mix_norm_w, v_w_in, v_s5_log_dt, v_s5_a_re, v_s5_a_im, v_s5_b_re, v_s5_b_im, v_s5_c_re, v_s5_c_im, v_s5_d, v_s5_glu_w, v_dn_conv_w, v_dn_a_log, v_dn_dt_bias, v_dn_norm_w, v_dn_proj_w, v_w_out, v_ffn_norm_w, v_ffn_up, v_ffn_conv_w, v_ffn_down, v_final_norm_w):
    args = locals()
    W = {n: args[n] for n in WEIGHTS}
    M = {n: args["m_" + n] for n in WEIGHTS}
    V = {n: args["v_" + n] for n in WEIGHTS}
    depth = mix_norm_w.shape[0]
    H = dn_a_log.shape[1]
    xs = x[0]
    target = loss_target[0]

    conv_full = gather_small_sharded({n: W[n] for n in SHARDED_SMALL})

    def layer_params(l, gathered):
        p = gather_finish(gathered)
        for n in REPL:
            p[n] = W[n][l]
        for n in SHARDED_SMALL:
            p[n] = conv_full[n][l]
        for n in ("mix_norm_w", "ffn_norm_w", "dn_norm_w", "dn_a_log", "dn_dt_bias", "s5_d"):
            p[n] = p[n].reshape(1, -1)
        return _layer_weights(p, H), _s5_params(p, l)

    def named(groups_results):
        return {n: buf for names, bufs in zip(AG_GROUPS, groups_results) for n, buf in zip(names, bufs)}

    layers, res = [], []
    begun = [gather_begin({n: W[n][0] for n in BIG}, names) for names in AG_GROUPS]
    gathered = [run_payload(p, "ag_ici_%d" % k) for k, p in enumerate(begun)]
    for l in range(depth):
        layers.append(layer_params(l, named(gathered)))
        nxt = (None, None, None)
        if l + 1 < depth:
            nxt = [gather_begin({n: W[n][l + 1] for n in BIG}, names) for names in AG_GROUPS]
        xs, r, gathered = layer_fwd(xs, layers[l][0], layers[l][1], H, nxt)
        res.append(r)
    dx, dxb, loss_part, dfinal = loss_head(xs, final_norm_w.reshape(1, -1), target, "loss_head")
    loss = lax.psum(loss_part[0, 0], ("x", "y", "c"))

    grads, sharded = [None] * depth, [None] * depth
    pending = None
    for l in reversed(range(depth)):
        dx, dxb, grads[l], parts = layer_bwd(dx, dxb, res[l], layers[l][0], layers[l][1], H, pending)
        if pending is not None:
            sharded[l + 1] = reduce_finish(parts)
        pending = reduce_begin(grads[l])
    theirs = run_payload(pending.pair, "rs_pair")
    sharded[0] = reduce_finish(run_payload(pending.chips(theirs), "rs_chips"))
    grad_x = dx[None]

    G = {}
    for n in BIG:
        G[n] = jnp.stack([sharded[l][n] for l in range(depth)])
    small_names = REPL + SHARDED_SMALL
    small = [jnp.stack([grads[l][n] for l in range(depth)]) for n in small_names] + [dfinal[0]]
    for n, a in zip(small_names + ("final_norm_w",), all_reduce_small(small)):
        G[n] = a
    chip = 2 * lax.axis_index("x") + lax.axis_index("y")
    for n in SHARDED_SMALL:
        cs = W[n].shape[-1]
        G[n] = lax.dynamic_slice_in_dim(G[n], chip * cs, cs, axis=-1)

    delta, new_m, new_v = {}, {}, {}
    for n in WEIGHTS:
        shape = W[n].shape
        size = math.prod(shape)
        if n in BIG + SHARDED_SMALL:
            two_d = (size // shape[-1], shape[-1])
        else:
            two_d = (size // 128, 128) if size % 128 == 0 else (1, size)
        d, nm, nv = adamw(W[n].reshape(two_d), G[n].reshape(two_d), M[n].reshape(two_d), V[n].reshape(two_d), "adamw_" + n)
        delta[n], new_m[n], new_v[n] = d.reshape(shape), nm.reshape(shape), nv.reshape(shape)
        G[n] = G[n].reshape(shape)
    return (loss, grad_x, *[G[n] for n in WEIGHTS], *[delta[n] for n in WEIGHTS],
            *[new_m[n] for n in WEIGHTS], *[new_v[n] for n in WEIGHTS])
```

```python
import functools
import math

import jax
import jax.numpy as jnp
from jax import lax
from jax.experimental import pallas as pl
from jax.experimental.pallas import tpu as pltpu

f32 = jnp.float32
bf16 = jnp.bfloat16
S = jax.ShapeDtypeStruct

NORM_EPS = 1e-6
DN_CHUNK = 64
S5_GROUP = 16
ADAM_LR, ADAM_B1, ADAM_B2, ADAM_EPS, ADAM_WD, ADAM_STEP = 0.001, 0.9, 0.999, 1e-08, 0.01, 10
VMEM_LIMIT_BYTES = 56 * 1024 * 1024
HALO = 8
MESH = pl.DeviceIdType.MESH
N_CHIPS = 4


def _pick(n, cands):
    for c in cands:
        if n % c == 0:
            return c
    return n


MM_VMEM_BUDGET = 40 * 1024 * 1024
MM_MAX_TK = 2816


def _pick_k(K):
    if K <= MM_MAX_TK or K % 128:
        return K
    return max(d for d in range(128, MM_MAX_TK + 1, 128) if K % d == 0)


MM_MAX_TN = 1536


def _pick_n(N):
    if N <= 1024 or N % 128:
        return N
    return max(d for d in range(128, MM_MAX_TN + 1, 128) if N % d == 0)


def _cparams(sem):
    return pltpu.CompilerParams(dimension_semantics=sem, vmem_limit_bytes=VMEM_LIMIT_BYTES)


_DIMS = {"nn": ((1,), (0,)), "nt": ((1,), (1,)), "tn": ((0,), (0,))}


class Win:
    def __init__(self, arr, c0, nc):
        self.arr, self.c0, self.nc = arr, c0, nc


def mm(a, b, mode, name, extras=(), epi=None, out_dtypes=(f32,), out_into=None, payload=None):
    barr, c0 = (b.arr, b.c0) if isinstance(b, Win) else (b, 0)
    if mode == "tn":
        K, M = a.shape
    else:
        M, K = a.shape
    if mode == "nt":
        N, K2 = barr.shape
        K2 = b.nc if isinstance(b, Win) else K2
        n_off, k_off = 0, c0
    else:
        K2, N = barr.shape
        N = b.nc if isinstance(b, Win) else N
        n_off, k_off = c0, 0
    assert K == K2, (a.shape, barr.shape, mode)
    o_tot, o_off, o_alias = out_into if out_into is not None else (N, 0, None)
    tm = _pick_n(M)
    tn = _pick_n(math.gcd(math.gcd(N, n_off), o_off))
    tk = _pick_k(math.gcd(K, k_off))

    def vmem_estimate(tm_):
        tiles = tm_ * tk * a.dtype.itemsize + tk * tn * barr.dtype.itemsize
        tiles += sum(tm_ * tn * e.dtype.itemsize for e in extras if e.shape == (M, N))
        tiles += sum(tm_ * tn * jnp.dtype(dt).itemsize for dt in out_dtypes)
        return 2 * tiles + 3 * tm_ * tn * 4

    while vmem_estimate(tm) > MM_VMEM_BUDGET:
        smaller = [d for d in range(128, tm, 128) if M % d == 0]
        if not smaller:
            break
        tm = smaller[-1]
    nk = K // tk
    assert M % tm == 0 and N % tn == 0 and K % tk == 0 and n_off % tn == 0 and k_off % tk == 0 and o_off % tn == 0
    nb, kb, ob = n_off // tn, k_off // tk, o_off // tn
    if mode == "tn":
        a_spec = pl.BlockSpec((tk, tm), lambda j, i, k: (k, i))
    else:
        a_spec = pl.BlockSpec((tm, tk), lambda j, i, k: (i, k))
    if mode == "nt":
        b_spec = pl.BlockSpec((tn, tk), lambda j, i, k: (j, k + kb))
    else:
        b_spec = pl.BlockSpec((tk, tn), lambda j, i, k: (k, j + nb))
    ex_specs = []
    for e in extras:
        if e.shape == (M, N):
            ex_specs.append(pl.BlockSpec((tm, tn), lambda j, i, k: (i, j)))
        elif e.shape == (1, N):
            ex_specs.append(pl.BlockSpec((1, tn), lambda j, i, k: (0, j)))
        elif e.shape == (M, 1):
            ex_specs.append(pl.BlockSpec((tm, 1), lambda j, i, k: (i, 0)))
        else:
            raise ValueError((e.shape, M, N))
    ne, no = len(extras), len(out_dtypes)
    na = 1 if o_alias is not None else 0
    assert out_into is None or no == 1
    dims = (_DIMS[mode], ((), ()))
    nj, ni = N // tn, M // tm
    split, hooks, extra = _host(payload, 2 + ne + na, no, 1 if nk > 1 else 0)

    def body(*refs):
        own, theirs = split(refs)
        a_ref, b_ref, rest = own[0], own[1], own[2:]
        ex, outs = rest[:ne], rest[ne + na:ne + na + no]
        j, i, k = pl.program_id(0), pl.program_id(1), pl.program_id(2)
        hooks(theirs, (j == 0) & (i == 0) & (k == 0), (j == nj - 1) & (i == ni - 1) & (k == nk - 1))
        p = lax.dot_general(a_ref[...].astype(bf16), b_ref[...].astype(bf16), dims, preferred_element_type=f32)

        def finish(acc):
            res = epi(acc, *[e[...] for e in ex]) if epi is not None else (acc,)
            for o, r in zip(outs, res):
                o[...] = r.astype(o.dtype)

        if nk == 1:
            finish(p)
        else:
            acc_ref = rest[-1]

            @pl.when(k == 0)
            def _():
                acc_ref[...] = p

            @pl.when(k > 0)
            def _():
                acc_ref[...] += p

            @pl.when(k == nk - 1)
            def _():
                finish(acc_ref[...])

    aliases = dict(extra["aliases"])
    if na:
        aliases[2 + ne] = 0
    res = pl.pallas_call(
        body,
        name=name,
        grid=(nj, ni, nk),
        in_specs=[a_spec, b_spec] + ex_specs + [pl.BlockSpec(memory_space=pl.ANY)] * na + extra["in_specs"],
        out_specs=[pl.BlockSpec((tm, tn), lambda j, i, k: (i, j + ob)) for _ in out_dtypes] + extra["out_specs"],
        out_shape=[S((M, o_tot), dt) for dt in out_dtypes] + extra["out_shape"],
        scratch_shapes=([pltpu.VMEM((tm, tn), f32)] if nk > 1 else []) + extra["scratch"],
        input_output_aliases=aliases,
        compiler_params=_cparams(("arbitrary",) * 3 if payload is not None else ("parallel", "parallel", "arbitrary")),
    )(a, barr, *extras, *([o_alias] if na else []), *extra["arrays"])
    outs = res[:no]
    if payload is not None:
        return tuple(outs) + (list(res[no:]),)
    return outs[0] if no == 1 else tuple(outs)


def mm_bd(a, b, mode, name, ga, gb, extras=(), epi=None, out_dtypes=(f32,)):
    T = max(1, min(256 // min(ga, gb), 1024 // max(ga, gb)))
    if mode == "tn":
        K, M = a.shape
        N = b.shape[1]
        G = M // ga
        T = min(T, G)
        tm, tn, tk = T * ga, T * gb, _pick_k(K)
        nk = K // tk
        grid = (G // T, 1, nk)
        a_spec = pl.BlockSpec((tk, tm), lambda j, i, k: (k, j))
        b_spec = pl.BlockSpec((tk, tn), lambda j, i, k: (k, j))
        o_spec = pl.BlockSpec((tm, tn), lambda j, i, k: (j, 0))
        out_shape = (M, tn)
    else:
        M = a.shape[0]
        if mode == "nn":
            G = b.shape[0] // ga
            T = min(T, G)
            kw, tn, N = T * ga, T * gb, G * gb
            b_spec = pl.BlockSpec((kw, tn), lambda j, i, k: (j, j))
        else:
            G = b.shape[0] // ga
            T = min(T, G)
            kw, tn, N = T * gb, T * ga, G * ga
            b_spec = pl.BlockSpec((tn, kw), lambda j, i, k: (j, j))
        tm = _pick(M, (1024, 512, 256, 128, 64, 32, 16, 8))
        nk = 1
        grid = (G // T, M // tm, 1)
        a_spec = pl.BlockSpec((tm, kw), lambda j, i, k: (i, j))
        o_spec = pl.BlockSpec((tm, tn), lambda j, i, k: (i, j))
        out_shape = (M, N)
    ex_specs = []
    for e in extras:
        if e.shape == out_shape:
            ex_specs.append(o_spec)
        elif e.shape == (1, out_shape[1]):
            ex_specs.append(pl.BlockSpec((1, tn), lambda j, i, k: (0, j)))
        else:
            raise ValueError((e.shape, out_shape))
    ne, no = len(extras), len(out_dtypes)
    dims = (_DIMS[mode], ((), ()))

    def body(a_ref, b_ref, *rest):
        ex, outs = rest[:ne], rest[ne:ne + no]
        p = lax.dot_general(a_ref[...].astype(bf16), b_ref[...].astype(bf16), dims, preferred_element_type=f32)

        def finish(acc):
            res = epi(acc, *[e[...] for e in ex]) if epi is not None else (acc,)
            for o, r in zip(outs, res):
                o[...] = r.astype(o.dtype)

        if nk == 1:
            finish(p)
        else:
            acc_ref = rest[-1]
            k = pl.program_id(2)

            @pl.when(k == 0)
            def _():
                acc_ref[...] = p

            @pl.when(k > 0)
            def _():
                acc_ref[...] += p

            @pl.when(k == nk - 1)
            def _():
                finish(acc_ref[...])

    outs = pl.pallas_call(
        body, name=name, grid=grid, in_specs=[a_spec, b_spec] + ex_specs, out_specs=[o_spec] * no,
        out_shape=[S(out_shape, dt) for dt in out_dtypes],
        scratch_shapes=[pltpu.VMEM((tm, tn), f32)] if nk > 1 else [],
        compiler_params=_cparams(("parallel", "parallel", "arbitrary")),
    )(a, b, *extras)
    return outs[0] if no == 1 else tuple(outs)


def _diag_blocks(tiles, G, ga, gb):
    T = tiles.shape[1] // gb
    t5 = tiles.reshape(G // T, T, ga, T, gb)
    return jnp.sum(t5 * jnp.eye(T, dtype=tiles.dtype)[None, :, None, :, None], axis=3).reshape(G, ga, gb)


def _add(acc, prev):
    return (acc + prev,)


def rowk(name, fn, L, tl, ncol, ins, outs, accs=()):
    nrow = L // tl
    assert L % tl == 0 and tl % HALO == 0
    hb = tl // HALO

    def cw_of(c_total):
        assert c_total % ncol == 0, (name, c_total, ncol)
        return c_total // ncol

    in_specs = []
    for arr, kind in ins:
        if kind == "rows":
            in_specs.append(pl.BlockSpec((tl, cw_of(arr.shape[1])), lambda j, i: (i, j)))
        elif kind == "prev":
            in_specs.append(pl.BlockSpec((HALO, cw_of(arr.shape[1])), lambda j, i: (jnp.maximum(i * hb - 1, 0), j)))
        elif kind == "next":
            in_specs.append(pl.BlockSpec((HALO, cw_of(arr.shape[1])), lambda j, i: (jnp.minimum((i + 1) * hb, nrow * hb - 1), j)))
        elif kind == "cols":
            in_specs.append(pl.BlockSpec((arr.shape[0], cw_of(arr.shape[1])), lambda j, i: (0, j)))
        elif kind == "const":
            in_specs.append(pl.BlockSpec(arr.shape, lambda j, i: (0,) * arr.ndim))
        else:
            raise ValueError(kind)
    out_specs = [pl.BlockSpec((tl, cw_of(c)), lambda j, i: (i, j)) for c, _ in outs]
    out_shape = [S((L, c), dt) for c, dt in outs]
    out_specs += [pl.BlockSpec((r, cw_of(c)), lambda j, i: (0, j)) for r, c, _ in accs]
    out_shape += [S((r, c), dt) for r, c, dt in accs]
    ni, no, na = len(ins), len(outs), len(accs)

    def body(*refs):
        i = pl.program_id(1)
        res = fn(i, nrow, *[r[...] for r in refs[:ni]])
        for o, r in zip(refs[ni:ni + no], res[:no]):
            o[...] = r.astype(o.dtype)
        for o, r in zip(refs[ni + no:ni + no + na], res[no:]):
            @pl.when(i == 0)
            def _(o=o, r=r):
                o[...] = r.astype(o.dtype)

            @pl.when(i > 0)
            def _(o=o, r=r):
                o[...] += r.astype(o.dtype)

    res = pl.pallas_call(
        body,
        name=name,
        grid=(ncol, nrow),
        in_specs=in_specs,
        out_specs=out_specs,
        out_shape=out_shape,
        compiler_params=_cparams(("parallel", "arbitrary")),
    )(*[a for a, _ in ins])
    return tuple(res)


def _sigmoid(x):
    return 1.0 / (1.0 + jnp.exp(-x))


def _silu(x):
    return x * _sigmoid(x)


def _dsilu(x):
    s = _sigmoid(x)
    return s * (1.0 + x * (1.0 - s))


def _erf(x):
    a = jnp.abs(x)
    t = 1.0 / (1.0 + 0.3275911 * a)
    poly = t * (0.254829592 + t * (-0.284496736 + t * (1.421413741 + t * (-1.453152027 + t * 1.061405429))))
    y = 1.0 - poly * jnp.exp(-a * a)
    return jnp.where(x < 0, -y, y)


def _gelu(x):
    return 0.5 * x * (1.0 + _erf(x * (2.0 ** -0.5)))


def _dgelu(x):
    cdf = 0.5 * (1.0 + _erf(x * (2.0 ** -0.5)))
    pdf = jnp.exp(-0.5 * x * x) * (1.0 / math.sqrt(2.0 * math.pi))
    return cdf + x * pdf


def _rms(x, w):
    return x * lax.rsqrt(jnp.mean(x * x, axis=-1, keepdims=True) + NORM_EPS) * w


def _rms_bwd(x, w, dy):
    d = x.shape[-1]
    r = lax.rsqrt(jnp.mean(x * x, axis=-1, keepdims=True) + NORM_EPS)
    wdy = w * dy
    dx = r * wdy - x * (r * r * r) * (jnp.sum(x * wdy, axis=-1, keepdims=True) / d)
    dw = jnp.sum(x * r * dy, axis=0, keepdims=True)
    return dx, dw


def _from_cols(cols, width):
    tl = cols[0].shape[0]
    lane = lax.broadcasted_iota(jnp.int32, (tl, width), 1)
    out = jnp.zeros((tl, width), f32)
    for n, col in enumerate(cols):
        out = jnp.where(lane == n, col, out)
    return out


def _from_rows(rows):
    c = rows[0].shape[1]
    sub = lax.broadcasted_iota(jnp.int32, (len(rows), c), 0)
    out = jnp.zeros((len(rows), c), f32)
    for n, row in enumerate(rows):
        out = jnp.where(sub == n, row, out)
    return out


def _shift_down(x, halo, s, first):
    if s == 0:
        return x
    tl = x.shape[0]
    halo = jnp.where(first, 0.0, halo)
    xx = jnp.concatenate([halo, x], axis=0)
    return pltpu.roll(xx, s, 0)[HALO:HALO + tl]


def _shift_up(x, halo, s, last):
    if s == 0:
        return x
    tl = x.shape[0]
    halo = jnp.where(last, 0.0, halo)
    xx = jnp.concatenate([x, halo], axis=0)
    return pltpu.roll(xx, tl + HALO - s, 0)[0:tl]


def _causal_conv(x, halo, w, first):
    kw = w.shape[0]
    shifted = [_shift_down(x, halo, kw - 1 - j, first) for j in range(kw)]
    out = shifted[0] * w[0:1]
    for j in range(1, kw):
        out = out + shifted[j] * w[j:j + 1]
    return out, shifted


def rms_fwd(x, w, name):
    L, D = x.shape

    def fn(i, n, xb, wb):
        return (_rms(xb, wb),)

    return rowk(name, fn, L, _pick(L, (256, 128, 64, 32, 16, 8)), 1, [(x, "rows"), (w, "const")], [(D, bf16)])[0]


def rms_bwd(x, w, dh, dres, name):
    L, D = x.shape

    def fn(i, n, xb, wb, dhb, drb):
        dx, dw = _rms_bwd(xb, wb, dhb)
        dx = dx + drb
        return dx, dx, dw

    return rowk(name, fn, L, _pick(L, (256, 128, 64, 32, 16, 8)), 1,
                [(x, "rows"), (w, "const"), (dh, "rows"), (dres, "rows")], [(D, f32), (D, bf16)], [(1, D, f32)])


def loss_head(x, w, target, name):
    L, D = x.shape

    def fn(i, n, xb, wb, tb):
        err = _rms(xb, wb) - tb
        loss = 0.5 * jnp.sum(err * err) / D
        dx, dw = _rms_bwd(xb, wb, err / D)
        return dx, dx, jnp.full((8, 128), loss, f32), dw

    return rowk(name, fn, L, _pick(L, (256, 128, 64, 32, 16, 8)), 1,
                [(x, "rows"), (w, "const"), (target, "rows")], [(D, f32), (D, bf16)], [(8, 128, f32), (1, D, f32)])


def _s5_disc_math(lr, li, logdt, br, bi):
    dt = jnp.exp(logdt)
    mag = jnp.exp(lr * dt)
    ar, ai = mag * jnp.cos(li * dt), mag * jnp.sin(li * dt)
    den = lr * lr + li * li
    nr, ni = ar - 1.0, ai
    cr = (nr * lr + ni * li) / den
    ci = (ni * lr - nr * li) / den
    return ar, ai, cr * br - ci * bi, cr * bi + ci * br


def _disc_call(body, name, ins, out_widths):
    GP = ins[0].shape[0]
    tl = _pick(GP, (512, 256, 128, 64, 32, 16, 8))
    spec = lambda w: pl.BlockSpec((tl, w), lambda i: (i, 0))
    return pl.pallas_call(body, name=name, grid=(GP // tl,),
                          in_specs=[spec(a.shape[1]) for a in ins], out_specs=[spec(w) for w in out_widths],
                          out_shape=[S((GP, w), f32) for w in out_widths], compiler_params=_cparams(("parallel",)))(*ins)


def s5_disc(lr, li, logdt, br, bi, name):
    HG = br.shape[1]

    def body(lr_ref, li_ref, dt_ref, br_ref, bi_ref, ar_ref, ai_ref, bbr_ref, bbi_ref):
        ar, ai, bbr, bbi = _s5_disc_math(lr_ref[...], li_ref[...], dt_ref[...], br_ref[...], bi_ref[...])
        ar_ref[...], ai_ref[...], bbr_ref[...], bbi_ref[...] = ar, ai, bbr, bbi

    return _disc_call(body, name, [lr, li, logdt, br, bi], [1, 1, HG, HG])


def s5_disc_bwd(lr, li, logdt, br, bi, dar, dai, dbbr, dbbi, name):
    HG = br.shape[1]

    def body(lr_ref, li_ref, dt_ref, br_ref, bi_ref, dar_ref, dai_ref, dbbr_ref, dbbi_ref, *outs):
        _, vjp = jax.vjp(_s5_disc_math, lr_ref[...], li_ref[...], dt_ref[...], br_ref[...], bi_ref[...])
        for o, g in zip(outs, vjp((dar_ref[...], dai_ref[...], dbbr_ref[...], dbbi_ref[...]))):
            o[...] = g

    return _disc_call(body, name, [lr, li, logdt, br, bi, dar, dai, dbbr, dbbi], [1, 1, 1, HG, HG])


SCAN_TB = 256


def s5_scan_fwd(bur, bui, ar, ai, name):
    L, GP = bur.shape
    cw = _pick(GP, (2048, 1024, 512, 256, 128))
    tb = _pick(L, (SCAN_TB, 128, 64, 32, 16, 8))

    def body(bur_ref, bui_ref, ar_ref, ai_ref, xr_ref, xi_ref, cr_ref, ci_ref):
        @pl.when(pl.program_id(1) == 0)
        def _():
            cr_ref[...] = jnp.zeros_like(cr_ref)
            ci_ref[...] = jnp.zeros_like(ci_ref)

        a_r, a_i = ar_ref[...], ai_ref[...]

        def step(t, carry):
            xr, xi = carry
            row = pl.ds(t, 1)
            nr = a_r * xr - a_i * xi + bur_ref[row, :]
            ni = a_r * xi + a_i * xr + bui_ref[row, :]
            xr_ref[row, :] = nr
            xi_ref[row, :] = ni
            return nr, ni

        xr, xi = lax.fori_loop(0, tb, step, (cr_ref[...], ci_ref[...]), unroll=8)
        cr_ref[...] = xr
        ci_ref[...] = xi

    blk = pl.BlockSpec((tb, cw), lambda j, i: (i, j))
    vec = pl.BlockSpec((1, cw), lambda j, i: (0, j))
    return pl.pallas_call(
        body, name=name, grid=(GP // cw, L // tb),
        in_specs=[blk, blk, vec, vec], out_specs=[blk, blk],
        out_shape=[S((L, GP), f32), S((L, GP), f32)],
        scratch_shapes=[pltpu.VMEM((1, cw), f32), pltpu.VMEM((1, cw), f32)],
        compiler_params=_cparams(("parallel", "arbitrary")),
    )(bur, bui, ar, ai)


def s5_scan_bwd(gr, gi, xr, xi, ar, ai, name):
    L, GP = gr.shape
    cw = _pick(GP, (2048, 1024, 512, 256, 128))
    tb = _pick(L, (SCAN_TB, 128, 64, 32, 16, 8))
    nt = L // tb

    def body(gr_ref, gi_ref, xr_ref, xi_ref, ar_ref, ai_ref, gxr_ref, gxi_ref, dar_ref, dai_ref, cr_ref, ci_ref):
        @pl.when(pl.program_id(1) == 0)
        def _():
            cr_ref[...] = jnp.zeros_like(cr_ref)
            ci_ref[...] = jnp.zeros_like(ci_ref)
            dar_ref[...] = jnp.zeros_like(dar_ref)
            dai_ref[...] = jnp.zeros_like(dai_ref)

        a_r, a_i = ar_ref[...], ai_ref[...]

        def step(s, carry):
            cr, ci, dr, di = carry
            row = pl.ds(tb - 1 - s, 1)
            x_r, x_i = xr_ref[row, :], xi_ref[row, :]
            dr = dr + cr * x_r + ci * x_i
            di = di + ci * x_r - cr * x_i
            nr = gr_ref[row, :] + a_r * cr + a_i * ci
            ni = gi_ref[row, :] + a_r * ci - a_i * cr
            gxr_ref[row, :] = nr
            gxi_ref[row, :] = ni
            return nr, ni, dr, di

        cr, ci, dr, di = lax.fori_loop(0, tb, step, (cr_ref[...], ci_ref[...], dar_ref[...], dai_ref[...]), unroll=8)
        cr_ref[...] = cr
        ci_ref[...] = ci
        dar_ref[...] = dr
        dai_ref[...] = di

    blk = pl.BlockSpec((tb, cw), lambda j, i: (nt - 1 - i, j))
    vec = pl.BlockSpec((1, cw), lambda j, i: (0, j))
    return pl.pallas_call(
        body, name=name, grid=(GP // cw, nt),
        in_specs=[blk, blk, blk, blk, vec, vec], out_specs=[blk, blk, vec, vec],
        out_shape=[S((L, GP), f32), S((L, GP), f32), S((1, GP), f32), S((1, GP), f32)],
        scratch_shapes=[pltpu.VMEM((1, cw), f32), pltpu.VMEM((1, cw), f32)],
        compiler_params=_cparams(("parallel", "arbitrary")),
    )(gr, gi, xr, xi, ar, ai)


def _dn_heads_math(cq, ck, cv, braw, araw, alog, dtb, dk):
    q, k, v = _silu(cq), _silu(ck), _silu(cv)
    q = q * lax.rsqrt(jnp.sum(q * q, axis=-1, keepdims=True) + NORM_EPS) * (dk ** -0.5)
    k = k * lax.rsqrt(jnp.sum(k * k, axis=-1, keepdims=True) + NORM_EPS)
    beta = _sigmoid(braw)
    g = -jnp.exp(alog) * jax.nn.softplus(araw + dtb)
    return q, k, v, beta, g


def dn_prep(qkv, convw, ba, alog, dtb, H, name):
    L, W = qkv.shape
    hk = W // 3
    dk = hk // H

    def fn(i, n, xb, hb, wb, bab, alb, dtbb):
        c, _ = _causal_conv(xb, hb, wb, i == 0)
        qs, ks, vs, bs, gs = [], [], [], [], []
        for h in range(H):
            sl = lambda o: c[:, o + h * dk:o + (h + 1) * dk]
            q, k, v, beta, g = _dn_heads_math(sl(0), sl(hk), sl(2 * hk), bab[:, h:h + 1], bab[:, H + h:H + h + 1],
                                              alb[:, h:h + 1], dtbb[:, h:h + 1], dk)
            qs.append(q), ks.append(k), vs.append(v), bs.append(beta), gs.append(g)
        cat = lambda xs: jnp.concatenate(xs, axis=1)
        return cat(qs), cat(ks), cat(vs), _from_cols(bs + gs, 2 * H)

    return rowk(name, fn, L, _pick(L, (128, 64, 32, 16, 8)), 1,
                [(qkv, "rows"), (qkv, "prev"), (convw, "const"), (ba, "rows"), (alog, "const"), (dtb, "const")],
                [(hk, f32), (hk, f32), (hk, f32), (2 * H, f32)])


def dn_prep_bwd(qkv, convw, ba, alog, dtb, dq, dk_, dv, dbg, H, name):
    L, W = qkv.shape
    hk = W // 3
    dk = hk // H
    kw = convw.shape[0]
    nba = ba.shape[1]

    def fn(i, n, xb, hb, wb, bab, alb, dtbb, dqb, dkb, dvb, dbgb):
        c, shifted = _causal_conv(xb, hb, wb, i == 0)
        dcs = [None] * (3 * H)
        dbr, dar, dal, ddt = [], [], [], []
        for h in range(H):
            sl = lambda a, o: a[:, o + h * dk:o + (h + 1) * dk]
            args = (sl(c, 0), sl(c, hk), sl(c, 2 * hk), bab[:, h:h + 1], bab[:, H + h:H + h + 1],
                    alb[:, h:h + 1], dtbb[:, h:h + 1])
            _, vjp = jax.vjp(lambda *a: _dn_heads_math(*a, dk), *args)
            g = vjp((sl(dqb, 0), sl(dkb, 0), sl(dvb, 0), dbgb[:, h:h + 1], dbgb[:, H + h:H + h + 1]))
            dcs[h], dcs[H + h], dcs[2 * H + h] = g[0], g[1], g[2]
            dbr.append(g[3]), dar.append(g[4]), dal.append(g[5]), ddt.append(g[6])
        dc = jnp.concatenate(dcs, axis=1)
        dba = _from_cols(dbr + dar, nba)
        dw = _from_rows([jnp.sum(dc * shifted[j], axis=0, keepdims=True) for j in range(kw)])
        return dc, dba, dw, _from_cols(dal, H), _from_cols(ddt, H)

    return rowk(name, fn, L, _pick(L, (128, 64, 32, 16, 8)), 1,
                [(qkv, "rows"), (qkv, "prev"), (convw, "const"), (ba, "rows"), (alog, "const"), (dtb, "const"),
                 (dq, "rows"), (dk_, "rows"), (dv, "rows"), (dbg, "rows")],
                [(W, f32), (nba, bf16)], [(kw, W, f32), (1, H, f32), (1, H, f32)])


def conv_t(dc, w, name):
    L, C = dc.shape
    kw = w.shape[0]
    ncol = C // _pick(C, (1536, 1408, 1024, 768, 512, 256, 128))

    def fn(i, n, db, hb, wb):
        out = db * wb[kw - 1:kw]
        for j in range(kw - 1):
            out = out + _shift_up(db, hb, kw - 1 - j, i == n - 1) * wb[j:j + 1]
        return (out,)

    return rowk(name, fn, L, _pick(L, (256, 128, 64, 32, 16, 8)), ncol,
                [(dc, "rows"), (dc, "next"), (w, "cols")], [(C, bf16)])[0]


_BDIMS = {"nn": (((2,), (1,)), ((0,), (0,))), "nt": (((2,), (2,)), ((0,), (0,))), "tn": (((1,), (1,)), ((0,), (0,)))}


def _bdot(a, b, mode):
    return lax.dot_general(a.astype(bf16), b.astype(bf16), _BDIMS[mode], preferred_element_type=f32)


def _split16(a):
    hi = a.astype(bf16)
    return hi, (a - hi.astype(f32)).astype(bf16)


def _hdot(a, b, mode):
    ah, al = _split16(a)
    bh, bl = _split16(b)
    d = lambda x, y: lax.dot_general(x, y, _BDIMS[mode], preferred_element_type=f32)
    return d(ah, bh) + (d(ah, bl) + d(al, bh))


def _make_dot(raw):
    @functools.partial(jax.custom_vjp, nondiff_argnums=(2,))
    def dot(a, b, mode):
        return raw(a, b, mode)

    def fwd(a, b, mode):
        return raw(a, b, mode), (a, b)

    def bwd(mode, res, ct):
        a, b = res
        if mode == "nn":
            return raw(ct, b, "nt"), raw(a, ct, "tn")
        if mode == "nt":
            return raw(ct, b, "nn"), raw(ct, a, "tn")
        return raw(b, ct, "nt"), raw(a, ct, "nn")

    dot.defvjp(fwd, bwd)
    return dot


_dot16 = _make_dot(_bdot)
_dot32 = _make_dot(_hdot)


@jax.custom_vjp
def _unit_lower_inv(lmat):
    c = lmat.shape[-1]
    eye = (lax.broadcasted_iota(jnp.int32, (c, c), 0) == lax.broadcasted_iota(jnp.int32, (c, c), 1)).astype(f32)
    p = -lmat
    t = eye + p
    for _ in range(int(math.log2(c)) - 1):
        p = _hdot(p, p, "nn")
        t = t + _hdot(t, p, "nn")
    return t


def _uli_fwd(lmat):
    t = _unit_lower_inv(lmat)
    return t, t


def _uli_bwd(t, dt):
    return (-_hdot(_hdot(t, dt, "tn"), t, "nt"),)


_unit_lower_inv.defvjp(_uli_fwd, _uli_bwd)


def _dn_chunk_math(s_in, q, k, v, gcol, grow, bcol):
    c = q.shape[1]
    ri = lax.broadcasted_iota(jnp.int32, (c, c), 0)
    ci = lax.broadcasted_iota(jnp.int32, (c, c), 1)
    tril = (ri >= ci).astype(f32)
    strict = (ri > ci).astype(f32)
    gc_col = jnp.sum(tril * grow, axis=2, keepdims=True)
    gc_row = jnp.sum((1.0 - strict) * gcol, axis=1, keepdims=True)
    g_last = jnp.sum(gcol, axis=1, keepdims=True)
    decay = jnp.exp((gc_col - gc_row) * tril) * tril
    kb = k * bcol
    vb = v * bcol
    lmat = _dot16(kb, k, "nt") * decay * strict
    t = _unit_lower_inv(lmat)
    u = _dot32(t, vb, "nn")
    w = _dot32(t, kb * jnp.exp(gc_col), "nn")
    attn = _dot16(q, k, "nt") * decay
    v_new = u - _dot16(w, s_in, "nn")
    o = _dot16(q * jnp.exp(gc_col), s_in, "nn") + _dot16(attn, v_new, "nn")
    s_out = s_in * jnp.exp(g_last) + _dot16(k * jnp.exp(g_last - gc_col), v_new, "tn")
    return o, s_out


def _dn_load(q_ref, k_ref, v_ref, bg_ref, bgt_ref, H, dk):
    heads = lambda ref: jnp.stack([ref[:, h * dk:(h + 1) * dk] for h in range(H)])
    bgb, bgtb = bg_ref[...], bgt_ref[0]
    gcol = jnp.stack([bgb[:, H + h:H + h + 1] for h in range(H)])
    bcol = jnp.stack([bgb[:, h:h + 1] for h in range(H)])
    grow = jnp.stack([bgtb[H + h:H + h + 1, :] for h in range(H)])
    return heads(q_ref), heads(k_ref), heads(v_ref), gcol, grow, bcol


def _host(payload, n_in, n_out, n_scratch):
    pi = len(payload.arrays) if payload is not None else 0
    po = len(payload.out_shapes) if payload is not None else 0

    def split(refs):
        own = refs[:n_in] + refs[n_in + pi:n_in + pi + n_out] + refs[n_in + pi + n_out + po:n_in + pi + n_out + po + n_scratch]
        theirs = (refs[n_in:n_in + pi], refs[n_in + pi + n_out:n_in + pi + n_out + po], refs[n_in + pi + n_out + po + n_scratch:])
        return own, theirs

    def hooks(theirs, first, last):
        if payload is None:
            return

        @pl.when(first)
        def _():
            payload.start(*theirs)

        @pl.when(last)
        def _():
            payload.finish(*theirs)

    extra = dict(in_specs=[_ANY] * pi, out_specs=[_ANY] * po, out_shape=list(payload.out_shapes) if payload else [],
                 scratch=list(payload.sems) if payload else [], arrays=list(payload.arrays) if payload else [],
                 aliases={n_in + a: n_out + b for a, b in payload.aliases.items()} if payload else {})
    return split, hooks, extra


def dn_chunk_fwd(qn, kn, vv, bg, bgt, H, name, payload=None):
    L, hk = qn.shape
    dk = hk // H
    c = DN_CHUNK
    nc = L // c
    split, hooks, extra = _host(payload, 5, 2, 1)

    def body(*refs):
        (q_ref, k_ref, v_ref, bg_ref, bgt_ref, o_ref, ss_ref, s_ref), theirs = split(refs)
        hooks(theirs, pl.program_id(0) == 0, pl.program_id(0) == nc - 1)

        @pl.when(pl.program_id(0) == 0)
        def _():
            s_ref[...] = jnp.zeros_like(s_ref)

        s_in = s_ref[...]
        ss_ref[0] = s_in
        o, s_out = _dn_chunk_math(s_in, *_dn_load(q_ref, k_ref, v_ref, bg_ref, bgt_ref, H, dk))
        for h in range(H):
            o_ref[:, h * dk:(h + 1) * dk] = o[h]
        s_ref[...] = s_out

    row = lambda w: pl.BlockSpec((c, w), lambda n: (n, 0))
    res = pl.pallas_call(
        body, name=name, grid=(nc,),
        in_specs=[row(hk), row(hk), row(hk), row(2 * H), pl.BlockSpec((1, 2 * H, c), lambda n: (n, 0, 0))] + extra["in_specs"],
        out_specs=[row(hk), pl.BlockSpec((1, H, dk, dk), lambda n: (n, 0, 0, 0))] + extra["out_specs"],
        out_shape=[S((L, hk), f32), S((nc, H, dk, dk), f32)] + extra["out_shape"],
        scratch_shapes=[pltpu.VMEM((H, dk, dk), f32)] + extra["scratch"],
        input_output_aliases=extra["aliases"],
        compiler_params=_cparams(("arbitrary",)),
    )(qn, kn, vv, bg, bgt, *extra["arrays"])
    return res[0], res[1], list(res[2:])


def dn_chunk_bwd(qn, kn, vv, bg, bgt, ss, do, H, name, payload=None):
    L, hk = qn.shape
    dk = hk // H
    c = DN_CHUNK
    nc = L // c
    split, hooks, extra = _host(payload, 7, 5, 1)

    def body(*refs):
        (q_ref, k_ref, v_ref, bg_ref, bgt_ref, ss_ref, do_ref, dq_ref, dk_ref, dv_ref, dbg_ref, dgt_ref, ds_ref), theirs = split(refs)
        hooks(theirs, pl.program_id(0) == 0, pl.program_id(0) == nc - 1)

        @pl.when(pl.program_id(0) == 0)
        def _():
            ds_ref[...] = jnp.zeros_like(ds_ref)

        args = (ss_ref[0],) + _dn_load(q_ref, k_ref, v_ref, bg_ref, bgt_ref, H, dk)
        _, vjp = jax.vjp(_dn_chunk_math, *args)
        do = jnp.stack([do_ref[:, h * dk:(h + 1) * dk] for h in range(H)])
        ds, dq, dkk, dv, dgcol, dgrow, dbcol = vjp((do, ds_ref[...]))
        ds_ref[...] = ds
        for h in range(H):
            sl = slice(h * dk, (h + 1) * dk)
            dq_ref[:, sl], dk_ref[:, sl], dv_ref[:, sl] = dq[h], dkk[h], dv[h]
        dbg_ref[...] = _from_cols([dbcol[h] for h in range(H)] + [dgcol[h] for h in range(H)], 2 * H)
        dgt_ref[0] = _from_rows([dgrow[h] for h in range(H)])

    row = lambda w: pl.BlockSpec((c, w), lambda n: (nc - 1 - n, 0))
    res = pl.pallas_call(
        body, name=name, grid=(nc,),
        in_specs=[row(hk), row(hk), row(hk), row(2 * H), pl.BlockSpec((1, 2 * H, c), lambda n: (nc - 1 - n, 0, 0)),
                  pl.BlockSpec((1, H, dk, dk), lambda n: (nc - 1 - n, 0, 0, 0)), row(hk)] + extra["in_specs"],
        out_specs=[row(hk), row(hk), row(hk), row(2 * H), pl.BlockSpec((1, H, c), lambda n: (nc - 1 - n, 0, 0))] + extra["out_specs"],
        out_shape=[S((L, hk), f32), S((L, hk), f32), S((L, hk), f32), S((L, 2 * H), f32), S((nc, H, c), f32)] + extra["out_shape"],
        scratch_shapes=[pltpu.VMEM((H, dk, dk), f32)] + extra["scratch"],
        input_output_aliases=extra["aliases"],
        compiler_params=_cparams(("arbitrary",)),
    )(qn, kn, vv, bg, bgt, ss, do, *extra["arrays"])
    return tuple(res[:5]) + (list(res[5:]),)


def _dn_gate_math(o, z, w):
    return _rms(o, w) * _silu(z)


def dn_gate(o, z, w, H, name):
    L, hv = o.shape
    dv = hv // H

    def fn(i, n, ob, zb, wb):
        return (jnp.concatenate([_dn_gate_math(ob[:, h * dv:(h + 1) * dv], zb[:, h * dv:(h + 1) * dv], wb)
                                 for h in range(H)], axis=1),)

    return rowk(name, fn, L, _pick(L, (256, 128, 64, 32, 16, 8)), 1, [(o, "rows"), (z, "rows"), (w, "const")], [(hv, bf16)])[0]


def dn_gate_bwd(o, z, w, dy, H, name):
    L, hv = o.shape
    dv = hv // H

    def fn(i, n, ob, zb, wb, dyb):
        dos, dzs, dw = [], [], 0.0
        for h in range(H):
            sl = slice(h * dv, (h + 1) * dv)
            _, vjp = jax.vjp(_dn_gate_math, ob[:, sl], zb[:, sl], wb)
            a, b, c = vjp(dyb[:, sl])
            dos.append(a), dzs.append(b)
            dw = dw + c
        return jnp.concatenate(dos, axis=1), jnp.concatenate(dzs, axis=1), dw

    return rowk(name, fn, L, _pick(L, (256, 128, 64, 32, 16, 8)), 1,
                [(o, "rows"), (z, "rows"), (w, "const"), (dy, "rows")], [(hv, f32), (hv, bf16)], [(1, dv, f32)])


def ffn_mid(ua, uv, wa, wv, name):
    L, F = ua.shape
    ncol = F // _pick(F, (1408, 1024, 512, 256, 128))

    def fn(i, n, ab, ah, vb, vh, wab, wvb):
        ca, _ = _causal_conv(ab, ah, wab, i == 0)
        cv, _ = _causal_conv(vb, vh, wvb, i == 0)
        return (_silu(ca) * cv,)

    return rowk(name, fn, L, _pick(L, (256, 128, 64, 32, 16, 8)), ncol,
                [(ua, "rows"), (ua, "prev"), (uv, "rows"), (uv, "prev"), (wa, "cols"), (wv, "cols")], [(F, bf16)])[0]


def ffn_mid_bwd(ua, uv, wa, wv, dh, name):
    L, F = ua.shape
    kw = wa.shape[0]
    ncol = F // _pick(F, (1408, 1024, 512, 256, 128))

    def fn(i, n, ab, ap, an, vb, vp, vn, wab, wvb, dhb, dhn):
        tl = ab.shape[0]
        last = i == n - 1
        ext = lambda blk, nxt: jnp.concatenate([blk, jnp.where(last, 0.0, nxt)], axis=0)
        dhe = ext(dhb, dhn)
        ca, sa = _causal_conv(ext(ab, an), ap, wab, i == 0)
        cv, sv = _causal_conv(ext(vb, vn), vp, wvb, i == 0)
        dca = dhe * cv * _dsilu(ca)
        dcv = dhe * _silu(ca)
        dwa = _from_rows([jnp.sum(dca[:tl] * sa[j][:tl], axis=0, keepdims=True) for j in range(kw)])
        dwv = _from_rows([jnp.sum(dcv[:tl] * sv[j][:tl], axis=0, keepdims=True) for j in range(kw)])

        def conv_t_rows(dc, w):
            out = dc[:tl] * w[kw - 1:kw]
            for j in range(kw - 1):
                out = out + pltpu.roll(dc, tl + HALO - (kw - 1 - j), 0)[:tl] * w[j:j + 1]
            return out

        return conv_t_rows(dca, wab), conv_t_rows(dcv, wvb), dwa, dwv

    return rowk(name, fn, L, _pick(L, (256, 128, 64, 32, 16, 8)), ncol,
                [(ua, "rows"), (ua, "prev"), (ua, "next"), (uv, "rows"), (uv, "prev"), (uv, "next"),
                 (wa, "cols"), (wv, "cols"), (dh, "rows"), (dh, "next")],
                [(F, bf16), (F, bf16)], [(kw, F, f32), (kw, F, f32)])


def _merge_epi(acc, gs, gd, ga, gb):
    return acc, _sigmoid(gs) * ga * _sigmoid(gb) + _sigmoid(gd) * acc


def merge_bwd(dm, gs, gd, ga, gb, brdn, name):
    L, D = dm.shape
    ncol = D // _pick(D, (1024, 512, 256, 128))

    def fn(i, n, dmb, gsb, gdb, gab, gbb, brb):
        ss, sd, sb = _sigmoid(gsb), _sigmoid(gdb), _sigmoid(gbb)
        br_s5 = gab * sb
        dbr_s5 = dmb * ss
        return (dmb * br_s5 * ss * (1.0 - ss), dmb * brb * sd * (1.0 - sd), dbr_s5 * sb,
                dbr_s5 * gab * sb * (1.0 - sb), dmb * sd)

    return rowk(name, fn, L, _pick(L, (256, 128, 64, 32, 16, 8)), ncol,
                [(a, "rows") for a in (dm, gs, gd, ga, gb, brdn)], [(D, bf16)] * 5)


def s5_out_bwd(dy, ypre, u, d, name):
    L, W = dy.shape

    def fn(i, n, dyb, yb, ub, db):
        dyp = dyb * _dgelu(yb)
        return dyp, db * dyp, jnp.sum(dyp * ub, axis=0, keepdims=True)

    return rowk(name, fn, L, _pick(L, (256, 128, 64, 32, 16, 8)), 1,
                [(dy, "rows"), (ypre, "rows"), (u, "rows"), (d, "const")], [(W, bf16), (W, f32)], [(1, W, f32)])


def _s5_y_epi(acc, y1, u, d):
    ypre = acc + y1 + d * u
    return ypre, _gelu(ypre)


def adamw(w, g, m, v, name):
    R, C = w.shape
    tl = _pick(R, (256, 128, 64, 32, 16, 8))
    if R * C * 4 <= 2 * 1024 * 1024:
        tl = R

    def body(w_ref, g_ref, m_ref, v_ref, d_ref, nm_ref, nv_ref):
        gg = g_ref[...]
        nm = ADAM_B1 * m_ref[...] + (1.0 - ADAM_B1) * gg
        nv = ADAM_B2 * v_ref[...] + (1.0 - ADAM_B2) * (gg * gg)
        m_hat = nm / (1.0 - ADAM_B1 ** ADAM_STEP)
        v_hat = nv / (1.0 - ADAM_B2 ** ADAM_STEP)
        d_ref[...] = -ADAM_LR * (m_hat / (jnp.sqrt(v_hat) + ADAM_EPS) + ADAM_WD * w_ref[...])
        nm_ref[...] = nm
        nv_ref[...] = nv

    blk = pl.BlockSpec((tl, C), lambda i: (i, 0))
    return pl.pallas_call(body, name=name, grid=(R // tl,), in_specs=[blk] * 4, out_specs=[blk] * 3,
                          out_shape=[S((R, C), f32)] * 3, compiler_params=_cparams(("parallel",)))(w, g, m, v)


def sum_slots(x, name, out_dtype=f32):
    n, R, C = x.shape
    tl = _pick(R, (512, 256, 128, 64, 32, 16, 8))

    def body(x_ref, o_ref):
        acc = x_ref[0].astype(f32)
        for s in range(1, n):
            acc = acc + x_ref[s].astype(f32)
        o_ref[...] = acc.astype(o_ref.dtype)

    return pl.pallas_call(body, name=name, grid=(R // tl,),
                          in_specs=[pl.BlockSpec((n, tl, C), lambda i: (0, i, 0))],
                          out_specs=pl.BlockSpec((tl, C), lambda i: (i, 0)),
                          out_shape=S((R, C), out_dtype), compiler_params=_cparams(("parallel",)))(x)


_ANY = pl.BlockSpec(memory_space=pl.ANY)


def _coords():
    return lax.axis_index("x"), lax.axis_index("y"), lax.axis_index("c")


def chip_exchange(src, name, same=False):
    out_shape = (N_CHIPS,) + src.shape if same else src.shape
    assert out_shape[0] == N_CHIPS

    def body(src_ref, out_ref, send_sems, recv_sems, local_sem):
        x, y, c = _coords()
        me = 2 * x + y
        slot = (lambda j: src_ref) if same else (lambda j: src_ref.at[j])
        mine = pltpu.make_async_copy(slot(me), out_ref.at[me], local_sem)
        mine.start()
        peers = [(1 - x, y), (x, 1 - y), (1 - x, 1 - y)]
        copies = []
        for k, (px, py) in enumerate(peers):
            cp = pltpu.make_async_remote_copy(
                src_ref=slot(2 * px + py), dst_ref=out_ref.at[me],
                send_sem=send_sems.at[k], recv_sem=recv_sems.at[k],
                device_id=(px, py, c), device_id_type=MESH)
            cp.start()
            copies.append(cp)
        for k, (px, py) in enumerate(peers):
            pltpu.make_async_remote_copy(
                src_ref=slot(me), dst_ref=out_ref.at[2 * px + py],
                send_sem=send_sems.at[k], recv_sem=recv_sems.at[k],
                device_id=(px, py, c), device_id_type=MESH).wait_recv()
        for cp in copies:
            cp.wait_send()
        mine.wait()

    return pl.pallas_call(
        body, name=name, in_specs=[_ANY], out_specs=_ANY, out_shape=S(out_shape, src.dtype),
        scratch_shapes=[pltpu.SemaphoreType.DMA((3,)), pltpu.SemaphoreType.DMA((3,)), pltpu.SemaphoreType.DMA],
    )(src)


def sibling_exchange(src, name):
    def body(src_ref, out_ref, send_sem, recv_sem):
        x, y, c = _coords()
        cp = pltpu.make_async_remote_copy(src_ref=src_ref, dst_ref=out_ref, send_sem=send_sem, recv_sem=recv_sem,
                                          device_id=(x, y, 1 - c), device_id_type=MESH)
        cp.start()
        cp.wait()

    return pl.pallas_call(
        body, name=name, in_specs=[_ANY], out_specs=_ANY, out_shape=S(src.shape, src.dtype),
        scratch_shapes=[pltpu.SemaphoreType.DMA, pltpu.SemaphoreType.DMA],
    )(src)


def _rows(c, half_rows):
    return pl.ds(pl.multiple_of(c * half_rows, 16), half_rows)


def _slot(ref, kind, j, rows=None):
    if kind == "slot":
        return ref.at[j] if rows is None else ref.at[j, rows]
    cs = ref.shape[1] // N_CHIPS
    cols = pl.ds(pl.multiple_of(j * cs, 128), cs)
    return ref.at[:, cols] if rows is None else ref.at[rows, cols]


def _sems(n):
    return [pltpu.SemaphoreType.DMA((n,))]


class Payload:
    def __init__(self, arrays, out_shapes, aliases, sems, start, finish):
        self.arrays, self.out_shapes, self.aliases, self.sems = arrays, out_shapes, aliases, sems
        self.start, self.finish = start, finish


def run_payload(p, name):
    ni, no = len(p.arrays), len(p.out_shapes)

    def body(*refs):
        args = (refs[:ni], refs[ni:ni + no], refs[ni + no:])
        p.start(*args)
        p.finish(*args)

    return pl.pallas_call(body, name=name, in_specs=[_ANY] * ni, out_specs=[_ANY] * no, out_shape=p.out_shapes,
                          input_output_aliases=p.aliases, scratch_shapes=p.sems)(*p.arrays)


def _half_rows(bufs, kinds):
    return [(a.shape[1] if k == "slot" else a.shape[0]) // 2 for a, k in zip(bufs, kinds)]


def ag_ici_payload(bufs, kinds):
    n = len(bufs)
    rh = _half_rows(bufs, kinds)

    def copies(outs, sems):
        send, recv = sems
        x, y, c = _coords()
        me = 2 * x + y
        res = []
        for t in range(n):
            own = _slot(outs[t], kinds[t], me, _rows(c, rh[t]))
            for k, (px, py) in enumerate([(1 - x, y), (x, 1 - y), (1 - x, 1 - y)]):
                landed = _slot(outs[t], kinds[t], 2 * px + py, _rows(c, rh[t]))
                sem = dict(send_sem=send.at[3 * t + k], recv_sem=recv.at[3 * t + k], device_id=(px, py, c), device_id_type=MESH)
                res.append((pltpu.make_async_remote_copy(src_ref=own, dst_ref=own, **sem),
                            pltpu.make_async_remote_copy(src_ref=landed, dst_ref=landed, **sem)))
        return res

    def start(ins, outs, sems):
        for mine, _ in copies(outs, sems):
            mine.start()

    def finish(ins, outs, sems):
        both = copies(outs, sems)
        for _, theirs in both:
            theirs.wait_recv()
        for mine, _ in both:
            mine.wait_send()

    return Payload(bufs, [S(a.shape, a.dtype) for a in bufs], {t: t for t in range(n)}, _sems(3 * n) * 2, start, finish)


def ag_d2d(bufs, kinds, name):
    n = len(bufs)
    rh = _half_rows(bufs, kinds)

    def body(*refs):
        outs = refs[n:2 * n]
        send, recv = refs[2 * n:]
        x, y, c = _coords()
        sib = dict(device_id=(x, y, 1 - c), device_id_type=MESH)
        sends = []
        for t in range(n):
            for k, (px, py) in enumerate([(1 - x, y), (x, 1 - y), (1 - x, 1 - y)]):
                landed = _slot(outs[t], kinds[t], 2 * px + py, _rows(c, rh[t]))
                cp = pltpu.make_async_remote_copy(src_ref=landed, dst_ref=landed, send_sem=send.at[3 * t + k],
                                                  recv_sem=recv.at[3 * t + k], **sib)
                cp.start()
                sends.append(cp)
        for t in range(n):
            for k, (px, py) in enumerate([(1 - x, y), (x, 1 - y), (1 - x, 1 - y)]):
                other = _slot(outs[t], kinds[t], 2 * px + py, _rows(1 - c, rh[t]))
                pltpu.make_async_remote_copy(src_ref=other, dst_ref=other, send_sem=send.at[3 * t + k],
                                             recv_sem=recv.at[3 * t + k], **sib).wait_recv()
        for cp in sends:
            cp.wait_send()

    return pl.pallas_call(body, name=name, in_specs=[_ANY] * n, out_specs=[_ANY] * n,
                          out_shape=[S(a.shape, a.dtype) for a in bufs], input_output_aliases={t: t for t in range(n)},
                          scratch_shapes=_sems(3 * n) * 2)(*bufs)


def rs_pair_payload(grads, kinds):
    n = len(grads)

    def half_shape(a, k):
        return (a.shape[0], a.shape[1] // 2, a.shape[2]) if k == "slot" else (a.shape[0] // 2, a.shape[1])

    def half(ref, k, c):
        return ref.at[:, _rows(c, ref.shape[1] // 2)] if k == "slot" else ref.at[_rows(c, ref.shape[0] // 2)]

    shapes = [S(half_shape(a, k), a.dtype) for a, k in zip(grads, kinds)]

    def copies(srcs, got, sems):
        send, recv = sems
        x, y, c = _coords()
        return [pltpu.make_async_remote_copy(src_ref=half(srcs[t], kinds[t], 1 - c), dst_ref=got[t], send_sem=send.at[t],
                                             recv_sem=recv.at[t], device_id=(x, y, 1 - c), device_id_type=MESH)
                for t in range(n)]

    def start(ins, outs, sems):
        for cp in copies(ins, outs, sems):
            cp.start()

    def finish(ins, outs, sems):
        for cp in copies(ins, outs, sems):
            cp.wait()

    return Payload(grads, shapes, {}, _sems(n) * 2, start, finish)


def rs_chips_payload(sums, kinds):
    n = len(sums)

    def part_shape(a, k):
        return a.shape[1:] if k == "slot" else (a.shape[0], a.shape[1] // N_CHIPS)

    shapes = [S((N_CHIPS,) + part_shape(a, k), a.dtype) for a, k in zip(sums, kinds)]

    def copies(srcs, outs, sems):
        send, recv, lsem = sems
        x, y, c = _coords()
        me = 2 * x + y
        local, remote = [], []
        for t in range(n):
            local.append(pltpu.make_async_copy(_slot(srcs[t], kinds[t], me), outs[t].at[me], lsem.at[t]))
            for k, (px, py) in enumerate([(1 - x, y), (x, 1 - y), (1 - x, 1 - y)]):
                landed = outs[t].at[2 * px + py]
                sem = dict(send_sem=send.at[3 * t + k], recv_sem=recv.at[3 * t + k], device_id=(px, py, c), device_id_type=MESH)
                remote.append((pltpu.make_async_remote_copy(src_ref=_slot(srcs[t], kinds[t], 2 * px + py), dst_ref=outs[t].at[me], **sem),
                               pltpu.make_async_remote_copy(src_ref=landed, dst_ref=landed, **sem)))
        return local, remote

    def start(ins, outs, sems):
        local, remote = copies(ins, outs, sems)
        for cp in local:
            cp.start()
        for mine, _ in remote:
            mine.start()

    def finish(ins, outs, sems):
        local, remote = copies(ins, outs, sems)
        for _, theirs in remote:
            theirs.wait_recv()
        for mine, _ in remote:
            mine.wait_send()
        for cp in local:
            cp.wait()

    return Payload(sums, shapes, {}, _sems(3 * n) * 2 + _sems(n), start, finish)


def rs_share(gs, name):
    n = len(gs)

    def body(*refs):
        outs = refs[n:2 * n]
        send, recv = refs[2 * n:]
        x, y, c = _coords()
        started = []
        for t in range(n):
            cp = pltpu.make_async_remote_copy(src_ref=outs[t].at[c], dst_ref=outs[t].at[c], send_sem=send.at[t],
                                              recv_sem=recv.at[t], device_id=(x, y, 1 - c), device_id_type=MESH)
            cp.start()
            started.append(cp)
        for cp in started:
            cp.wait()

    return pl.pallas_call(body, name=name, in_specs=[_ANY] * n, out_specs=[_ANY] * n,
                          out_shape=[S(a.shape, a.dtype) for a in gs], input_output_aliases={t: t for t in range(n)},
                          scratch_shapes=_sems(n) * 2)(*gs)


def _core_index():
    return lax.axis_index("c").astype(jnp.int32).reshape(1)


def place_block(blk, kind, name):
    R, cs = blk.shape
    tl = _pick(R, (256, 128, 64, 32, 16, 8))

    def body(me_ref, b_ref, o_ref):
        o_ref[...] = b_ref[...].astype(o_ref.dtype)

    if kind == "slot":
        out_spec, out_shape = pl.BlockSpec((None, tl, cs), lambda i, me: (me[0], i, 0)), (N_CHIPS, R, cs)
    else:
        out_spec, out_shape = pl.BlockSpec((tl, cs), lambda i, me: (i, me[0])), (R, N_CHIPS * cs)
    me = (2 * lax.axis_index("x") + lax.axis_index("y")).astype(jnp.int32).reshape(1)
    return pl.pallas_call(
        body, name=name,
        grid_spec=pltpu.PrefetchScalarGridSpec(num_scalar_prefetch=1, grid=(R // tl,),
                                               in_specs=[pl.BlockSpec((tl, cs), lambda i, me: (i, 0))], out_specs=out_spec),
        out_shape=S(out_shape, bf16), compiler_params=_cparams(("parallel",)),
    )(me, blk)


def add_half(g, got, kind, name):
    g4 = g.reshape((g.shape[0] if kind == "slot" else 1, 2, -1, g.shape[-1]))
    parts, _, rh, C = g4.shape
    got3 = got.reshape(parts, rh, C)
    tl = _pick(rh, (256, 128, 64, 32, 16, 8))

    def body(c_ref, g_ref, got_ref, o_ref):
        o_ref[...] = (g_ref[...] + got_ref[...]).astype(o_ref.dtype)

    blk = pl.BlockSpec((None, tl, C), lambda s, i, c: (s, i, 0))
    out = pl.pallas_call(
        body, name=name,
        grid_spec=pltpu.PrefetchScalarGridSpec(
            num_scalar_prefetch=1, grid=(parts, rh // tl),
            in_specs=[pl.BlockSpec((None, None, tl, C), lambda s, i, c: (s, c[0], i, 0)), blk], out_specs=blk),
        out_shape=S((parts, rh, C), bf16), compiler_params=_cparams(("parallel", "parallel")),
    )(_core_index(), g4, got3)
    return out.reshape(got.shape)


def sum_slots_half(x, name):
    n, rh, C = x.shape
    tl = _pick(rh, (512, 256, 128, 64, 32, 16, 8))

    def body(c_ref, x_ref, o_ref):
        acc = x_ref[0].astype(f32)
        for s in range(1, n):
            acc = acc + x_ref[s].astype(f32)
        o_ref[...] = acc

    return pl.pallas_call(
        body, name=name,
        grid_spec=pltpu.PrefetchScalarGridSpec(
            num_scalar_prefetch=1, grid=(rh // tl,),
            in_specs=[pl.BlockSpec((n, tl, C), lambda i, c: (0, i, 0))],
            out_specs=pl.BlockSpec((None, tl, C), lambda i, c: (c[0], i, 0))),
        out_shape=S((2, rh, C), f32), compiler_params=_cparams(("parallel",)),
    )(_core_index(), x)


def _block_diag(blocks):
    G, a, b = blocks.shape
    eye = jnp.eye(G, dtype=blocks.dtype)
    return (blocks[:, :, None, :] * eye[:, None, :, None]).reshape(G * a, G * b)


def _flat_pack(arrs, dtype, lanes=1024, row_mult=16):
    flat = jnp.concatenate([a.reshape(-1).astype(dtype) for a in arrs])
    n = flat.shape[0]
    per = lanes * row_mult
    pad = (-n) % per
    if pad:
        flat = jnp.concatenate([flat, jnp.zeros((pad,), dtype)])
    return flat.reshape(-1, lanes)


def _flat_unpack(buf, shapes):
    flat = buf.reshape(-1)
    out, off = [], 0
    for s in shapes:
        n = math.prod(s)
        out.append(flat[off:off + n].reshape(s))
        off += n
    return out


def _layer_weights(p, H):
    w_in = p["w_in"]
    D = w_in.shape[0]
    s5w = p["s5_d"].shape[-1]
    hk = p["dn_proj_w"].shape[0]
    off_z = s5w + 4 * hk
    off_a = off_z + 2 * H
    wp = jnp.concatenate([w_in[:, :off_z], w_in[:, off_a:], jnp.pad(w_in[:, off_z:off_a], ((0, 0), (0, 128 - 2 * H)))], axis=1)
    q = {}
    q["Wu"] = Win(wp, 0, s5w)
    q["Wqkv"] = Win(wp, s5w, 3 * hk)
    q["Wz"] = Win(wp, s5w + 3 * hk, hk)
    q["Wgs"] = Win(wp, off_z, D)
    q["Wgd"] = Win(wp, off_z + D, D)
    q["Wba"] = Win(wp, off_z + 2 * D, 128)
    q["Wga"], q["Wgb"] = Win(p["s5_glu_w"], 0, D), Win(p["s5_glu_w"], D, D)
    F = p["ffn_down"].shape[0]
    q["Wupa"], q["Wupv"] = Win(p["ffn_up"], 0, F), Win(p["ffn_up"], F, F)
    q["cwa"], q["cwv"] = p["ffn_conv_w"][:, :F], p["ffn_conv_w"][:, F:]
    for k in ("dn_proj_w", "w_out", "ffn_down", "dn_conv_w", "mix_norm_w", "ffn_norm_w", "dn_norm_w",
              "dn_a_log", "dn_dt_bias", "s5_d"):
        q[k] = p[k]
    return q


def _s5_params(p, tag):
    G, P = p["s5_a_re"].shape
    HG = p["s5_b_re"].shape[-1]
    col = lambda a: a.reshape(G * P, 1)
    lr, li = col(p["s5_a_re"]), col(p["s5_a_im"])
    logdt = col(jnp.broadcast_to(p["s5_log_dt"][:, None], (G, P)))
    br, bi = p["s5_b_re"].reshape(G * P, HG), p["s5_b_im"].reshape(G * P, HG)
    ar, ai, bbr, bbi = s5_disc(lr, li, logdt, br, bi, "s5_disc")
    bd = lambda m: _block_diag(m.reshape(G, P, HG).transpose(0, 2, 1)).astype(bf16)
    cd = lambda m: _block_diag(m.transpose(0, 2, 1)).astype(bf16)
    return dict(lr=lr, li=li, logdt=logdt, br=br, bi=bi, ar=ar.reshape(1, G * P), ai=ai.reshape(1, G * P),
                Bre=bd(bbr), Bim=bd(bbi), CreT=cd(p["s5_c_re"]), mCimT=cd(-p["s5_c_im"]), G=G, P=P, HG=HG)


def layer_fwd(x, q, s5, H, payloads=(None, None, None)):
    r = {"x": x}
    h1 = rms_fwd(x, q["mix_norm_w"], "rms_mix")
    r["h1"] = h1
    u32, u16 = mm(h1, q["Wu"], "nn", "proj_u", out_dtypes=(f32, bf16), epi=lambda a: (a, a))
    qkv = mm(h1, q["Wqkv"], "nn", "proj_qkv")
    z = mm(h1, q["Wz"], "nn", "proj_z")
    ba = mm(h1, q["Wba"], "nn", "proj_ba")
    gs = mm(h1, q["Wgs"], "nn", "proj_gs")
    gd = mm(h1, q["Wgd"], "nn", "proj_gd")
    r.update(u32=u32, u16=u16, qkv=qkv, z=z, ba=ba, gs=gs, gd=gd)
    HG, P = s5["HG"], s5["P"]
    bur = mm_bd(u16, s5["Bre"], "nn", "s5_bu_re", HG, P)
    bui = mm_bd(u16, s5["Bim"], "nn", "s5_bu_im", HG, P)
    xr, xi = s5_scan_fwd(bur, bui, s5["ar"], s5["ai"], "s5_scan_fwd")
    y1 = mm_bd(xr, s5["CreT"], "nn", "s5_y_re", P, HG)
    ypre, ys5 = mm_bd(xi, s5["mCimT"], "nn", "s5_y_im", P, HG, extras=(y1, u32, q["s5_d"]), epi=_s5_y_epi, out_dtypes=(f32, bf16))
    ga = mm(ys5, q["Wga"], "nn", "glu_a")
    gb = mm(ys5, q["Wgb"], "nn", "glu_b")
    r.update(xr=xr, xi=xi, ypre=ypre, ys5=ys5, ga=ga, gb=gb)
    qn, kn, vv, bg = dn_prep(qkv, q["dn_conv_w"], ba, q["dn_a_log"], q["dn_dt_bias"], H, "dn_prep")
    L = x.shape[0]
    bgt = bg.reshape(L // DN_CHUNK, DN_CHUNK, 2 * H).transpose(0, 2, 1)
    o, ss, carried = dn_chunk_fwd(qn, kn, vv, bg, bgt, H, "dn_chunk_fwd", payloads[0])
    ydn = dn_gate(o, z, q["dn_norm_w"], H, "dn_gate")
    brdn, merged = mm(ydn, q["dn_proj_w"], "nn", "dn_proj", extras=(gs, gd, ga, gb), epi=_merge_epi, out_dtypes=(f32, bf16))
    r.update(qn=qn, kn=kn, vv=vv, bg=bg, bgt=bgt, ss=ss, o=o, ydn=ydn, brdn=brdn, merged=merged)
    x1 = mm(merged, q["w_out"], "nn", "out_proj", extras=(x,), epi=_add)
    h2 = rms_fwd(x1, q["ffn_norm_w"], "rms_ffn")
    ua = mm(h2, q["Wupa"], "nn", "ffn_up_a", payload=payloads[1])
    uv = mm(h2, q["Wupv"], "nn", "ffn_up_v", payload=payloads[2])
    if payloads[1] is not None:
        (ua, got_a), (uv, got_v) = ua, uv
        carried = [carried, got_a, got_v]
    hmid = ffn_mid(ua, uv, q["cwa"], q["cwv"], "ffn_mid")
    x2 = mm(hmid, q["ffn_down"], "nn", "ffn_down", extras=(x1,), epi=_add)
    r.update(x1=x1, h2=h2, ua=ua, uv=uv, hmid=hmid)
    return x2, r, carried


def layer_bwd(dx2, dx2b, r, q, s5, H, reduction=None):
    g = {}
    payload = None
    if reduction is None:
        dhmid = mm(dx2b, q["ffn_down"], "nt", "d_hmid")
    else:
        dhmid, theirs = mm(dx2b, q["ffn_down"], "nt", "d_hmid", payload=reduction.pair)
        payload = reduction.chips(theirs)
    g["ffn_down"] = mm(r["hmid"], dx2b, "tn", "dw_ffn_down")
    dua, duv, dwa, dwv = ffn_mid_bwd(r["ua"], r["uv"], q["cwa"], q["cwv"], dhmid, "ffn_mid_bwd")
    g["ffn_conv_w"] = jnp.concatenate([dwa, dwv], axis=1)
    dh2 = mm(dua, q["Wupa"], "nt", "d_h2_a")
    dh2 = mm(duv, q["Wupv"], "nt", "d_h2_v", extras=(dh2,), epi=_add)
    F = dua.shape[1]
    dwup = mm(r["h2"], dua, "tn", "dw_up_a", out_into=(2 * F, 0, None))
    g["ffn_up"] = mm(r["h2"], duv, "tn", "dw_up_v", out_into=(2 * F, F, dwup))
    dx1, dx1b, dffn_w = rms_bwd(r["x1"], q["ffn_norm_w"], dh2, dx2, "rms_ffn_bwd")
    g["ffn_norm_w"] = dffn_w[0]
    dm = mm(dx1b, q["w_out"], "nt", "d_merged")
    g["w_out"] = mm(r["merged"], dx1b, "tn", "dw_out")
    dgs, dgd, dga, dgb, dbrdn = merge_bwd(dm, r["gs"], r["gd"], r["ga"], r["gb"], r["brdn"], "merge_bwd")
    dydn = mm(dbrdn, q["dn_proj_w"], "nt", "d_ydn")
    g["dn_proj_w"] = mm(r["ydn"], dbrdn, "tn", "dw_dn_proj")
    do, dz, dnw = dn_gate_bwd(r["o"], r["z"], q["dn_norm_w"], dydn, H, "dn_gate_bwd")
    g["dn_norm_w"] = dnw[0]
    dq, dk, dv, dbg, dgt, carried = dn_chunk_bwd(r["qn"], r["kn"], r["vv"], r["bg"], r["bgt"], r["ss"], do, H,
                                                 "dn_chunk_bwd", payload)
    L = dq.shape[0]
    dbg = dbg + jnp.concatenate([jnp.zeros((L, H), f32), dgt.transpose(0, 2, 1).reshape(L, H)], axis=1)
    dc, dba, dcw, dal, ddt = dn_prep_bwd(r["qkv"], q["dn_conv_w"], r["ba"], q["dn_a_log"], q["dn_dt_bias"],
                                         dq, dk, dv, dbg, H, "dn_prep_bwd")
    g["dn_conv_w"], g["dn_a_log"], g["dn_dt_bias"] = dcw, dal[0], ddt[0]
    dqkv = conv_t(dc, q["dn_conv_w"], "dn_conv_t")
    dys5 = mm(dga, q["Wga"], "nt", "d_ys5_a")
    dys5 = mm(dgb, q["Wgb"], "nt", "d_ys5_b", extras=(dys5,), epi=_add)
    D = dga.shape[1]
    dwglu = mm(r["ys5"], dga, "tn", "dw_glu_a", out_into=(2 * D, 0, None))
    g["s5_glu_w"] = mm(r["ys5"], dgb, "tn", "dw_glu_b", out_into=(2 * D, D, dwglu))
    dyp, du_direct, dd = s5_out_bwd(dys5, r["ypre"], r["u32"], q["s5_d"], "s5_out_bwd")
    g["s5_d"] = dd[0]
    G, P, HG = s5["G"], s5["P"], s5["HG"]
    gdr = mm_bd(dyp, s5["CreT"], "nt", "s5_gx_re", P, HG)
    gdi = mm_bd(dyp, s5["mCimT"], "nt", "s5_gx_im", P, HG)
    dcre = _diag_blocks(mm_bd(r["xr"], dyp, "tn", "dw_s5_c_re", P, HG), G, P, HG)
    dcim = _diag_blocks(mm_bd(r["xi"], dyp, "tn", "dw_s5_c_im", P, HG), G, P, HG)
    g["s5_c_re"], g["s5_c_im"] = dcre.transpose(0, 2, 1), -dcim.transpose(0, 2, 1)
    gxr, gxi, dar, dai = s5_scan_bwd(gdr, gdi, r["xr"], r["xi"], s5["ar"], s5["ai"], "s5_scan_bwd")
    dbre = _diag_blocks(mm_bd(r["u16"], gxr, "tn", "dw_s5_b_re", HG, P), G, HG, P)
    dbim = _diag_blocks(mm_bd(r["u16"], gxi, "tn", "dw_s5_b_im", HG, P), G, HG, P)
    tocol = lambda m: m.transpose(0, 2, 1).reshape(G * P, HG)
    dlr, dli, dlogdt, dbr, dbi = s5_disc_bwd(s5["lr"], s5["li"], s5["logdt"], s5["br"], s5["bi"],
                                             dar.reshape(G * P, 1), dai.reshape(G * P, 1), tocol(dbre), tocol(dbim), "s5_disc_bwd")
    g["s5_a_re"], g["s5_a_im"] = dlr.reshape(G, P), dli.reshape(G, P)
    g["s5_log_dt"] = jnp.sum(dlogdt.reshape(G, P), axis=1)
    g["s5_b_re"], g["s5_b_im"] = dbr.reshape(G, P, HG), dbi.reshape(G, P, HG)
    du = mm_bd(gxr, s5["Bre"], "nt", "d_u_re", HG, P)
    du = mm_bd(gxi, s5["Bim"], "nt", "d_u_im", HG, P, extras=(du, du_direct), epi=lambda a, b, c: (a + b + c,), out_dtypes=(bf16,))
    h1 = r["h1"]
    segs = [("Wu", du), ("Wqkv", dqkv), ("Wz", dz), ("Wba", dba), ("Wgs", dgs), ("Wgd", dgd)]
    dh1 = None
    dws = []
    for name, dseg in segs:
        if dh1 is None:
            dh1 = mm(dseg, q[name], "nt", "d_h1_" + name)
        else:
            dh1 = mm(dseg, q[name], "nt", "d_h1_" + name, extras=(dh1,), epi=_add)
        dw = mm(h1, dseg, "tn", "dw_in_" + name)
        dws.append(dw[:, :2 * H] if name == "Wba" else dw)
    g["w_in"] = jnp.concatenate(dws, axis=1)
    dx, dxb, dmix = rms_bwd(r["x"], q["mix_norm_w"], dh1, dx1, "rms_mix_bwd")
    g["mix_norm_w"] = dmix[0]
    return dx, dxb, g, carried


BIG = ("w_in", "s5_glu_w", "dn_proj_w", "w_out", "ffn_up", "ffn_down")
SHARDED_SMALL = ("dn_conv_w", "ffn_conv_w")
COL_SHARDED = ("w_in", "s5_glu_w", "dn_proj_w", "ffn_up", "dn_conv_w", "ffn_conv_w")
REPL = ("mix_norm_w", "s5_log_dt", "s5_a_re", "s5_a_im", "s5_b_re", "s5_b_im", "s5_c_re", "s5_c_im", "s5_d",
        "dn_a_log", "dn_dt_bias", "dn_norm_w", "ffn_norm_w")
WEIGHTS = ['mix_norm_w', 'w_in', 's5_log_dt', 's5_a_re', 's5_a_im', 's5_b_re', 's5_b_im', 's5_c_re', 's5_c_im', 's5_d',
           's5_glu_w', 'dn_conv_w', 'dn_a_log', 'dn_dt_bias', 'dn_norm_w', 'dn_proj_w', 'w_out', 'ffn_norm_w', 'ffn_up',
           'ffn_conv_w', 'ffn_down', 'final_norm_w']


def _join_shards(name, shards):
    return jnp.concatenate(shards, axis=-1 if name in COL_SHARDED else -2)


KINDS = {"w_in": "slot", "s5_glu_w": "col", "dn_proj_w": "col", "w_out": "slot", "ffn_up": "col", "ffn_down": "slot"}


BIG_KINDS = [KINDS[n] for n in BIG]


AG_GROUPS = (("w_in", "s5_glu_w", "dn_proj_w", "w_out"), ("ffn_up",), ("ffn_down",))


def gather_begin(shards, names):
    return ag_ici_payload([place_block(shards[n], KINDS[n], "ag_place_" + n) for n in names], [KINDS[n] for n in names])


def gather_finish(bufs):
    full = dict(zip(BIG, ag_d2d([bufs[n] for n in BIG], BIG_KINDS, "ag_d2d")))
    for n in ("w_out", "ffn_down"):
        full[n] = full[n].reshape(-1, full[n].shape[-1])
    w = full["w_in"]
    full["w_in"] = w.transpose(1, 0, 2).reshape(w.shape[1], -1)
    return full


def gather_small_sharded(shards):
    names = SHARDED_SMALL
    shapes = [shards[n].shape for n in names]
    got = chip_exchange(_flat_pack([shards[n] for n in names], f32, lanes=128, row_mult=8), "ag_small", same=True)
    per_chip = [_flat_unpack(got[j], shapes) for j in range(N_CHIPS)]
    return {n: _join_shards(n, [per_chip[j][k] for j in range(N_CHIPS)]) for k, n in enumerate(names)}


class Reduction:
    def __init__(self, tens):
        self.tens = tens
        self.pair = rs_pair_payload(tens, BIG_KINDS)

    def chips(self, theirs):
        sums = [add_half(a, t, k, "rs_pair_sum_" + n) for n, a, t, k in zip(BIG, self.tens, theirs, BIG_KINDS)]
        return rs_chips_payload(sums, BIG_KINDS)


def reduce_begin(g):
    tens = []
    for n in BIG:
        a = g[n]
        if n == "w_in":
            a = a.reshape(a.shape[0], N_CHIPS, -1).transpose(1, 0, 2)
        elif KINDS[n] == "slot":
            a = a.reshape(N_CHIPS, -1, a.shape[-1])
        tens.append(a)
    return Reduction(tens)


def reduce_finish(parts):
    halves = [sum_slots_half(x, "rs_chip_sum_" + n) for n, x in zip(BIG, parts)]
    return {n: a.reshape(-1, a.shape[-1]) for n, a in zip(BIG, rs_share(halves, "rs_share"))}


def all_reduce_small(arrs):
    shapes = [a.shape for a in arrs]
    pack = _flat_pack(arrs, f32, lanes=1024, row_mult=64)
    from_chips = chip_exchange(pack, "ar_chips", same=True)
    from_sib = sibling_exchange(from_chips, "ar_sibling")
    c = lax.axis_index("c")
    both = jnp.concatenate([jnp.where(c == 0, from_chips, from_sib), jnp.where(c == 0, from_sib, from_chips)], axis=0)
    return _flat_unpack(sum_slots(both, "ar_sum"), shapes)


def kernel(x, mix_norm_w, w_in, s5_log_dt, s5_a_re, s5_a_im, s5_b_re, s5_b_im, s5_c_re, s5_c_im, s5_d, s5_glu_w, dn_conv_w, dn_a_log, dn_dt_bias, dn_norm_w, dn_proj_w, w_out, ffn_norm_w, ffn_up, ffn_conv_w, ffn_down, final_norm_w, loss_target, m_mix_norm_w, m_w_in, m_s5_log_dt, m_s5_a_re, m_s5_a_im, m_s5_b_re, m_s5_b_im, m_s5_c_re, m_s5_c_im, m_s5_d, m_s5_glu_w, m_dn_conv_w, m_dn_a_log, m_dn_dt_bias, m_dn_norm_w, m_dn_proj_w, m_w_out, m_ffn_norm_w, m_ffn_up, m_ffn_conv_w, m_ffn_down, m_final_norm_w, v_mix_norm_w, v_w_in, v_s5_log_dt, v_s5_a_re, v_s5_a_im, v_s5_b_re, v_s5_b_im, v_s5_c_re, v_s5_c_im, v_s5_d, v_s5_glu_w, v_dn_conv_w, v_dn_a_log, v_dn_dt_bias, v_dn_norm_w, v_dn_proj_w, v_w_out, v_ffn_norm_w, v_ffn_up, v_ffn_conv_w, v_ffn_down, v_final_norm_w):
    args = locals()
    W = {n: args[n] for n in WEIGHTS}
    M = {n: args["m_" + n] for n in WEIGHTS}
    V = {n: args["v_" + n] for n in WEIGHTS}
    depth = mix_norm_w.shape[0]
    H = dn_a_log.shape[1]
    xs = x[0]
    target = loss_target[0]

    conv_full = gather_small_sharded({n: W[n] for n in SHARDED_SMALL})

    def layer_params(l, gathered):
        p = gather_finish(gathered)
        for n in REPL:
            p[n] = W[n][l]
        for n in SHARDED_SMALL:
            p[n] = conv_full[n][l]
        for n in ("mix_norm_w", "ffn_norm_w", "dn_norm_w", "dn_a_log", "dn_dt_bias", "s5_d"):
            p[n] = p[n].reshape(1, -1)
        return _layer_weights(p, H), _s5_params(p, l)

    def named(groups_results):
        return {n: buf for names, bufs in zip(AG_GROUPS, groups_results) for n, buf in zip(names, bufs)}

    layers, res = [], []
    begun = [gather_begin({n: W[n][0] for n in BIG}, names) for names in AG_GROUPS]
    gathered = [run_payload(p, "ag_ici_%d" % k) for k, p in enumerate(begun)]
    for l in range(depth):
        layers.append(layer_params(l, named(gathered)))
        nxt = (None, None, None)
        if l + 1 < depth:
            nxt = [gather_begin({n: W[n][l + 1] for n in BIG}, names) for names in AG_GROUPS]
        xs, r, gathered = layer_fwd(xs, layers[l][0], layers[l][1], H, nxt)
        res.append(r)
    dx, dxb, loss_part, dfinal = loss_head(xs, final_norm_w.reshape(1, -1), target, "loss_head")
    loss = lax.psum(loss_part[0, 0], ("x", "y", "c"))

    grads, sharded = [None] * depth, [None] * depth
    pending = None
    for l in reversed(range(depth)):
        dx, dxb, grads[l], parts = layer_bwd(dx, dxb, res[l], layers[l][0], layers[l][1], H, pending)
        if pending is not None:
            sharded[l + 1] = reduce_finish(parts)
        pending = reduce_begin(grads[l])
    theirs = run_payload(pending.pair, "rs_pair")
    sharded[0] = reduce_finish(run_payload(pending.chips(theirs), "rs_chips"))
    grad_x = dx[None]

    G = {}
    for n in BIG:
        G[n] = jnp.stack([sharded[l][n] for l in range(depth)])
    small_names = REPL + SHARDED_SMALL
    small = [jnp.stack([grads[l][n] for l in range(depth)]) for n in small_names] + [dfinal[0]]
    for n, a in zip(small_names + ("final_norm_w",), all_reduce_small(small)):
        G[n] = a
    chip = 2 * lax.axis_index("x") + lax.axis_index("y")
    for n in SHARDED_SMALL:
        cs = W[n].shape[-1]
        G[n] = lax.dynamic_slice_in_dim(G[n], chip * cs, cs, axis=-1)

    delta, new_m, new_v = {}, {}, {}
    for n in WEIGHTS:
        shape = W[n].shape
        size = math.prod(shape)
        if n in BIG + SHARDED_SMALL:
            two_d = (size // shape[-1], shape[-1])
        else:
            two_d = (size // 128, 128) if size % 128 == 0 else (1, size)
        d, nm, nv = adamw(W[n].reshape(two_d), G[n].reshape(two_d), M[n].reshape(two_d), V[n].reshape(two_d), "adamw_" + n)
        delta[n], new_m[n], new_v[n] = d.reshape(shape), nm.reshape(shape), nv.reshape(shape)
        G[n] = G[n].reshape(shape)
    return (loss, grad_x, *[G[n] for n in WEIGHTS], *[delta[n] for n in WEIGHTS],
            *[new_m[n] for n in WEIGHTS], *[new_v[n] for n in WEIGHTS])
```

```python
import functools
import math

import jax
import jax.numpy as jnp
from jax import lax
from jax.experimental import pallas as pl
from jax.experimental.pallas import tpu as pltpu

f32 = jnp.float32
bf16 = jnp.bfloat16
S = jax.ShapeDtypeStruct

NORM_EPS = 1e-6
DN_CHUNK = 64
S5_GROUP = 16
ADAM_LR, ADAM_B1, ADAM_B2, ADAM_EPS, ADAM_WD, ADAM_STEP = 0.001, 0.9, 0.999, 1e-08, 0.01, 10
VMEM_LIMIT_BYTES = 56 * 1024 * 1024
HALO = 8
MESH = pl.DeviceIdType.MESH
N_CHIPS = 4


def _pick(n, cands):
    for c in cands:
        if n % c == 0:
            return c
    return n


MM_VMEM_BUDGET = 40 * 1024 * 1024
MM_MAX_TK = 2816


def _pick_k(K):
    if K <= MM_MAX_TK or K % 128:
        return K
    return max(d for d in range(128, MM_MAX_TK + 1, 128) if K % d == 0)


MM_MAX_TN = 1536


def _pick_n(N):
    if N <= 1024 or N % 128:
        return N
    return max(d for d in range(128, MM_MAX_TN + 1, 128) if N % d == 0)


def _cparams(sem):
    return pltpu.CompilerParams(dimension_semantics=sem, vmem_limit_bytes=VMEM_LIMIT_BYTES)


_DIMS = {"nn": ((1,), (0,)), "nt": ((1,), (1,)), "tn": ((0,), (0,))}


class Win:
    def __init__(self, arr, c0, nc):
        self.arr, self.c0, self.nc = arr, c0, nc


def mm(a, b, mode, name, extras=(), epi=None, out_dtypes=(f32,), out_into=None, payload=None):
    barr, c0 = (b.arr, b.c0) if isinstance(b, Win) else (b, 0)
    if mode == "tn":
        K, M = a.shape
    else:
        M, K = a.shape
    if mode == "nt":
        N, K2 = barr.shape
        K2 = b.nc if isinstance(b, Win) else K2
        n_off, k_off = 0, c0
    else:
        K2, N = barr.shape
        N = b.nc if isinstance(b, Win) else N
        n_off, k_off = c0, 0
    assert K == K2, (a.shape, barr.shape, mode)
    o_tot, o_off, o_alias = out_into if out_into is not None else (N, 0, None)
    tm = _pick_n(M)
    tn = _pick_n(math.gcd(math.gcd(N, n_off), o_off))
    tk = _pick_k(math.gcd(K, k_off))

    def vmem_estimate(tm_):
        tiles = tm_ * tk * a.dtype.itemsize + tk * tn * barr.dtype.itemsize
        tiles += sum(tm_ * tn * e.dtype.itemsize for e in extras if e.shape == (M, N))
        tiles += sum(tm_ * tn * jnp.dtype(dt).itemsize for dt in out_dtypes)
        return 2 * tiles + 3 * tm_ * tn * 4

    while vmem_estimate(tm) > MM_VMEM_BUDGET:
        smaller = [d for d in range(128, tm, 128) if M % d == 0]
        if not smaller:
            break
        tm = smaller[-1]
    nk = K // tk
    assert M % tm == 0 and N % tn == 0 and K % tk == 0 and n_off % tn == 0 and k_off % tk == 0 and o_off % tn == 0
    nb, kb, ob = n_off // tn, k_off // tk, o_off // tn
    if mode == "tn":
        a_spec = pl.BlockSpec((tk, tm), lambda j, i, k: (k, i))
    else:
        a_spec = pl.BlockSpec((tm, tk), lambda j, i, k: (i, k))
    if mode == "nt":
        b_spec = pl.BlockSpec((tn, tk), lambda j, i, k: (j, k + kb))
    else:
        b_spec = pl.BlockSpec((tk, tn), lambda j, i, k: (k, j + nb))
    ex_specs = []
    for e in extras:
        if e.shape == (M, N):
            ex_specs.append(pl.BlockSpec((tm, tn), lambda j, i, k: (i, j)))
        elif e.shape == (1, N):
            ex_specs.append(pl.BlockSpec((1, tn), lambda j, i, k: (0, j)))
        elif e.shape == (M, 1):
            ex_specs.append(pl.BlockSpec((tm, 1), lambda j, i, k: (i, 0)))
        else:
            raise ValueError((e.shape, M, N))
    ne, no = len(extras), len(out_dtypes)
    na = 1 if o_alias is not None else 0
    assert out_into is None or no == 1
    dims = (_DIMS[mode], ((), ()))
    nj, ni = N // tn, M // tm
    split, hooks, extra = _host(payload, 2 + ne + na, no, 1 if nk > 1 else 0)

    def body(*refs):
        own, theirs = split(refs)
        a_ref, b_ref, rest = own[0], own[1], own[2:]
        ex, outs = rest[:ne], rest[ne + na:ne + na + no]
        j, i, k = pl.program_id(0), pl.program_id(1), pl.program_id(2)
        hooks(theirs, (j == 0) & (i == 0) & (k == 0), (j == nj - 1) & (i == ni - 1) & (k == nk - 1))
        p = lax.dot_general(a_ref[...].astype(bf16), b_ref[...].astype(bf16), dims, preferred_element_type=f32)

        def finish(acc):
            res = epi(acc, *[e[...] for e in ex]) if epi is not None else (acc,)
            for o, r in zip(outs, res):
                o[...] = r.astype(o.dtype)

        if nk == 1:
            finish(p)
        else:
            acc_ref = rest[-1]

            @pl.when(k == 0)
            def _():
                acc_ref[...] = p

            @pl.when(k > 0)
            def _():
                acc_ref[...] += p

            @pl.when(k == nk - 1)
            def _():
                finish(acc_ref[...])

    aliases = dict(extra["aliases"])
    if na:
        aliases[2 + ne] = 0
    res = pl.pallas_call(
        body,
        name=name,
        grid=(nj, ni, nk),
        in_specs=[a_spec, b_spec] + ex_specs + [pl.BlockSpec(memory_space=pl.ANY)] * na + extra["in_specs"],
        out_specs=[pl.BlockSpec((tm, tn), lambda j, i, k: (i, j + ob)) for _ in out_dtypes] + extra["out_specs"],
        out_shape=[S((M, o_tot), dt) for dt in out_dtypes] + extra["out_shape"],
        scratch_shapes=([pltpu.VMEM((tm, tn), f32)] if nk > 1 else []) + extra["scratch"],
        input_output_aliases=aliases,
        compiler_params=_cparams(("arbitrary",) * 3 if payload is not None else ("parallel", "parallel", "arbitrary")),
    )(a, barr, *extras, *([o_alias] if na else []), *extra["arrays"])
    outs = res[:no]
    if payload is not None:
        return tuple(outs) + (list(res[no:]),)
    return outs[0] if no == 1 else tuple(outs)


def mm_bd(a, b, mode, name, ga, gb, extras=(), epi=None, out_dtypes=(f32,)):
    T = max(1, min(256 // min(ga, gb), 1024 // max(ga, gb)))
    if mode == "tn":
        K, M = a.shape
        N = b.shape[1]
        G = M // ga
        T = min(T, G)
        tm, tn, tk = T * ga, T * gb, _pick_k(K)
        nk = K // tk
        grid = (G // T, 1, nk)
        a_spec = pl.BlockSpec((tk, tm), lambda j, i, k: (k, j))
        b_spec = pl.BlockSpec((tk, tn), lambda j, i, k: (k, j))
        o_spec = pl.BlockSpec((tm, tn), lambda j, i, k: (j, 0))
        out_shape = (M, tn)
    else:
        M = a.shape[0]
        if mode == "nn":
            G = b.shape[0] // ga
            T = min(T, G)
            kw, tn, N = T * ga, T * gb, G * gb
            b_spec = pl.BlockSpec((kw, tn), lambda j, i, k: (j, j))
        else:
            G = b.shape[0] // ga
            T = min(T, G)
            kw, tn, N = T * gb, T * ga, G * ga
            b_spec = pl.BlockSpec((tn, kw), lambda j, i, k: (j, j))
        tm = _pick(M, (1024, 512, 256, 128, 64, 32, 16, 8))
        nk = 1
        grid = (G // T, M // tm, 1)
        a_spec = pl.BlockSpec((tm, kw), lambda j, i, k: (i, j))
        o_spec = pl.BlockSpec((tm, tn), lambda j, i, k: (i, j))
        out_shape = (M, N)
    ex_specs = []
    for e in extras:
        if e.shape == out_shape:
            ex_specs.append(o_spec)
        elif e.shape == (1, out_shape[1]):
            ex_specs.append(pl.BlockSpec((1, tn), lambda j, i, k: (0, j)))
        else:
            raise ValueError((e.shape, out_shape))
    ne, no = len(extras), len(out_dtypes)
    dims = (_DIMS[mode], ((), ()))

    def body(a_ref, b_ref, *rest):
        ex, outs = rest[:ne], rest[ne:ne + no]
        p = lax.dot_general(a_ref[...].astype(bf16), b_ref[...].astype(bf16), dims, preferred_element_type=f32)

        def finish(acc):
            res = epi(acc, *[e[...] for e in ex]) if epi is not None else (acc,)
            for o, r in zip(outs, res):
                o[...] = r.astype(o.dtype)

        if nk == 1:
            finish(p)
        else:
            acc_ref = rest[-1]
            k = pl.program_id(2)

            @pl.when(k == 0)
            def _():
                acc_ref[...] = p

            @pl.when(k > 0)
            def _():
                acc_ref[...] += p

            @pl.when(k == nk - 1)
            def _():
                finish(acc_ref[...])

    outs = pl.pallas_call(
        body, name=name, grid=grid, in_specs=[a_spec, b_spec] + ex_specs, out_specs=[o_spec] * no,
        out_shape=[S(out_shape, dt) for dt in out_dtypes],
        scratch_shapes=[pltpu.VMEM((tm, tn), f32)] if nk > 1 else [],
        compiler_params=_cparams(("parallel", "parallel", "arbitrary")),
    )(a, b, *extras)
    return outs[0] if no == 1 else tuple(outs)


def _diag_blocks(tiles, G, ga, gb):
    T = tiles.shape[1] // gb
    t5 = tiles.reshape(G // T, T, ga, T, gb)
    return jnp.sum(t5 * jnp.eye(T, dtype=tiles.dtype)[None, :, None, :, None], axis=3).reshape(G, ga, gb)


def _add(acc, prev):
    return (acc + prev,)


def rowk(name, fn, L, tl, ncol, ins, outs, accs=()):
    nrow = L // tl
    assert L % tl == 0 and tl % HALO == 0
    hb = tl // HALO

    def cw_of(c_total):
        assert c_total % ncol == 0, (name, c_total, ncol)
        return c_total // ncol

    in_specs = []
    for arr, kind in ins:
        if kind == "rows":
            in_specs.append(pl.BlockSpec((tl, cw_of(arr.shape[1])), lambda j, i: (i, j)))
        elif kind == "prev":
            in_specs.append(pl.BlockSpec((HALO, cw_of(arr.shape[1])), lambda j, i: (jnp.maximum(i * hb - 1, 0), j)))
        elif kind == "next":
            in_specs.append(pl.BlockSpec((HALO, cw_of(arr.shape[1])), lambda j, i: (jnp.minimum((i + 1) * hb, nrow * hb - 1), j)))
        elif kind == "cols":
            in_specs.append(pl.BlockSpec((arr.shape[0], cw_of(arr.shape[1])), lambda j, i: (0, j)))
        elif kind == "const":
            in_specs.append(pl.BlockSpec(arr.shape, lambda j, i: (0,) * arr.ndim))
        else:
            raise ValueError(kind)
    out_specs = [pl.BlockSpec((tl, cw_of(c)), lambda j, i: (i, j)) for c, _ in outs]
    out_shape = [S((L, c), dt) for c, dt in outs]
    out_specs += [pl.BlockSpec((r, cw_of(c)), lambda j, i: (0, j)) for r, c, _ in accs]
    out_shape += [S((r, c), dt) for r, c, dt in accs]
    ni, no, na = len(ins), len(outs), len(accs)

    def body(*refs):
        i = pl.program_id(1)
        res = fn(i, nrow, *[r[...] for r in refs[:ni]])
        for o, r in zip(refs[ni:ni + no], res[:no]):
            o[...] = r.astype(o.dtype)
        for o, r in zip(refs[ni + no:ni + no + na], res[no:]):
            @pl.when(i == 0)
            def _(o=o, r=r):
                o[...] = r.astype(o.dtype)

            @pl.when(i > 0)
            def _(o=o, r=r):
                o[...] += r.astype(o.dtype)

    res = pl.pallas_call(
        body,
        name=name,
        grid=(ncol, nrow),
        in_specs=in_specs,
        out_specs=out_specs,
        out_shape=out_shape,
        compiler_params=_cparams(("parallel", "arbitrary")),
    )(*[a for a, _ in ins])
    return tuple(res)


def _sigmoid(x):
    return 1.0 / (1.0 + jnp.exp(-x))


def _silu(x):
    return x * _sigmoid(x)


def _dsilu(x):
    s = _sigmoid(x)
    return s * (1.0 + x * (1.0 - s))


def _erf(x):
    a = jnp.abs(x)
    t = 1.0 / (1.0 + 0.3275911 * a)
    poly = t * (0.254829592 + t * (-0.284496736 + t * (1.421413741 + t * (-1.453152027 + t * 1.061405429))))
    y = 1.0 - poly * jnp.exp(-a * a)
    return jnp.where(x < 0, -y, y)


def _gelu(x):
    return 0.5 * x * (1.0 + _erf(x * (2.0 ** -0.5)))


def _dgelu(x):
    cdf = 0.5 * (1.0 + _erf(x * (2.0 ** -0.5)))
    pdf = jnp.exp(-0.5 * x * x) * (1.0 / math.sqrt(2.0 * math.pi))
    return cdf + x * pdf


def _rms(x, w):
    return x * lax.rsqrt(jnp.mean(x * x, axis=-1, keepdims=True) + NORM_EPS) * w


def _rms_bwd(x, w, dy):
    d = x.shape[-1]
    r = lax.rsqrt(jnp.mean(x * x, axis=-1, keepdims=True) + NORM_EPS)
    wdy = w * dy
    dx = r * wdy - x * (r * r * r) * (jnp.sum(x * wdy, axis=-1, keepdims=True) / d)
    dw = jnp.sum(x * r * dy, axis=0, keepdims=True)
    return dx, dw


def _from_cols(cols, width):
    tl = cols[0].shape[0]
    lane = lax.broadcasted_iota(jnp.int32, (tl, width), 1)
    out = jnp.zeros((tl, width), f32)
    for n, col in enumerate(cols):
        out = jnp.where(lane == n, col, out)
    return out


def _from_rows(rows):
    c = rows[0].shape[1]
    sub = lax.broadcasted_iota(jnp.int32, (len(rows), c), 0)
    out = jnp.zeros((len(rows), c), f32)
    for n, row in enumerate(rows):
        out = jnp.where(sub == n, row, out)
    return out


def _shift_down(x, halo, s, first):
    if s == 0:
        return x
    tl = x.shape[0]
    halo = jnp.where(first, 0.0, halo)
    xx = jnp.concatenate([halo, x], axis=0)
    return pltpu.roll(xx, s, 0)[HALO:HALO + tl]


def _shift_up(x, halo, s, last):
    if s == 0:
        return x
    tl = x.shape[0]
    halo = jnp.where(last, 0.0, halo)
    xx = jnp.concatenate([x, halo], axis=0)
    return pltpu.roll(xx, tl + HALO - s, 0)[0:tl]


def _causal_conv(x, halo, w, first):
    kw = w.shape[0]
    shifted = [_shift_down(x, halo, kw - 1 - j, first) for j in range(kw)]
    out = shifted[0] * w[0:1]
    for j in range(1, kw):
        out = out + shifted[j] * w[j:j + 1]
    return out, shifted


def rms_fwd(x, w, name):
    L, D = x.shape

    def fn(i, n, xb, wb):
        return (_rms(xb, wb),)

    return rowk(name, fn, L, _pick(L, (256, 128, 64, 32, 16, 8)), 1, [(x, "rows"), (w, "const")], [(D, bf16)])[0]


def rms_bwd(x, w, dh, dres, name):
    L, D = x.shape

    def fn(i, n, xb, wb, dhb, drb):
        dx, dw = _rms_bwd(xb, wb, dhb)
        dx = dx + drb
        return dx, dx, dw

    return rowk(name, fn, L, _pick(L, (256, 128, 64, 32, 16, 8)), 1,
                [(x, "rows"), (w, "const"), (dh, "rows"), (dres, "rows")], [(D, f32), (D, bf16)], [(1, D, f32)])


def loss_head(x, w, target, name):
    L, D = x.shape

    def fn(i, n, xb, wb, tb):
        err = _rms(xb, wb) - tb
        loss = 0.5 * jnp.sum(err * err) / D
        dx, dw = _rms_bwd(xb, wb, err / D)
        return dx, dx, jnp.full((8, 128), loss, f32), dw

    return rowk(name, fn, L, _pick(L, (256, 128, 64, 32, 16, 8)), 1,
                [(x, "rows"), (w, "const"), (target, "rows")], [(D, f32), (D, bf16)], [(8, 128, f32), (1, D, f32)])


def _s5_disc_math(lr, li, logdt, br, bi):
    dt = jnp.exp(logdt)
    mag = jnp.exp(lr * dt)
    ar, ai = mag * jnp.cos(li * dt), mag * jnp.sin(li * dt)
    den = lr * lr + li * li
    nr, ni = ar - 1.0, ai
    cr = (nr * lr + ni * li) / den
    ci = (ni * lr - nr * li) / den
    return ar, ai, cr * br - ci * bi, cr * bi + ci * br


def _disc_call(body, name, ins, out_widths):
    GP = ins[0].shape[0]
    tl = _pick(GP, (512, 256, 128, 64, 32, 16, 8))
    spec = lambda w: pl.BlockSpec((tl, w), lambda i: (i, 0))
    return pl.pallas_call(body, name=name, grid=(GP // tl,),
                          in_specs=[spec(a.shape[1]) for a in ins], out_specs=[spec(w) for w in out_widths],
                          out_shape=[S((GP, w), f32) for w in out_widths], compiler_params=_cparams(("parallel",)))(*ins)


def s5_disc(lr, li, logdt, br, bi, name):
    HG = br.shape[1]

    def body(lr_ref, li_ref, dt_ref, br_ref, bi_ref, ar_ref, ai_ref, bbr_ref, bbi_ref):
        ar, ai, bbr, bbi = _s5_disc_math(lr_ref[...], li_ref[...], dt_ref[...], br_ref[...], bi_ref[...])
        ar_ref[...], ai_ref[...], bbr_ref[...], bbi_ref[...] = ar, ai, bbr, bbi

    return _disc_call(body, name, [lr, li, logdt, br, bi], [1, 1, HG, HG])


def s5_disc_bwd(lr, li, logdt, br, bi, dar, dai, dbbr, dbbi, name):
    HG = br.shape[1]

    def body(lr_ref, li_ref, dt_ref, br_ref, bi_ref, dar_ref, dai_ref, dbbr_ref, dbbi_ref, *outs):
        _, vjp = jax.vjp(_s5_disc_math, lr_ref[...], li_ref[...], dt_ref[...], br_ref[...], bi_ref[...])
        for o, g in zip(outs, vjp((dar_ref[...], dai_ref[...], dbbr_ref[...], dbbi_ref[...]))):
            o[...] = g

    return _disc_call(body, name, [lr, li, logdt, br, bi, dar, dai, dbbr, dbbi], [1, 1, 1, HG, HG])


SCAN_TB = 512


def s5_scan_fwd(bur, bui, ar, ai, name):
    L, GP = bur.shape
    cw = _pick(GP, (2048, 1024, 512, 256, 128))
    tb = _pick(L, (SCAN_TB, 128, 64, 32, 16, 8))

    def body(bur_ref, bui_ref, ar_ref, ai_ref, xr_ref, xi_ref, cr_ref, ci_ref):
        @pl.when(pl.program_id(1) == 0)
        def _():
            cr_ref[...] = jnp.zeros_like(cr_ref)
            ci_ref[...] = jnp.zeros_like(ci_ref)

        a_r, a_i = ar_ref[...], ai_ref[...]

        def step(t, carry):
            xr, xi = carry
            row = pl.ds(t, 1)
            nr = a_r * xr - a_i * xi + bur_ref[row, :]
            ni = a_r * xi + a_i * xr + bui_ref[row, :]
            xr_ref[row, :] = nr
            xi_ref[row, :] = ni
            return nr, ni

        xr, xi = lax.fori_loop(0, tb, step, (cr_ref[...], ci_ref[...]), unroll=8)
        cr_ref[...] = xr
        ci_ref[...] = xi

    blk = pl.BlockSpec((tb, cw), lambda j, i: (i, j))
    vec = pl.BlockSpec((1, cw), lambda j, i: (0, j))
    return pl.pallas_call(
        body, name=name, grid=(GP // cw, L // tb),
        in_specs=[blk, blk, vec, vec], out_specs=[blk, blk],
        out_shape=[S((L, GP), f32), S((L, GP), f32)],
        scratch_shapes=[pltpu.VMEM((1, cw), f32), pltpu.VMEM((1, cw), f32)],
        compiler_params=_cparams(("parallel", "arbitrary")),
    )(bur, bui, ar, ai)


def s5_scan_bwd(gr, gi, xr, xi, ar, ai, name):
    L, GP = gr.shape
    cw = _pick(GP, (2048, 1024, 512, 256, 128))
    tb = _pick(L, (SCAN_TB, 128, 64, 32, 16, 8))
    nt = L // tb

    def body(gr_ref, gi_ref, xr_ref, xi_ref, ar_ref, ai_ref, gxr_ref, gxi_ref, dar_ref, dai_ref, cr_ref, ci_ref):
        @pl.when(pl.program_id(1) == 0)
        def _():
            cr_ref[...] = jnp.zeros_like(cr_ref)
            ci_ref[...] = jnp.zeros_like(ci_ref)
            dar_ref[...] = jnp.zeros_like(dar_ref)
            dai_ref[...] = jnp.zeros_like(dai_ref)

        a_r, a_i = ar_ref[...], ai_ref[...]

        def step(s, carry):
            cr, ci, dr, di = carry
            row = pl.ds(tb - 1 - s, 1)
            x_r, x_i = xr_ref[row, :], xi_ref[row, :]
            dr = dr + cr * x_r + ci * x_i
            di = di + ci * x_r - cr * x_i
            nr = gr_ref[row, :] + a_r * cr + a_i * ci
            ni = gi_ref[row, :] + a_r * ci - a_i * cr
            gxr_ref[row, :] = nr
            gxi_ref[row, :] = ni
            return nr, ni, dr, di

        cr, ci, dr, di = lax.fori_loop(0, tb, step, (cr_ref[...], ci_ref[...], dar_ref[...], dai_ref[...]), unroll=8)
        cr_ref[...] = cr
        ci_ref[...] = ci
        dar_ref[...] = dr
        dai_ref[...] = di

    blk = pl.BlockSpec((tb, cw), lambda j, i: (nt - 1 - i, j))
    vec = pl.BlockSpec((1, cw), lambda j, i: (0, j))
    return pl.pallas_call(
        body, name=name, grid=(GP // cw, nt),
        in_specs=[blk, blk, blk, blk, vec, vec], out_specs=[blk, blk, vec, vec],
        out_shape=[S((L, GP), f32), S((L, GP), f32), S((1, GP), f32), S((1, GP), f32)],
        scratch_shapes=[pltpu.VMEM((1, cw), f32), pltpu.VMEM((1, cw), f32)],
        compiler_params=_cparams(("parallel", "arbitrary")),
    )(gr, gi, xr, xi, ar, ai)


def _dn_heads_math(cq, ck, cv, braw, araw, alog, dtb, dk):
    q, k, v = _silu(cq), _silu(ck), _silu(cv)
    q = q * lax.rsqrt(jnp.sum(q * q, axis=-1, keepdims=True) + NORM_EPS) * (dk ** -0.5)
    k = k * lax.rsqrt(jnp.sum(k * k, axis=-1, keepdims=True) + NORM_EPS)
    beta = _sigmoid(braw)
    g = -jnp.exp(alog) * jax.nn.softplus(araw + dtb)
    return q, k, v, beta, g


def dn_prep(qkv, convw, ba, alog, dtb, H, name):
    L, W = qkv.shape
    hk = W // 3
    dk = hk // H

    def fn(i, n, xb, hb, wb, bab, alb, dtbb):
        c, _ = _causal_conv(xb, hb, wb, i == 0)
        qs, ks, vs, bs, gs = [], [], [], [], []
        for h in range(H):
            sl = lambda o: c[:, o + h * dk:o + (h + 1) * dk]
            q, k, v, beta, g = _dn_heads_math(sl(0), sl(hk), sl(2 * hk), bab[:, h:h + 1], bab[:, H + h:H + h + 1],
                                              alb[:, h:h + 1], dtbb[:, h:h + 1], dk)
            qs.append(q), ks.append(k), vs.append(v), bs.append(beta), gs.append(g)
        cat = lambda xs: jnp.concatenate(xs, axis=1)
        return cat(qs), cat(ks), cat(vs), _from_cols(bs + gs, 2 * H)

    return rowk(name, fn, L, _pick(L, (128, 64, 32, 16, 8)), 1,
                [(qkv, "rows"), (qkv, "prev"), (convw, "const"), (ba, "rows"), (alog, "const"), (dtb, "const")],
                [(hk, f32), (hk, f32), (hk, f32), (2 * H, f32)])


def dn_prep_bwd(qkv, convw, ba, alog, dtb, dq, dk_, dv, dbg, H, name):
    L, W = qkv.shape
    hk = W // 3
    dk = hk // H
    kw = convw.shape[0]
    nba = ba.shape[1]

    def fn(i, n, xb, hb, wb, bab, alb, dtbb, dqb, dkb, dvb, dbgb):
        c, shifted = _causal_conv(xb, hb, wb, i == 0)
        dcs = [None] * (3 * H)
        dbr, dar, dal, ddt = [], [], [], []
        for h in range(H):
            sl = lambda a, o: a[:, o + h * dk:o + (h + 1) * dk]
            args = (sl(c, 0), sl(c, hk), sl(c, 2 * hk), bab[:, h:h + 1], bab[:, H + h:H + h + 1],
                    alb[:, h:h + 1], dtbb[:, h:h + 1])
            _, vjp = jax.vjp(lambda *a: _dn_heads_math(*a, dk), *args)
            g = vjp((sl(dqb, 0), sl(dkb, 0), sl(dvb, 0), dbgb[:, h:h + 1], dbgb[:, H + h:H + h + 1]))
            dcs[h], dcs[H + h], dcs[2 * H + h] = g[0], g[1], g[2]
            dbr.append(g[3]), dar.append(g[4]), dal.append(g[5]), ddt.append(g[6])
        dc = jnp.concatenate(dcs, axis=1)
        dba = _from_cols(dbr + dar, nba)
        dw = _from_rows([jnp.sum(dc * shifted[j], axis=0, keepdims=True) for j in range(kw)])
        return dc, dba, dw, _from_cols(dal, H), _from_cols(ddt, H)

    return rowk(name, fn, L, _pick(L, (128, 64, 32, 16, 8)), 1,
                [(qkv, "rows"), (qkv, "prev"), (convw, "const"), (ba, "rows"), (alog, "const"), (dtb, "const"),
                 (dq, "rows"), (dk_, "rows"), (dv, "rows"), (dbg, "rows")],
                [(W, f32), (nba, bf16)], [(kw, W, f32), (1, H, f32), (1, H, f32)])


def conv_t(dc, w, name):
    L, C = dc.shape
    kw = w.shape[0]
    ncol = C // _pick(C, (1536, 1408, 1024, 768, 512, 256, 128))

    def fn(i, n, db, hb, wb):
        out = db * wb[kw - 1:kw]
        for j in range(kw - 1):
            out = out + _shift_up(db, hb, kw - 1 - j, i == n - 1) * wb[j:j + 1]
        return (out,)

    return rowk(name, fn, L, _pick(L, (256, 128, 64, 32, 16, 8)), ncol,
                [(dc, "rows"), (dc, "next"), (w, "cols")], [(C, bf16)])[0]


_BDIMS = {"nn": (((2,), (1,)), ((0,), (0,))), "nt": (((2,), (2,)), ((0,), (0,))), "tn": (((1,), (1,)), ((0,), (0,)))}


def _bdot(a, b, mode):
    return lax.dot_general(a.astype(bf16), b.astype(bf16), _BDIMS[mode], preferred_element_type=f32)


def _split16(a):
    hi = a.astype(bf16)
    return hi, (a - hi.astype(f32)).astype(bf16)


def _hdot(a, b, mode):
    ah, al = _split16(a)
    bh, bl = _split16(b)
    d = lambda x, y: lax.dot_general(x, y, _BDIMS[mode], preferred_element_type=f32)
    return d(ah, bh) + (d(ah, bl) + d(al, bh))


def _make_dot(raw):
    @functools.partial(jax.custom_vjp, nondiff_argnums=(2,))
    def dot(a, b, mode):
        return raw(a, b, mode)

    def fwd(a, b, mode):
        return raw(a, b, mode), (a, b)

    def bwd(mode, res, ct):
        a, b = res
        if mode == "nn":
            return raw(ct, b, "nt"), raw(a, ct, "tn")
        if mode == "nt":
            return raw(ct, b, "nn"), raw(ct, a, "tn")
        return raw(b, ct, "nt"), raw(a, ct, "nn")

    dot.defvjp(fwd, bwd)
    return dot


_dot16 = _make_dot(_bdot)
_dot32 = _make_dot(_hdot)


@jax.custom_vjp
def _unit_lower_inv(lmat):
    c = lmat.shape[-1]
    eye = (lax.broadcasted_iota(jnp.int32, (c, c), 0) == lax.broadcasted_iota(jnp.int32, (c, c), 1)).astype(f32)
    p = -lmat
    t = eye + p
    for _ in range(int(math.log2(c)) - 1):
        p = _hdot(p, p, "nn")
        t = t + _hdot(t, p, "nn")
    return t


def _uli_fwd(lmat):
    t = _unit_lower_inv(lmat)
    return t, t


def _uli_bwd(t, dt):
    return (-_hdot(_hdot(t, dt, "tn"), t, "nt"),)


_unit_lower_inv.defvjp(_uli_fwd, _uli_bwd)


def _dn_chunk_math(s_in, q, k, v, gcol, grow, bcol):
    c = q.shape[1]
    ri = lax.broadcasted_iota(jnp.int32, (c, c), 0)
    ci = lax.broadcasted_iota(jnp.int32, (c, c), 1)
    tril = (ri >= ci).astype(f32)
    strict = (ri > ci).astype(f32)
    gc_col = jnp.sum(tril * grow, axis=2, keepdims=True)
    gc_row = jnp.sum((1.0 - strict) * gcol, axis=1, keepdims=True)
    g_last = jnp.sum(gcol, axis=1, keepdims=True)
    decay = jnp.exp((gc_col - gc_row) * tril) * tril
    kb = k * bcol
    vb = v * bcol
    lmat = _dot16(kb, k, "nt") * decay * strict
    t = _unit_lower_inv(lmat)
    u = _dot32(t, vb, "nn")
    w = _dot32(t, kb * jnp.exp(gc_col), "nn")
    attn = _dot16(q, k, "nt") * decay
    v_new = u - _dot16(w, s_in, "nn")
    o = _dot16(q * jnp.exp(gc_col), s_in, "nn") + _dot16(attn, v_new, "nn")
    s_out = s_in * jnp.exp(g_last) + _dot16(k * jnp.exp(g_last - gc_col), v_new, "tn")
    return o, s_out


def _dn_load(q_ref, k_ref, v_ref, bg_ref, bgt_ref, H, dk):
    heads = lambda ref: jnp.stack([ref[:, h * dk:(h + 1) * dk] for h in range(H)])
    bgb, bgtb = bg_ref[...], bgt_ref[0]
    gcol = jnp.stack([bgb[:, H + h:H + h + 1] for h in range(H)])
    bcol = jnp.stack([bgb[:, h:h + 1] for h in range(H)])
    grow = jnp.stack([bgtb[H + h:H + h + 1, :] for h in range(H)])
    return heads(q_ref), heads(k_ref), heads(v_ref), gcol, grow, bcol


def _host(payload, n_in, n_out, n_scratch):
    pi = len(payload.arrays) if payload is not None else 0
    po = len(payload.out_shapes) if payload is not None else 0

    def split(refs):
        own = refs[:n_in] + refs[n_in + pi:n_in + pi + n_out] + refs[n_in + pi + n_out + po:n_in + pi + n_out + po + n_scratch]
        theirs = (refs[n_in:n_in + pi], refs[n_in + pi + n_out:n_in + pi + n_out + po], refs[n_in + pi + n_out + po + n_scratch:])
        return own, theirs

    def hooks(theirs, first, last):
        if payload is None:
            return

        @pl.when(first)
        def _():
            payload.start(*theirs)

        @pl.when(last)
        def _():
            payload.finish(*theirs)

    extra = dict(in_specs=[_ANY] * pi, out_specs=[_ANY] * po, out_shape=list(payload.out_shapes) if payload else [],
                 scratch=list(payload.sems) if payload else [], arrays=list(payload.arrays) if payload else [],
                 aliases={n_in + a: n_out + b for a, b in payload.aliases.items()} if payload else {})
    return split, hooks, extra


def dn_chunk_fwd(qn, kn, vv, bg, bgt, H, name, payload=None):
    L, hk = qn.shape
    dk = hk // H
    c = DN_CHUNK
    nc = L // c
    split, hooks, extra = _host(payload, 5, 2, 1)

    def body(*refs):
        (q_ref, k_ref, v_ref, bg_ref, bgt_ref, o_ref, ss_ref, s_ref), theirs = split(refs)
        hooks(theirs, pl.program_id(0) == 0, pl.program_id(0) == nc - 1)

        @pl.when(pl.program_id(0) == 0)
        def _():
            s_ref[...] = jnp.zeros_like(s_ref)

        s_in = s_ref[...]
        ss_ref[0] = s_in
        o, s_out = _dn_chunk_math(s_in, *_dn_load(q_ref, k_ref, v_ref, bg_ref, bgt_ref, H, dk))
        for h in range(H):
            o_ref[:, h * dk:(h + 1) * dk] = o[h]
        s_ref[...] = s_out

    row = lambda w: pl.BlockSpec((c, w), lambda n: (n, 0))
    res = pl.pallas_call(
        body, name=name, grid=(nc,),
        in_specs=[row(hk), row(hk), row(hk), row(2 * H), pl.BlockSpec((1, 2 * H, c), lambda n: (n, 0, 0))] + extra["in_specs"],
        out_specs=[row(hk), pl.BlockSpec((1, H, dk, dk), lambda n: (n, 0, 0, 0))] + extra["out_specs"],
        out_shape=[S((L, hk), f32), S((nc, H, dk, dk), f32)] + extra["out_shape"],
        scratch_shapes=[pltpu.VMEM((H, dk, dk), f32)] + extra["scratch"],
        input_output_aliases=extra["aliases"],
        compiler_params=_cparams(("arbitrary",)),
    )(qn, kn, vv, bg, bgt, *extra["arrays"])
    return res[0], res[1], list(res[2:])


def dn_chunk_bwd(qn, kn, vv, bg, bgt, ss, do, H, name, payload=None):
    L, hk = qn.shape
    dk = hk // H
    c = DN_CHUNK
    nc = L // c
    split, hooks, extra = _host(payload, 7, 5, 1)

    def body(*refs):
        (q_ref, k_ref, v_ref, bg_ref, bgt_ref, ss_ref, do_ref, dq_ref, dk_ref, dv_ref, dbg_ref, dgt_ref, ds_ref), theirs = split(refs)
        hooks(theirs, pl.program_id(0) == 0, pl.program_id(0) == nc - 1)

        @pl.when(pl.program_id(0) == 0)
        def _():
            ds_ref[...] = jnp.zeros_like(ds_ref)

        args = (ss_ref[0],) + _dn_load(q_ref, k_ref, v_ref, bg_ref, bgt_ref, H, dk)
        _, vjp = jax.vjp(_dn_chunk_math, *args)
        do = jnp.stack([do_ref[:, h * dk:(h + 1) * dk] for h in range(H)])
        ds, dq, dkk, dv, dgcol, dgrow, dbcol = vjp((do, ds_ref[...]))
        ds_ref[...] = ds
        for h in range(H):
            sl = slice(h * dk, (h + 1) * dk)
            dq_ref[:, sl], dk_ref[:, sl], dv_ref[:, sl] = dq[h], dkk[h], dv[h]
        dbg_ref[...] = _from_cols([dbcol[h] for h in range(H)] + [dgcol[h] for h in range(H)], 2 * H)
        dgt_ref[0] = _from_rows([dgrow[h] for h in range(H)])

    row = lambda w: pl.BlockSpec((c, w), lambda n: (nc - 1 - n, 0))
    res = pl.pallas_call(
        body, name=name, grid=(nc,),
        in_specs=[row(hk), row(hk), row(hk), row(2 * H), pl.BlockSpec((1, 2 * H, c), lambda n: (nc - 1 - n, 0, 0)),
                  pl.BlockSpec((1, H, dk, dk), lambda n: (nc - 1 - n, 0, 0, 0)), row(hk)] + extra["in_specs"],
        out_specs=[row(hk), row(hk), row(hk), row(2 * H), pl.BlockSpec((1, H, c), lambda n: (nc - 1 - n, 0, 0))] + extra["out_specs"],
        out_shape=[S((L, hk), f32), S((L, hk), f32), S((L, hk), f32), S((L, 2 * H), f32), S((nc, H, c), f32)] + extra["out_shape"],
        scratch_shapes=[pltpu.VMEM((H, dk, dk), f32)] + extra["scratch"],
        input_output_aliases=extra["aliases"],
        compiler_params=_cparams(("arbitrary",)),
    )(qn, kn, vv, bg, bgt, ss, do, *extra["arrays"])
    return tuple(res[:5]) + (list(res[5:]),)


def _dn_gate_math(o, z, w):
    return _rms(o, w) * _silu(z)


def dn_gate(o, z, w, H, name):
    L, hv = o.shape
    dv = hv // H

    def fn(i, n, ob, zb, wb):
        return (jnp.concatenate([_dn_gate_math(ob[:, h * dv:(h + 1) * dv], zb[:, h * dv:(h + 1) * dv], wb)
                                 for h in range(H)], axis=1),)

    return rowk(name, fn, L, _pick(L, (256, 128, 64, 32, 16, 8)), 1, [(o, "rows"), (z, "rows"), (w, "const")], [(hv, bf16)])[0]


def dn_gate_bwd(o, z, w, dy, H, name):
    L, hv = o.shape
    dv = hv // H

    def fn(i, n, ob, zb, wb, dyb):
        dos, dzs, dw = [], [], 0.0
        for h in range(H):
            sl = slice(h * dv, (h + 1) * dv)
            _, vjp = jax.vjp(_dn_gate_math, ob[:, sl], zb[:, sl], wb)
            a, b, c = vjp(dyb[:, sl])
            dos.append(a), dzs.append(b)
            dw = dw + c
        return jnp.concatenate(dos, axis=1), jnp.concatenate(dzs, axis=1), dw

    return rowk(name, fn, L, _pick(L, (256, 128, 64, 32, 16, 8)), 1,
                [(o, "rows"), (z, "rows"), (w, "const"), (dy, "rows")], [(hv, f32), (hv, bf16)], [(1, dv, f32)])


def ffn_mid(ua, uv, wa, wv, name):
    L, F = ua.shape
    ncol = F // _pick(F, (1408, 1024, 512, 256, 128))

    def fn(i, n, ab, ah, vb, vh, wab, wvb):
        ca, _ = _causal_conv(ab, ah, wab, i == 0)
        cv, _ = _causal_conv(vb, vh, wvb, i == 0)
        return (_silu(ca) * cv,)

    return rowk(name, fn, L, _pick(L, (256, 128, 64, 32, 16, 8)), ncol,
                [(ua, "rows"), (ua, "prev"), (uv, "rows"), (uv, "prev"), (wa, "cols"), (wv, "cols")], [(F, bf16)])[0]


def ffn_mid_bwd(ua, uv, wa, wv, dh, name):
    L, F = ua.shape
    kw = wa.shape[0]
    ncol = F // _pick(F, (1408, 1024, 512, 256, 128))

    def fn(i, n, ab, ap, an, vb, vp, vn, wab, wvb, dhb, dhn):
        tl = ab.shape[0]
        last = i == n - 1
        ext = lambda blk, nxt: jnp.concatenate([blk, jnp.where(last, 0.0, nxt)], axis=0)
        dhe = ext(dhb, dhn)
        ca, sa = _causal_conv(ext(ab, an), ap, wab, i == 0)
        cv, sv = _causal_conv(ext(vb, vn), vp, wvb, i == 0)
        dca = dhe * cv * _dsilu(ca)
        dcv = dhe * _silu(ca)
        dwa = _from_rows([jnp.sum(dca[:tl] * sa[j][:tl], axis=0, keepdims=True) for j in range(kw)])
        dwv = _from_rows([jnp.sum(dcv[:tl] * sv[j][:tl], axis=0, keepdims=True) for j in range(kw)])

        def conv_t_rows(dc, w):
            out = dc[:tl] * w[kw - 1:kw]
            for j in range(kw - 1):
                out = out + pltpu.roll(dc, tl + HALO - (kw - 1 - j), 0)[:tl] * w[j:j + 1]
            return out

        return conv_t_rows(dca, wab), conv_t_rows(dcv, wvb), dwa, dwv

    return rowk(name, fn, L, _pick(L, (256, 128, 64, 32, 16, 8)), ncol,
                [(ua, "rows"), (ua, "prev"), (ua, "next"), (uv, "rows"), (uv, "prev"), (uv, "next"),
                 (wa, "cols"), (wv, "cols"), (dh, "rows"), (dh, "next")],
                [(F, bf16), (F, bf16)], [(kw, F, f32), (kw, F, f32)])


def _merge_epi(acc, gs, gd, ga, gb):
    return acc, _sigmoid(gs) * ga * _sigmoid(gb) + _sigmoid(gd) * acc


def merge_bwd(dm, gs, gd, ga, gb, brdn, name):
    L, D = dm.shape
    ncol = D // _pick(D, (1024, 512, 256, 128))

    def fn(i, n, dmb, gsb, gdb, gab, gbb, brb):
        ss, sd, sb = _sigmoid(gsb), _sigmoid(gdb), _sigmoid(gbb)
        br_s5 = gab * sb
        dbr_s5 = dmb * ss
        return (dmb * br_s5 * ss * (1.0 - ss), dmb * brb * sd * (1.0 - sd), dbr_s5 * sb,
                dbr_s5 * gab * sb * (1.0 - sb), dmb * sd)

    return rowk(name, fn, L, _pick(L, (256, 128, 64, 32, 16, 8)), ncol,
                [(a, "rows") for a in (dm, gs, gd, ga, gb, brdn)], [(D, bf16)] * 5)


def s5_out_bwd(dy, ypre, u, d, name):
    L, W = dy.shape

    def fn(i, n, dyb, yb, ub, db):
        dyp = dyb * _dgelu(yb)
        return dyp, db * dyp, jnp.sum(dyp * ub, axis=0, keepdims=True)

    return rowk(name, fn, L, _pick(L, (256, 128, 64, 32, 16, 8)), 1,
                [(dy, "rows"), (ypre, "rows"), (u, "rows"), (d, "const")], [(W, bf16), (W, f32)], [(1, W, f32)])


def _s5_y_epi(acc, y1, u, d):
    ypre = acc + y1 + d * u
    return ypre, _gelu(ypre)


def adamw(w, g, m, v, name):
    R, C = w.shape
    tl = _pick(R, (256, 128, 64, 32, 16, 8))
    if R * C * 4 <= 2 * 1024 * 1024:
        tl = R

    def body(w_ref, g_ref, m_ref, v_ref, d_ref, nm_ref, nv_ref):
        gg = g_ref[...]
        nm = ADAM_B1 * m_ref[...] + (1.0 - ADAM_B1) * gg
        nv = ADAM_B2 * v_ref[...] + (1.0 - ADAM_B2) * (gg * gg)
        m_hat = nm / (1.0 - ADAM_B1 ** ADAM_STEP)
        v_hat = nv / (1.0 - ADAM_B2 ** ADAM_STEP)
        d_ref[...] = -ADAM_LR * (m_hat / (jnp.sqrt(v_hat) + ADAM_EPS) + ADAM_WD * w_ref[...])
        nm_ref[...] = nm
        nv_ref[...] = nv

    blk = pl.BlockSpec((tl, C), lambda i: (i, 0))
    return pl.pallas_call(body, name=name, grid=(R // tl,), in_specs=[blk] * 4, out_specs=[blk] * 3,
                          out_shape=[S((R, C), f32)] * 3, compiler_params=_cparams(("parallel",)))(w, g, m, v)


def sum_slots(x, name, out_dtype=f32):
    n, R, C = x.shape
    tl = _pick(R, (512, 256, 128, 64, 32, 16, 8))

    def body(x_ref, o_ref):
        acc = x_ref[0].astype(f32)
        for s in range(1, n):
            acc = acc + x_ref[s].astype(f32)
        o_ref[...] = acc.astype(o_ref.dtype)

    return pl.pallas_call(body, name=name, grid=(R // tl,),
                          in_specs=[pl.BlockSpec((n, tl, C), lambda i: (0, i, 0))],
                          out_specs=pl.BlockSpec((tl, C), lambda i: (i, 0)),
                          out_shape=S((R, C), out_dtype), compiler_params=_cparams(("parallel",)))(x)


_ANY = pl.BlockSpec(memory_space=pl.ANY)


def _coords():
    return lax.axis_index("x"), lax.axis_index("y"), lax.axis_index("c")


def chip_exchange(src, name, same=False):
    out_shape = (N_CHIPS,) + src.shape if same else src.shape
    assert out_shape[0] == N_CHIPS

    def body(src_ref, out_ref, send_sems, recv_sems, local_sem):
        x, y, c = _coords()
        me = 2 * x + y
        slot = (lambda j: src_ref) if same else (lambda j: src_ref.at[j])
        mine = pltpu.make_async_copy(slot(me), out_ref.at[me], local_sem)
        mine.start()
        peers = [(1 - x, y), (x, 1 - y), (1 - x, 1 - y)]
        copies = []
        for k, (px, py) in enumerate(peers):
            cp = pltpu.make_async_remote_copy(
                src_ref=slot(2 * px + py), dst_ref=out_ref.at[me],
                send_sem=send_sems.at[k], recv_sem=recv_sems.at[k],
                device_id=(px, py, c), device_id_type=MESH)
            cp.start()
            copies.append(cp)
        for k, (px, py) in enumerate(peers):
            pltpu.make_async_remote_copy(
                src_ref=slot(me), dst_ref=out_ref.at[2 * px + py],
                send_sem=send_sems.at[k], recv_sem=recv_sems.at[k],
                device_id=(px, py, c), device_id_type=MESH).wait_recv()
        for cp in copies:
            cp.wait_send()
        mine.wait()

    return pl.pallas_call(
        body, name=name, in_specs=[_ANY], out_specs=_ANY, out_shape=S(out_shape, src.dtype),
        scratch_shapes=[pltpu.SemaphoreType.DMA((3,)), pltpu.SemaphoreType.DMA((3,)), pltpu.SemaphoreType.DMA],
    )(src)


def sibling_exchange(src, name):
    def body(src_ref, out_ref, send_sem, recv_sem):
        x, y, c = _coords()
        cp = pltpu.make_async_remote_copy(src_ref=src_ref, dst_ref=out_ref, send_sem=send_sem, recv_sem=recv_sem,
                                          device_id=(x, y, 1 - c), device_id_type=MESH)
        cp.start()
        cp.wait()

    return pl.pallas_call(
        body, name=name, in_specs=[_ANY], out_specs=_ANY, out_shape=S(src.shape, src.dtype),
        scratch_shapes=[pltpu.SemaphoreType.DMA, pltpu.SemaphoreType.DMA],
    )(src)


def _rows(c, half_rows):
    return pl.ds(pl.multiple_of(c * half_rows, 16), half_rows)


def _slot(ref, kind, j, rows=None):
    if kind == "slot":
        return ref.at[j] if rows is None else ref.at[j, rows]
    cs = ref.shape[1] // N_CHIPS
    cols = pl.ds(pl.multiple_of(j * cs, 128), cs)
    return ref.at[:, cols] if rows is None else ref.at[rows, cols]


def _sems(n):
    return [pltpu.SemaphoreType.DMA((n,))]


class Payload:
    def __init__(self, arrays, out_shapes, aliases, sems, start, finish):
        self.arrays, self.out_shapes, self.aliases, self.sems = arrays, out_shapes, aliases, sems
        self.start, self.finish = start, finish


def run_payload(p, name):
    ni, no = len(p.arrays), len(p.out_shapes)

    def body(*refs):
        args = (refs[:ni], refs[ni:ni + no], refs[ni + no:])
        p.start(*args)
        p.finish(*args)

    return pl.pallas_call(body, name=name, in_specs=[_ANY] * ni, out_specs=[_ANY] * no, out_shape=p.out_shapes,
                          input_output_aliases=p.aliases, scratch_shapes=p.sems)(*p.arrays)


def _half_rows(bufs, kinds):
    return [(a.shape[1] if k == "slot" else a.shape[0]) // 2 for a, k in zip(bufs, kinds)]


def ag_ici_payload(bufs, kinds):
    n = len(bufs)
    rh = _half_rows(bufs, kinds)

    def copies(outs, sems):
        send, recv = sems
        x, y, c = _coords()
        me = 2 * x + y
        res = []
        for t in range(n):
            own = _slot(outs[t], kinds[t], me, _rows(c, rh[t]))
            for k, (px, py) in enumerate([(1 - x, y), (x, 1 - y), (1 - x, 1 - y)]):
                landed = _slot(outs[t], kinds[t], 2 * px + py, _rows(c, rh[t]))
                sem = dict(send_sem=send.at[3 * t + k], recv_sem=recv.at[3 * t + k], device_id=(px, py, c), device_id_type=MESH)
                res.append((pltpu.make_async_remote_copy(src_ref=own, dst_ref=own, **sem),
                            pltpu.make_async_remote_copy(src_ref=landed, dst_ref=landed, **sem)))
        return res

    def start(ins, outs, sems):
        for mine, _ in copies(outs, sems):
            mine.start()

    def finish(ins, outs, sems):
        both = copies(outs, sems)
        for _, theirs in both:
            theirs.wait_recv()
        for mine, _ in both:
            mine.wait_send()

    return Payload(bufs, [S(a.shape, a.dtype) for a in bufs], {t: t for t in range(n)}, _sems(3 * n) * 2, start, finish)


def ag_d2d(bufs, kinds, name):
    n = len(bufs)
    rh = _half_rows(bufs, kinds)

    def body(*refs):
        outs = refs[n:2 * n]
        send, recv = refs[2 * n:]
        x, y, c = _coords()
        sib = dict(device_id=(x, y, 1 - c), device_id_type=MESH)
        sends = []
        for t in range(n):
            for k, (px, py) in enumerate([(1 - x, y), (x, 1 - y), (1 - x, 1 - y)]):
                landed = _slot(outs[t], kinds[t], 2 * px + py, _rows(c, rh[t]))
                cp = pltpu.make_async_remote_copy(src_ref=landed, dst_ref=landed, send_sem=send.at[3 * t + k],
                                                  recv_sem=recv.at[3 * t + k], **sib)
                cp.start()
                sends.append(cp)
        for t in range(n):
            for k, (px, py) in enumerate([(1 - x, y), (x, 1 - y), (1 - x, 1 - y)]):
                other = _slot(outs[t], kinds[t], 2 * px + py, _rows(1 - c, rh[t]))
                pltpu.make_async_remote_copy(src_ref=other, dst_ref=other, send_sem=send.at[3 * t + k],
                                             recv_sem=recv.at[3 * t + k], **sib).wait_recv()
        for cp in sends:
            cp.wait_send()

    return pl.pallas_call(body, name=name, in_specs=[_ANY] * n, out_specs=[_ANY] * n,
                          out_shape=[S(a.shape, a.dtype) for a in bufs], input_output_aliases={t: t for t in range(n)},
                          scratch_shapes=_sems(3 * n) * 2)(*bufs)


def rs_pair_payload(grads, kinds):
    n = len(grads)

    def half_shape(a, k):
        return (a.shape[0], a.shape[1] // 2, a.shape[2]) if k == "slot" else (a.shape[0] // 2, a.shape[1])

    def half(ref, k, c):
        return ref.at[:, _rows(c, ref.shape[1] // 2)] if k == "slot" else ref.at[_rows(c, ref.shape[0] // 2)]

    shapes = [S(half_shape(a, k), a.dtype) for a, k in zip(grads, kinds)]

    def copies(srcs, got, sems):
        send, recv = sems
        x, y, c = _coords()
        return [pltpu.make_async_remote_copy(src_ref=half(srcs[t], kinds[t], 1 - c), dst_ref=got[t], send_sem=send.at[t],
                                             recv_sem=recv.at[t], device_id=(x, y, 1 - c), device_id_type=MESH)
                for t in range(n)]

    def start(ins, outs, sems):
        for cp in copies(ins, outs, sems):
            cp.start()

    def finish(ins, outs, sems):
        for cp in copies(ins, outs, sems):
            cp.wait()

    return Payload(grads, shapes, {}, _sems(n) * 2, start, finish)


def rs_chips_payload(sums, kinds):
    n = len(sums)

    def part_shape(a, k):
        return a.shape[1:] if k == "slot" else (a.shape[0], a.shape[1] // N_CHIPS)

    shapes = [S((N_CHIPS,) + part_shape(a, k), a.dtype) for a, k in zip(sums, kinds)]

    def copies(srcs, outs, sems):
        send, recv, lsem = sems
        x, y, c = _coords()
        me = 2 * x + y
        local, remote = [], []
        for t in range(n):
            local.append(pltpu.make_async_copy(_slot(srcs[t], kinds[t], me), outs[t].at[me], lsem.at[t]))
            for k, (px, py) in enumerate([(1 - x, y), (x, 1 - y), (1 - x, 1 - y)]):
                landed = outs[t].at[2 * px + py]
                sem = dict(send_sem=send.at[3 * t + k], recv_sem=recv.at[3 * t + k], device_id=(px, py, c), device_id_type=MESH)
                remote.append((pltpu.make_async_remote_copy(src_ref=_slot(srcs[t], kinds[t], 2 * px + py), dst_ref=outs[t].at[me], **sem),
                               pltpu.make_async_remote_copy(src_ref=landed, dst_ref=landed, **sem)))
        return local, remote

    def start(ins, outs, sems):
        local, remote = copies(ins, outs, sems)
        for cp in local:
            cp.start()
        for mine, _ in remote:
            mine.start()

    def finish(ins, outs, sems):
        local, remote = copies(ins, outs, sems)
        for _, theirs in remote:
            theirs.wait_recv()
        for mine, _ in remote:
            mine.wait_send()
        for cp in local:
            cp.wait()

    return Payload(sums, shapes, {}, _sems(3 * n) * 2 + _sems(n), start, finish)


def rs_share(gs, name):
    n = len(gs)

    def body(*refs):
        outs = refs[n:2 * n]
        send, recv = refs[2 * n:]
        x, y, c = _coords()
        started = []
        for t in range(n):
            cp = pltpu.make_async_remote_copy(src_ref=outs[t].at[c], dst_ref=outs[t].at[c], send_sem=send.at[t],
                                              recv_sem=recv.at[t], device_id=(x, y, 1 - c), device_id_type=MESH)
            cp.start()
            started.append(cp)
        for cp in started:
            cp.wait()

    return pl.pallas_call(body, name=name, in_specs=[_ANY] * n, out_specs=[_ANY] * n,
                          out_shape=[S(a.shape, a.dtype) for a in gs], input_output_aliases={t: t for t in range(n)},
                          scratch_shapes=_sems(n) * 2)(*gs)


def _core_index():
    return lax.axis_index("c").astype(jnp.int32).reshape(1)


def place_block(blk, kind, name):
    R, cs = blk.shape
    tl = _pick(R, (256, 128, 64, 32, 16, 8))

    def body(me_ref, b_ref, o_ref):
        o_ref[...] = b_ref[...].astype(o_ref.dtype)

    if kind == "slot":
        out_spec, out_shape = pl.BlockSpec((None, tl, cs), lambda i, me: (me[0], i, 0)), (N_CHIPS, R, cs)
    else:
        out_spec, out_shape = pl.BlockSpec((tl, cs), lambda i, me: (i, me[0])), (R, N_CHIPS * cs)
    me = (2 * lax.axis_index("x") + lax.axis_index("y")).astype(jnp.int32).reshape(1)
    return pl.pallas_call(
        body, name=name,
        grid_spec=pltpu.PrefetchScalarGridSpec(num_scalar_prefetch=1, grid=(R // tl,),
                                               in_specs=[pl.BlockSpec((tl, cs), lambda i, me: (i, 0))], out_specs=out_spec),
        out_shape=S(out_shape, bf16), compiler_params=_cparams(("parallel",)),
    )(me, blk)


def add_half(g, got, kind, name):
    g4 = g.reshape((g.shape[0] if kind == "slot" else 1, 2, -1, g.shape[-1]))
    parts, _, rh, C = g4.shape
    got3 = got.reshape(parts, rh, C)
    tl = _pick(rh, (256, 128, 64, 32, 16, 8))

    def body(c_ref, g_ref, got_ref, o_ref):
        o_ref[...] = (g_ref[...] + got_ref[...]).astype(o_ref.dtype)

    blk = pl.BlockSpec((None, tl, C), lambda s, i, c: (s, i, 0))
    out = pl.pallas_call(
        body, name=name,
        grid_spec=pltpu.PrefetchScalarGridSpec(
            num_scalar_prefetch=1, grid=(parts, rh // tl),
            in_specs=[pl.BlockSpec((None, None, tl, C), lambda s, i, c: (s, c[0], i, 0)), blk], out_specs=blk),
        out_shape=S((parts, rh, C), bf16), compiler_params=_cparams(("parallel", "parallel")),
    )(_core_index(), g4, got3)
    return out.reshape(got.shape)


def sum_slots_half(x, name):
    n, rh, C = x.shape
    tl = _pick(rh, (512, 256, 128, 64, 32, 16, 8))

    def body(c_ref, x_ref, o_ref):
        acc = x_ref[0].astype(f32)
        for s in range(1, n):
            acc = acc + x_ref[s].astype(f32)
        o_ref[...] = acc

    return pl.pallas_call(
        body, name=name,
        grid_spec=pltpu.PrefetchScalarGridSpec(
            num_scalar_prefetch=1, grid=(rh // tl,),
            in_specs=[pl.BlockSpec((n, tl, C), lambda i, c: (0, i, 0))],
            out_specs=pl.BlockSpec((None, tl, C), lambda i, c: (c[0], i, 0))),
        out_shape=S((2, rh, C), f32), compiler_params=_cparams(("parallel",)),
    )(_core_index(), x)


def _block_diag(blocks):
    G, a, b = blocks.shape
    eye = jnp.eye(G, dtype=blocks.dtype)
    return (blocks[:, :, None, :] * eye[:, None, :, None]).reshape(G * a, G * b)


def _flat_pack(arrs, dtype, lanes=1024, row_mult=16):
    flat = jnp.concatenate([a.reshape(-1).astype(dtype) for a in arrs])
    n = flat.shape[0]
    per = lanes * row_mult
    pad = (-n) % per
    if pad:
        flat = jnp.concatenate([flat, jnp.zeros((pad,), dtype)])
    return flat.reshape(-1, lanes)


def _flat_unpack(buf, shapes):
    flat = buf.reshape(-1)
    out, off = [], 0
    for s in shapes:
        n = math.prod(s)
        out.append(flat[off:off + n].reshape(s))
        off += n
    return out


def _layer_weights(p, H):
    w_in = p["w_in"]
    D = w_in.shape[0]
    s5w = p["s5_d"].shape[-1]
    hk = p["dn_proj_w"].shape[0]
    off_z = s5w + 4 * hk
    off_a = off_z + 2 * H
    wp = jnp.concatenate([w_in[:, :off_z], w_in[:, off_a:], jnp.pad(w_in[:, off_z:off_a], ((0, 0), (0, 128 - 2 * H)))], axis=1)
    q = {}
    q["Wu"] = Win(wp, 0, s5w)
    q["Wqkv"] = Win(wp, s5w, 3 * hk)
    q["Wz"] = Win(wp, s5w + 3 * hk, hk)
    q["Wgs"] = Win(wp, off_z, D)
    q["Wgd"] = Win(wp, off_z + D, D)
    q["Wba"] = Win(wp, off_z + 2 * D, 128)
    q["Wga"], q["Wgb"] = Win(p["s5_glu_w"], 0, D), Win(p["s5_glu_w"], D, D)
    F = p["ffn_down"].shape[0]
    q["Wupa"], q["Wupv"] = Win(p["ffn_up"], 0, F), Win(p["ffn_up"], F, F)
    q["cwa"], q["cwv"] = p["ffn_conv_w"][:, :F], p["ffn_conv_w"][:, F:]
    for k in ("dn_proj_w", "w_out", "ffn_down", "dn_conv_w", "mix_norm_w", "ffn_norm_w", "dn_norm_w",
              "dn_a_log", "dn_dt_bias", "s5_d"):
        q[k] = p[k]
    return q


def _s5_params(p, tag):
    G, P = p["s5_a_re"].shape
    HG = p["s5_b_re"].shape[-1]
    col = lambda a: a.reshape(G * P, 1)
    lr, li = col(p["s5_a_re"]), col(p["s5_a_im"])
    logdt = col(jnp.broadcast_to(p["s5_log_dt"][:, None], (G, P)))
    br, bi = p["s5_b_re"].reshape(G * P, HG), p["s5_b_im"].reshape(G * P, HG)
    ar, ai, bbr, bbi = s5_disc(lr, li, logdt, br, bi, "s5_disc")
    bd = lambda m: _block_diag(m.reshape(G, P, HG).transpose(0, 2, 1)).astype(bf16)
    cd = lambda m: _block_diag(m.transpose(0, 2, 1)).astype(bf16)
    return dict(lr=lr, li=li, logdt=logdt, br=br, bi=bi, ar=ar.reshape(1, G * P), ai=ai.reshape(1, G * P),
                Bre=bd(bbr), Bim=bd(bbi), CreT=cd(p["s5_c_re"]), mCimT=cd(-p["s5_c_im"]), G=G, P=P, HG=HG)


def layer_fwd(x, q, s5, H, payloads=(None, None, None)):
    r = {"x": x}
    h1 = rms_fwd(x, q["mix_norm_w"], "rms_mix")
    r["h1"] = h1
    u32, u16 = mm(h1, q["Wu"], "nn", "proj_u", out_dtypes=(f32, bf16), epi=lambda a: (a, a))
    qkv = mm(h1, q["Wqkv"], "nn", "proj_qkv")
    z = mm(h1, q["Wz"], "nn", "proj_z")
    ba = mm(h1, q["Wba"], "nn", "proj_ba")
    gs = mm(h1, q["Wgs"], "nn", "proj_gs")
    gd = mm(h1, q["Wgd"], "nn", "proj_gd")
    r.update(u32=u32, u16=u16, qkv=qkv, z=z, ba=ba, gs=gs, gd=gd)
    HG, P = s5["HG"], s5["P"]
    bur = mm_bd(u16, s5["Bre"], "nn", "s5_bu_re", HG, P)
    bui = mm_bd(u16, s5["Bim"], "nn", "s5_bu_im", HG, P)
    xr, xi = s5_scan_fwd(bur, bui, s5["ar"], s5["ai"], "s5_scan_fwd")
    y1 = mm_bd(xr, s5["CreT"], "nn", "s5_y_re", P, HG)
    ypre, ys5 = mm_bd(xi, s5["mCimT"], "nn", "s5_y_im", P, HG, extras=(y1, u32, q["s5_d"]), epi=_s5_y_epi, out_dtypes=(f32, bf16))
    ga = mm(ys5, q["Wga"], "nn", "glu_a")
    gb = mm(ys5, q["Wgb"], "nn", "glu_b")
    r.update(xr=xr, xi=xi, ypre=ypre, ys5=ys5, ga=ga, gb=gb)
    qn, kn, vv, bg = dn_prep(qkv, q["dn_conv_w"], ba, q["dn_a_log"], q["dn_dt_bias"], H, "dn_prep")
    L = x.shape[0]
    bgt = bg.reshape(L // DN_CHUNK, DN_CHUNK, 2 * H).transpose(0, 2, 1)
    o, ss, carried = dn_chunk_fwd(qn, kn, vv, bg, bgt, H, "dn_chunk_fwd", payloads[0])
    ydn = dn_gate(o, z, q["dn_norm_w"], H, "dn_gate")
    brdn, merged = mm(ydn, q["dn_proj_w"], "nn", "dn_proj", extras=(gs, gd, ga, gb), epi=_merge_epi, out_dtypes=(f32, bf16))
    r.update(qn=qn, kn=kn, vv=vv, bg=bg, bgt=bgt, ss=ss, o=o, ydn=ydn, brdn=brdn, merged=merged)
    x1 = mm(merged, q["w_out"], "nn", "out_proj", extras=(x,), epi=_add)
    h2 = rms_fwd(x1, q["ffn_norm_w"], "rms_ffn")
    ua = mm(h2, q["Wupa"], "nn", "ffn_up_a", payload=payloads[1])
    uv = mm(h2, q["Wupv"], "nn", "ffn_up_v", payload=payloads[2])
    if payloads[1] is not None:
        (ua, got_a), (uv, got_v) = ua, uv
        carried = [carried, got_a, got_v]
    hmid = ffn_mid(ua, uv, q["cwa"], q["cwv"], "ffn_mid")
    x2 = mm(hmid, q["ffn_down"], "nn", "ffn_down", extras=(x1,), epi=_add)
    r.update(x1=x1, h2=h2, ua=ua, uv=uv, hmid=hmid)
    return x2, r, carried


def layer_bwd(dx2, dx2b, r, q, s5, H, reduction=None):
    g = {}
    payload = None
    if reduction is None:
        dhmid = mm(dx2b, q["ffn_down"], "nt", "d_hmid")
    else:
        dhmid, theirs = mm(dx2b, q["ffn_down"], "nt", "d_hmid", payload=reduction.pair)
        payload = reduction.chips(theirs)
    g["ffn_down"] = mm(r["hmid"], dx2b, "tn", "dw_ffn_down")
    dua, duv, dwa, dwv = ffn_mid_bwd(r["ua"], r["uv"], q["cwa"], q["cwv"], dhmid, "ffn_mid_bwd")
    g["ffn_conv_w"] = jnp.concatenate([dwa, dwv], axis=1)
    dh2 = mm(dua, q["Wupa"], "nt", "d_h2_a")
    dh2 = mm(duv, q["Wupv"], "nt", "d_h2_v", extras=(dh2,), epi=_add)
    F = dua.shape[1]
    dwup = mm(r["h2"], dua, "tn", "dw_up_a", out_into=(2 * F, 0, None))
    g["ffn_up"] = mm(r["h2"], duv, "tn", "dw_up_v", out_into=(2 * F, F, dwup))
    dx1, dx1b, dffn_w = rms_bwd(r["x1"], q["ffn_norm_w"], dh2, dx2, "rms_ffn_bwd")
    g["ffn_norm_w"] = dffn_w[0]
    dm = mm(dx1b, q["w_out"], "nt", "d_merged")
    g["w_out"] = mm(r["merged"], dx1b, "tn", "dw_out")
    dgs, dgd, dga, dgb, dbrdn = merge_bwd(dm, r["gs"], r["gd"], r["ga"], r["gb"], r["brdn"], "merge_bwd")
    dydn = mm(dbrdn, q["dn_proj_w"], "nt", "d_ydn")
    g["dn_proj_w"] = mm(r["ydn"], dbrdn, "tn", "dw_dn_proj")
    do, dz, dnw = dn_gate_bwd(r["o"], r["z"], q["dn_norm_w"], dydn, H, "dn_gate_bwd")
    g["dn_norm_w"] = dnw[0]
    dq, dk, dv, dbg, dgt, carried = dn_chunk_bwd(r["qn"], r["kn"], r["vv"], r["bg"], r["bgt"], r["ss"], do, H,
                                                 "dn_chunk_bwd", payload)
    L = dq.shape[0]
    dbg = dbg + jnp.concatenate([jnp.zeros((L, H), f32), dgt.transpose(0, 2, 1).reshape(L, H)], axis=1)
    dc, dba, dcw, dal, ddt = dn_prep_bwd(r["qkv"], q["dn_conv_w"], r["ba"], q["dn_a_log"], q["dn_dt_bias"],
                                         dq, dk, dv, dbg, H, "dn_prep_bwd")
    g["dn_conv_w"], g["dn_a_log"], g["dn_dt_bias"] = dcw, dal[0], ddt[0]
    dqkv = conv_t(dc, q["dn_conv_w"], "dn_conv_t")
    dys5 = mm(dga, q["Wga"], "nt", "d_ys5_a")
    dys5 = mm(dgb, q["Wgb"], "nt", "d_ys5_b", extras=(dys5,), epi=_add)
    D = dga.shape[1]
    dwglu = mm(r["ys5"], dga, "tn", "dw_glu_a", out_into=(2 * D, 0, None))
    g["s5_glu_w"] = mm(r["ys5"], dgb, "tn", "dw_glu_b", out_into=(2 * D, D, dwglu))
    dyp, du_direct, dd = s5_out_bwd(dys5, r["ypre"], r["u32"], q["s5_d"], "s5_out_bwd")
    g["s5_d"] = dd[0]
    G, P, HG = s5["G"], s5["P"], s5["HG"]
    gdr = mm_bd(dyp, s5["CreT"], "nt", "s5_gx_re", P, HG)
    gdi = mm_bd(dyp, s5["mCimT"], "nt", "s5_gx_im", P, HG)
    dcre = _diag_blocks(mm_bd(r["xr"], dyp, "tn", "dw_s5_c_re", P, HG), G, P, HG)
    dcim = _diag_blocks(mm_bd(r["xi"], dyp, "tn", "dw_s5_c_im", P, HG), G, P, HG)
    g["s5_c_re"], g["s5_c_im"] = dcre.transpose(0, 2, 1), -dcim.transpose(0, 2, 1)
    gxr, gxi, dar, dai = s5_scan_bwd(gdr, gdi, r["xr"], r["xi"], s5["ar"], s5["ai"], "s5_scan_bwd")
    dbre = _diag_blocks(mm_bd(r["u16"], gxr, "tn", "dw_s5_b_re", HG, P), G, HG, P)
    dbim = _diag_blocks(mm_bd(r["u16"], gxi, "tn", "dw_s5_b_im", HG, P), G, HG, P)
    tocol = lambda m: m.transpose(0, 2, 1).reshape(G * P, HG)
    dlr, dli, dlogdt, dbr, dbi = s5_disc_bwd(s5["lr"], s5["li"], s5["logdt"], s5["br"], s5["bi"],
                                             dar.reshape(G * P, 1), dai.reshape(G * P, 1), tocol(dbre), tocol(dbim), "s5_disc_bwd")
    g["s5_a_re"], g["s5_a_im"] = dlr.reshape(G, P), dli.reshape(G, P)
    g["s5_log_dt"] = jnp.sum(dlogdt.reshape(G, P), axis=1)
    g["s5_b_re"], g["s5_b_im"] = dbr.reshape(G, P, HG), dbi.reshape(G, P, HG)
    du = mm_bd(gxr, s5["Bre"], "nt", "d_u_re", HG, P)
    du = mm_bd(gxi, s5["Bim"], "nt", "d_u_im", HG, P, extras=(du, du_direct), epi=lambda a, b, c: (a + b + c,), out_dtypes=(bf16,))
    h1 = r["h1"]
    segs = [("Wu", du), ("Wqkv", dqkv), ("Wz", dz), ("Wba", dba), ("Wgs", dgs), ("Wgd", dgd)]
    dh1 = None
    dws = []
    for name, dseg in segs:
        if dh1 is None:
            dh1 = mm(dseg, q[name], "nt", "d_h1_" + name)
        else:
            dh1 = mm(dseg, q[name], "nt", "d_h1_" + name, extras=(dh1,), epi=_add)
        dw = mm(h1, dseg, "tn", "dw_in_" + name)
        dws.append(dw[:, :2 * H] if name == "Wba" else dw)
    g["w_in"] = jnp.concatenate(dws, axis=1)
    dx, dxb, dmix = rms_bwd(r["x"], q["mix_norm_w"], dh1, dx1, "rms_mix_bwd")
    g["mix_norm_w"] = dmix[0]
    return dx, dxb, g, carried


BIG = ("w_in", "s5_glu_w", "dn_proj_w", "w_out", "ffn_up", "ffn_down")
SHARDED_SMALL = ("dn_conv_w", "ffn_conv_w")
COL_SHARDED = ("w_in", "s5_glu_w", "dn_proj_w", "ffn_up", "dn_conv_w", "ffn_conv_w")
REPL = ("mix_norm_w", "s5_log_dt", "s5_a_re", "s5_a_im", "s5_b_re", "s5_b_im", "s5_c_re", "s5_c_im", "s5_d",
        "dn_a_log", "dn_dt_bias", "dn_norm_w", "ffn_norm_w")
WEIGHTS = ['mix_norm_w', 'w_in', 's5_log_dt', 's5_a_re', 's5_a_im', 's5_b_re', 's5_b_im', 's5_c_re', 's5_c_im', 's5_d',
           's5_glu_w', 'dn_conv_w', 'dn_a_log', 'dn_dt_bias', 'dn_norm_w', 'dn_proj_w', 'w_out', 'ffn_norm_w', 'ffn_up',
           'ffn_conv_w', 'ffn_down', 'final_norm_w']


def _join_shards(name, shards):
    return jnp.concatenate(shards, axis=-1 if name in COL_SHARDED else -2)


KINDS = {"w_in": "slot", "s5_glu_w": "col", "dn_proj_w": "col", "w_out": "slot", "ffn_up": "col", "ffn_down": "slot"}


BIG_KINDS = [KINDS[n] for n in BIG]


AG_GROUPS = (("w_in", "s5_glu_w", "dn_proj_w", "w_out"), ("ffn_up",), ("ffn_down",))


def gather_begin(shards, names):
    return ag_ici_payload([place_block(shards[n], KINDS[n], "ag_place_" + n) for n in names], [KINDS[n] for n in names])


def gather_finish(bufs):
    full = dict(zip(BIG, ag_d2d([bufs[n] for n in BIG], BIG_KINDS, "ag_d2d")))
    for n in ("w_out", "ffn_down"):
        full[n] = full[n].reshape(-1, full[n].shape[-1])
    w = full["w_in"]
    full["w_in"] = w.transpose(1, 0, 2).reshape(w.shape[1], -1)
    return full


def gather_small_sharded(shards):
    names = SHARDED_SMALL
    shapes = [shards[n].shape for n in names]
    got = chip_exchange(_flat_pack([shards[n] for n in names], f32, lanes=128, row_mult=8), "ag_small", same=True)
    per_chip = [_flat_unpack(got[j], shapes) for j in range(N_CHIPS)]
    return {n: _join_shards(n, [per_chip[j][k] for j in range(N_CHIPS)]) for k, n in enumerate(names)}


class Reduction:
    def __init__(self, tens):
        self.tens = tens
        self.pair = rs_pair_payload(tens, BIG_KINDS)

    def chips(self, theirs):
        sums = [add_half(a, t, k, "rs_pair_sum_" + n) for n, a, t, k in zip(BIG, self.tens, theirs, BIG_KINDS)]
        return rs_chips_payload(sums, BIG_KINDS)


def reduce_begin(g):
    tens = []
    for n in BIG:
        a = g[n]
        if n == "w_in":
            a = a.reshape(a.shape[0], N_CHIPS, -1).transpose(1, 0, 2)
        elif KINDS[n] == "slot":
            a = a.reshape(N_CHIPS, -1, a.shape[-1])
        tens.append(a)
    return Reduction(tens)


def reduce_finish(parts):
    halves = [sum_slots_half(x, "rs_chip_sum_" + n) for n, x in zip(BIG, parts)]
    return {n: a.reshape(-1, a.shape[-1]) for n, a in zip(BIG, rs_share(halves, "rs_share"))}


def all_reduce_small(arrs):
    shapes = [a.shape for a in arrs]
    pack = _flat_pack(arrs, f32, lanes=1024, row_mult=64)
    from_chips = chip_exchange(pack, "ar_chips", same=True)
    from_sib = sibling_exchange(from_chips, "ar_sibling")
    c = lax.axis_index("c")
    both = jnp.concatenate([jnp.where(c == 0, from_chips, from_sib), jnp.where(c == 0, from_sib, from_chips)], axis=0)
    return _flat_unpack(sum_slots(both, "ar_sum"), shapes)


def kernel(x, mix_norm_w, w_in, s5_log_dt, s5_a_re, s5_a_im, s5_b_re, s5_b_im, s5_c_re, s5_c_im, s5_d, s5_glu_w, dn_conv_w, dn_a_log, dn_dt_bias, dn_norm_w, dn_proj_w, w_out, ffn_norm_w, ffn_up, ffn_conv_w, ffn_down, final_norm_w, loss_target, m_mix_norm_w, m_w_in, m_s5_log_dt, m_s5_a_re, m_s5_a_im, m_s5_b_re, m_s5_b_im, m_s5_c_re, m_s5_c_im, m_s5_d, m_s5_glu_w, m_dn_conv_w, m_dn_a_log, m_dn_dt_bias, m_dn_norm_w, m_dn_proj_w, m_w_out, m_ffn_norm_w, m_ffn_up, m_ffn_conv_w, m_ffn_down, m_final_norm_w, v_mix_norm_w, v_w_in, v_s5_log_dt, v_s5_a_re, v_s5_a_im, v_s5_b_re, v_s5_b_im, v_s5_c_re, v_s5_c_im, v_s5_d, v_s5_glu_w, v_dn_conv_w, v_dn_a_log, v_dn_dt_bias, v_dn_norm_w, v_dn_proj_w, v_w_out, v_ffn_norm_w, v_ffn_up, v_ffn_conv_w, v_ffn_down, v_final_norm_w):
    args = locals()
    W = {n: args[n] for n in WEIGHTS}
    M = {n: args["m_" + n] for n in WEIGHTS}
    V = {n: args["v_" + n] for n in WEIGHTS}
    depth = mix_norm_w.shape[0]
    H = dn_a_log.shape[1]
    xs = x[0]
    target = loss_target[0]

    conv_full = gather_small_sharded({n: W[n] for n in SHARDED_SMALL})

    def layer_params(l, gathered):
        p = gather_finish(gathered)
        for n in REPL:
            p[n] = W[n][l]
        for n in SHARDED_SMALL:
            p[n] = conv_full[n][l]
        for n in ("mix_norm_w", "ffn_norm_w", "dn_norm_w", "dn_a_log", "dn_dt_bias", "s5_d"):
            p[n] = p[n].reshape(1, -1)
        return _layer_weights(p, H), _s5_params(p, l)

    def named(groups_results):
        return {n: buf for names, bufs in zip(AG_GROUPS, groups_results) for n, buf in zip(names, bufs)}

    layers, res = [], []
    begun = [gather_begin({n: W[n][0] for n in BIG}, names) for names in AG_GROUPS]
    gathered = [run_payload(p, "ag_ici_%d" % k) for k, p in enumerate(begun)]
    for l in range(depth):
        layers.append(layer_params(l, named(gathered)))
        nxt = (None, None, None)
        if l + 1 < depth:
            nxt = [gather_begin({n: W[n][l + 1] for n in BIG}, names) for names in AG_GROUPS]
        xs, r, gathered = layer_fwd(xs, layers[l][0], layers[l][1], H, nxt)
        res.append(r)
    dx, dxb, loss_part, dfinal = loss_head(xs, final_norm_w.reshape(1, -1), target, "loss_head")
    loss = lax.psum(loss_part[0, 0], ("x", "y", "c"))

    grads, sharded = [None] * depth, [None] * depth
    pending = None
    for l in reversed(range(depth)):
        dx, dxb, grads[l], parts = layer_bwd(dx, dxb, res[l], layers[l][0], layers[l][1], H, pending)
        if pending is not None:
            sharded[l + 1] = reduce_finish(parts)
        pending = reduce_begin(grads[l])
    theirs = run_payload(pending.pair, "rs_pair")
    sharded[0] = reduce_finish(run_payload(pending.chips(theirs), "rs_chips"))
    grad_x = dx[None]

    G = {}
    for n in BIG:
        G[n] = jnp.stack([sharded[l][n] for l in range(depth)])
    small_names = REPL + SHARDED_SMALL
    small = [jnp.stack([grads[l][n] for l in range(depth)]) for n in small_names] + [dfinal[0]]
    for n, a in zip(small_names + ("final_norm_w",), all_reduce_small(small)):
        G[n] = a
    chip = 2 * lax.axis_index("x") + lax.axis_index("y")
    for n in SHARDED_SMALL:
        cs = W[n].shape[-1]
        G[n] = lax.dynamic_slice_in_dim(G[n], chip * cs, cs, axis=-1)

    delta, new_m, new_v = {}, {}, {}
    for n in WEIGHTS:
        shape = W[n].shape
        size = math.prod(shape)
        if n in BIG + SHARDED_SMALL:
            two_d = (size // shape[-1], shape[-1])
        else:
            two_d = (size // 128, 128) if size % 128 == 0 else (1, size)
        d, nm, nv = adamw(W[n].reshape(two_d), G[n].reshape(two_d), M[n].reshape(two_d), V[n].reshape(two_d), "adamw_" + n)
        delta[n], new_m[n], new_v[n] = d.reshape(shape), nm.reshape(shape), nv.reshape(shape)
        G[n] = G[n].reshape(shape)
    return (loss, grad_x, *[G[n] for n in WEIGHTS], *[delta[n] for n in WEIGHTS],
            *[new_m[n] for n in WEIGHTS], *[new_v[n] for n in WEIGHTS])
```
